```python
import jax, jax.numpy as jnp
from jax import lax
import numpy as np

D_MODEL = 1024
BATCH = 2
SEQ = 8192
DEPTH = 1

N_META = 16
HGRN_HEADS = 8
HGRN_EXPAND = 128
HGRN_FDIM = HGRN_HEADS * HGRN_EXPAND
HGRN_VDIM = D_MODEL
HGRN_HEAD_V = HGRN_VDIM // HGRN_HEADS
CONV_DIM = D_MODEL
CONV_WIDTH = 3
D_FF = 2816
FFN_CONV_WIDTH = 3
CHUNK = 64
EPS = 1e-6
IN_SIZES = (HGRN_FDIM,
            HGRN_FDIM,
            HGRN_VDIM,
            HGRN_VDIM,
            CONV_DIM,
            CONV_DIM,
            CONV_DIM,
            D_MODEL,
            D_MODEL)
IN_TOTAL = sum(IN_SIZES)

kernel_name = "hgrn2_shortconv_gated_hybrid"


def _split_points(sizes):
    pts, acc = [], 0
    for s in sizes[:-1]:
        acc += s
        pts.append(acc)
    return pts


def rmsnorm(x, w):
    xf = x.astype(jnp.float32)
    y = xf * lax.rsqrt(jnp.mean(xf * xf, axis=-1, keepdims=True) + EPS)
    return (y * w.astype(jnp.float32)).astype(x.dtype)


def causal_dwconv(x, w, b=None):
    K = w.shape[0]
    L = x.shape[1]
    xp = jnp.pad(x, ((0, 0), (K - 1, 0), (0, 0)))
    y = w[0] * xp[:, 0:L]
    for j in range(1, K):
        y = y + w[j] * xp[:, j:j + L]
    if b is not None:
        y = y + b
    return y


def layer_lower_bounds(lb_param):
    p = jax.nn.softmax(lb_param.astype(jnp.float32), axis=0)
    return jnp.cumsum(p, axis=0)[:DEPTH]


def hgrn2_chunked(q, k, v, logf):
    Bsz, H, T, DK = q.shape
    DV = v.shape[-1]
    n = T // CHUNK

    def to_chunks(a):
        return jnp.moveaxis(a.reshape(Bsz, H, n, CHUNK, a.shape[-1]), 2, 0)

    qc, kc, vc, gc = to_chunks(q), to_chunks(k), to_chunks(v), to_chunks(logf)
    causal = jnp.tril(jnp.ones((CHUNK, CHUNK), dtype=bool))[:, :, None]

    def step(S, inp):
        qb, kb, vb, gb = inp
        G = jnp.cumsum(gb, axis=-2)
        o_inter = jnp.einsum('bhtd,bhdv->bhtv', qb * jnp.exp(G), S)
        diff = G[:, :, :, None, :] - G[:, :, None, :, :]
        decay = jnp.exp(jnp.where(causal, diff, -jnp.inf))
        A = jnp.sum(qb[:, :, :, None, :] * kb[:, :, None, :, :] * decay, axis=-1)
        o = o_inter + jnp.einsum('bhts,bhsv->bhtv', A, vb)
        G_last = G[:, :, -1:, :]
        k_dec = kb * jnp.exp(G_last - G)
        S_new = jnp.exp(G_last[:, :, 0, :])[..., None] * S + jnp.einsum('bhsd,bhsv->bhdv', k_dec, vb)
        return S_new, o

    S0 = jnp.zeros((Bsz, H, DK, DV), jnp.float32)
    _, o = lax.scan(step, S0, (qc, kc, vc, gc))
    return jnp.moveaxis(o, 0, 2).reshape(Bsz, H, T, DV)


def hybrid_layer(h, lb, attn_norm_w, w_in, hgrn_norm_w, conv_w, w_out,
                 ffn_norm_w, w_up, ffn_conv_w, ffn_conv_b, w_down):
    Bsz, L, _ = h.shape
    dt = h.dtype
    u = rmsnorm(h, attn_norm_w)
    proj = u @ w_in
    q, f_raw, i_in, g_out, b_gate, c_gate, x_conv, gate_a, gate_b = jnp.split(
        proj, _split_points(IN_SIZES), axis=-1)

    q = jax.nn.silu(q.astype(jnp.float32))
    f = lb + (1.0 - lb) * jax.nn.sigmoid(f_raw.astype(jnp.float32))
    logf = jnp.log(f)
    k = 1.0 - f
    v = i_in.astype(jnp.float32)

    def heads(a):
        return jnp.transpose(a.reshape(Bsz, L, HGRN_HEADS, -1), (0, 2, 1, 3))

    pad = (-N_META) % CHUNK
    tpad = ((0, 0), (0, 0), (pad, 0), (0, 0))
    qh, kh, vh, gh = (jnp.pad(heads(a), tpad) for a in (q, k, v, logf))
    o = hgrn2_chunked(qh, kh, vh, gh)[:, :, pad:]
    o = jnp.transpose(o, (0, 2, 1, 3))
    o = rmsnorm(o, hgrn_norm_w).reshape(Bsz, L, HGRN_VDIM)
    y_a = (o * jax.nn.silu(g_out.astype(jnp.float32))).astype(dt)

    y_b = b_gate * causal_dwconv(c_gate * x_conv, conv_w)

    merged = jax.nn.sigmoid(gate_a) * y_a + jax.nn.sigmoid(gate_b) * y_b
    h = h + merged @ w_out

    u2 = rmsnorm(h, ffn_norm_w)
    a, val = jnp.split(u2 @ w_up, 2, axis=-1)
    a = causal_dwconv(a, ffn_conv_w, ffn_conv_b)
    h = h + (jax.nn.silu(a) * val) @ w_down
    return h


def setup_inputs(seed: int = 0) -> dict:
    key = jax.random.key(seed)
    ks = jax.random.split(key, 16)
    f32 = jnp.float32
    nrm = lambda k, shape, s: jax.random.normal(k, shape, f32) * s
    return {
        "x": nrm(ks[0], (BATCH, SEQ, D_MODEL), 1.0),
        "meta_tokens": nrm(ks[1], (N_META, D_MODEL), 1.0),
        "lb_param": nrm(ks[2], (DEPTH + 1, HGRN_FDIM), 0.1),
        "attn_norm_w": 1.0 + nrm(ks[3], (DEPTH, D_MODEL), 0.02),
        "w_in": nrm(ks[4], (DEPTH, D_MODEL, IN_TOTAL), D_MODEL ** -0.5),
        "hgrn_norm_w": 1.0 + nrm(ks[5], (DEPTH, HGRN_HEAD_V), 0.02),
        "conv_w": nrm(ks[6], (DEPTH, CONV_WIDTH, CONV_DIM), CONV_WIDTH ** -0.5),
        "w_out": nrm(ks[7], (DEPTH, D_MODEL, D_MODEL), D_MODEL ** -0.5),
        "ffn_norm_w": 1.0 + nrm(ks[8], (DEPTH, D_MODEL), 0.02),
        "w_up": nrm(ks[9], (DEPTH, D_MODEL, 2 * D_FF), D_MODEL ** -0.5),
        "ffn_conv_w": nrm(ks[10], (DEPTH, FFN_CONV_WIDTH, D_FF), FFN_CONV_WIDTH ** -0.5),
        "ffn_conv_b": nrm(ks[11], (DEPTH, D_FF), 0.02),
        "w_down": nrm(ks[12], (DEPTH, D_FF, D_MODEL), D_FF ** -0.5),
        "final_norm_w": 1.0 + nrm(ks[13], (D_MODEL,), 0.02),
    }


def reference(x, meta_tokens, lb_param, attn_norm_w, w_in, hgrn_norm_w, conv_w, w_out,
              ffn_norm_w, w_up, ffn_conv_w, ffn_conv_b, w_down, final_norm_w):
    Bsz = x.shape[0]
    meta = jnp.broadcast_to(meta_tokens.astype(x.dtype)[None], (Bsz, N_META, D_MODEL))
    h = jnp.concatenate([meta, x], axis=1)
    lbs = layer_lower_bounds(lb_param)
    for l in range(DEPTH):
        h = hybrid_layer(h, lbs[l], attn_norm_w[l], w_in[l], hgrn_norm_w[l], conv_w[l],
                         w_out[l], ffn_norm_w[l], w_up[l], ffn_conv_w[l], ffn_conv_b[l],
                         w_down[l])
    h = rmsnorm(h, final_norm_w)
    return h[:, N_META:]
```

```python
import functools

import numpy as np
import jax
import jax.numpy as jnp
from jax import lax
from jax.experimental import pallas as pl
from jax.experimental.pallas import tpu as pltpu

D_MODEL = 1024
N_META = 16
N_HEADS = 8
HEAD_DIM = 128
D_FF = 2816
N_IN_SECTIONS = 9
EPS = 1e-6

SUBLANES = 8
CHUNK = 128
MIXER_TILE = 256
FFN_TILE = 512
FFN_COL_BLOCK = 1408
VMEM_LIMIT_BYTES = 56 * 1024 * 1024

F32 = jnp.float32
BF16 = jnp.bfloat16

_NT = (((1,), (1,)), ((), ()))
_TN = (((0,), (0,)), ((), ()))


def _rms(x, w):
    ms = jnp.mean(x * x, axis=-1, keepdims=True)
    return x * lax.rsqrt(ms + EPS) * w


def _sigmoid(x):
    return 1.0 / (1.0 + jnp.exp(-x))


def _bcast_rows(ref, h, row0, block, offset, nrows):
    pieces = [jnp.broadcast_to(ref[h, pl.ds(row0 + i * block + offset, 1), :], (block, HEAD_DIM))
              for i in range(nrows // block)]
    return pieces[0] if len(pieces) == 1 else jnp.concatenate(pieces, axis=0)


def _hgrn_chunk(h, row0, valid, st, q_ref, g_ref, k_ref, v_ref, rsel):
    rows = pl.ds(row0, CHUNK)
    q = q_ref[h, rows, :]
    G = g_ref[h, rows, :]
    k = k_ref[h, rows, :]
    v = v_ref[h, rows, :]
    row = lax.broadcasted_iota(jnp.int32, (CHUNK, HEAD_DIM), 0)

    sub = row & (SUBLANES - 1)
    ps = []
    for j in range(SUBLANES):
        gj = _bcast_rows(g_ref, h, row0, SUBLANES, j, CHUNK)
        kj = _bcast_rows(k_ref, h, row0, SUBLANES, j, CHUNK)
        p = q * kj * jnp.exp(G - gj)
        ps.append(jnp.where(sub >= j, p, 0.0).astype(BF16))
    pcat = jnp.concatenate(ps, axis=1)
    a = jnp.dot(pcat, rsel, preferred_element_type=F32)

    tl_xor = (lax.broadcasted_iota(jnp.int32, (CHUNK, CHUNK), 0)
              ^ lax.broadcasted_iota(jnp.int32, (CHUNK, CHUNK), 1))
    a = jnp.where(tl_xor < SUBLANES, a, 0.0)

    b = 2 * SUBLANES
    while b <= CHUNK:
        gm = _bcast_rows(g_ref, h, row0, b, b // 2 - 1, CHUNK)
        e = jnp.exp(-jnp.abs(G - gm))
        upper = (row & (b // 2)) != 0
        qt = jnp.where(upper, q * e, 0.0).astype(BF16)
        kt = jnp.where(upper, 0.0, k * e).astype(BF16)
        ab = lax.dot_general(qt, kt, _NT, preferred_element_type=F32)
        a = a + (ab if b == CHUNK else jnp.where(tl_xor < b, ab, 0.0))
        b *= 2

    qi = (q * jnp.exp(G)).astype(BF16)
    o = lax.dot_general(qi, st.astype(BF16), _NT, preferred_element_type=F32)
    o = o + jnp.dot(a.astype(BF16), v, preferred_element_type=F32)

    glast = g_ref[h, pl.ds(row0 + valid - 1, 1), :]
    kd = k * jnp.exp(glast - G)
    if valid < CHUNK:
        kd = jnp.where(row < valid, kd, 0.0)
    st = st * jnp.exp(glast) + lax.dot_general(v, kd.astype(BF16), _TN, preferred_element_type=F32)
    return o, st


def _mixer_tile(x, R, valid, refs):
    (lbp_ref, anw_ref, win_ref, hnw_ref, cw_ref, wout_ref, rsel_ref,
     st_ref, q_ref, g_ref, k_ref, v_ref, gs_ref, sa_ref, mb_ref, mg_ref, zs_ref) = refs

    u = _rms(x, anw_ref[...]).astype(BF16)

    def proj(sec):
        return jnp.dot(u, win_ref[:, sec * D_MODEL:(sec + 1) * D_MODEL], preferred_element_type=F32)

    def to_heads(ref, val):
        for hh in range(N_HEADS):
            ref[hh, 0:R, :] = val[:, hh * HEAD_DIM:(hh + 1) * HEAD_DIM].astype(ref.dtype)

    qv = proj(0)
    to_heads(q_ref, qv * _sigmoid(qv))

    lbp = lbp_ref[...]
    mx = jnp.max(lbp, axis=0, keepdims=True)
    ex = jnp.exp(lbp - mx)
    lb = ex[0:1, :] / jnp.sum(ex, axis=0, keepdims=True)
    f = lb + (1.0 - lb) * _sigmoid(proj(1))
    to_heads(k_ref, 1.0 - f)
    lf = jnp.log(f)
    tri = (lax.broadcasted_iota(jnp.int32, (CHUNK, CHUNK), 1)
           <= lax.broadcasted_iota(jnp.int32, (CHUNK, CHUNK), 0)).astype(BF16)
    for c in range(R // CHUNK):
        lfc = lf[c * CHUNK:(c + 1) * CHUNK, :]
        hi = lfc.astype(BF16)
        lo = (lfc - hi.astype(F32)).astype(BF16)
        gc = (jnp.dot(tri, hi, preferred_element_type=F32)
              + jnp.dot(tri, lo, preferred_element_type=F32))
        for hh in range(N_HEADS):
            g_ref[hh, c * CHUNK:(c + 1) * CHUNK, :] = gc[:, hh * HEAD_DIM:(hh + 1) * HEAD_DIM]

    to_heads(v_ref, proj(2))
    gv = proj(3)
    to_heads(gs_ref, gv * _sigmoid(gv))

    bg = proj(4)
    z = proj(5) * proj(6)
    zs_ref[SUBLANES:SUBLANES + R, :] = z
    z1 = zs_ref[SUBLANES - 1:SUBLANES - 1 + R, :]
    z2 = zs_ref[SUBLANES - 2:SUBLANES - 2 + R, :]
    cw = cw_ref[...]
    yb = bg * (cw[0:1, :] * z2 + cw[1:2, :] * z1 + cw[2:3, :] * z)
    zs_ref[0:SUBLANES, :] = zs_ref[valid:valid + SUBLANES, :]

    to_heads(sa_ref, _sigmoid(proj(7)))
    to_heads(mb_ref, _sigmoid(proj(8)) * yb)

    rsel = rsel_ref[...]
    hnw = hnw_ref[...]

    def head_body(h, carry):
        st = st_ref[h]
        for c in range(R // CHUNK):
            row0 = c * CHUNK
            o, st = _hgrn_chunk(h, row0, min(valid - row0, CHUNK), st, q_ref, g_ref, k_ref, v_ref, rsel)
            rows = pl.ds(row0, CHUNK)
            on = _rms(o, hnw)
            m = on * gs_ref[h, rows, :] * sa_ref[h, rows, :] + mb_ref[h, rows, :]
            mg_ref[h, rows, :] = m.astype(BF16)
        st_ref[h] = st
        return carry

    lax.fori_loop(0, N_HEADS, head_body, 0)

    merged = jnp.concatenate([mg_ref[hh, 0:R, :] for hh in range(N_HEADS)], axis=1)
    return x + jnp.dot(merged, wout_ref[...], preferred_element_type=F32)


def _mixer_kernel(x_ref, meta_ref, lbp_ref, anw_ref, win_ref, hnw_ref, cw_ref, wout_ref, rsel_ref,
                  h1_ref, h1m_ref, *scratch):
    refs = (lbp_ref, anw_ref, win_ref, hnw_ref, cw_ref, wout_ref, rsel_ref) + scratch
    st_ref, zs_ref = scratch[0], scratch[-1]

    @pl.when(pl.program_id(1) == 0)
    def _():
        st_ref[...] = jnp.zeros_like(st_ref)
        zs_ref[0:SUBLANES, :] = jnp.zeros((SUBLANES, D_MODEL), F32)
        xm = jnp.concatenate([meta_ref[...], jnp.zeros((CHUNK - N_META, D_MODEL), F32)], axis=0)
        hm = _mixer_tile(xm, CHUNK, N_META, refs)
        h1m_ref[0] = hm[0:N_META, :]

    h1_ref[0] = _mixer_tile(x_ref[0], MIXER_TILE, MIXER_TILE, refs)


def _ffn_kernel(h1_ref, h1m_ref, fnw_ref, wup_ref, fcw_ref, fcb_ref, wdn_ref, finw_ref,
                out_ref, as_ref):
    T = FFN_TILE
    fnw = fnw_ref[...]

    @pl.when(pl.program_id(1) == 0)
    def _():
        um = _rms(h1m_ref[0], fnw).astype(BF16)
        am = jnp.dot(um, wup_ref[:, 0:D_FF], preferred_element_type=F32)
        as_ref[0:SUBLANES, :] = am[N_META - SUBLANES:N_META, :]

    x = h1_ref[0]
    u = _rms(x, fnw).astype(BF16)
    fcw = fcw_ref[...]
    fcb = fcb_ref[...]
    y = x
    for cb in range(D_FF // FFN_COL_BLOCK):
        cols = slice(cb * FFN_COL_BLOCK, (cb + 1) * FFN_COL_BLOCK)
        a = jnp.dot(u, wup_ref[:, cols], preferred_element_type=F32)
        as_ref[SUBLANES:SUBLANES + T, cols] = a
        a1 = as_ref[SUBLANES - 1:SUBLANES - 1 + T, cols]
        a2 = as_ref[SUBLANES - 2:SUBLANES - 2 + T, cols]
        ac = fcw[0:1, cols] * a2 + fcw[1:2, cols] * a1 + fcw[2:3, cols] * a + fcb[:, cols]
        as_ref[0:SUBLANES, cols] = as_ref[T:T + SUBLANES, cols]
        val = jnp.dot(u, wup_ref[:, D_FF + cb * FFN_COL_BLOCK:D_FF + (cb + 1) * FFN_COL_BLOCK],
                      preferred_element_type=F32)
        gated = (ac * _sigmoid(ac) * val).astype(BF16)
        y = y + jnp.dot(gated, wdn_ref[cols, :], preferred_element_type=F32)
    out_ref[0] = _rms(y, finw_ref[...])


def _resident(shape):
    return pl.BlockSpec(shape, lambda b, t: (0,) * len(shape), pipeline_mode=pl.Buffered(1))


def _rsel_matrix():
    r = np.arange(N_HEADS * HEAD_DIM)[:, None] // HEAD_DIM
    l = np.arange(CHUNK)[None, :] % SUBLANES
    return jnp.asarray(r == l, dtype=BF16)


def kernel(x, meta_tokens, lb_param, attn_norm_w, w_in, hgrn_norm_w, conv_w, w_out, ffn_norm_w, w_up,
           ffn_conv_w, ffn_conv_b, w_down, final_norm_w):
    B, L, D = x.shape
    assert D == D_MODEL and L % MIXER_TILE == 0 and L % FFN_TILE == 0
    assert w_in.shape == (1, D, N_IN_SECTIONS * D) and w_up.shape == (1, D, 2 * D_FF)
    assert SUBLANES * HEAD_DIM == D_MODEL and N_HEADS == SUBLANES

    tile = lambda T: pl.BlockSpec((1, T, D), lambda b, t: (b, t, 0))
    meta_tile = pl.BlockSpec((1, N_META, D), lambda b, t: (b, 0, 0))
    params = pltpu.CompilerParams(dimension_semantics=("arbitrary", "arbitrary"),
                                  vmem_limit_bytes=VMEM_LIMIT_BYTES)

    TM = MIXER_TILE
    head_f32 = pltpu.VMEM((N_HEADS, TM, HEAD_DIM), F32)
    head_bf16 = pltpu.VMEM((N_HEADS, TM, HEAD_DIM), BF16)
    h1, h1m = pl.pallas_call(
        _mixer_kernel,
        grid=(B, L // TM),
        in_specs=[tile(TM), _resident((N_META, D)), _resident((2, D)), _resident((1, D)),
                  _resident((D, N_IN_SECTIONS * D)), _resident((1, HEAD_DIM)), _resident((3, D)),
                  _resident((D, D)), _resident((N_HEADS * HEAD_DIM, CHUNK))],
        out_specs=[tile(TM), meta_tile],
        out_shape=[jax.ShapeDtypeStruct((B, L, D), F32), jax.ShapeDtypeStruct((B, N_META, D), F32)],
        scratch_shapes=[pltpu.VMEM((N_HEADS, HEAD_DIM, HEAD_DIM), F32),
                        head_f32, head_f32, head_f32, head_bf16,
                        head_f32, head_f32, head_f32, head_bf16,
                        pltpu.VMEM((TM + SUBLANES, D), F32)],
        compiler_params=params,
        name="mixer",
    )(x, meta_tokens, lb_param, attn_norm_w, w_in[0].astype(BF16), hgrn_norm_w, conv_w[0],
      w_out[0].astype(BF16), _rsel_matrix())

    TF = FFN_TILE
    out = pl.pallas_call(
        _ffn_kernel,
        grid=(B, L // TF),
        in_specs=[tile(TF), meta_tile, _resident((1, D)), _resident((D, 2 * D_FF)), _resident((3, D_FF)),
                  _resident((1, D_FF)), _resident((D_FF, D)), _resident((1, D))],
        out_specs=tile(TF),
        out_shape=jax.ShapeDtypeStruct((B, L, D), F32),
        scratch_shapes=[pltpu.VMEM((TF + SUBLANES, D_FF), F32)],
        compiler_params=params,
        name="ffn",
    )(h1, h1m, ffn_norm_w, w_up[0].astype(BF16), ffn_conv_w[0], ffn_conv_b, w_down[0].astype(BF16),
      final_norm_w.reshape(1, D))
    return out
```

```python
import numpy as np
import jax
import jax.numpy as jnp
from jax import lax
from jax.experimental import pallas as pl
from jax.experimental.pallas import tpu as pltpu

D_MODEL = 1024
N_META = 16
N_HEADS = 8
HEAD_DIM = 128
D_FF = 2816
N_IN_SECTIONS = 9
EPS = 1e-6

SUBLANES = 8
CHUNK = 128
MIXER_TILE = 256
FFN_TILE = 512
FFN_COL_BLOCK = 1408
VMEM_LIMIT_BYTES = 56 * 1024 * 1024

F32 = jnp.float32
BF16 = jnp.bfloat16

_NT = (((1,), (1,)), ((), ()))
_TN = (((0,), (0,)), ((), ()))


def _rms(x, w):
    ms = jnp.mean(x * x, axis=-1, keepdims=True)
    return x * lax.rsqrt(ms + EPS) * w


def _sigmoid(x):
    return 1.0 / (1.0 + jnp.exp(-x))


def _bcast_rows(ref, row0, block, offset, nrows):
    pieces = [jnp.broadcast_to(ref[pl.ds(row0 + i * block + offset, 1), :], (block, HEAD_DIM))
              for i in range(nrows // block)]
    return pieces[0] if len(pieces) == 1 else jnp.concatenate(pieces, axis=0)


def _hgrn_chunk(row0, valid, st, q_ref, g_ref, k_ref, v_ref, rsel):
    rows = pl.ds(row0, CHUNK)
    q = q_ref[rows, :]
    G = g_ref[rows, :]
    k = k_ref[rows, :]
    v = v_ref[rows, :]
    row = lax.broadcasted_iota(jnp.int32, (CHUNK, HEAD_DIM), 0)

    sub = row & (SUBLANES - 1)
    ps = []
    for j in range(SUBLANES):
        gj = _bcast_rows(g_ref, row0, SUBLANES, j, CHUNK)
        kj = _bcast_rows(k_ref, row0, SUBLANES, j, CHUNK)
        p = q * kj * jnp.exp(G - gj)
        ps.append(jnp.where(sub >= j, p, 0.0).astype(BF16))
    pcat = jnp.concatenate(ps, axis=1)
    a = jnp.dot(pcat, rsel, preferred_element_type=F32)

    tl_xor = (lax.broadcasted_iota(jnp.int32, (CHUNK, CHUNK), 0)
              ^ lax.broadcasted_iota(jnp.int32, (CHUNK, CHUNK), 1))
    a = jnp.where(tl_xor < SUBLANES, a, 0.0)

    b = 2 * SUBLANES
    while b <= CHUNK:
        gm = _bcast_rows(g_ref, row0, b, b // 2 - 1, CHUNK)
        e = jnp.exp(-jnp.abs(G - gm))
        upper = (row & (b // 2)) != 0
        qt = jnp.where(upper, q * e, 0.0).astype(BF16)
        kt = jnp.where(upper, 0.0, k * e).astype(BF16)
        ab = lax.dot_general(qt, kt, _NT, preferred_element_type=F32)
        a = a + (ab if b == CHUNK else jnp.where(tl_xor < b, ab, 0.0))
        b *= 2

    qi = (q * jnp.exp(G)).astype(BF16)
    o = lax.dot_general(qi, st.astype(BF16), _NT, preferred_element_type=F32)
    o = o + jnp.dot(a.astype(BF16), v, preferred_element_type=F32)

    glast = g_ref[pl.ds(row0 + valid - 1, 1), :]
    kd = k * jnp.exp(glast - G)
    if valid < CHUNK:
        kd = jnp.where(row < valid, kd, 0.0)
    st = st * jnp.exp(glast) + lax.dot_general(v, kd.astype(BF16), _TN, preferred_element_type=F32)
    return o, st


def _project_head(u, h, R, valid, prm, slot):
    lbp_ref, win_ref, cw_ref, zc_ref = prm
    q_ref, g_ref, k_ref, v_ref, gs_ref, sa_ref, mb_ref, zs_ref = slot
    p = jnp.dot(u, win_ref[h], preferred_element_type=F32)
    sec = lambda i: p[:, i * HEAD_DIM:(i + 1) * HEAD_DIM]

    qv = sec(0)
    q_ref[0:R, :] = qv * _sigmoid(qv)

    lbp = lbp_ref[h]
    mx = jnp.max(lbp, axis=0, keepdims=True)
    ex = jnp.exp(lbp - mx)
    lb = ex[0:1, :] / jnp.sum(ex, axis=0, keepdims=True)
    f = lb + (1.0 - lb) * _sigmoid(sec(1))
    k_ref[0:R, :] = 1.0 - f
    lf = jnp.log(f)
    tri = (lax.broadcasted_iota(jnp.int32, (CHUNK, CHUNK), 1)
           <= lax.broadcasted_iota(jnp.int32, (CHUNK, CHUNK), 0)).astype(BF16)
    for c in range(R // CHUNK):
        lfc = lf[c * CHUNK:(c + 1) * CHUNK, :]
        hi = lfc.astype(BF16)
        lo = (lfc - hi.astype(F32)).astype(BF16)
        g_ref[c * CHUNK:(c + 1) * CHUNK, :] = (jnp.dot(tri, hi, preferred_element_type=F32)
                                               + jnp.dot(tri, lo, preferred_element_type=F32))

    v_ref[0:R, :] = sec(2).astype(BF16)
    gv = sec(3)
    gs_ref[0:R, :] = gv * _sigmoid(gv)

    z = sec(5) * sec(6)
    zs_ref[0:SUBLANES, :] = zc_ref[h]
    zs_ref[SUBLANES:SUBLANES + R, :] = z
    z1 = zs_ref[SUBLANES - 1:SUBLANES - 1 + R, :]
    z2 = zs_ref[SUBLANES - 2:SUBLANES - 2 + R, :]
    cw = cw_ref[h]
    yb = sec(4) * (cw[0:1, :] * z2 + cw[1:2, :] * z1 + cw[2:3, :] * z)
    zc_ref[h] = zs_ref[valid:valid + SUBLANES, :]

    sa_ref[0:R, :] = _sigmoid(sec(7))
    mb_ref[0:R, :] = _sigmoid(sec(8)) * yb


def _recur_head(h, R, valid, hnw, rsel, st_ref, mg_ref, slot):
    q_ref, g_ref, k_ref, v_ref, gs_ref, sa_ref, mb_ref, _ = slot
    st = st_ref[h]
    for c in range(R // CHUNK):
        row0 = c * CHUNK
        o, st = _hgrn_chunk(row0, min(valid - row0, CHUNK), st, q_ref, g_ref, k_ref, v_ref, rsel)
        rows = pl.ds(row0, CHUNK)
        m = _rms(o, hnw) * gs_ref[rows, :] * sa_ref[rows, :] + mb_ref[rows, :]
        mg_ref[h, rows, :] = m.astype(BF16)
    st_ref[h] = st


def _mixer_tile(x, R, valid, overlap, refs):
    (lbp_ref, anw_ref, win_ref, hnw_ref, cw_ref, wout_ref, rsel_ref,
     st_ref, zc_ref, mg_ref, slot_a, slot_b) = refs
    prm = (lbp_ref, win_ref, cw_ref, zc_ref)
    u = _rms(x, anw_ref[...]).astype(BF16)
    rsel = rsel_ref[...]
    hnw = hnw_ref[...]
    project = lambda h, slot: _project_head(u, h, R, valid, prm, slot)
    recur = lambda h, slot: _recur_head(h, R, valid, hnw, rsel, st_ref, mg_ref, slot)

    if overlap:
        project(0, slot_a)

        def pair_body(i, carry):
            h = 2 * i
            project(h + 1, slot_b)
            recur(h, slot_a)
            project(h + 2, slot_a)
            recur(h + 1, slot_b)
            return carry

        lax.fori_loop(0, N_HEADS // 2 - 1, pair_body, 0)
        project(N_HEADS - 1, slot_b)
        recur(N_HEADS - 2, slot_a)
        recur(N_HEADS - 1, slot_b)
    else:
        def head_body(h, carry):
            project(h, slot_a)
            recur(h, slot_a)
            return carry

        lax.fori_loop(0, N_HEADS, head_body, 0)

    merged = jnp.concatenate([mg_ref[hh, 0:R, :] for hh in range(N_HEADS)], axis=1)
    return x + jnp.dot(merged, wout_ref[...], preferred_element_type=F32)


def _mixer_kernel(x_ref, meta_ref, lbp_ref, anw_ref, win_ref, hnw_ref, cw_ref, wout_ref, rsel_ref,
                  h1_ref, h1m_ref, st_ref, zc_ref, mg_ref, *slots):
    n = len(slots) // 2
    refs = (lbp_ref, anw_ref, win_ref, hnw_ref, cw_ref, wout_ref, rsel_ref,
            st_ref, zc_ref, mg_ref, slots[:n], slots[n:])

    @pl.when(pl.program_id(1) == 0)
    def _():
        st_ref[...] = jnp.zeros_like(st_ref)
        zc_ref[...] = jnp.zeros_like(zc_ref)
        xm = jnp.concatenate([meta_ref[...], jnp.zeros((CHUNK - N_META, D_MODEL), F32)], axis=0)
        hm = _mixer_tile(xm, CHUNK, N_META, False, refs)
        h1m_ref[0] = hm[0:N_META, :]

    h1_ref[0] = _mixer_tile(x_ref[0], MIXER_TILE, MIXER_TILE, True, refs)


def _ffn_kernel(h1_ref, h1m_ref, fnw_ref, wup_ref, fcw_ref, fcb_ref, wdn_ref, finw_ref,
                out_ref, as_ref):
    T = FFN_TILE
    fnw = fnw_ref[...]

    @pl.when(pl.program_id(1) == 0)
    def _():
        um = _rms(h1m_ref[0], fnw).astype(BF16)
        am = jnp.dot(um, wup_ref[:, 0:D_FF], preferred_element_type=F32)
        as_ref[0:SUBLANES, :] = am[N_META - SUBLANES:N_META, :]

    x = h1_ref[0]
    u = _rms(x, fnw).astype(BF16)
    fcw = fcw_ref[...]
    fcb = fcb_ref[...]
    y = x
    for cb in range(D_FF // FFN_COL_BLOCK):
        cols = slice(cb * FFN_COL_BLOCK, (cb + 1) * FFN_COL_BLOCK)
        a = jnp.dot(u, wup_ref[:, cols], preferred_element_type=F32)
        as_ref[SUBLANES:SUBLANES + T, cols] = a
        a1 = as_ref[SUBLANES - 1:SUBLANES - 1 + T, cols]
        a2 = as_ref[SUBLANES - 2:SUBLANES - 2 + T, cols]
        ac = fcw[0:1, cols] * a2 + fcw[1:2, cols] * a1 + fcw[2:3, cols] * a + fcb[:, cols]
        as_ref[0:SUBLANES, cols] = as_ref[T:T + SUBLANES, cols]
        val = jnp.dot(u, wup_ref[:, D_FF + cb * FFN_COL_BLOCK:D_FF + (cb + 1) * FFN_COL_BLOCK],
                      preferred_element_type=F32)
        gated = (ac * _sigmoid(ac) * val).astype(BF16)
        y = y + jnp.dot(gated, wdn_ref[cols, :], preferred_element_type=F32)
    out_ref[0] = _rms(y, finw_ref[...])


def _resident(shape):
    return pl.BlockSpec(shape, lambda b, t: (0,) * len(shape), pipeline_mode=pl.Buffered(1))


def _rsel_matrix():
    r = np.arange(N_HEADS * HEAD_DIM)[:, None] // HEAD_DIM
    l = np.arange(CHUNK)[None, :] % SUBLANES
    return jnp.asarray(r == l, dtype=BF16)


def kernel(x, meta_tokens, lb_param, attn_norm_w, w_in, hgrn_norm_w, conv_w, w_out, ffn_norm_w, w_up,
           ffn_conv_w, ffn_conv_b, w_down, final_norm_w):
    B, L, D = x.shape
    assert D == D_MODEL and L % MIXER_TILE == 0 and L % FFN_TILE == 0
    assert w_in.shape == (1, D, N_IN_SECTIONS * D) and w_up.shape == (1, D, 2 * D_FF)
    assert SUBLANES * HEAD_DIM == D_MODEL and N_HEADS == SUBLANES

    tile = lambda T: pl.BlockSpec((1, T, D), lambda b, t: (b, t, 0))
    meta_tile = pl.BlockSpec((1, N_META, D), lambda b, t: (b, 0, 0))
    params = pltpu.CompilerParams(dimension_semantics=("arbitrary", "arbitrary"),
                                  vmem_limit_bytes=VMEM_LIMIT_BYTES)

    TM = MIXER_TILE
    per_head = lambda a, n: a.reshape(n, N_HEADS, HEAD_DIM).transpose(1, 0, 2)
    w_in_heads = (w_in[0].reshape(D, N_IN_SECTIONS, N_HEADS, HEAD_DIM).transpose(2, 0, 1, 3)
                  .reshape(N_HEADS, D, N_IN_SECTIONS * HEAD_DIM).astype(BF16))
    slot = [pltpu.VMEM((TM, HEAD_DIM), F32)] * 3 + [pltpu.VMEM((TM, HEAD_DIM), BF16)] \
        + [pltpu.VMEM((TM, HEAD_DIM), F32)] * 3 + [pltpu.VMEM((TM + SUBLANES, HEAD_DIM), F32)]
    h1, h1m = pl.pallas_call(
        _mixer_kernel,
        grid=(B, L // TM),
        in_specs=[tile(TM), _resident((N_META, D)), _resident((N_HEADS, 2, HEAD_DIM)), _resident((1, D)),
                  _resident((N_HEADS, D, N_IN_SECTIONS * HEAD_DIM)), _resident((1, HEAD_DIM)),
                  _resident((N_HEADS, 3, HEAD_DIM)), _resident((D, D)),
                  _resident((N_HEADS * HEAD_DIM, CHUNK))],
        out_specs=[tile(TM), meta_tile],
        out_shape=[jax.ShapeDtypeStruct((B, L, D), F32), jax.ShapeDtypeStruct((B, N_META, D), F32)],
        scratch_shapes=[pltpu.VMEM((N_HEADS, HEAD_DIM, HEAD_DIM), F32),
                        pltpu.VMEM((N_HEADS, SUBLANES, HEAD_DIM), F32),
                        pltpu.VMEM((N_HEADS, TM, HEAD_DIM), BF16),
                        *slot, *slot],
        compiler_params=params,
        name="mixer",
    )(x, meta_tokens, per_head(lb_param, 2), attn_norm_w, w_in_heads, hgrn_norm_w, per_head(conv_w[0], 3),
      w_out[0].astype(BF16), _rsel_matrix())

    TF = FFN_TILE
    out = pl.pallas_call(
        _ffn_kernel,
        grid=(B, L // TF),
        in_specs=[tile(TF), meta_tile, _resident((1, D)), _resident((D, 2 * D_FF)), _resident((3, D_FF)),
                  _resident((1, D_FF)), _resident((D_FF, D)), _resident((1, D))],
        out_specs=tile(TF),
        out_shape=jax.ShapeDtypeStruct((B, L, D), F32),
        scratch_shapes=[pltpu.VMEM((TF + SUBLANES, D_FF), F32)],
        compiler_params=params,
        name="ffn",
    )(h1, h1m, ffn_norm_w, w_up[0].astype(BF16), ffn_conv_w[0], ffn_conv_b, w_down[0].astype(BF16),
      final_norm_w.reshape(1, D))
    return out
```

```python
import numpy as np
import jax
import jax.numpy as jnp
from jax import lax
from jax.experimental import pallas as pl
from jax.experimental.pallas import tpu as pltpu

D_MODEL = 1024
N_META = 16
N_HEADS = 8
HEAD_DIM = 128
D_FF = 2816
N_IN_SECTIONS = 9
EPS = 1e-6

SUBLANES = 8
CHUNK = 128
PAIR_WIDTH = 2 * HEAD_DIM
MAX_SAFE_EXPONENT = 60.0
MIXER_TILE = 256
FFN_TILE = 512
FFN_COL_BLOCK = 1408
VMEM_LIMIT_BYTES = 56 * 1024 * 1024

F32 = jnp.float32
BF16 = jnp.bfloat16

_NT = (((1,), (1,)), ((), ()))
_TN = (((0,), (0,)), ((), ()))


def _rms(x, w):
    ms = jnp.mean(x * x, axis=-1, keepdims=True)
    return x * lax.rsqrt(ms + EPS) * w


def _sigmoid(x):
    return 1.0 / (1.0 + jnp.exp(-x))


def _bcast_rows(ref, lanes, row0, block, offset, nrows):
    pieces = [jnp.broadcast_to(ref[pl.ds(row0 + i * block + offset, 1), lanes], (block, HEAD_DIM))
              for i in range(nrows // block)]
    return pieces[0] if len(pieces) == 1 else jnp.concatenate(pieces, axis=0)


def _hgrn_chunk_exact(row0, lanes, valid, st, q_ref, g_ref, k_ref, v_ref, rsel):
    rows = pl.ds(row0, CHUNK)
    q = q_ref[rows, lanes]
    G = g_ref[rows, lanes]
    k = k_ref[rows, lanes]
    v = v_ref[rows, lanes]
    row = lax.broadcasted_iota(jnp.int32, (CHUNK, HEAD_DIM), 0)

    sub = row & (SUBLANES - 1)
    ps = []
    for j in range(SUBLANES):
        gj = _bcast_rows(g_ref, lanes, row0, SUBLANES, j, CHUNK)
        kj = _bcast_rows(k_ref, lanes, row0, SUBLANES, j, CHUNK)
        p = q * kj * jnp.exp(G - gj)
        ps.append(jnp.where(sub >= j, p, 0.0).astype(BF16))
    pcat = jnp.concatenate(ps, axis=1)
    a = jnp.dot(pcat, rsel, preferred_element_type=F32)

    tl_xor = (lax.broadcasted_iota(jnp.int32, (CHUNK, CHUNK), 0)
              ^ lax.broadcasted_iota(jnp.int32, (CHUNK, CHUNK), 1))
    a = jnp.where(tl_xor < SUBLANES, a, 0.0)

    b = 2 * SUBLANES
    while b <= CHUNK:
        gm = _bcast_rows(g_ref, lanes, row0, b, b // 2 - 1, CHUNK)
        e = jnp.exp(-jnp.abs(G - gm))
        upper = (row & (b // 2)) != 0
        qt = jnp.where(upper, q * e, 0.0).astype(BF16)
        kt = jnp.where(upper, 0.0, k * e).astype(BF16)
        ab = lax.dot_general(qt, kt, _NT, preferred_element_type=F32)
        a = a + (ab if b == CHUNK else jnp.where(tl_xor < b, ab, 0.0))
        b *= 2

    qi = (q * jnp.exp(G)).astype(BF16)
    o = lax.dot_general(qi, st.astype(BF16), _NT, preferred_element_type=F32)
    o = o + jnp.dot(a.astype(BF16), v, preferred_element_type=F32)

    glast = g_ref[pl.ds(row0 + valid - 1, 1), lanes]
    kd = k * jnp.exp(glast - G)
    if valid < CHUNK:
        kd = jnp.where(row < valid, kd, 0.0)
    st = st * jnp.exp(glast) + lax.dot_general(v, kd.astype(BF16), _TN, preferred_element_type=F32)
    return o, st


def _hgrn_chunk_fast(row0, lanes, st, q_ref, g_ref, k_ref, v_ref):
    rows = pl.ds(row0, CHUNK)
    q = q_ref[rows, lanes]
    G = g_ref[rows, lanes]
    k = k_ref[rows, lanes]
    v = v_ref[rows, lanes]
    gmid = g_ref[pl.ds(row0 + CHUNK // 2 - 1, 1), lanes]
    glast = g_ref[pl.ds(row0 + CHUNK - 1, 1), lanes]
    d = G - gmid
    qh = q * jnp.exp(d)
    kh = k * jnp.exp(-d)
    a = lax.dot_general(qh.astype(BF16), kh.astype(BF16), _NT, preferred_element_type=F32)
    causal = (lax.broadcasted_iota(jnp.int32, (CHUNK, CHUNK), 1)
              <= lax.broadcasted_iota(jnp.int32, (CHUNK, CHUNK), 0))
    a = jnp.where(causal, a, 0.0)
    qi = (qh * jnp.exp(gmid)).astype(BF16)
    o = lax.dot_general(qi, st.astype(BF16), _NT, preferred_element_type=F32)
    o = o + jnp.dot(a.astype(BF16), v, preferred_element_type=F32)
    kd = (kh * jnp.exp(glast - gmid)).astype(BF16)
    st = st * jnp.exp(glast) + lax.dot_general(v, kd, _TN, preferred_element_type=F32)
    return o, st


def _project_pair(u, pair, R, valid, prm, slot):
    lbp_ref, win_ref, cw_ref, zc_ref, safe_ref = prm
    q_ref, g_ref, k_ref, v_ref, gs_ref, sa_ref, mb_ref, zs_ref = slot
    c0 = pair * PAIR_WIDTH
    cols = slice(c0, c0 + PAIR_WIDTH)

    def sec(i):
        w = win_ref[:, i * D_MODEL + c0:i * D_MODEL + c0 + PAIR_WIDTH]
        return jnp.dot(u, w, preferred_element_type=F32)

    qv = sec(0)
    q_ref[0:R, :] = qv * _sigmoid(qv)

    lbp = lbp_ref[:, cols]
    mx = jnp.max(lbp, axis=0, keepdims=True)
    ex = jnp.exp(lbp - mx)
    lb = ex[0:1, :] / jnp.sum(ex, axis=0, keepdims=True)
    f = lb + (1.0 - lb) * _sigmoid(sec(1))
    k_ref[0:R, :] = 1.0 - f
    lf = jnp.log(f)
    tri = (lax.broadcasted_iota(jnp.int32, (CHUNK, CHUNK), 1)
           <= lax.broadcasted_iota(jnp.int32, (CHUNK, CHUNK), 0)).astype(BF16)
    worst = jnp.zeros((1, PAIR_WIDTH), F32)
    for c in range(R // CHUNK):
        lfc = lf[c * CHUNK:(c + 1) * CHUNK, :]
        hi = lfc.astype(BF16)
        lo = (lfc - hi.astype(F32)).astype(BF16)
        gc = (jnp.dot(tri, hi, preferred_element_type=F32)
              + jnp.dot(tri, lo, preferred_element_type=F32))
        g_ref[c * CHUNK:(c + 1) * CHUNK, :] = gc
        gmid = gc[CHUNK // 2 - 1:CHUNK // 2, :]
        glast = gc[CHUNK - 1:CHUNK, :]
        worst = jnp.maximum(worst, jnp.maximum(-gmid, gmid - glast))
    safe_ref[pair] = (jnp.max(worst) <= MAX_SAFE_EXPONENT).astype(jnp.int32)

    v_ref[0:R, :] = sec(2).astype(BF16)
    gv = sec(3)
    gs_ref[0:R, :] = gv * _sigmoid(gv)

    bg = sec(4)
    z = sec(5) * sec(6)
    zs_ref[0:SUBLANES, :] = zc_ref[:, cols]
    zs_ref[SUBLANES:SUBLANES + R, :] = z
    z1 = zs_ref[SUBLANES - 1:SUBLANES - 1 + R, :]
    z2 = zs_ref[SUBLANES - 2:SUBLANES - 2 + R, :]
    cw = cw_ref[:, cols]
    yb = bg * (cw[0:1, :] * z2 + cw[1:2, :] * z1 + cw[2:3, :] * z)
    zc_ref[:, cols] = zs_ref[valid:valid + SUBLANES, :]

    sa_ref[0:R, :] = _sigmoid(sec(7))
    mb_ref[0:R, :] = _sigmoid(sec(8)) * yb


def _recur_pair(pair, R, valid, fast, hnw, rsel, st_ref, mg_ref, slot):
    q_ref, g_ref, k_ref, v_ref, gs_ref, sa_ref, mb_ref, _ = slot
    for hh in range(2):
        h = 2 * pair + hh
        lanes = slice(hh * HEAD_DIM, (hh + 1) * HEAD_DIM)
        st = st_ref[h]
        for c in range(R // CHUNK):
            row0 = c * CHUNK
            if fast:
                o, st = _hgrn_chunk_fast(row0, lanes, st, q_ref, g_ref, k_ref, v_ref)
            else:
                o, st = _hgrn_chunk_exact(row0, lanes, min(valid - row0, CHUNK), st,
                                          q_ref, g_ref, k_ref, v_ref, rsel)
            rows = pl.ds(row0, CHUNK)
            m = _rms(o, hnw) * gs_ref[rows, lanes] * sa_ref[rows, lanes] + mb_ref[rows, lanes]
            mg_ref[rows, h * HEAD_DIM:(h + 1) * HEAD_DIM] = m.astype(BF16)
        st_ref[h] = st


def _mixer_tile(x, R, valid, is_meta, refs):
    (lbp_ref, anw_ref, win_ref, hnw_ref, cw_ref, wout_ref, rsel_ref,
     st_ref, zc_ref, mg_ref, safe_ref, slot_a, slot_b) = refs
    prm = (lbp_ref, win_ref, cw_ref, zc_ref, safe_ref)
    u = _rms(x, anw_ref[...]).astype(BF16)
    rsel = rsel_ref[...]
    hnw = hnw_ref[...]
    project = lambda pair, slot: _project_pair(u, pair, R, valid, prm, slot)

    def recur(pair, slot):
        run = lambda fast: _recur_pair(pair, R, valid, fast, hnw, rsel, st_ref, mg_ref, slot)
        if is_meta:
            run(False)
        else:
            safe = safe_ref[pair] != 0
            pl.when(safe)(lambda: run(True))
            pl.when(jnp.logical_not(safe))(lambda: run(False))

    n_pairs = N_HEADS // 2
    slots = (slot_a, slot_b)
    project(0, slot_a)
    for pair in range(n_pairs):
        if pair + 1 < n_pairs:
            project(pair + 1, slots[(pair + 1) % 2])
        recur(pair, slots[pair % 2])

    return x + jnp.dot(mg_ref[0:R, :], wout_ref[...], preferred_element_type=F32)


def _mixer_kernel(x_ref, meta_ref, lbp_ref, anw_ref, win_ref, hnw_ref, cw_ref, wout_ref, rsel_ref,
                  h1_ref, h1m_ref, st_ref, zc_ref, mg_ref, safe_ref, *slots):
    n = len(slots) // 2
    refs = (lbp_ref, anw_ref, win_ref, hnw_ref, cw_ref, wout_ref, rsel_ref,
            st_ref, zc_ref, mg_ref, safe_ref, slots[:n], slots[n:])

    @pl.when(pl.program_id(1) == 0)
    def _():
        st_ref[...] = jnp.zeros_like(st_ref)
        zc_ref[...] = jnp.zeros_like(zc_ref)
        xm = jnp.concatenate([meta_ref[...], jnp.zeros((CHUNK - N_META, D_MODEL), F32)], axis=0)
        hm = _mixer_tile(xm, CHUNK, N_META, True, refs)
        h1m_ref[0] = hm[0:N_META, :]

    h1_ref[0] = _mixer_tile(x_ref[0], MIXER_TILE, MIXER_TILE, False, refs)


def _ffn_kernel(h1_ref, h1m_ref, fnw_ref, wup_ref, fcw_ref, fcb_ref, wdn_ref, finw_ref,
                out_ref, as_ref):
    T = FFN_TILE
    fnw = fnw_ref[...]

    @pl.when(pl.program_id(1) == 0)
    def _():
        um = _rms(h1m_ref[0], fnw).astype(BF16)
        am = jnp.dot(um, wup_ref[:, 0:D_FF], preferred_element_type=F32)
        as_ref[0:SUBLANES, :] = am[N_META - SUBLANES:N_META, :]

    x = h1_ref[0]
    u = _rms(x, fnw).astype(BF16)
    fcw = fcw_ref[...]
    fcb = fcb_ref[...]
    y = x
    for cb in range(D_FF // FFN_COL_BLOCK):
        cols = slice(cb * FFN_COL_BLOCK, (cb + 1) * FFN_COL_BLOCK)
        a = jnp.dot(u, wup_ref[:, cols], preferred_element_type=F32)
        as_ref[SUBLANES:SUBLANES + T, cols] = a
        a1 = as_ref[SUBLANES - 1:SUBLANES - 1 + T, cols]
        a2 = as_ref[SUBLANES - 2:SUBLANES - 2 + T, cols]
        ac = fcw[0:1, cols] * a2 + fcw[1:2, cols] * a1 + fcw[2:3, cols] * a + fcb[:, cols]
        as_ref[0:SUBLANES, cols] = as_ref[T:T + SUBLANES, cols]
        val = jnp.dot(u, wup_ref[:, D_FF + cb * FFN_COL_BLOCK:D_FF + (cb + 1) * FFN_COL_BLOCK],
                      preferred_element_type=F32)
        gated = (ac * _sigmoid(ac) * val).astype(BF16)
        y = y + jnp.dot(gated, wdn_ref[cols, :], preferred_element_type=F32)
    out_ref[0] = _rms(y, finw_ref[...])


def _resident(shape):
    return pl.BlockSpec(shape, lambda b, t: (0,) * len(shape), pipeline_mode=pl.Buffered(1))


def _rsel_matrix():
    r = np.arange(N_HEADS * HEAD_DIM)[:, None] // HEAD_DIM
    l = np.arange(CHUNK)[None, :] % SUBLANES
    return jnp.asarray(r == l, dtype=BF16)


def kernel(x, meta_tokens, lb_param, attn_norm_w, w_in, hgrn_norm_w, conv_w, w_out, ffn_norm_w, w_up,
           ffn_conv_w, ffn_conv_b, w_down, final_norm_w):
    B, L, D = x.shape
    assert D == D_MODEL and L % MIXER_TILE == 0 and L % FFN_TILE == 0
    assert w_in.shape == (1, D, N_IN_SECTIONS * D) and w_up.shape == (1, D, 2 * D_FF)
    assert SUBLANES * HEAD_DIM == D_MODEL and N_HEADS == SUBLANES

    tile = lambda T: pl.BlockSpec((1, T, D), lambda b, t: (b, t, 0))
    meta_tile = pl.BlockSpec((1, N_META, D), lambda b, t: (b, 0, 0))
    params = pltpu.CompilerParams(dimension_semantics=("arbitrary", "arbitrary"),
                                  vmem_limit_bytes=VMEM_LIMIT_BYTES)

    TM = MIXER_TILE
    slot = [pltpu.VMEM((TM, PAIR_WIDTH), F32)] * 3 + [pltpu.VMEM((TM, PAIR_WIDTH), BF16)] \
        + [pltpu.VMEM((TM, PAIR_WIDTH), F32)] * 3 + [pltpu.VMEM((TM + SUBLANES, PAIR_WIDTH), F32)]
    h1, h1m = pl.pallas_call(
        _mixer_kernel,
        grid=(B, L // TM),
        in_specs=[tile(TM), _resident((N_META, D)), _resident((2, D)), _resident((1, D)),
                  _resident((D, N_IN_SECTIONS * D)), _resident((1, HEAD_DIM)), _resident((3, D)),
                  _resident((D, D)), _resident((N_HEADS * HEAD_DIM, CHUNK))],
        out_specs=[tile(TM), meta_tile],
        out_shape=[jax.ShapeDtypeStruct((B, L, D), F32), jax.ShapeDtypeStruct((B, N_META, D), F32)],
        scratch_shapes=[pltpu.VMEM((N_HEADS, HEAD_DIM, HEAD_DIM), F32),
                        pltpu.VMEM((SUBLANES, D), F32),
                        pltpu.VMEM((TM, D), BF16),
                        pltpu.SMEM((N_HEADS // 2,), jnp.int32),
                        *slot, *slot],
        compiler_params=params,
        name="mixer",
    )(x, meta_tokens, lb_param, attn_norm_w, w_in[0].astype(BF16), hgrn_norm_w, conv_w[0],
      w_out[0].astype(BF16), _rsel_matrix())

    TF = FFN_TILE
    out = pl.pallas_call(
        _ffn_kernel,
        grid=(B, L // TF),
        in_specs=[tile(TF), meta_tile, _resident((1, D)), _resident((D, 2 * D_FF)), _resident((3, D_FF)),
                  _resident((1, D_FF)), _resident((D_FF, D)), _resident((1, D))],
        out_specs=tile(TF),
        out_shape=jax.ShapeDtypeStruct((B, L, D), F32),
        scratch_shapes=[pltpu.VMEM((TF + SUBLANES, D_FF), F32)],
        compiler_params=params,
        name="ffn",
    )(h1, h1m, ffn_norm_w, w_up[0].astype(BF16), ffn_conv_w[0], ffn_conv_b, w_down[0].astype(BF16),
      final_norm_w.reshape(1, D))
    return out
```

```python
import numpy as np
import jax
import jax.numpy as jnp
from jax import lax
from jax.experimental import pallas as pl
from jax.experimental.pallas import tpu as pltpu

D_MODEL = 1024
N_META = 16
N_HEADS = 8
HEAD_DIM = 128
D_FF = 2816
N_IN_SECTIONS = 9
EPS = 1e-6

SUBLANES = 8
CHUNK = 128
PAIR_WIDTH = 2 * HEAD_DIM
MAX_SAFE_EXPONENT = 60.0
MIXER_TILE = 512
FFN_TILE = 512
FFN_COL_BLOCK = 1408
VMEM_LIMIT_BYTES = 56 * 1024 * 1024

F32 = jnp.float32
BF16 = jnp.bfloat16

_NT = (((1,), (1,)), ((), ()))
_TN = (((0,), (0,)), ((), ()))


def _rms(x, w):
    ms = jnp.mean(x * x, axis=-1, keepdims=True)
    return x * lax.rsqrt(ms + EPS) * w


def _sigmoid(x):
    return 1.0 / (1.0 + jnp.exp(-x))


def _bcast_rows(ref, lanes, row0, block, offset, nrows):
    pieces = [jnp.broadcast_to(_row(ref, lanes, row0, i * block + offset), (block, HEAD_DIM))
              for i in range(nrows // block)]
    return pieces[0] if len(pieces) == 1 else jnp.concatenate(pieces, axis=0)


def _row(ref, lanes, row0, r):
    group = ref[pl.ds(row0 + r // SUBLANES * SUBLANES, SUBLANES), lanes]
    return group[r % SUBLANES:r % SUBLANES + 1, :]


def _hgrn_chunk_exact(row0, lanes, valid, st, q_ref, g_ref, k_ref, v_ref, rsel):
    rows = pl.ds(row0, CHUNK)
    q = q_ref[rows, lanes]
    G = g_ref[rows, lanes]
    k = k_ref[rows, lanes]
    v = v_ref[rows, lanes]
    row = lax.broadcasted_iota(jnp.int32, (CHUNK, HEAD_DIM), 0)

    sub = row & (SUBLANES - 1)
    ps = []
    for j in range(SUBLANES):
        gj = _bcast_rows(g_ref, lanes, row0, SUBLANES, j, CHUNK)
        kj = _bcast_rows(k_ref, lanes, row0, SUBLANES, j, CHUNK)
        p = q * kj * jnp.exp(G - gj)
        ps.append(jnp.where(sub >= j, p, 0.0).astype(BF16))
    pcat = jnp.concatenate(ps, axis=1)
    a = jnp.dot(pcat, rsel, preferred_element_type=F32)

    tl_xor = (lax.broadcasted_iota(jnp.int32, (CHUNK, CHUNK), 0)
              ^ lax.broadcasted_iota(jnp.int32, (CHUNK, CHUNK), 1))
    a = jnp.where(tl_xor < SUBLANES, a, 0.0)

    b = 2 * SUBLANES
    while b <= CHUNK:
        gm = _bcast_rows(g_ref, lanes, row0, b, b // 2 - 1, CHUNK)
        e = jnp.exp(-jnp.abs(G - gm))
        upper = (row & (b // 2)) != 0
        qt = jnp.where(upper, q * e, 0.0).astype(BF16)
        kt = jnp.where(upper, 0.0, k * e).astype(BF16)
        ab = lax.dot_general(qt, kt, _NT, preferred_element_type=F32)
        a = a + (ab if b == CHUNK else jnp.where(tl_xor < b, ab, 0.0))
        b *= 2

    qi = (q * jnp.exp(G)).astype(BF16)
    o = lax.dot_general(qi, st.astype(BF16), _NT, preferred_element_type=F32)
    o = o + jnp.dot(a.astype(BF16), v, preferred_element_type=F32)

    glast = _row(g_ref, lanes, row0, valid - 1)
    kd = k * jnp.exp(glast - G)
    if valid < CHUNK:
        kd = jnp.where(row < valid, kd, 0.0)
    st = st * jnp.exp(glast) + lax.dot_general(v, kd.astype(BF16), _TN, preferred_element_type=F32)
    return o, st


def _hgrn_chunk_fast(row0, lanes, st, q_ref, g_ref, k_ref, v_ref):
    rows = pl.ds(row0, CHUNK)
    q = q_ref[rows, lanes]
    G = g_ref[rows, lanes]
    k = k_ref[rows, lanes]
    v = v_ref[rows, lanes]
    gmid = g_ref[pl.ds(row0 + CHUNK // 2 - 1, 1), lanes]
    glast = g_ref[pl.ds(row0 + CHUNK - 1, 1), lanes]
    d = G - gmid
    qh = q * jnp.exp(d)
    kh = k * jnp.exp(-d)
    a = lax.dot_general(qh.astype(BF16), kh.astype(BF16), _NT, preferred_element_type=F32)
    causal = (lax.broadcasted_iota(jnp.int32, (CHUNK, CHUNK), 1)
              <= lax.broadcasted_iota(jnp.int32, (CHUNK, CHUNK), 0))
    a = jnp.where(causal, a, 0.0)
    qi = (qh * jnp.exp(gmid)).astype(BF16)
    o = lax.dot_general(qi, st.astype(BF16), _NT, preferred_element_type=F32)
    o = o + jnp.dot(a.astype(BF16), v, preferred_element_type=F32)
    kd = (kh * jnp.exp(glast - gmid)).astype(BF16)
    st = st * jnp.exp(glast) + lax.dot_general(v, kd, _TN, preferred_element_type=F32)
    return o, st


def _project_pair(u, pair, R, valid, prm, slot):
    lbp_ref, win_ref, cw_ref, zc_ref, safe_ref = prm
    q_ref, g_ref, k_ref, v_ref, gs_ref, sa_ref, mb_ref, zs_ref = slot
    c0 = pair * PAIR_WIDTH
    cols = slice(c0, c0 + PAIR_WIDTH)

    def sec(i):
        w = win_ref[:, i * D_MODEL + c0:i * D_MODEL + c0 + PAIR_WIDTH]
        return jnp.dot(u, w, preferred_element_type=F32)

    qv = sec(0)
    q_ref[0:R, :] = qv * _sigmoid(qv)

    lbp = lbp_ref[:, cols]
    mx = jnp.max(lbp, axis=0, keepdims=True)
    ex = jnp.exp(lbp - mx)
    lb = ex[0:1, :] / jnp.sum(ex, axis=0, keepdims=True)
    f = lb + (1.0 - lb) * _sigmoid(sec(1))
    k_ref[0:R, :] = 1.0 - f
    lf = jnp.log(f)
    tri = (lax.broadcasted_iota(jnp.int32, (CHUNK, CHUNK), 1)
           <= lax.broadcasted_iota(jnp.int32, (CHUNK, CHUNK), 0)).astype(BF16)
    worst = jnp.zeros((1, PAIR_WIDTH), F32)
    for c in range(R // CHUNK):
        lfc = lf[c * CHUNK:(c + 1) * CHUNK, :]
        hi = lfc.astype(BF16)
        lo = (lfc - hi.astype(F32)).astype(BF16)
        gc = (jnp.dot(tri, hi, preferred_element_type=F32)
              + jnp.dot(tri, lo, preferred_element_type=F32))
        g_ref[c * CHUNK:(c + 1) * CHUNK, :] = gc
        gmid = gc[CHUNK // 2 - 1:CHUNK // 2, :]
        glast = gc[CHUNK - 1:CHUNK, :]
        worst = jnp.maximum(worst, jnp.maximum(-gmid, gmid - glast))
    safe_ref[pair] = (jnp.max(worst) <= MAX_SAFE_EXPONENT).astype(jnp.int32)

    v_ref[0:R, :] = sec(2).astype(BF16)
    gv = sec(3)
    gs_ref[0:R, :] = gv * _sigmoid(gv)

    bg = sec(4)
    z = sec(5) * sec(6)
    zs_ref[0:SUBLANES, :] = zc_ref[:, cols]
    zs_ref[SUBLANES:SUBLANES + R, :] = z
    z1 = zs_ref[SUBLANES - 1:SUBLANES - 1 + R, :]
    z2 = zs_ref[SUBLANES - 2:SUBLANES - 2 + R, :]
    cw = cw_ref[:, cols]
    yb = bg * (cw[0:1, :] * z2 + cw[1:2, :] * z1 + cw[2:3, :] * z)
    zc_ref[:, cols] = zs_ref[valid:valid + SUBLANES, :]

    sa_ref[0:R, :] = _sigmoid(sec(7))
    mb_ref[0:R, :] = _sigmoid(sec(8)) * yb


def _recur_pair(pair, R, valid, fast, hnw, rsel, st_in_ref, st_ref, st_old_ref, mg_ref, slot):
    q_ref, g_ref, k_ref, v_ref, gs_ref, sa_ref, mb_ref, _ = slot
    for hh in range(2):
        h = 2 * pair + hh
        lanes = slice(hh * HEAD_DIM, (hh + 1) * HEAD_DIM)

        def finish(row0, o):
            rows = pl.ds(row0, CHUNK)
            m = _rms(o, hnw) * gs_ref[rows, lanes] * sa_ref[rows, lanes] + mb_ref[rows, lanes]
            mg_ref[rows, h * HEAD_DIM:(h + 1) * HEAD_DIM] = m.astype(BF16)

        st = st_in_ref[h]
        if fast:
            st_old_ref[h] = st
            for c in range(R // CHUNK):
                o, st = _hgrn_chunk_fast(c * CHUNK, lanes, st, q_ref, g_ref, k_ref, v_ref)
                finish(c * CHUNK, o)
        elif valid < R:
            o, st = _hgrn_chunk_exact(0, lanes, valid, st, q_ref, g_ref, k_ref, v_ref, rsel)
            finish(0, o)
        else:
            def chunk_body(c, st):
                row0 = pl.multiple_of(c * CHUNK, CHUNK)
                o, st = _hgrn_chunk_exact(row0, lanes, CHUNK, st, q_ref, g_ref, k_ref, v_ref, rsel)
                finish(row0, o)
                return st

            st = lax.fori_loop(0, R // CHUNK, chunk_body, st)
        st_ref[h] = st


def _mixer_tile(x, R, valid, is_meta, refs):
    (lbp_ref, anw_ref, win_ref, hnw_ref, cw_ref, wout_ref, rsel_ref,
     st_ref, st_old_ref, zc_ref, mg_ref, safe_ref, slot_a, slot_b) = refs
    prm = (lbp_ref, win_ref, cw_ref, zc_ref, safe_ref)
    u = _rms(x, anw_ref[...]).astype(BF16)
    rsel = rsel_ref[...]
    hnw = hnw_ref[...]
    project = lambda pair, slot: _project_pair(u, pair, R, valid, prm, slot)

    def recur(pair, slot):
        run = lambda fast, st_in_ref: _recur_pair(pair, R, valid, fast, hnw, rsel, st_in_ref, st_ref,
                                                  st_old_ref, mg_ref, slot)
        if is_meta:
            run(False, st_ref)
        else:
            run(True, st_ref)
            pl.when(safe_ref[pair] == 0)(lambda: run(False, st_old_ref))

    n_pairs = N_HEADS // 2
    slots = (slot_a, slot_b)
    project(0, slot_a)
    for pair in range(n_pairs):
        if pair + 1 < n_pairs:
            project(pair + 1, slots[(pair + 1) % 2])
        recur(pair, slots[pair % 2])

    return x + jnp.dot(mg_ref[0:R, :], wout_ref[...], preferred_element_type=F32)


def _mixer_kernel(x_ref, meta_ref, lbp_ref, anw_ref, win_ref, hnw_ref, cw_ref, wout_ref, rsel_ref,
                  h1_ref, h1m_ref, st_ref, st_old_ref, zc_ref, mg_ref, safe_ref, *slots):
    n = len(slots) // 2
    refs = (lbp_ref, anw_ref, win_ref, hnw_ref, cw_ref, wout_ref, rsel_ref,
            st_ref, st_old_ref, zc_ref, mg_ref, safe_ref, slots[:n], slots[n:])

    @pl.when(pl.program_id(1) == 0)
    def _():
        st_ref[...] = jnp.zeros_like(st_ref)
        zc_ref[...] = jnp.zeros_like(zc_ref)
        xm = jnp.concatenate([meta_ref[...], jnp.zeros((CHUNK - N_META, D_MODEL), F32)], axis=0)
        hm = _mixer_tile(xm, CHUNK, N_META, True, refs)
        h1m_ref[0] = hm[0:N_META, :]

    h1_ref[0] = _mixer_tile(x_ref[0], MIXER_TILE, MIXER_TILE, False, refs)


def _ffn_kernel(h1_ref, h1m_ref, fnw_ref, wup_ref, fcw_ref, fcb_ref, wdn_ref, finw_ref,
                out_ref, as_ref):
    T = FFN_TILE
    fnw = fnw_ref[...]

    @pl.when(pl.program_id(1) == 0)
    def _():
        um = _rms(h1m_ref[0], fnw).astype(BF16)
        am = jnp.dot(um, wup_ref[:, 0:D_FF], preferred_element_type=F32)
        as_ref[0:SUBLANES, :] = am[N_META - SUBLANES:N_META, :]

    x = h1_ref[0]
    u = _rms(x, fnw).astype(BF16)
    fcw = fcw_ref[...]
    fcb = fcb_ref[...]
    y = x
    for cb in range(D_FF // FFN_COL_BLOCK):
        cols = slice(cb * FFN_COL_BLOCK, (cb + 1) * FFN_COL_BLOCK)
        a = jnp.dot(u, wup_ref[:, cols], preferred_element_type=F32)
        as_ref[SUBLANES:SUBLANES + T, cols] = a
        a1 = as_ref[SUBLANES - 1:SUBLANES - 1 + T, cols]
        a2 = as_ref[SUBLANES - 2:SUBLANES - 2 + T, cols]
        ac = fcw[0:1, cols] * a2 + fcw[1:2, cols] * a1 + fcw[2:3, cols] * a + fcb[:, cols]
        as_ref[0:SUBLANES, cols] = as_ref[T:T + SUBLANES, cols]
        val = jnp.dot(u, wup_ref[:, D_FF + cb * FFN_COL_BLOCK:D_FF + (cb + 1) * FFN_COL_BLOCK],
                      preferred_element_type=F32)
        gated = (ac * _sigmoid(ac) * val).astype(BF16)
        y = y + jnp.dot(gated, wdn_ref[cols, :], preferred_element_type=F32)
    out_ref[0] = _rms(y, finw_ref[...])


def _resident(shape):
    return pl.BlockSpec(shape, lambda b, t: (0,) * len(shape), pipeline_mode=pl.Buffered(1))


def _rsel_matrix():
    r = np.arange(N_HEADS * HEAD_DIM)[:, None] // HEAD_DIM
    l = np.arange(CHUNK)[None, :] % SUBLANES
    return jnp.asarray(r == l, dtype=BF16)


def kernel(x, meta_tokens, lb_param, attn_norm_w, w_in, hgrn_norm_w, conv_w, w_out, ffn_norm_w, w_up,
           ffn_conv_w, ffn_conv_b, w_down, final_norm_w):
    B, L, D = x.shape
    assert D == D_MODEL and L % MIXER_TILE == 0 and L % FFN_TILE == 0
    assert w_in.shape == (1, D, N_IN_SECTIONS * D) and w_up.shape == (1, D, 2 * D_FF)
    assert SUBLANES * HEAD_DIM == D_MODEL and N_HEADS == SUBLANES

    tile = lambda T: pl.BlockSpec((1, T, D), lambda b, t: (b, t, 0))
    meta_tile = pl.BlockSpec((1, N_META, D), lambda b, t: (b, 0, 0))
    params = pltpu.CompilerParams(dimension_semantics=("arbitrary", "arbitrary"),
                                  vmem_limit_bytes=VMEM_LIMIT_BYTES)

    TM = MIXER_TILE
    slot = [pltpu.VMEM((TM, PAIR_WIDTH), F32)] * 3 + [pltpu.VMEM((TM, PAIR_WIDTH), BF16)] \
        + [pltpu.VMEM((TM, PAIR_WIDTH), F32)] * 3 + [pltpu.VMEM((TM + SUBLANES, PAIR_WIDTH), F32)]
    h1, h1m = pl.pallas_call(
        _mixer_kernel,
        grid=(B, L // TM),
        in_specs=[tile(TM), _resident((N_META, D)), _resident((2, D)), _resident((1, D)),
                  _resident((D, N_IN_SECTIONS * D)), _resident((1, HEAD_DIM)), _resident((3, D)),
                  _resident((D, D)), _resident((N_HEADS * HEAD_DIM, CHUNK))],
        out_specs=[tile(TM), meta_tile],
        out_shape=[jax.ShapeDtypeStruct((B, L, D), F32), jax.ShapeDtypeStruct((B, N_META, D), F32)],
        scratch_shapes=[pltpu.VMEM((N_HEADS, HEAD_DIM, HEAD_DIM), F32),
                        pltpu.VMEM((N_HEADS, HEAD_DIM, HEAD_DIM), F32),
                        pltpu.VMEM((SUBLANES, D), F32),
                        pltpu.VMEM((TM, D), BF16),
                        pltpu.SMEM((N_HEADS // 2,), jnp.int32),
                        *slot, *slot],
        compiler_params=params,
        name="mixer",
    )(x, meta_tokens, lb_param, attn_norm_w, w_in[0].astype(BF16), hgrn_norm_w, conv_w[0],
      w_out[0].astype(BF16), _rsel_matrix())

    TF = FFN_TILE
    out = pl.pallas_call(
        _ffn_kernel,
        grid=(B, L // TF),
        in_specs=[tile(TF), meta_tile, _resident((1, D)), _resident((D, 2 * D_FF)), _resident((3, D_FF)),
                  _resident((1, D_FF)), _resident((D_FF, D)), _resident((1, D))],
        out_specs=tile(TF),
        out_shape=jax.ShapeDtypeStruct((B, L, D), F32),
        scratch_shapes=[pltpu.VMEM((TF + SUBLANES, D_FF), F32)],
        compiler_params=params,
        name="ffn",
    )(h1, h1m, ffn_norm_w, w_up[0].astype(BF16), ffn_conv_w[0], ffn_conv_b, w_down[0].astype(BF16),
      final_norm_w.reshape(1, D))
    return out
```

```python
import numpy as np
import jax
import jax.numpy as jnp
from jax import lax
from jax.experimental import pallas as pl
from jax.experimental.pallas import tpu as pltpu

D_MODEL = 1024
N_META = 16
N_HEADS = 8
HEAD_DIM = 128
D_FF = 2816
N_IN_SECTIONS = 9
EPS = 1e-6

SUBLANES = 8
CHUNK = 128
GROUP_HEADS = 4
GROUP_WIDTH = GROUP_HEADS * HEAD_DIM
N_GROUPS = N_HEADS // GROUP_HEADS
MAX_SAFE_EXPONENT = 60.0
MIXER_TILE = 256
FFN_TILE = 512
FFN_COL_BLOCK = 1408
WEIGHT_LANE_PAD = 128
VMEM_LIMIT_BYTES = 56 * 1024 * 1024

F32 = jnp.float32
BF16 = jnp.bfloat16

_NT = (((1,), (1,)), ((), ()))
_TN = (((0,), (0,)), ((), ()))


def _rms(x, w):
    ms = jnp.mean(x * x, axis=-1, keepdims=True)
    return x * lax.rsqrt(ms + EPS) * w


def _sigmoid(x):
    return 1.0 / (1.0 + jnp.exp(-x))


def _bcast_rows(ref, lanes, row0, block, offset, nrows):
    pieces = [jnp.broadcast_to(_row(ref, lanes, row0, i * block + offset), (block, HEAD_DIM))
              for i in range(nrows // block)]
    return pieces[0] if len(pieces) == 1 else jnp.concatenate(pieces, axis=0)


def _row(ref, lanes, row0, r):
    group = ref[pl.ds(row0 + r // SUBLANES * SUBLANES, SUBLANES), lanes]
    return group[r % SUBLANES:r % SUBLANES + 1, :]


def _hgrn_chunk_exact(row0, lanes, valid, st, q_ref, g_ref, k_ref, v_ref, rsel):
    rows = pl.ds(row0, CHUNK)
    q = q_ref[rows, lanes]
    G = g_ref[rows, lanes]
    k = k_ref[rows, lanes]
    v = v_ref[rows, lanes]
    row = lax.broadcasted_iota(jnp.int32, (CHUNK, HEAD_DIM), 0)

    sub = row & (SUBLANES - 1)
    ps = []
    for j in range(SUBLANES):
        gj = _bcast_rows(g_ref, lanes, row0, SUBLANES, j, CHUNK)
        kj = _bcast_rows(k_ref, lanes, row0, SUBLANES, j, CHUNK)
        p = q * kj * jnp.exp(G - gj)
        ps.append(jnp.where(sub >= j, p, 0.0).astype(BF16))
    pcat = jnp.concatenate(ps, axis=1)
    a = jnp.dot(pcat, rsel, preferred_element_type=F32)

    tl_xor = (lax.broadcasted_iota(jnp.int32, (CHUNK, CHUNK), 0)
              ^ lax.broadcasted_iota(jnp.int32, (CHUNK, CHUNK), 1))
    a = jnp.where(tl_xor < SUBLANES, a, 0.0)

    b = 2 * SUBLANES
    while b <= CHUNK:
        gm = _bcast_rows(g_ref, lanes, row0, b, b // 2 - 1, CHUNK)
        e = jnp.exp(-jnp.abs(G - gm))
        upper = (row & (b // 2)) != 0
        qt = jnp.where(upper, q * e, 0.0).astype(BF16)
        kt = jnp.where(upper, 0.0, k * e).astype(BF16)
        ab = lax.dot_general(qt, kt, _NT, preferred_element_type=F32)
        a = a + (ab if b == CHUNK else jnp.where(tl_xor < b, ab, 0.0))
        b *= 2

    qi = (q * jnp.exp(G)).astype(BF16)
    o = lax.dot_general(qi, st.astype(BF16), _NT, preferred_element_type=F32)
    o = o + jnp.dot(a.astype(BF16), v, preferred_element_type=F32)

    glast = _row(g_ref, lanes, row0, valid - 1)
    kd = k * jnp.exp(glast - G)
    if valid < CHUNK:
        kd = jnp.where(row < valid, kd, 0.0)
    st = st * jnp.exp(glast) + lax.dot_general(v, kd.astype(BF16), _TN, preferred_element_type=F32)
    return o, st


def _hgrn_chunk_fast(row0, lanes, st, q_ref, g_ref, k_ref, v_ref):
    rows = pl.ds(row0, CHUNK)
    q = q_ref[rows, lanes]
    G = g_ref[rows, lanes]
    k = k_ref[rows, lanes]
    v = v_ref[rows, lanes]
    gmid = g_ref[pl.ds(row0 + CHUNK // 2 - 1, 1), lanes]
    glast = g_ref[pl.ds(row0 + CHUNK - 1, 1), lanes]
    d = G - gmid
    qh = q * jnp.exp(d)
    kh = k * jnp.exp(-d)
    a = lax.dot_general(qh.astype(BF16), kh.astype(BF16), _NT, preferred_element_type=F32)
    causal = (lax.broadcasted_iota(jnp.int32, (CHUNK, CHUNK), 1)
              <= lax.broadcasted_iota(jnp.int32, (CHUNK, CHUNK), 0))
    a = jnp.where(causal, a, 0.0)
    qi = (qh * jnp.exp(gmid)).astype(BF16)
    o = lax.dot_general(qi, st.astype(BF16), _NT, preferred_element_type=F32)
    o = o + jnp.dot(a.astype(BF16), v, preferred_element_type=F32)
    kd = (kh * jnp.exp(glast - gmid)).astype(BF16)
    st = st * jnp.exp(glast) + lax.dot_general(v, kd, _TN, preferred_element_type=F32)
    return o, st


def _project_pair(u, pair, R, valid, prm, slot):
    lbp_ref, win_ref, cw_ref, zc_ref, safe_ref = prm
    q_ref, g_ref, k_ref, v_ref, gs_ref, sa_ref, mb_ref, zs_ref = slot
    c0 = pair * GROUP_WIDTH
    cols = slice(c0, c0 + GROUP_WIDTH)

    def sec(i):
        w = win_ref[:, i * D_MODEL + c0:i * D_MODEL + c0 + GROUP_WIDTH]
        return jnp.dot(u, w, preferred_element_type=F32)

    qv = sec(0)
    q_ref[0:R, :] = qv * _sigmoid(qv)
    yield

    lbp = lbp_ref[:, cols]
    mx = jnp.max(lbp, axis=0, keepdims=True)
    ex = jnp.exp(lbp - mx)
    lb = ex[0:1, :] / jnp.sum(ex, axis=0, keepdims=True)
    f = lb + (1.0 - lb) * _sigmoid(sec(1))
    k_ref[0:R, :] = 1.0 - f
    lf = jnp.log(f)
    tri = (lax.broadcasted_iota(jnp.int32, (CHUNK, CHUNK), 1)
           <= lax.broadcasted_iota(jnp.int32, (CHUNK, CHUNK), 0)).astype(BF16)
    worst = jnp.zeros((1, GROUP_WIDTH), F32)
    for c in range(R // CHUNK):
        lfc = lf[c * CHUNK:(c + 1) * CHUNK, :]
        hi = lfc.astype(BF16)
        lo = (lfc - hi.astype(F32)).astype(BF16)
        gc = (jnp.dot(tri, hi, preferred_element_type=F32)
              + jnp.dot(tri, lo, preferred_element_type=F32))
        g_ref[c * CHUNK:(c + 1) * CHUNK, :] = gc
        gmid = gc[CHUNK // 2 - 1:CHUNK // 2, :]
        glast = gc[CHUNK - 1:CHUNK, :]
        worst = jnp.maximum(worst, jnp.maximum(-gmid, gmid - glast))
    safe_ref[pair] = (jnp.max(worst) <= MAX_SAFE_EXPONENT).astype(jnp.int32)
    yield

    v_ref[0:R, :] = sec(2).astype(BF16)
    yield
    gv = sec(3)
    gs_ref[0:R, :] = gv * _sigmoid(gv)
    yield

    bg = sec(4)
    z = sec(5) * sec(6)
    zs_ref[0:SUBLANES, :] = zc_ref[:, cols]
    zs_ref[SUBLANES:SUBLANES + R, :] = z
    z1 = zs_ref[SUBLANES - 1:SUBLANES - 1 + R, :]
    z2 = zs_ref[SUBLANES - 2:SUBLANES - 2 + R, :]
    cw = cw_ref[:, cols]
    yb = bg * (cw[0:1, :] * z2 + cw[1:2, :] * z1 + cw[2:3, :] * z)
    zc_ref[:, cols] = zs_ref[valid:valid + SUBLANES, :]
    yield

    sa_ref[0:R, :] = _sigmoid(sec(7))
    yield
    mb_ref[0:R, :] = _sigmoid(sec(8)) * yb
    yield


def _finish_chunk(row0, h, lanes, o, hnw, mg_ref, slot):
    gs_ref, sa_ref, mb_ref = slot[4:7]
    rows = pl.ds(row0, CHUNK)
    m = _rms(o, hnw) * gs_ref[rows, lanes] * sa_ref[rows, lanes] + mb_ref[rows, lanes]
    mg_ref[rows, h * HEAD_DIM:(h + 1) * HEAD_DIM] = m.astype(BF16)


def _recur_pair_fast(pair, R, hnw, st_ref, st_old_ref, mg_ref, slot):
    q_ref, g_ref, k_ref, v_ref = slot[0:4]
    for hh in range(GROUP_HEADS):
        h = GROUP_HEADS * pair + hh
        lanes = slice(hh * HEAD_DIM, (hh + 1) * HEAD_DIM)
        st = st_ref[h]
        st_old_ref[h] = st
        for c in range(R // CHUNK):
            o, st = _hgrn_chunk_fast(c * CHUNK, lanes, st, q_ref, g_ref, k_ref, v_ref)
            _finish_chunk(c * CHUNK, h, lanes, o, hnw, mg_ref, slot)
            yield
        st_ref[h] = st


def _recur_pair_exact(pair, R, valid, hnw, rsel, st_in_ref, st_ref, mg_ref, slot):
    q_ref, g_ref, k_ref, v_ref = slot[0:4]
    for hh in range(GROUP_HEADS):
        h = GROUP_HEADS * pair + hh
        lanes = slice(hh * HEAD_DIM, (hh + 1) * HEAD_DIM)
        st = st_in_ref[h]
        if valid < R:
            o, st = _hgrn_chunk_exact(0, lanes, valid, st, q_ref, g_ref, k_ref, v_ref, rsel)
            _finish_chunk(0, h, lanes, o, hnw, mg_ref, slot)
        else:
            def chunk_body(c, st):
                row0 = pl.multiple_of(c * CHUNK, CHUNK)
                o, st = _hgrn_chunk_exact(row0, lanes, CHUNK, st, q_ref, g_ref, k_ref, v_ref, rsel)
                _finish_chunk(row0, h, lanes, o, hnw, mg_ref, slot)
                return st

            st = lax.fori_loop(0, R // CHUNK, chunk_body, st)
        st_ref[h] = st


def _interleave(*gens):
    gens = list(gens)
    while gens:
        for g in list(gens):
            try:
                next(g)
            except StopIteration:
                gens.remove(g)


def _mixer_tile(x, R, valid, is_meta, refs):
    (lbp_ref, anw_ref, win_ref, hnw_ref, cw_ref, wout_ref, rsel_ref,
     st_ref, st_old_ref, zc_ref, mg_ref, safe_ref, slots) = refs
    prm = (lbp_ref, win_ref, cw_ref, zc_ref, safe_ref)
    u = _rms(x, anw_ref[...]).astype(BF16)
    rsel = rsel_ref[...]
    hnw = hnw_ref[...]
    project = lambda pair: _project_pair(u, pair, R, valid, prm, slots[pair])
    n_pairs = N_GROUPS

    if is_meta:
        for pair in range(n_pairs):
            _interleave(project(pair))
            _recur_pair_exact(pair, R, valid, hnw, rsel, st_ref, st_ref, mg_ref, slots[pair])
    else:
        _interleave(project(0))
        for pair in range(n_pairs):
            if pair + 1 < n_pairs:
                _interleave(project(pair + 1))
            _interleave(_recur_pair_fast(pair, R, hnw, st_ref, st_old_ref, mg_ref, slots[pair]))
        for pair in range(n_pairs):
            pl.when(safe_ref[pair] == 0)(
                lambda pair=pair: _recur_pair_exact(pair, R, valid, hnw, rsel, st_old_ref, st_ref, mg_ref,
                                                    slots[pair]))

    return x + jnp.dot(mg_ref[0:R, :], wout_ref[:, 0:D_MODEL], preferred_element_type=F32)


def _mixer_kernel(x_ref, meta_ref, lbp_ref, anw_ref, win_ref, hnw_ref, cw_ref, wout_ref, rsel_ref,
                  h1_ref, h1m_ref, st_ref, st_old_ref, zc_ref, mg_ref, safe_ref, *slots):
    n = len(slots) // (N_GROUPS)
    refs = (lbp_ref, anw_ref, win_ref, hnw_ref, cw_ref, wout_ref, rsel_ref,
            st_ref, st_old_ref, zc_ref, mg_ref, safe_ref,
            [slots[i * n:(i + 1) * n] for i in range(N_GROUPS)])

    @pl.when(pl.program_id(1) == 0)
    def _():
        st_ref[...] = jnp.zeros_like(st_ref)
        zc_ref[...] = jnp.zeros_like(zc_ref)
        xm = jnp.concatenate([meta_ref[...], jnp.zeros((CHUNK - N_META, D_MODEL), F32)], axis=0)
        hm = _mixer_tile(xm, CHUNK, N_META, True, refs)
        h1m_ref[0] = hm[0:N_META, :]

    h1_ref[0] = _mixer_tile(x_ref[0], MIXER_TILE, MIXER_TILE, False, refs)


def _ffn_kernel(h1_ref, h1m_ref, fnw_ref, wup_ref, fcw_ref, fcb_ref, wdn_ref, finw_ref,
                out_ref, as_ref):
    T = FFN_TILE
    fnw = fnw_ref[...]

    @pl.when(pl.program_id(1) == 0)
    def _():
        um = _rms(h1m_ref[0], fnw).astype(BF16)
        am = jnp.dot(um, wup_ref[:, 0:D_FF], preferred_element_type=F32)
        as_ref[0:SUBLANES, :] = am[N_META - SUBLANES:N_META, :]

    x = h1_ref[0]
    u = _rms(x, fnw).astype(BF16)
    fcw = fcw_ref[...]
    fcb = fcb_ref[...]
    y = x
    for cb in range(D_FF // FFN_COL_BLOCK):
        cols = slice(cb * FFN_COL_BLOCK, (cb + 1) * FFN_COL_BLOCK)
        a = jnp.dot(u, wup_ref[:, cols], preferred_element_type=F32)
        as_ref[SUBLANES:SUBLANES + T, cols] = a
        a1 = as_ref[SUBLANES - 1:SUBLANES - 1 + T, cols]
        a2 = as_ref[SUBLANES - 2:SUBLANES - 2 + T, cols]
        ac = fcw[0:1, cols] * a2 + fcw[1:2, cols] * a1 + fcw[2:3, cols] * a + fcb[:, cols]
        as_ref[0:SUBLANES, cols] = as_ref[T:T + SUBLANES, cols]
        val = jnp.dot(u, wup_ref[:, D_FF + cb * FFN_COL_BLOCK:D_FF + (cb + 1) * FFN_COL_BLOCK],
                      preferred_element_type=F32)
        gated = (ac * _sigmoid(ac) * val).astype(BF16)
        y = y + jnp.dot(gated, wdn_ref[cols, 0:D_MODEL], preferred_element_type=F32)
    out_ref[0] = _rms(y, finw_ref[...])


def _resident(shape):
    return pl.BlockSpec(shape, lambda b, t: (0,) * len(shape), pipeline_mode=pl.Buffered(1))


def _weight(w):
    return jnp.pad(w.astype(BF16), ((0, 0), (0, WEIGHT_LANE_PAD)))


def _rsel_matrix():
    r = np.arange(N_HEADS * HEAD_DIM)[:, None] // HEAD_DIM
    l = np.arange(CHUNK)[None, :] % SUBLANES
    return jnp.asarray(r == l, dtype=BF16)


def kernel(x, meta_tokens, lb_param, attn_norm_w, w_in, hgrn_norm_w, conv_w, w_out, ffn_norm_w, w_up,
           ffn_conv_w, ffn_conv_b, w_down, final_norm_w):
    B, L, D = x.shape
    assert D == D_MODEL and L % MIXER_TILE == 0 and L % FFN_TILE == 0
    assert w_in.shape == (1, D, N_IN_SECTIONS * D) and w_up.shape == (1, D, 2 * D_FF)
    assert SUBLANES * HEAD_DIM == D_MODEL and N_HEADS == SUBLANES

    tile = lambda T: pl.BlockSpec((1, T, D), lambda b, t: (b, t, 0))
    meta_tile = pl.BlockSpec((1, N_META, D), lambda b, t: (b, 0, 0))
    params = pltpu.CompilerParams(dimension_semantics=("arbitrary", "arbitrary"),
                                  vmem_limit_bytes=VMEM_LIMIT_BYTES)

    TM = MIXER_TILE
    slot = [pltpu.VMEM((TM, GROUP_WIDTH), F32)] * 3 + [pltpu.VMEM((TM, GROUP_WIDTH), BF16)] \
        + [pltpu.VMEM((TM, GROUP_WIDTH), F32)] * 3 + [pltpu.VMEM((TM + SUBLANES, GROUP_WIDTH), F32)]
    h1, h1m = pl.pallas_call(
        _mixer_kernel,
        grid=(B, L // TM),
        in_specs=[tile(TM), _resident((N_META, D)), _resident((2, D)), _resident((1, D)),
                  _resident((D, N_IN_SECTIONS * D + WEIGHT_LANE_PAD)), _resident((1, HEAD_DIM)),
                  _resident((3, D)), _resident((D, D + WEIGHT_LANE_PAD)),
                  _resident((N_HEADS * HEAD_DIM, CHUNK))],
        out_specs=[tile(TM), meta_tile],
        out_shape=[jax.ShapeDtypeStruct((B, L, D), F32), jax.ShapeDtypeStruct((B, N_META, D), F32)],
        scratch_shapes=[pltpu.VMEM((N_HEADS, HEAD_DIM, HEAD_DIM), F32),
                        pltpu.VMEM((N_HEADS, HEAD_DIM, HEAD_DIM), F32),
                        pltpu.VMEM((SUBLANES, D), F32),
                        pltpu.VMEM((TM, D), BF16),
                        pltpu.SMEM((N_GROUPS,), jnp.int32),
                        *(slot * (N_GROUPS))],
        compiler_params=params,
        name="mixer",
    )(x, meta_tokens, lb_param, attn_norm_w, _weight(w_in[0]), hgrn_norm_w, conv_w[0],
      _weight(w_out[0]), _rsel_matrix())

    TF = FFN_TILE
    out = pl.pallas_call(
        _ffn_kernel,
        grid=(B, L // TF),
        in_specs=[tile(TF), meta_tile, _resident((1, D)), _resident((D, 2 * D_FF)), _resident((3, D_FF)),
                  _resident((1, D_FF)), _resident((D_FF, D + WEIGHT_LANE_PAD)), _resident((1, D))],
        out_specs=tile(TF),
        out_shape=jax.ShapeDtypeStruct((B, L, D), F32),
        scratch_shapes=[pltpu.VMEM((TF + SUBLANES, D_FF), F32)],
        compiler_params=params,
        name="ffn",
    )(h1, h1m, ffn_norm_w, w_up[0].astype(BF16), ffn_conv_w[0], ffn_conv_b, _weight(w_down[0]),
      final_norm_w.reshape(1, D))
    return out
```

```python
import numpy as np
import jax
import jax.numpy as jnp
from jax import lax
from jax.experimental import pallas as pl
from jax.experimental.pallas import tpu as pltpu

D_MODEL = 1024
N_META = 16
N_HEADS = 8
HEAD_DIM = 128
D_FF = 2816
N_IN_SECTIONS = 9
EPS = 1e-6

SUBLANES = 8
CHUNK = 128
GROUP_HEADS = 4
GROUP_WIDTH = GROUP_HEADS * HEAD_DIM
N_GROUPS = N_HEADS // GROUP_HEADS
MAX_SAFE_EXPONENT = 60.0
MIXER_TILE = 512
FFN_TILE = 512
FFN_COL_BLOCK = 1408
WEIGHT_LANE_PAD = 128
VMEM_LIMIT_BYTES = 60 * 1024 * 1024

F32 = jnp.float32
BF16 = jnp.bfloat16

_NT = (((1,), (1,)), ((), ()))
_TN = (((0,), (0,)), ((), ()))


def _rms(x, w):
    ms = jnp.mean(x * x, axis=-1, keepdims=True)
    return x * lax.rsqrt(ms + EPS) * w


def _sigmoid(x):
    return 1.0 / (1.0 + jnp.exp(-x))


def _bcast_rows(ref, lanes, row0, block, offset, nrows):
    pieces = [jnp.broadcast_to(_row(ref, lanes, row0, i * block + offset), (block, HEAD_DIM))
              for i in range(nrows // block)]
    return pieces[0] if len(pieces) == 1 else jnp.concatenate(pieces, axis=0)


def _row(ref, lanes, row0, r):
    group = ref[pl.ds(row0 + r // SUBLANES * SUBLANES, SUBLANES), lanes]
    return group[r % SUBLANES:r % SUBLANES + 1, :]


def _hgrn_chunk_exact(row0, lanes, valid, st, q_ref, g_ref, k_ref, v_ref, rsel):
    rows = pl.ds(row0, CHUNK)
    q = q_ref[rows, lanes]
    G = g_ref[rows, lanes]
    k = k_ref[rows, lanes]
    v = v_ref[rows, lanes]
    row = lax.broadcasted_iota(jnp.int32, (CHUNK, HEAD_DIM), 0)

    sub = row & (SUBLANES - 1)
    ps = []
    for j in range(SUBLANES):
        gj = _bcast_rows(g_ref, lanes, row0, SUBLANES, j, CHUNK)
        kj = _bcast_rows(k_ref, lanes, row0, SUBLANES, j, CHUNK)
        p = q * kj * jnp.exp(G - gj)
        ps.append(jnp.where(sub >= j, p, 0.0).astype(BF16))
    pcat = jnp.concatenate(ps, axis=1)
    a = jnp.dot(pcat, rsel, preferred_element_type=F32)

    tl_xor = (lax.broadcasted_iota(jnp.int32, (CHUNK, CHUNK), 0)
              ^ lax.broadcasted_iota(jnp.int32, (CHUNK, CHUNK), 1))
    a = jnp.where(tl_xor < SUBLANES, a, 0.0)

    b = 2 * SUBLANES
    while b <= CHUNK:
        gm = _bcast_rows(g_ref, lanes, row0, b, b // 2 - 1, CHUNK)
        e = jnp.exp(-jnp.abs(G - gm))
        upper = (row & (b // 2)) != 0
        qt = jnp.where(upper, q * e, 0.0).astype(BF16)
        kt = jnp.where(upper, 0.0, k * e).astype(BF16)
        ab = lax.dot_general(qt, kt, _NT, preferred_element_type=F32)
        a = a + (ab if b == CHUNK else jnp.where(tl_xor < b, ab, 0.0))
        b *= 2

    qi = (q * jnp.exp(G)).astype(BF16)
    o = lax.dot_general(qi, st.astype(BF16), _NT, preferred_element_type=F32)
    o = o + jnp.dot(a.astype(BF16), v, preferred_element_type=F32)

    glast = _row(g_ref, lanes, row0, valid - 1)
    kd = k * jnp.exp(glast - G)
    if valid < CHUNK:
        kd = jnp.where(row < valid, kd, 0.0)
    st = st * jnp.exp(glast) + lax.dot_general(v, kd.astype(BF16), _TN, preferred_element_type=F32)
    return o, st


def _hgrn_chunk_fast(row0, lanes, st, q_ref, g_ref, k_ref, v_ref):
    rows = pl.ds(row0, CHUNK)
    q = q_ref[rows, lanes]
    G = g_ref[rows, lanes]
    k = k_ref[rows, lanes]
    v = v_ref[rows, lanes]
    gmid = g_ref[pl.ds(row0 + CHUNK // 2 - 1, 1), lanes]
    glast = g_ref[pl.ds(row0 + CHUNK - 1, 1), lanes]
    d = G - gmid
    qh = q * jnp.exp(d)
    kh = k * jnp.exp(-d)
    a = lax.dot_general(qh.astype(BF16), kh.astype(BF16), _NT, preferred_element_type=F32)
    causal = (lax.broadcasted_iota(jnp.int32, (CHUNK, CHUNK), 1)
              <= lax.broadcasted_iota(jnp.int32, (CHUNK, CHUNK), 0))
    a = jnp.where(causal, a, 0.0)
    qi = (qh * jnp.exp(gmid)).astype(BF16)
    o = lax.dot_general(qi, st.astype(BF16), _NT, preferred_element_type=F32)
    o = o + jnp.dot(a.astype(BF16), v, preferred_element_type=F32)
    kd = (kh * jnp.exp(glast - gmid)).astype(BF16)
    st = st * jnp.exp(glast) + lax.dot_general(v, kd, _TN, preferred_element_type=F32)
    return o, st


def _project_pair(u, pair, R, valid, prm, slot):
    lbp_ref, win_ref, cw_ref, zc_ref, safe_ref = prm
    q_ref, g_ref, k_ref, v_ref, gs_ref, sa_ref, mb_ref, zs_ref = slot
    c0 = pair * GROUP_WIDTH
    cols = slice(c0, c0 + GROUP_WIDTH)

    def sec(i):
        w = win_ref[:, i * D_MODEL + c0:i * D_MODEL + c0 + GROUP_WIDTH]
        return jnp.dot(u, w, preferred_element_type=F32)

    qv = sec(0)
    q_ref[0:R, :] = qv * _sigmoid(qv)
    yield

    lbp = lbp_ref[:, cols]
    mx = jnp.max(lbp, axis=0, keepdims=True)
    ex = jnp.exp(lbp - mx)
    lb = ex[0:1, :] / jnp.sum(ex, axis=0, keepdims=True)
    f = lb + (1.0 - lb) * _sigmoid(sec(1))
    k_ref[0:R, :] = 1.0 - f
    lf = jnp.log(f)
    tri = (lax.broadcasted_iota(jnp.int32, (CHUNK, CHUNK), 1)
           <= lax.broadcasted_iota(jnp.int32, (CHUNK, CHUNK), 0)).astype(BF16)
    worst = jnp.zeros((1, GROUP_WIDTH), F32)
    for c in range(R // CHUNK):
        lfc = lf[c * CHUNK:(c + 1) * CHUNK, :]
        hi = lfc.astype(BF16)
        lo = (lfc - hi.astype(F32)).astype(BF16)
        gc = (jnp.dot(tri, hi, preferred_element_type=F32)
              + jnp.dot(tri, lo, preferred_element_type=F32))
        g_ref[c * CHUNK:(c + 1) * CHUNK, :] = gc
        gmid = gc[CHUNK // 2 - 1:CHUNK // 2, :]
        glast = gc[CHUNK - 1:CHUNK, :]
        worst = jnp.maximum(worst, jnp.maximum(-gmid, gmid - glast))
    safe_ref[pair] = (jnp.max(worst) <= MAX_SAFE_EXPONENT).astype(jnp.int32)
    yield

    v_ref[0:R, :] = sec(2).astype(BF16)
    yield
    gv = sec(3)
    gs_ref[0:R, :] = gv * _sigmoid(gv)
    yield

    bg = sec(4)
    z = sec(5) * sec(6)
    zs_ref[0:SUBLANES, :] = zc_ref[:, cols]
    zs_ref[SUBLANES:SUBLANES + R, :] = z
    z1 = zs_ref[SUBLANES - 1:SUBLANES - 1 + R, :]
    z2 = zs_ref[SUBLANES - 2:SUBLANES - 2 + R, :]
    cw = cw_ref[:, cols]
    yb = bg * (cw[0:1, :] * z2 + cw[1:2, :] * z1 + cw[2:3, :] * z)
    zc_ref[:, cols] = zs_ref[valid:valid + SUBLANES, :]
    yield

    sa_ref[0:R, :] = _sigmoid(sec(7))
    yield
    mb_ref[0:R, :] = _sigmoid(sec(8)) * yb
    yield


def _finish_chunk(row0, h, lanes, o, hnw, mg_ref, slot):
    gs_ref, sa_ref, mb_ref = slot[4:7]
    rows = pl.ds(row0, CHUNK)
    m = _rms(o, hnw) * gs_ref[rows, lanes] * sa_ref[rows, lanes] + mb_ref[rows, lanes]
    mg_ref[rows, h * HEAD_DIM:(h + 1) * HEAD_DIM] = m.astype(BF16)


def _recur_pair_fast(pair, R, hnw, st_ref, st_old_ref, mg_ref, slot):
    q_ref, g_ref, k_ref, v_ref = slot[0:4]
    for hh in range(GROUP_HEADS):
        h = GROUP_HEADS * pair + hh
        lanes = slice(hh * HEAD_DIM, (hh + 1) * HEAD_DIM)
        st = st_ref[h]
        st_old_ref[h] = st
        for c in range(R // CHUNK):
            o, st = _hgrn_chunk_fast(c * CHUNK, lanes, st, q_ref, g_ref, k_ref, v_ref)
            _finish_chunk(c * CHUNK, h, lanes, o, hnw, mg_ref, slot)
            yield
        st_ref[h] = st


def _recur_pair_exact(pair, R, valid, hnw, rsel, st_in_ref, st_ref, mg_ref, slot):
    q_ref, g_ref, k_ref, v_ref = slot[0:4]
    for hh in range(GROUP_HEADS):
        h = GROUP_HEADS * pair + hh
        lanes = slice(hh * HEAD_DIM, (hh + 1) * HEAD_DIM)
        st = st_in_ref[h]
        if valid < R:
            o, st = _hgrn_chunk_exact(0, lanes, valid, st, q_ref, g_ref, k_ref, v_ref, rsel)
            _finish_chunk(0, h, lanes, o, hnw, mg_ref, slot)
        else:
            def chunk_body(c, st):
                row0 = pl.multiple_of(c * CHUNK, CHUNK)
                o, st = _hgrn_chunk_exact(row0, lanes, CHUNK, st, q_ref, g_ref, k_ref, v_ref, rsel)
                _finish_chunk(row0, h, lanes, o, hnw, mg_ref, slot)
                return st

            st = lax.fori_loop(0, R // CHUNK, chunk_body, st)
        st_ref[h] = st


def _interleave(*gens):
    gens = list(gens)
    while gens:
        for g in list(gens):
            try:
                next(g)
            except StopIteration:
                gens.remove(g)


def _mixer_tile(x, R, valid, is_meta, refs):
    (lbp_ref, anw_ref, win_ref, hnw_ref, cw_ref, wout_ref, rsel_ref,
     st_ref, st_old_ref, zc_ref, mg_ref, safe_ref, slots) = refs
    prm = (lbp_ref, win_ref, cw_ref, zc_ref, safe_ref)
    u = _rms(x, anw_ref[...]).astype(BF16)
    rsel = rsel_ref[...]
    hnw = hnw_ref[...]
    project = lambda pair: _project_pair(u, pair, R, valid, prm, slots[pair])
    n_pairs = N_GROUPS

    if is_meta:
        for pair in range(n_pairs):
            _interleave(project(pair))
            _recur_pair_exact(pair, R, valid, hnw, rsel, st_ref, st_ref, mg_ref, slots[pair])
    else:
        _interleave(project(0))
        for pair in range(n_pairs):
            if pair + 1 < n_pairs:
                _interleave(project(pair + 1))
            _interleave(_recur_pair_fast(pair, R, hnw, st_ref, st_old_ref, mg_ref, slots[pair]))
        for pair in range(n_pairs):
            pl.when(safe_ref[pair] == 0)(
                lambda pair=pair: _recur_pair_exact(pair, R, valid, hnw, rsel, st_old_ref, st_ref, mg_ref,
                                                    slots[pair]))

    return x + jnp.dot(mg_ref[0:R, :], wout_ref[:, 0:D_MODEL], preferred_element_type=F32)


def _mixer_kernel(x_ref, meta_ref, lbp_ref, anw_ref, win_ref, hnw_ref, cw_ref, wout_ref, rsel_ref,
                  h1_ref, h1m_ref, st_ref, st_old_ref, zc_ref, mg_ref, safe_ref, *slots):
    n = len(slots) // (N_GROUPS)
    refs = (lbp_ref, anw_ref, win_ref, hnw_ref, cw_ref, wout_ref, rsel_ref,
            st_ref, st_old_ref, zc_ref, mg_ref, safe_ref,
            [slots[i * n:(i + 1) * n] for i in range(N_GROUPS)])

    @pl.when(pl.program_id(1) == 0)
    def _():
        st_ref[...] = jnp.zeros_like(st_ref)
        zc_ref[...] = jnp.zeros_like(zc_ref)
        xm = jnp.concatenate([meta_ref[...], jnp.zeros((CHUNK - N_META, D_MODEL), F32)], axis=0)
        hm = _mixer_tile(xm, CHUNK, N_META, True, refs)
        h1m_ref[0] = hm[0:N_META, :]

    h1_ref[0] = _mixer_tile(x_ref[0], MIXER_TILE, MIXER_TILE, False, refs)


def _ffn_kernel(h1_ref, h1m_ref, fnw_ref, wup_ref, fcw_ref, fcb_ref, wdn_ref, finw_ref,
                out_ref, as_ref):
    T = FFN_TILE
    fnw = fnw_ref[...]

    @pl.when(pl.program_id(1) == 0)
    def _():
        um = _rms(h1m_ref[0], fnw).astype(BF16)
        am = jnp.dot(um, wup_ref[:, 0:D_FF], preferred_element_type=F32)
        as_ref[0:SUBLANES, :] = am[N_META - SUBLANES:N_META, :]

    x = h1_ref[0]
    u = _rms(x, fnw).astype(BF16)
    fcw = fcw_ref[...]
    fcb = fcb_ref[...]
    y = x
    for cb in range(D_FF // FFN_COL_BLOCK):
        cols = slice(cb * FFN_COL_BLOCK, (cb + 1) * FFN_COL_BLOCK)
        a = jnp.dot(u, wup_ref[:, cols], preferred_element_type=F32)
        as_ref[SUBLANES:SUBLANES + T, cols] = a
        a1 = as_ref[SUBLANES - 1:SUBLANES - 1 + T, cols]
        a2 = as_ref[SUBLANES - 2:SUBLANES - 2 + T, cols]
        ac = fcw[0:1, cols] * a2 + fcw[1:2, cols] * a1 + fcw[2:3, cols] * a + fcb[:, cols]
        as_ref[0:SUBLANES, cols] = as_ref[T:T + SUBLANES, cols]
        val = jnp.dot(u, wup_ref[:, D_FF + cb * FFN_COL_BLOCK:D_FF + (cb + 1) * FFN_COL_BLOCK],
                      preferred_element_type=F32)
        gated = (ac * _sigmoid(ac) * val).astype(BF16)
        y = y + jnp.dot(gated, wdn_ref[cols, 0:D_MODEL], preferred_element_type=F32)
    out_ref[0] = _rms(y, finw_ref[...])


def _resident(shape):
    return pl.BlockSpec(shape, lambda b, t: (0,) * len(shape), pipeline_mode=pl.Buffered(1))


def _weight(w):
    return jnp.pad(w.astype(BF16), ((0, 0), (0, WEIGHT_LANE_PAD)))


def _rsel_matrix():
    r = np.arange(N_HEADS * HEAD_DIM)[:, None] // HEAD_DIM
    l = np.arange(CHUNK)[None, :] % SUBLANES
    return jnp.asarray(r == l, dtype=BF16)


def kernel(x, meta_tokens, lb_param, attn_norm_w, w_in, hgrn_norm_w, conv_w, w_out, ffn_norm_w, w_up,
           ffn_conv_w, ffn_conv_b, w_down, final_norm_w):
    B, L, D = x.shape
    assert D == D_MODEL and L % MIXER_TILE == 0 and L % FFN_TILE == 0
    assert w_in.shape == (1, D, N_IN_SECTIONS * D) and w_up.shape == (1, D, 2 * D_FF)
    assert SUBLANES * HEAD_DIM == D_MODEL and N_HEADS == SUBLANES

    tile = lambda T: pl.BlockSpec((1, T, D), lambda b, t: (b, t, 0))
    meta_tile = pl.BlockSpec((1, N_META, D), lambda b, t: (b, 0, 0))
    params = pltpu.CompilerParams(dimension_semantics=("arbitrary", "arbitrary"),
                                  vmem_limit_bytes=VMEM_LIMIT_BYTES)

    TM = MIXER_TILE
    slot = [pltpu.VMEM((TM, GROUP_WIDTH), F32)] * 3 + [pltpu.VMEM((TM, GROUP_WIDTH), BF16)] \
        + [pltpu.VMEM((TM, GROUP_WIDTH), F32)] * 3 + [pltpu.VMEM((TM + SUBLANES, GROUP_WIDTH), F32)]
    h1, h1m = pl.pallas_call(
        _mixer_kernel,
        grid=(B, L // TM),
        in_specs=[tile(TM), _resident((N_META, D)), _resident((2, D)), _resident((1, D)),
                  _resident((D, N_IN_SECTIONS * D + WEIGHT_LANE_PAD)), _resident((1, HEAD_DIM)),
                  _resident((3, D)), _resident((D, D + WEIGHT_LANE_PAD)),
                  _resident((N_HEADS * HEAD_DIM, CHUNK))],
        out_specs=[tile(TM), meta_tile],
        out_shape=[jax.ShapeDtypeStruct((B, L, D), F32), jax.ShapeDtypeStruct((B, N_META, D), F32)],
        scratch_shapes=[pltpu.VMEM((N_HEADS, HEAD_DIM, HEAD_DIM), F32),
                        pltpu.VMEM((N_HEADS, HEAD_DIM, HEAD_DIM), F32),
                        pltpu.VMEM((SUBLANES, D), F32),
                        pltpu.VMEM((TM, D), BF16),
                        pltpu.SMEM((N_GROUPS,), jnp.int32),
                        *(slot * (N_GROUPS))],
        compiler_params=params,
        name="mixer",
    )(x, meta_tokens, lb_param, attn_norm_w, _weight(w_in[0]), hgrn_norm_w, conv_w[0],
      _weight(w_out[0]), _rsel_matrix())

    TF = FFN_TILE
    out = pl.pallas_call(
        _ffn_kernel,
        grid=(B, L // TF),
        in_specs=[tile(TF), meta_tile, _resident((1, D)), _resident((D, 2 * D_FF)), _resident((3, D_FF)),
                  _resident((1, D_FF)), _resident((D_FF, D + WEIGHT_LANE_PAD)), _resident((1, D))],
        out_specs=tile(TF),
        out_shape=jax.ShapeDtypeStruct((B, L, D), F32),
        scratch_shapes=[pltpu.VMEM((TF + SUBLANES, D_FF), F32)],
        compiler_params=params,
        name="ffn",
    )(h1, h1m, ffn_norm_w, w_up[0].astype(BF16), ffn_conv_w[0], ffn_conv_b, _weight(w_down[0]),
      final_norm_w.reshape(1, D))
    return out
```

```python
import numpy as np
import jax
import jax.numpy as jnp
from jax import lax
from jax.experimental import pallas as pl
from jax.experimental.pallas import tpu as pltpu

D_MODEL = 1024
N_META = 16
N_HEADS = 8
HEAD_DIM = 128
D_FF = 2816
N_IN_SECTIONS = 9
EPS = 1e-6

SUBLANES = 8
CHUNK = 128
GROUP_HEADS = 4
GROUP_WIDTH = GROUP_HEADS * HEAD_DIM
N_GROUPS = N_HEADS // GROUP_HEADS
MAX_SAFE_EXPONENT = 60.0
LOG2_E = 1.4426950408889634
MIXER_TILE = 512
FFN_TILE = 512
FFN_COL_BLOCK = 1408
WEIGHT_LANE_PAD = 128
VMEM_LIMIT_BYTES = 60 * 1024 * 1024

F32 = jnp.float32
BF16 = jnp.bfloat16

_NT = (((1,), (1,)), ((), ()))
_TN = (((0,), (0,)), ((), ()))


def _rms(x, w):
    ms = jnp.mean(x * x, axis=-1, keepdims=True)
    return x * lax.rsqrt(ms + EPS) * w


def _sigmoid(x):
    return 1.0 / (1.0 + jnp.exp2(x * -LOG2_E))


def _bcast_rows(ref, lanes, row0, block, offset, nrows):
    pieces = [jnp.broadcast_to(_row(ref, lanes, row0, i * block + offset), (block, HEAD_DIM))
              for i in range(nrows // block)]
    return pieces[0] if len(pieces) == 1 else jnp.concatenate(pieces, axis=0)


def _row(ref, lanes, row0, r):
    group = ref[pl.ds(row0 + r // SUBLANES * SUBLANES, SUBLANES), lanes]
    return group[r % SUBLANES:r % SUBLANES + 1, :]


def _hgrn_chunk_exact(row0, lanes, valid, st, q_ref, g_ref, k_ref, v_ref, rsel):
    rows = pl.ds(row0, CHUNK)
    q = q_ref[rows, lanes]
    G = g_ref[rows, lanes]
    k = k_ref[rows, lanes]
    v = v_ref[rows, lanes]
    row = lax.broadcasted_iota(jnp.int32, (CHUNK, HEAD_DIM), 0)

    sub = row & (SUBLANES - 1)
    ps = []
    for j in range(SUBLANES):
        gj = _bcast_rows(g_ref, lanes, row0, SUBLANES, j, CHUNK)
        kj = _bcast_rows(k_ref, lanes, row0, SUBLANES, j, CHUNK)
        p = q * kj * jnp.exp2(G - gj)
        ps.append(jnp.where(sub >= j, p, 0.0).astype(BF16))
    pcat = jnp.concatenate(ps, axis=1)
    a = jnp.dot(pcat, rsel, preferred_element_type=F32)

    tl_xor = (lax.broadcasted_iota(jnp.int32, (CHUNK, CHUNK), 0)
              ^ lax.broadcasted_iota(jnp.int32, (CHUNK, CHUNK), 1))
    a = jnp.where(tl_xor < SUBLANES, a, 0.0)

    b = 2 * SUBLANES
    while b <= CHUNK:
        gm = _bcast_rows(g_ref, lanes, row0, b, b // 2 - 1, CHUNK)
        e = jnp.exp2(-jnp.abs(G - gm))
        upper = (row & (b // 2)) != 0
        qt = jnp.where(upper, q * e, 0.0).astype(BF16)
        kt = jnp.where(upper, 0.0, k * e).astype(BF16)
        ab = lax.dot_general(qt, kt, _NT, preferred_element_type=F32)
        a = a + (ab if b == CHUNK else jnp.where(tl_xor < b, ab, 0.0))
        b *= 2

    qi = (q * jnp.exp2(G)).astype(BF16)
    o = lax.dot_general(qi, st.astype(BF16), _NT, preferred_element_type=F32)
    o = o + jnp.dot(a.astype(BF16), v, preferred_element_type=F32)

    glast = _row(g_ref, lanes, row0, valid - 1)
    kd = k * jnp.exp2(glast - G)
    if valid < CHUNK:
        kd = jnp.where(row < valid, kd, 0.0)
    st = st * jnp.exp2(glast) + lax.dot_general(v, kd.astype(BF16), _TN, preferred_element_type=F32)
    return o, st


def _hgrn_chunk_fast(row0, lanes, st, q_ref, g_ref, k_ref, v_ref):
    rows = pl.ds(row0, CHUNK)
    q = q_ref[rows, lanes]
    G = g_ref[rows, lanes]
    k = k_ref[rows, lanes]
    v = v_ref[rows, lanes]
    gmid = g_ref[pl.ds(row0 + CHUNK // 2 - 1, 1), lanes]
    glast = g_ref[pl.ds(row0 + CHUNK - 1, 1), lanes]
    d = G - gmid
    qh = q * jnp.exp2(d)
    kh = k * jnp.exp2(-d)
    a = lax.dot_general(qh.astype(BF16), kh.astype(BF16), _NT, preferred_element_type=F32)
    causal = (lax.broadcasted_iota(jnp.int32, (CHUNK, CHUNK), 1)
              <= lax.broadcasted_iota(jnp.int32, (CHUNK, CHUNK), 0))
    a = jnp.where(causal, a, 0.0)
    qi = (qh * jnp.exp2(gmid)).astype(BF16)
    o = lax.dot_general(qi, st.astype(BF16), _NT, preferred_element_type=F32)
    o = o + jnp.dot(a.astype(BF16), v, preferred_element_type=F32)
    kd = (kh * jnp.exp2(glast - gmid)).astype(BF16)
    st = st * jnp.exp2(glast) + lax.dot_general(v, kd, _TN, preferred_element_type=F32)
    return o, st


def _project_group(u, g, R, valid, prm, slot):
    lb_ref, win_ref, cw_ref, zc_ref, safe_ref = prm
    q_ref, g_ref, k_ref, v_ref, gs_ref, sa_ref, mb_ref, zs_ref = slot
    c0 = g * GROUP_WIDTH
    cols = slice(c0, c0 + GROUP_WIDTH)

    def sec(i):
        w = win_ref[:, i * D_MODEL + c0:i * D_MODEL + c0 + GROUP_WIDTH]
        return jnp.dot(u, w, preferred_element_type=F32)

    qv = sec(0)
    q_ref[0:R, :] = qv * _sigmoid(qv)
    yield

    f = lb_ref[0:1, cols] + lb_ref[1:2, cols] * _sigmoid(sec(1))
    k_ref[0:R, :] = 1.0 - f
    lf = jnp.log2(f)
    tri = (lax.broadcasted_iota(jnp.int32, (CHUNK, CHUNK), 1)
           <= lax.broadcasted_iota(jnp.int32, (CHUNK, CHUNK), 0)).astype(BF16)
    worst = jnp.zeros((1, GROUP_WIDTH), F32)
    for c in range(R // CHUNK):
        lfc = lf[c * CHUNK:(c + 1) * CHUNK, :]
        hi = lfc.astype(BF16)
        lo = (lfc - hi.astype(F32)).astype(BF16)
        gc = (jnp.dot(tri, hi, preferred_element_type=F32)
              + jnp.dot(tri, lo, preferred_element_type=F32))
        g_ref[c * CHUNK:(c + 1) * CHUNK, :] = gc
        gmid = gc[CHUNK // 2 - 1:CHUNK // 2, :]
        glast = gc[CHUNK - 1:CHUNK, :]
        worst = jnp.maximum(worst, jnp.maximum(-gmid, gmid - glast))
    safe_ref[g] = (jnp.max(worst) <= MAX_SAFE_EXPONENT * LOG2_E).astype(jnp.int32)
    yield

    v_ref[0:R, :] = sec(2).astype(BF16)
    yield
    gv = sec(3)
    gs_ref[0:R, :] = gv * _sigmoid(gv)
    yield

    bg = sec(4)
    z = sec(5) * sec(6)
    zs_ref[0:SUBLANES, :] = zc_ref[:, cols]
    zs_ref[SUBLANES:SUBLANES + R, :] = z
    z1 = zs_ref[SUBLANES - 1:SUBLANES - 1 + R, :]
    z2 = zs_ref[SUBLANES - 2:SUBLANES - 2 + R, :]
    yb = bg * (cw_ref[0:1, cols] * z2 + cw_ref[1:2, cols] * z1 + cw_ref[2:3, cols] * z)
    zc_ref[:, cols] = zs_ref[valid:valid + SUBLANES, :]
    yield

    sa_ref[0:R, :] = _sigmoid(sec(7))
    yield
    mb_ref[0:R, :] = _sigmoid(sec(8)) * yb
    yield


def _finish_chunk(row0, h, lanes, o, hnw, mg_ref, slot):
    gs_ref, sa_ref, mb_ref = slot[4:7]
    rows = pl.ds(row0, CHUNK)
    m = _rms(o, hnw) * gs_ref[rows, lanes] * sa_ref[rows, lanes] + mb_ref[rows, lanes]
    mg_ref[rows, h * HEAD_DIM:(h + 1) * HEAD_DIM] = m.astype(BF16)


def _recur_group_fast(g, R, hnw, st_ref, st_old_ref, mg_ref, slot):
    q_ref, g_ref, k_ref, v_ref = slot[0:4]
    for hh in range(GROUP_HEADS):
        h = GROUP_HEADS * g + hh
        lanes = slice(hh * HEAD_DIM, (hh + 1) * HEAD_DIM)
        st = st_ref[h]
        st_old_ref[h] = st
        for c in range(R // CHUNK):
            o, st = _hgrn_chunk_fast(c * CHUNK, lanes, st, q_ref, g_ref, k_ref, v_ref)
            _finish_chunk(c * CHUNK, h, lanes, o, hnw, mg_ref, slot)
            yield
        st_ref[h] = st


def _recur_group_exact(g, R, valid, hnw, rsel, st_in_ref, st_ref, mg_ref, slot):
    q_ref, g_ref, k_ref, v_ref = slot[0:4]
    for hh in range(GROUP_HEADS):
        h = GROUP_HEADS * g + hh
        lanes = slice(hh * HEAD_DIM, (hh + 1) * HEAD_DIM)
        st = st_in_ref[h]
        if valid < R:
            o, st = _hgrn_chunk_exact(0, lanes, valid, st, q_ref, g_ref, k_ref, v_ref, rsel)
            _finish_chunk(0, h, lanes, o, hnw, mg_ref, slot)
        else:
            def chunk_body(c, st):
                row0 = pl.multiple_of(c * CHUNK, CHUNK)
                o, st = _hgrn_chunk_exact(row0, lanes, CHUNK, st, q_ref, g_ref, k_ref, v_ref, rsel)
                _finish_chunk(row0, h, lanes, o, hnw, mg_ref, slot)
                return st

            st = lax.fori_loop(0, R // CHUNK, chunk_body, st)
        st_ref[h] = st


def _interleave(*gens):
    gens = list(gens)
    while gens:
        for g in list(gens):
            try:
                next(g)
            except StopIteration:
                gens.remove(g)


def _out_partial(g, R, mg_ref, wout_ref):
    rows = slice(g * GROUP_WIDTH, (g + 1) * GROUP_WIDTH)
    return jnp.dot(mg_ref[0:R, rows], wout_ref[rows, 0:D_MODEL], preferred_element_type=F32)


def _mixer_tile(load_x, store_out, R, valid, is_meta, refs):
    (lbp_ref, anw_ref, win_ref, hnw_ref, cw_ref, wout_ref, rsel_ref,
     st_ref, st_old_ref, zc_ref, lb_ref, mg_ref, safe_ref, slots) = refs
    prm = (lb_ref, win_ref, cw_ref, zc_ref, safe_ref)
    u = _rms(load_x(), anw_ref[...]).astype(BF16)
    rsel = rsel_ref[...]
    hnw = hnw_ref[...]
    project = lambda g: _project_group(u, g, R, valid, prm, slots[g])

    lbp = lbp_ref[...]
    mx = jnp.max(lbp, axis=0, keepdims=True)
    ex = jnp.exp(lbp - mx)
    lb = ex[0:1, :] / jnp.sum(ex, axis=0, keepdims=True)
    lb_ref[0:1, :] = lb
    lb_ref[1:2, :] = 1.0 - lb

    if is_meta:
        for g in range(N_GROUPS):
            _interleave(project(g))
            _recur_group_exact(g, R, valid, hnw, rsel, st_ref, st_ref, mg_ref, slots[g])
        acc = load_x()
        for g in range(N_GROUPS):
            acc = acc + _out_partial(g, R, mg_ref, wout_ref)
        store_out(acc, False)
        return

    def store_early():
        acc = load_x()
        for g in range(N_GROUPS - 1):
            acc = acc + _out_partial(g, R, mg_ref, wout_ref)
        store_out(acc, False)

    _interleave(project(0))
    for g in range(N_GROUPS):
        if g + 1 < N_GROUPS:
            _interleave(project(g + 1))
        _interleave(_recur_group_fast(g, R, hnw, st_ref, st_old_ref, mg_ref, slots[g]))
        if g == N_GROUPS - 2:
            store_early()
    for g in range(N_GROUPS):
        def redo(g=g):
            _recur_group_exact(g, R, valid, hnw, rsel, st_old_ref, st_ref, mg_ref, slots[g])
            if g < N_GROUPS - 1:
                store_early()

        pl.when(safe_ref[g] == 0)(redo)
    store_out(_out_partial(N_GROUPS - 1, R, mg_ref, wout_ref), True)


def _mixer_kernel(x_ref, meta_ref, lbp_ref, anw_ref, win_ref, hnw_ref, cw_ref, wout_ref, rsel_ref,
                  h1_ref, h1m_ref, st_ref, st_old_ref, zc_ref, lb_ref, mg_ref, safe_ref, *slots):
    n = len(slots) // N_GROUPS
    refs = (lbp_ref, anw_ref, win_ref, hnw_ref, cw_ref, wout_ref, rsel_ref,
            st_ref, st_old_ref, zc_ref, lb_ref, mg_ref, safe_ref,
            [slots[i * n:(i + 1) * n] for i in range(N_GROUPS)])

    @pl.when(pl.program_id(1) == 0)
    def _():
        st_ref[...] = jnp.zeros_like(st_ref)
        zc_ref[...] = jnp.zeros_like(zc_ref)
        load_meta = lambda: jnp.concatenate(
            [meta_ref[...], jnp.zeros((CHUNK - N_META, D_MODEL), F32)], axis=0)

        def store_meta(val, accumulate):
            h1m_ref[0] = val[0:N_META, :]

        _mixer_tile(load_meta, store_meta, CHUNK, N_META, True, refs)

    def store_tile(val, accumulate):
        h1_ref[0] = h1_ref[0] + val if accumulate else val

    _mixer_tile(lambda: x_ref[0], store_tile, MIXER_TILE, MIXER_TILE, False, refs)


def _ffn_kernel(h1_ref, h1m_ref, fnw_ref, wup_ref, fcw_ref, fcb_ref, wdn_ref, finw_ref,
                out_ref, as_ref):
    T = FFN_TILE
    fnw = fnw_ref[...]

    @pl.when(pl.program_id(1) == 0)
    def _():
        um = _rms(h1m_ref[0], fnw).astype(BF16)
        am = jnp.dot(um, wup_ref[:, 0:D_FF], preferred_element_type=F32)
        as_ref[0:SUBLANES, :] = am[N_META - SUBLANES:N_META, :]

    x = h1_ref[0]
    u = _rms(x, fnw).astype(BF16)
    y = x
    for cb in range(D_FF // FFN_COL_BLOCK):
        cols = slice(cb * FFN_COL_BLOCK, (cb + 1) * FFN_COL_BLOCK)
        a = jnp.dot(u, wup_ref[:, cols], preferred_element_type=F32)
        as_ref[SUBLANES:SUBLANES + T, cols] = a
        a1 = as_ref[SUBLANES - 1:SUBLANES - 1 + T, cols]
        a2 = as_ref[SUBLANES - 2:SUBLANES - 2 + T, cols]
        ac = (fcw_ref[0:1, cols] * a2 + fcw_ref[1:2, cols] * a1 + fcw_ref[2:3, cols] * a
              + fcb_ref[:, cols])
        as_ref[0:SUBLANES, cols] = as_ref[T:T + SUBLANES, cols]
        val = jnp.dot(u, wup_ref[:, D_FF + cb * FFN_COL_BLOCK:D_FF + (cb + 1) * FFN_COL_BLOCK],
                      preferred_element_type=F32)
        gated = (ac * _sigmoid(ac) * val).astype(BF16)
        y = y + jnp.dot(gated, wdn_ref[cols, 0:D_MODEL], preferred_element_type=F32)
    out_ref[0] = _rms(y, finw_ref[...])


def _resident(shape):
    return pl.BlockSpec(shape, lambda b, t: (0,) * len(shape), pipeline_mode=pl.Buffered(1))


def _weight(w):
    return jnp.pad(w.astype(BF16), ((0, 0), (0, WEIGHT_LANE_PAD)))


def _rsel_matrix():
    r = np.arange(N_HEADS * HEAD_DIM)[:, None] // HEAD_DIM
    l = np.arange(CHUNK)[None, :] % SUBLANES
    return jnp.asarray(r == l, dtype=BF16)


def kernel(x, meta_tokens, lb_param, attn_norm_w, w_in, hgrn_norm_w, conv_w, w_out, ffn_norm_w, w_up,
           ffn_conv_w, ffn_conv_b, w_down, final_norm_w):
    B, L, D = x.shape
    assert D == D_MODEL and L % MIXER_TILE == 0 and L % FFN_TILE == 0
    assert w_in.shape == (1, D, N_IN_SECTIONS * D) and w_up.shape == (1, D, 2 * D_FF)
    assert SUBLANES * HEAD_DIM == D_MODEL and N_HEADS == SUBLANES

    tile = lambda T: pl.BlockSpec((1, T, D), lambda b, t: (b, t, 0))
    meta_tile = pl.BlockSpec((1, N_META, D), lambda b, t: (b, 0, 0))
    params = pltpu.CompilerParams(dimension_semantics=("arbitrary", "arbitrary"),
                                  vmem_limit_bytes=VMEM_LIMIT_BYTES)

    TM = MIXER_TILE
    slot = [pltpu.VMEM((TM, GROUP_WIDTH), F32)] * 3 + [pltpu.VMEM((TM, GROUP_WIDTH), BF16)] \
        + [pltpu.VMEM((TM, GROUP_WIDTH), F32)] * 3 + [pltpu.VMEM((TM + SUBLANES, GROUP_WIDTH), F32)]
    h1, h1m = pl.pallas_call(
        _mixer_kernel,
        grid=(B, L // TM),
        in_specs=[tile(TM), _resident((N_META, D)), _resident((2, D)), _resident((1, D)),
                  _resident((D, N_IN_SECTIONS * D + WEIGHT_LANE_PAD)), _resident((1, HEAD_DIM)),
                  _resident((3, D)), _resident((D, D + WEIGHT_LANE_PAD)),
                  _resident((N_HEADS * HEAD_DIM, CHUNK))],
        out_specs=[tile(TM), meta_tile],
        out_shape=[jax.ShapeDtypeStruct((B, L, D), F32), jax.ShapeDtypeStruct((B, N_META, D), F32)],
        scratch_shapes=[pltpu.VMEM((N_HEADS, HEAD_DIM, HEAD_DIM), F32),
                        pltpu.VMEM((N_HEADS, HEAD_DIM, HEAD_DIM), F32),
                        pltpu.VMEM((SUBLANES, D), F32),
                        pltpu.VMEM((SUBLANES, D), F32),
                        pltpu.VMEM((TM, D), BF16),
                        pltpu.SMEM((N_GROUPS,), jnp.int32),
                        *(slot * N_GROUPS)],
        compiler_params=params,
        name="mixer",
    )(x, meta_tokens, lb_param, attn_norm_w, _weight(w_in[0]), hgrn_norm_w, conv_w[0],
      _weight(w_out[0]), _rsel_matrix())

    TF = FFN_TILE
    out = pl.pallas_call(
        _ffn_kernel,
        grid=(B, L // TF),
        in_specs=[tile(TF), meta_tile, _resident((1, D)), _resident((D, 2 * D_FF)), _resident((3, D_FF)),
                  _resident((1, D_FF)), _resident((D_FF, D + WEIGHT_LANE_PAD)), _resident((1, D))],
        out_specs=tile(TF),
        out_shape=jax.ShapeDtypeStruct((B, L, D), F32),
        scratch_shapes=[pltpu.VMEM((TF + SUBLANES, D_FF), F32)],
        compiler_params=params,
        name="ffn",
    )(h1, h1m, ffn_norm_w, w_up[0].astype(BF16), ffn_conv_w[0], ffn_conv_b, _weight(w_down[0]),
      final_norm_w.reshape(1, D))
    return out
```

```python
import numpy as np
import jax
import jax.numpy as jnp
from jax import lax
from jax.experimental import pallas as pl
from jax.experimental.pallas import tpu as pltpu

D_MODEL = 1024
N_META = 16
N_HEADS = 8
HEAD_DIM = 128
D_FF = 2816
N_IN_SECTIONS = 9
EPS = 1e-6

SUBLANES = 8
CHUNK = 128
GROUP_HEADS = 4
GROUP_WIDTH = GROUP_HEADS * HEAD_DIM
N_GROUPS = N_HEADS // GROUP_HEADS
MAX_SAFE_EXPONENT = 60.0
LOG2_E = 1.4426950408889634
MIXER_TILE = 512
FFN_TILE = 512
FFN_COL_BLOCK = 1408
WEIGHT_LANE_PAD = 128
VMEM_LIMIT_BYTES = 60 * 1024 * 1024

F32 = jnp.float32
BF16 = jnp.bfloat16

_NT = (((1,), (1,)), ((), ()))
_TN = (((0,), (0,)), ((), ()))


def _rms(x, w):
    ms = jnp.mean(x * x, axis=-1, keepdims=True)
    return x * lax.rsqrt(ms + EPS) * w


def _sigmoid(x):
    return 1.0 / (1.0 + jnp.exp2(x * -LOG2_E))


def _bcast_rows(ref, lanes, row0, block, offset, nrows):
    pieces = [jnp.broadcast_to(_row(ref, lanes, row0, i * block + offset), (block, HEAD_DIM))
              for i in range(nrows // block)]
    return pieces[0] if len(pieces) == 1 else jnp.concatenate(pieces, axis=0)


def _row(ref, lanes, row0, r):
    group = ref[pl.ds(row0 + r // SUBLANES * SUBLANES, SUBLANES), lanes]
    return group[r % SUBLANES:r % SUBLANES + 1, :]


def _hgrn_chunk_exact(row0, lanes, valid, st, q_ref, g_ref, k_ref, v_ref, rsel):
    rows = pl.ds(row0, CHUNK)
    q = q_ref[rows, lanes]
    G = g_ref[rows, lanes]
    k = k_ref[rows, lanes]
    v = v_ref[rows, lanes]
    row = lax.broadcasted_iota(jnp.int32, (CHUNK, HEAD_DIM), 0)

    sub = row & (SUBLANES - 1)
    ps = []
    for j in range(SUBLANES):
        gj = _bcast_rows(g_ref, lanes, row0, SUBLANES, j, CHUNK)
        kj = _bcast_rows(k_ref, lanes, row0, SUBLANES, j, CHUNK)
        p = q * kj * jnp.exp2(G - gj)
        ps.append(jnp.where(sub >= j, p, 0.0).astype(BF16))
    pcat = jnp.concatenate(ps, axis=1)
    a = jnp.dot(pcat, rsel, preferred_element_type=F32)

    tl_xor = (lax.broadcasted_iota(jnp.int32, (CHUNK, CHUNK), 0)
              ^ lax.broadcasted_iota(jnp.int32, (CHUNK, CHUNK), 1))
    a = jnp.where(tl_xor < SUBLANES, a, 0.0)

    b = 2 * SUBLANES
    while b <= CHUNK:
        gm = _bcast_rows(g_ref, lanes, row0, b, b // 2 - 1, CHUNK)
        e = jnp.exp2(-jnp.abs(G - gm))
        upper = (row & (b // 2)) != 0
        qt = jnp.where(upper, q * e, 0.0).astype(BF16)
        kt = jnp.where(upper, 0.0, k * e).astype(BF16)
        ab = lax.dot_general(qt, kt, _NT, preferred_element_type=F32)
        a = a + (ab if b == CHUNK else jnp.where(tl_xor < b, ab, 0.0))
        b *= 2

    qi = (q * jnp.exp2(G)).astype(BF16)
    o = lax.dot_general(qi, st.astype(BF16), _NT, preferred_element_type=F32)
    o = o + jnp.dot(a.astype(BF16), v, preferred_element_type=F32)

    glast = _row(g_ref, lanes, row0, valid - 1)
    kd = k * jnp.exp2(glast - G)
    if valid < CHUNK:
        kd = jnp.where(row < valid, kd, 0.0)
    st = st * jnp.exp2(glast) + lax.dot_general(v, kd.astype(BF16), _TN, preferred_element_type=F32)
    return o, st


def _hgrn_chunk_fast(row0, lanes, st, q_ref, g_ref, k_ref, v_ref):
    rows = pl.ds(row0, CHUNK)
    q = q_ref[rows, lanes]
    G = g_ref[rows, lanes]
    k = k_ref[rows, lanes]
    v = v_ref[rows, lanes]
    gmid = g_ref[pl.ds(row0 + CHUNK // 2 - 1, 1), lanes]
    glast = g_ref[pl.ds(row0 + CHUNK - 1, 1), lanes]
    d = G - gmid
    qh = q * jnp.exp2(d)
    kh = k * jnp.exp2(-d)
    a = lax.dot_general(qh.astype(BF16), kh.astype(BF16), _NT, preferred_element_type=F32)
    causal = (lax.broadcasted_iota(jnp.int32, (CHUNK, CHUNK), 1)
              <= lax.broadcasted_iota(jnp.int32, (CHUNK, CHUNK), 0))
    a = jnp.where(causal, a, 0.0)
    qi = (qh * jnp.exp2(gmid)).astype(BF16)
    o = lax.dot_general(qi, st.astype(BF16), _NT, preferred_element_type=F32)
    o = o + jnp.dot(a.astype(BF16), v, preferred_element_type=F32)
    kd = (kh * jnp.exp2(glast - gmid)).astype(BF16)
    st = st * jnp.exp2(glast) + lax.dot_general(v, kd, _TN, preferred_element_type=F32)
    return o, st


def _project_group(u, g, R, valid, prm, slot, recurrence_inputs):
    lb_ref, win_ref, cw_ref, zc_ref, safe_ref = prm
    q_ref, g_ref, k_ref, v_ref, gs_ref, sa_ref, mb_ref, zs_ref = slot
    c0 = g * GROUP_WIDTH
    cols = slice(c0, c0 + GROUP_WIDTH)

    def sec(i):
        w = win_ref[:, i * D_MODEL + c0:i * D_MODEL + c0 + GROUP_WIDTH]
        return jnp.dot(u, w, preferred_element_type=F32)

    if not recurrence_inputs:
        yield from _project_gates(sec, cols, R, valid, cw_ref, zc_ref, slot)
        return

    qv = sec(0)
    q_ref[0:R, :] = qv * _sigmoid(qv)
    yield

    f = lb_ref[0:1, cols] + lb_ref[1:2, cols] * _sigmoid(sec(1))
    k_ref[0:R, :] = 1.0 - f
    lf = jnp.log2(f)
    tri = (lax.broadcasted_iota(jnp.int32, (CHUNK, CHUNK), 1)
           <= lax.broadcasted_iota(jnp.int32, (CHUNK, CHUNK), 0)).astype(BF16)
    worst = jnp.zeros((1, GROUP_WIDTH), F32)
    for c in range(R // CHUNK):
        lfc = lf[c * CHUNK:(c + 1) * CHUNK, :]
        hi = lfc.astype(BF16)
        lo = (lfc - hi.astype(F32)).astype(BF16)
        gc = (jnp.dot(tri, hi, preferred_element_type=F32)
              + jnp.dot(tri, lo, preferred_element_type=F32))
        g_ref[c * CHUNK:(c + 1) * CHUNK, :] = gc
        gmid = gc[CHUNK // 2 - 1:CHUNK // 2, :]
        glast = gc[CHUNK - 1:CHUNK, :]
        worst = jnp.maximum(worst, jnp.maximum(-gmid, gmid - glast))
    safe_ref[g] = (jnp.max(worst) <= MAX_SAFE_EXPONENT * LOG2_E).astype(jnp.int32)
    yield

    v_ref[0:R, :] = sec(2).astype(BF16)
    yield


def _project_gates(sec, cols, R, valid, cw_ref, zc_ref, slot):
    gs_ref, sa_ref, mb_ref, zs_ref = slot[4:8]
    gv = sec(3)
    gs_ref[0:R, :] = gv * _sigmoid(gv)
    yield

    bg = sec(4)
    z = sec(5) * sec(6)
    zs_ref[0:SUBLANES, :] = zc_ref[:, cols]
    zs_ref[SUBLANES:SUBLANES + R, :] = z
    z1 = zs_ref[SUBLANES - 1:SUBLANES - 1 + R, :]
    z2 = zs_ref[SUBLANES - 2:SUBLANES - 2 + R, :]
    yb = bg * (cw_ref[0:1, cols] * z2 + cw_ref[1:2, cols] * z1 + cw_ref[2:3, cols] * z)
    zc_ref[:, cols] = zs_ref[valid:valid + SUBLANES, :]
    yield

    sa_ref[0:R, :] = _sigmoid(sec(7))
    yield
    mb_ref[0:R, :] = _sigmoid(sec(8)) * yb
    yield


def _finish_chunk(row0, h, lanes, o, hnw, mg_ref, slot):
    gs_ref, sa_ref, mb_ref = slot[4:7]
    rows = pl.ds(row0, CHUNK)
    m = _rms(o, hnw) * gs_ref[rows, lanes] * sa_ref[rows, lanes] + mb_ref[rows, lanes]
    mg_ref[rows, h * HEAD_DIM:(h + 1) * HEAD_DIM] = m.astype(BF16)


def _recur_group_fast(g, R, hnw, st_ref, st_old_ref, mg_ref, slot):
    q_ref, g_ref, k_ref, v_ref = slot[0:4]
    for hh in range(GROUP_HEADS):
        h = GROUP_HEADS * g + hh
        lanes = slice(hh * HEAD_DIM, (hh + 1) * HEAD_DIM)
        st = st_ref[h]
        st_old_ref[h] = st
        for c in range(R // CHUNK):
            o, st = _hgrn_chunk_fast(c * CHUNK, lanes, st, q_ref, g_ref, k_ref, v_ref)
            mg_ref[c * CHUNK:(c + 1) * CHUNK, h * HEAD_DIM:(h + 1) * HEAD_DIM] = _rms(o, hnw).astype(BF16)
            yield
        st_ref[h] = st


def _finish_group_fast(g, R, mg_ref, slot):
    gs_ref, sa_ref, mb_ref = slot[4:7]
    cols = slice(g * GROUP_WIDTH, (g + 1) * GROUP_WIDTH)
    m = mg_ref[0:R, cols].astype(F32) * gs_ref[0:R, :] * sa_ref[0:R, :] + mb_ref[0:R, :]
    mg_ref[0:R, cols] = m.astype(BF16)


def _recur_group_exact(g, R, valid, hnw, rsel, st_in_ref, st_ref, mg_ref, slot):
    q_ref, g_ref, k_ref, v_ref = slot[0:4]
    for hh in range(GROUP_HEADS):
        h = GROUP_HEADS * g + hh
        lanes = slice(hh * HEAD_DIM, (hh + 1) * HEAD_DIM)
        st = st_in_ref[h]
        if valid < R:
            o, st = _hgrn_chunk_exact(0, lanes, valid, st, q_ref, g_ref, k_ref, v_ref, rsel)
            _finish_chunk(0, h, lanes, o, hnw, mg_ref, slot)
        else:
            def chunk_body(c, st):
                row0 = pl.multiple_of(c * CHUNK, CHUNK)
                o, st = _hgrn_chunk_exact(row0, lanes, CHUNK, st, q_ref, g_ref, k_ref, v_ref, rsel)
                _finish_chunk(row0, h, lanes, o, hnw, mg_ref, slot)
                return st

            st = lax.fori_loop(0, R // CHUNK, chunk_body, st)
        st_ref[h] = st


def _interleave(*gens):
    gens = list(gens)
    while gens:
        for g in list(gens):
            try:
                next(g)
            except StopIteration:
                gens.remove(g)


def _out_partial(g, R, mg_ref, wout_ref):
    rows = slice(g * GROUP_WIDTH, (g + 1) * GROUP_WIDTH)
    return jnp.dot(mg_ref[0:R, rows], wout_ref[rows, 0:D_MODEL], preferred_element_type=F32)


def _mixer_tile(load_x, store_out, R, valid, is_meta, refs):
    (lbp_ref, anw_ref, win_ref, hnw_ref, cw_ref, wout_ref, rsel_ref,
     st_ref, st_old_ref, zc_ref, lb_ref, mg_ref, safe_ref, slots) = refs
    prm = (lb_ref, win_ref, cw_ref, zc_ref, safe_ref)
    u = _rms(load_x(), anw_ref[...]).astype(BF16)
    rsel = rsel_ref[...]
    hnw = hnw_ref[...]
    project = lambda g, first: _interleave(_project_group(u, g, R, valid, prm, slots[g], first))

    lbp = lbp_ref[...]
    mx = jnp.max(lbp, axis=0, keepdims=True)
    ex = jnp.exp(lbp - mx)
    lb = ex[0:1, :] / jnp.sum(ex, axis=0, keepdims=True)
    lb_ref[0:1, :] = lb
    lb_ref[1:2, :] = 1.0 - lb

    if is_meta:
        for g in range(N_GROUPS):
            project(g, True)
            project(g, False)
            _recur_group_exact(g, R, valid, hnw, rsel, st_ref, st_ref, mg_ref, slots[g])
        acc = load_x()
        for g in range(N_GROUPS):
            acc = acc + _out_partial(g, R, mg_ref, wout_ref)
        store_out(acc, False)
        return

    def store_early():
        acc = load_x()
        for g in range(N_GROUPS - 1):
            acc = acc + _out_partial(g, R, mg_ref, wout_ref)
        store_out(acc, False)

    for g in range(N_GROUPS):
        project(g, True)
        if g > 0:
            _finish_group_fast(g - 1, R, mg_ref, slots[g - 1])
            if g == N_GROUPS - 1:
                store_early()
        _interleave(_recur_group_fast(g, R, hnw, st_ref, st_old_ref, mg_ref, slots[g]))
        project(g, False)
    _finish_group_fast(N_GROUPS - 1, R, mg_ref, slots[N_GROUPS - 1])
    for g in range(N_GROUPS):
        def redo(g=g):
            _recur_group_exact(g, R, valid, hnw, rsel, st_old_ref, st_ref, mg_ref, slots[g])
            if g < N_GROUPS - 1:
                store_early()

        pl.when(safe_ref[g] == 0)(redo)
    store_out(_out_partial(N_GROUPS - 1, R, mg_ref, wout_ref), True)


def _mixer_kernel(x_ref, meta_ref, lbp_ref, anw_ref, win_ref, hnw_ref, cw_ref, wout_ref, rsel_ref,
                  h1_ref, h1m_ref, st_ref, st_old_ref, zc_ref, lb_ref, mg_ref, safe_ref, *slots):
    n = len(slots) // N_GROUPS
    refs = (lbp_ref, anw_ref, win_ref, hnw_ref, cw_ref, wout_ref, rsel_ref,
            st_ref, st_old_ref, zc_ref, lb_ref, mg_ref, safe_ref,
            [slots[i * n:(i + 1) * n] for i in range(N_GROUPS)])

    @pl.when(pl.program_id(1) == 0)
    def _():
        st_ref[...] = jnp.zeros_like(st_ref)
        zc_ref[...] = jnp.zeros_like(zc_ref)
        load_meta = lambda: jnp.concatenate(
            [meta_ref[...], jnp.zeros((CHUNK - N_META, D_MODEL), F32)], axis=0)

        def store_meta(val, accumulate):
            h1m_ref[0] = val[0:N_META, :]

        _mixer_tile(load_meta, store_meta, CHUNK, N_META, True, refs)

    def store_tile(val, accumulate):
        h1_ref[0] = h1_ref[0] + val if accumulate else val

    _mixer_tile(lambda: x_ref[0], store_tile, MIXER_TILE, MIXER_TILE, False, refs)


def _ffn_kernel(h1_ref, h1m_ref, fnw_ref, wup_ref, fcw_ref, fcb_ref, wdn_ref, finw_ref,
                out_ref, as_ref):
    T = FFN_TILE
    fnw = fnw_ref[...]

    @pl.when(pl.program_id(1) == 0)
    def _():
        um = _rms(h1m_ref[0], fnw).astype(BF16)
        am = jnp.dot(um, wup_ref[:, 0:D_FF], preferred_element_type=F32)
        as_ref[0:SUBLANES, :] = am[N_META - SUBLANES:N_META, :]

    x = h1_ref[0]
    u = _rms(x, fnw).astype(BF16)
    y = x
    for cb in range(D_FF // FFN_COL_BLOCK):
        cols = slice(cb * FFN_COL_BLOCK, (cb + 1) * FFN_COL_BLOCK)
        a = jnp.dot(u, wup_ref[:, cols], preferred_element_type=F32)
        as_ref[SUBLANES:SUBLANES + T, cols] = a
        a1 = as_ref[SUBLANES - 1:SUBLANES - 1 + T, cols]
        a2 = as_ref[SUBLANES - 2:SUBLANES - 2 + T, cols]
        ac = (fcw_ref[0:1, cols] * a2 + fcw_ref[1:2, cols] * a1 + fcw_ref[2:3, cols] * a
              + fcb_ref[:, cols])
        as_ref[0:SUBLANES, cols] = as_ref[T:T + SUBLANES, cols]
        val = jnp.dot(u, wup_ref[:, D_FF + cb * FFN_COL_BLOCK:D_FF + (cb + 1) * FFN_COL_BLOCK],
                      preferred_element_type=F32)
        gated = (ac * _sigmoid(ac) * val).astype(BF16)
        y = y + jnp.dot(gated, wdn_ref[cols, 0:D_MODEL], preferred_element_type=F32)
    out_ref[0] = _rms(y, finw_ref[...])


def _resident(shape):
    return pl.BlockSpec(shape, lambda b, t: (0,) * len(shape), pipeline_mode=pl.Buffered(1))


def _weight(w):
    return jnp.pad(w.astype(BF16), ((0, 0), (0, WEIGHT_LANE_PAD)))


def _rsel_matrix():
    r = np.arange(N_HEADS * HEAD_DIM)[:, None] // HEAD_DIM
    l = np.arange(CHUNK)[None, :] % SUBLANES
    return jnp.asarray(r == l, dtype=BF16)


def kernel(x, meta_tokens, lb_param, attn_norm_w, w_in, hgrn_norm_w, conv_w, w_out, ffn_norm_w, w_up,
           ffn_conv_w, ffn_conv_b, w_down, final_norm_w):
    B, L, D = x.shape
    assert D == D_MODEL and L % MIXER_TILE == 0 and L % FFN_TILE == 0
    assert w_in.shape == (1, D, N_IN_SECTIONS * D) and w_up.shape == (1, D, 2 * D_FF)
    assert SUBLANES * HEAD_DIM == D_MODEL and N_HEADS == SUBLANES

    tile = lambda T: pl.BlockSpec((1, T, D), lambda b, t: (b, t, 0))
    meta_tile = pl.BlockSpec((1, N_META, D), lambda b, t: (b, 0, 0))
    params = pltpu.CompilerParams(dimension_semantics=("arbitrary", "arbitrary"),
                                  vmem_limit_bytes=VMEM_LIMIT_BYTES)

    TM = MIXER_TILE
    slot = [pltpu.VMEM((TM, GROUP_WIDTH), F32)] * 3 + [pltpu.VMEM((TM, GROUP_WIDTH), BF16)] \
        + [pltpu.VMEM((TM, GROUP_WIDTH), F32)] * 3 + [pltpu.VMEM((TM + SUBLANES, GROUP_WIDTH), F32)]
    h1, h1m = pl.pallas_call(
        _mixer_kernel,
        grid=(B, L // TM),
        in_specs=[tile(TM), _resident((N_META, D)), _resident((2, D)), _resident((1, D)),
                  _resident((D, N_IN_SECTIONS * D + WEIGHT_LANE_PAD)), _resident((1, HEAD_DIM)),
                  _resident((3, D)), _resident((D, D + WEIGHT_LANE_PAD)),
                  _resident((N_HEADS * HEAD_DIM, CHUNK))],
        out_specs=[tile(TM), meta_tile],
        out_shape=[jax.ShapeDtypeStruct((B, L, D), F32), jax.ShapeDtypeStruct((B, N_META, D), F32)],
        scratch_shapes=[pltpu.VMEM((N_HEADS, HEAD_DIM, HEAD_DIM), F32),
                        pltpu.VMEM((N_HEADS, HEAD_DIM, HEAD_DIM), F32),
                        pltpu.VMEM((SUBLANES, D), F32),
                        pltpu.VMEM((SUBLANES, D), F32),
                        pltpu.VMEM((TM, D), BF16),
                        pltpu.SMEM((N_GROUPS,), jnp.int32),
                        *(slot * N_GROUPS)],
        compiler_params=params,
        name="mixer",
    )(x, meta_tokens, lb_param, attn_norm_w, _weight(w_in[0]), hgrn_norm_w, conv_w[0],
      _weight(w_out[0]), _rsel_matrix())

    TF = FFN_TILE
    out = pl.pallas_call(
        _ffn_kernel,
        grid=(B, L // TF),
        in_specs=[tile(TF), meta_tile, _resident((1, D)), _resident((D, 2 * D_FF)), _resident((3, D_FF)),
                  _resident((1, D_FF)), _resident((D_FF, D + WEIGHT_LANE_PAD)), _resident((1, D))],
        out_specs=tile(TF),
        out_shape=jax.ShapeDtypeStruct((B, L, D), F32),
        scratch_shapes=[pltpu.VMEM((TF + SUBLANES, D_FF), F32)],
        compiler_params=params,
        name="ffn",
    )(h1, h1m, ffn_norm_w, w_up[0].astype(BF16), ffn_conv_w[0], ffn_conv_b, _weight(w_down[0]),
      final_norm_w.reshape(1, D))
    return out
```

```python
import numpy as np
import jax
import jax.numpy as jnp
from jax import lax
from jax.experimental import pallas as pl
from jax.experimental.pallas import tpu as pltpu

D_MODEL = 1024
N_META = 16
N_HEADS = 8
HEAD_DIM = 128
D_FF = 2816
N_IN_SECTIONS = 9
EPS = 1e-6

SUBLANES = 8
CHUNK = 128
GROUP_HEADS = 4
GROUP_WIDTH = GROUP_HEADS * HEAD_DIM
N_GROUPS = N_HEADS // GROUP_HEADS
MAX_SAFE_EXPONENT = 60.0
LOG2_E = 1.4426950408889634
MIXER_TILE = 512
FFN_TILE = 512
FFN_COL_BLOCK = 1408
WEIGHT_LOAD_COLS = 256
WEIGHT_LOAD_ROWS = 352
VMEM_LIMIT_BYTES = 60 * 1024 * 1024

F32 = jnp.float32
BF16 = jnp.bfloat16

_NT = (((1,), (1,)), ((), ()))
_TN = (((0,), (0,)), ((), ()))


def _rms(x, w):
    ms = jnp.mean(x * x, axis=-1, keepdims=True)
    return x * lax.rsqrt(ms + EPS) * w


def _sigmoid(x):
    return 1.0 / (1.0 + jnp.exp2(x * -LOG2_E))


def _load_weight(src_hbm, dst_ref, stage_ref, sem_ref, axis):
    block = stage_ref.shape[1 + axis]
    assert src_hbm.shape[axis] % block == 0 and src_hbm.shape[1 - axis] == stage_ref.shape[2 - axis]
    n = src_hbm.shape[axis] // block

    def window(ref, i):
        return ref.at[:, pl.ds(i * block, block)] if axis == 1 else ref.at[pl.ds(i * block, block), :]

    def copy(i):
        return pltpu.make_async_copy(window(src_hbm, i), stage_ref.at[i % 2], sem_ref.at[i % 2])

    copy(0).start()
    for i in range(n):
        if i + 1 < n:
            copy(i + 1).start()
        copy(i).wait()
        window(dst_ref, i)[...] = stage_ref[i % 2].astype(BF16)


def _first_step():
    return jnp.logical_and(pl.program_id(0) == 0, pl.program_id(1) == 0)


def _bcast_rows(ref, lanes, row0, block, offset, nrows):
    pieces = [jnp.broadcast_to(_row(ref, lanes, row0, i * block + offset), (block, HEAD_DIM))
              for i in range(nrows // block)]
    return pieces[0] if len(pieces) == 1 else jnp.concatenate(pieces, axis=0)


def _row(ref, lanes, row0, r):
    group = ref[pl.ds(row0 + r // SUBLANES * SUBLANES, SUBLANES), lanes]
    return group[r % SUBLANES:r % SUBLANES + 1, :]


def _hgrn_chunk_exact(row0, lanes, valid, st, q_ref, g_ref, k_ref, v_ref, rsel):
    rows = pl.ds(row0, CHUNK)
    q = q_ref[rows, lanes]
    G = g_ref[rows, lanes]
    k = k_ref[rows, lanes]
    v = v_ref[rows, lanes]
    row = lax.broadcasted_iota(jnp.int32, (CHUNK, HEAD_DIM), 0)

    sub = row & (SUBLANES - 1)
    ps = []
    for j in range(SUBLANES):
        gj = _bcast_rows(g_ref, lanes, row0, SUBLANES, j, CHUNK)
        kj = _bcast_rows(k_ref, lanes, row0, SUBLANES, j, CHUNK)
        p = q * kj * jnp.exp2(G - gj)
        ps.append(jnp.where(sub >= j, p, 0.0).astype(BF16))
    pcat = jnp.concatenate(ps, axis=1)
    a = jnp.dot(pcat, rsel, preferred_element_type=F32)

    tl_xor = (lax.broadcasted_iota(jnp.int32, (CHUNK, CHUNK), 0)
              ^ lax.broadcasted_iota(jnp.int32, (CHUNK, CHUNK), 1))
    a = jnp.where(tl_xor < SUBLANES, a, 0.0)

    b = 2 * SUBLANES
    while b <= CHUNK:
        gm = _bcast_rows(g_ref, lanes, row0, b, b // 2 - 1, CHUNK)
        e = jnp.exp2(-jnp.abs(G - gm))
        upper = (row & (b // 2)) != 0
        qt = jnp.where(upper, q * e, 0.0).astype(BF16)
        kt = jnp.where(upper, 0.0, k * e).astype(BF16)
        ab = lax.dot_general(qt, kt, _NT, preferred_element_type=F32)
        a = a + (ab if b == CHUNK else jnp.where(tl_xor < b, ab, 0.0))
        b *= 2

    qi = (q * jnp.exp2(G)).astype(BF16)
    o = lax.dot_general(qi, st.astype(BF16), _NT, preferred_element_type=F32)
    o = o + jnp.dot(a.astype(BF16), v, preferred_element_type=F32)

    glast = _row(g_ref, lanes, row0, valid - 1)
    kd = k * jnp.exp2(glast - G)
    if valid < CHUNK:
        kd = jnp.where(row < valid, kd, 0.0)
    st = st * jnp.exp2(glast) + lax.dot_general(v, kd.astype(BF16), _TN, preferred_element_type=F32)
    return o, st


def _hgrn_chunk_fast(row0, lanes, st, q_ref, g_ref, k_ref, v_ref):
    rows = pl.ds(row0, CHUNK)
    q = q_ref[rows, lanes]
    G = g_ref[rows, lanes]
    k = k_ref[rows, lanes]
    v = v_ref[rows, lanes]
    gmid = g_ref[pl.ds(row0 + CHUNK // 2 - 1, 1), lanes]
    glast = g_ref[pl.ds(row0 + CHUNK - 1, 1), lanes]
    d = G - gmid
    qh = q * jnp.exp2(d)
    kh = k * jnp.exp2(-d)
    a = lax.dot_general(qh.astype(BF16), kh.astype(BF16), _NT, preferred_element_type=F32)
    causal = (lax.broadcasted_iota(jnp.int32, (CHUNK, CHUNK), 1)
              <= lax.broadcasted_iota(jnp.int32, (CHUNK, CHUNK), 0))
    a = jnp.where(causal, a, 0.0)
    qi = (qh * jnp.exp2(gmid)).astype(BF16)
    o = lax.dot_general(qi, st.astype(BF16), _NT, preferred_element_type=F32)
    o = o + jnp.dot(a.astype(BF16), v, preferred_element_type=F32)
    kd = (kh * jnp.exp2(glast - gmid)).astype(BF16)
    st = st * jnp.exp2(glast) + lax.dot_general(v, kd, _TN, preferred_element_type=F32)
    return o, st


def _project_group(u, g, R, valid, prm, slot, recurrence_inputs):
    lb_ref, win_ref, cw_ref, zc_ref, safe_ref = prm
    q_ref, g_ref, k_ref, v_ref, gs_ref, sa_ref, mb_ref, zs_ref = slot
    c0 = g * GROUP_WIDTH
    cols = slice(c0, c0 + GROUP_WIDTH)

    def sec(i):
        w = win_ref[:, i * D_MODEL + c0:i * D_MODEL + c0 + GROUP_WIDTH]
        return jnp.dot(u, w, preferred_element_type=F32)

    if not recurrence_inputs:
        yield from _project_gates(sec, cols, R, valid, cw_ref, zc_ref, slot)
        return

    qv = sec(0)
    q_ref[0:R, :] = qv * _sigmoid(qv)
    yield

    f = lb_ref[0:1, cols] + lb_ref[1:2, cols] * _sigmoid(sec(1))
    k_ref[0:R, :] = 1.0 - f
    lf = jnp.log2(f)
    tri = (lax.broadcasted_iota(jnp.int32, (CHUNK, CHUNK), 1)
           <= lax.broadcasted_iota(jnp.int32, (CHUNK, CHUNK), 0)).astype(BF16)
    worst = jnp.zeros((1, GROUP_WIDTH), F32)
    for c in range(R // CHUNK):
        lfc = lf[c * CHUNK:(c + 1) * CHUNK, :]
        hi = lfc.astype(BF16)
        lo = (lfc - hi.astype(F32)).astype(BF16)
        gc = (jnp.dot(tri, hi, preferred_element_type=F32)
              + jnp.dot(tri, lo, preferred_element_type=F32))
        g_ref[c * CHUNK:(c + 1) * CHUNK, :] = gc
        gmid = gc[CHUNK // 2 - 1:CHUNK // 2, :]
        glast = gc[CHUNK - 1:CHUNK, :]
        worst = jnp.maximum(worst, jnp.maximum(-gmid, gmid - glast))
    safe_ref[g] = (jnp.max(worst) <= MAX_SAFE_EXPONENT * LOG2_E).astype(jnp.int32)
    yield

    v_ref[0:R, :] = sec(2).astype(BF16)
    yield


def _project_gates(sec, cols, R, valid, cw_ref, zc_ref, slot):
    gs_ref, sa_ref, mb_ref, zs_ref = slot[4:8]
    gv = sec(3)
    gs_ref[0:R, :] = gv * _sigmoid(gv)
    yield

    bg = sec(4)
    z = sec(5) * sec(6)
    zs_ref[0:SUBLANES, :] = zc_ref[:, cols]
    zs_ref[SUBLANES:SUBLANES + R, :] = z
    z1 = zs_ref[SUBLANES - 1:SUBLANES - 1 + R, :]
    z2 = zs_ref[SUBLANES - 2:SUBLANES - 2 + R, :]
    yb = bg * (cw_ref[0:1, cols] * z2 + cw_ref[1:2, cols] * z1 + cw_ref[2:3, cols] * z)
    zc_ref[:, cols] = zs_ref[valid:valid + SUBLANES, :]
    yield

    sa_ref[0:R, :] = _sigmoid(sec(7))
    yield
    mb_ref[0:R, :] = _sigmoid(sec(8)) * yb
    yield


def _finish_chunk(row0, h, lanes, o, hnw, mg_ref, slot):
    gs_ref, sa_ref, mb_ref = slot[4:7]
    rows = pl.ds(row0, CHUNK)
    m = _rms(o, hnw) * gs_ref[rows, lanes] * sa_ref[rows, lanes] + mb_ref[rows, lanes]
    mg_ref[rows, h * HEAD_DIM:(h + 1) * HEAD_DIM] = m.astype(BF16)


def _recur_group_fast(g, R, hnw, st_ref, st_old_ref, mg_ref, slot):
    q_ref, g_ref, k_ref, v_ref = slot[0:4]
    for hh in range(GROUP_HEADS):
        h = GROUP_HEADS * g + hh
        lanes = slice(hh * HEAD_DIM, (hh + 1) * HEAD_DIM)
        st = st_ref[h]
        st_old_ref[h] = st
        for c in range(R // CHUNK):
            o, st = _hgrn_chunk_fast(c * CHUNK, lanes, st, q_ref, g_ref, k_ref, v_ref)
            mg_ref[c * CHUNK:(c + 1) * CHUNK, h * HEAD_DIM:(h + 1) * HEAD_DIM] = _rms(o, hnw).astype(BF16)
            yield
        st_ref[h] = st


def _finish_group_fast(g, R, mg_ref, slot):
    gs_ref, sa_ref, mb_ref = slot[4:7]
    cols = slice(g * GROUP_WIDTH, (g + 1) * GROUP_WIDTH)
    m = mg_ref[0:R, cols].astype(F32) * gs_ref[0:R, :] * sa_ref[0:R, :] + mb_ref[0:R, :]
    mg_ref[0:R, cols] = m.astype(BF16)


def _recur_group_exact(g, R, valid, hnw, rsel, st_in_ref, st_ref, mg_ref, slot):
    q_ref, g_ref, k_ref, v_ref = slot[0:4]
    for hh in range(GROUP_HEADS):
        h = GROUP_HEADS * g + hh
        lanes = slice(hh * HEAD_DIM, (hh + 1) * HEAD_DIM)
        st = st_in_ref[h]
        if valid < R:
            o, st = _hgrn_chunk_exact(0, lanes, valid, st, q_ref, g_ref, k_ref, v_ref, rsel)
            _finish_chunk(0, h, lanes, o, hnw, mg_ref, slot)
        else:
            def chunk_body(c, st):
                row0 = pl.multiple_of(c * CHUNK, CHUNK)
                o, st = _hgrn_chunk_exact(row0, lanes, CHUNK, st, q_ref, g_ref, k_ref, v_ref, rsel)
                _finish_chunk(row0, h, lanes, o, hnw, mg_ref, slot)
                return st

            st = lax.fori_loop(0, R // CHUNK, chunk_body, st)
        st_ref[h] = st


def _interleave(*gens):
    gens = list(gens)
    while gens:
        for g in list(gens):
            try:
                next(g)
            except StopIteration:
                gens.remove(g)


def _out_partial(g, R, mg_ref, wout_ref):
    rows = slice(g * GROUP_WIDTH, (g + 1) * GROUP_WIDTH)
    return jnp.dot(mg_ref[0:R, rows], wout_ref[rows, :], preferred_element_type=F32)


def _mixer_tile(load_x, store_out, R, valid, is_meta, refs):
    (lbp_ref, anw_ref, win_ref, hnw_ref, cw_ref, wout_ref, rsel_ref,
     st_ref, st_old_ref, zc_ref, lb_ref, mg_ref, safe_ref, slots) = refs
    prm = (lb_ref, win_ref, cw_ref, zc_ref, safe_ref)
    u = _rms(load_x(), anw_ref[...]).astype(BF16)
    rsel = rsel_ref[...]
    hnw = hnw_ref[...]
    project = lambda g, first: _interleave(_project_group(u, g, R, valid, prm, slots[g], first))

    lbp = lbp_ref[...]
    mx = jnp.max(lbp, axis=0, keepdims=True)
    ex = jnp.exp(lbp - mx)
    lb = ex[0:1, :] / jnp.sum(ex, axis=0, keepdims=True)
    lb_ref[0:1, :] = lb
    lb_ref[1:2, :] = 1.0 - lb

    if is_meta:
        for g in range(N_GROUPS):
            project(g, True)
            project(g, False)
            _recur_group_exact(g, R, valid, hnw, rsel, st_ref, st_ref, mg_ref, slots[g])
        acc = load_x()
        for g in range(N_GROUPS):
            acc = acc + _out_partial(g, R, mg_ref, wout_ref)
        store_out(acc, False)
        return

    def store_early():
        acc = load_x()
        for g in range(N_GROUPS - 1):
            acc = acc + _out_partial(g, R, mg_ref, wout_ref)
        store_out(acc, False)

    for g in range(N_GROUPS):
        project(g, True)
        if g > 0:
            _finish_group_fast(g - 1, R, mg_ref, slots[g - 1])
            if g == N_GROUPS - 1:
                store_early()
        _interleave(_recur_group_fast(g, R, hnw, st_ref, st_old_ref, mg_ref, slots[g]))
        project(g, False)
    _finish_group_fast(N_GROUPS - 1, R, mg_ref, slots[N_GROUPS - 1])
    for g in range(N_GROUPS):
        def redo(g=g):
            _recur_group_exact(g, R, valid, hnw, rsel, st_old_ref, st_ref, mg_ref, slots[g])
            if g < N_GROUPS - 1:
                store_early()

        pl.when(safe_ref[g] == 0)(redo)
    store_out(_out_partial(N_GROUPS - 1, R, mg_ref, wout_ref), True)


def _mixer_kernel(x_ref, meta_ref, lbp_ref, anw_ref, win_hbm, hnw_ref, cw_ref, wout_hbm, rsel_ref,
                  h1_ref, h1m_ref, win_ref, wout_ref, stage_ref, sem_ref,
                  st_ref, st_old_ref, zc_ref, lb_ref, mg_ref, safe_ref, *slots):
    @pl.when(_first_step())
    def _():
        _load_weight(win_hbm, win_ref, stage_ref, sem_ref, 1)
        _load_weight(wout_hbm, wout_ref, stage_ref, sem_ref, 1)

    n = len(slots) // N_GROUPS
    refs = (lbp_ref, anw_ref, win_ref, hnw_ref, cw_ref, wout_ref, rsel_ref,
            st_ref, st_old_ref, zc_ref, lb_ref, mg_ref, safe_ref,
            [slots[i * n:(i + 1) * n] for i in range(N_GROUPS)])

    @pl.when(pl.program_id(1) == 0)
    def _():
        st_ref[...] = jnp.zeros_like(st_ref)
        zc_ref[...] = jnp.zeros_like(zc_ref)
        load_meta = lambda: jnp.concatenate(
            [meta_ref[...], jnp.zeros((CHUNK - N_META, D_MODEL), F32)], axis=0)

        def store_meta(val, accumulate):
            h1m_ref[0] = val[0:N_META, :]

        _mixer_tile(load_meta, store_meta, CHUNK, N_META, True, refs)

    def store_tile(val, accumulate):
        h1_ref[0] = h1_ref[0] + val if accumulate else val

    _mixer_tile(lambda: x_ref[0], store_tile, MIXER_TILE, MIXER_TILE, False, refs)


def _ffn_kernel(h1_ref, h1m_ref, fnw_ref, wup_hbm, fcw_ref, fcb_ref, wdn_hbm, finw_ref,
                out_ref, wup_ref, wdn_ref, stage_up_ref, stage_dn_ref, sem_ref, as_ref):
    T = FFN_TILE
    fnw = fnw_ref[...]

    @pl.when(_first_step())
    def _():
        _load_weight(wup_hbm, wup_ref, stage_up_ref, sem_ref, 1)
        _load_weight(wdn_hbm, wdn_ref, stage_dn_ref, sem_ref, 0)

    @pl.when(pl.program_id(1) == 0)
    def _():
        um = _rms(h1m_ref[0], fnw).astype(BF16)
        am = jnp.dot(um, wup_ref[:, 0:D_FF], preferred_element_type=F32)
        as_ref[0:SUBLANES, :] = am[N_META - SUBLANES:N_META, :]

    x = h1_ref[0]
    u = _rms(x, fnw).astype(BF16)
    y = x
    for cb in range(D_FF // FFN_COL_BLOCK):
        cols = slice(cb * FFN_COL_BLOCK, (cb + 1) * FFN_COL_BLOCK)
        a = jnp.dot(u, wup_ref[:, cols], preferred_element_type=F32)
        as_ref[SUBLANES:SUBLANES + T, cols] = a
        a1 = as_ref[SUBLANES - 1:SUBLANES - 1 + T, cols]
        a2 = as_ref[SUBLANES - 2:SUBLANES - 2 + T, cols]
        ac = (fcw_ref[0:1, cols] * a2 + fcw_ref[1:2, cols] * a1 + fcw_ref[2:3, cols] * a
              + fcb_ref[:, cols])
        as_ref[0:SUBLANES, cols] = as_ref[T:T + SUBLANES, cols]
        val = jnp.dot(u, wup_ref[:, D_FF + cb * FFN_COL_BLOCK:D_FF + (cb + 1) * FFN_COL_BLOCK],
                      preferred_element_type=F32)
        gated = (ac * _sigmoid(ac) * val).astype(BF16)
        y = y + jnp.dot(gated, wdn_ref[cols, :], preferred_element_type=F32)
    out_ref[0] = _rms(y, finw_ref[...])


def _resident(shape):
    return pl.BlockSpec(shape, lambda b, t: (0,) * len(shape), pipeline_mode=pl.Buffered(1))


def _rsel_matrix():
    r = np.arange(N_HEADS * HEAD_DIM)[:, None] // HEAD_DIM
    l = np.arange(CHUNK)[None, :] % SUBLANES
    return jnp.asarray(r == l, dtype=BF16)


def kernel(x, meta_tokens, lb_param, attn_norm_w, w_in, hgrn_norm_w, conv_w, w_out, ffn_norm_w, w_up,
           ffn_conv_w, ffn_conv_b, w_down, final_norm_w):
    B, L, D = x.shape
    assert D == D_MODEL and L % MIXER_TILE == 0 and L % FFN_TILE == 0
    assert w_in.shape == (1, D, N_IN_SECTIONS * D) and w_up.shape == (1, D, 2 * D_FF)
    assert SUBLANES * HEAD_DIM == D_MODEL and N_HEADS == SUBLANES

    tile = lambda T: pl.BlockSpec((1, T, D), lambda b, t: (b, t, 0))
    meta_tile = pl.BlockSpec((1, N_META, D), lambda b, t: (b, 0, 0))
    in_hbm = pl.BlockSpec(memory_space=pl.ANY)
    params = pltpu.CompilerParams(dimension_semantics=("arbitrary", "arbitrary"),
                                  vmem_limit_bytes=VMEM_LIMIT_BYTES)

    TM = MIXER_TILE
    slot = [pltpu.VMEM((TM, GROUP_WIDTH), F32)] * 3 + [pltpu.VMEM((TM, GROUP_WIDTH), BF16)] \
        + [pltpu.VMEM((TM, GROUP_WIDTH), F32)] * 3 + [pltpu.VMEM((TM + SUBLANES, GROUP_WIDTH), F32)]
    h1, h1m = pl.pallas_call(
        _mixer_kernel,
        grid=(B, L // TM),
        in_specs=[tile(TM), _resident((N_META, D)), _resident((2, D)), _resident((1, D)),
                  in_hbm, _resident((1, HEAD_DIM)), _resident((3, D)), in_hbm,
                  _resident((N_HEADS * HEAD_DIM, CHUNK))],
        out_specs=[tile(TM), meta_tile],
        out_shape=[jax.ShapeDtypeStruct((B, L, D), F32), jax.ShapeDtypeStruct((B, N_META, D), F32)],
        scratch_shapes=[pltpu.VMEM((D, N_IN_SECTIONS * D), BF16),
                        pltpu.VMEM((D, D), BF16),
                        pltpu.VMEM((2, D, WEIGHT_LOAD_COLS), F32),
                        pltpu.SemaphoreType.DMA((2,)),
                        pltpu.VMEM((N_HEADS, HEAD_DIM, HEAD_DIM), F32),
                        pltpu.VMEM((N_HEADS, HEAD_DIM, HEAD_DIM), F32),
                        pltpu.VMEM((SUBLANES, D), F32),
                        pltpu.VMEM((SUBLANES, D), F32),
                        pltpu.VMEM((TM, D), BF16),
                        pltpu.SMEM((N_GROUPS,), jnp.int32),
                        *(slot * N_GROUPS)],
        compiler_params=params,
        name="mixer",
    )(x, meta_tokens, lb_param, attn_norm_w, w_in[0], hgrn_norm_w, conv_w[0], w_out[0], _rsel_matrix())

    TF = FFN_TILE
    out = pl.pallas_call(
        _ffn_kernel,
        grid=(B, L // TF),
        in_specs=[tile(TF), meta_tile, _resident((1, D)), in_hbm, _resident((3, D_FF)),
                  _resident((1, D_FF)), in_hbm, _resident((1, D))],
        out_specs=tile(TF),
        out_shape=jax.ShapeDtypeStruct((B, L, D), F32),
        scratch_shapes=[pltpu.VMEM((D, 2 * D_FF), BF16),
                        pltpu.VMEM((D_FF, D), BF16),
                        pltpu.VMEM((2, D, WEIGHT_LOAD_COLS), F32),
                        pltpu.VMEM((2, WEIGHT_LOAD_ROWS, D), F32),
                        pltpu.SemaphoreType.DMA((2,)),
                        pltpu.VMEM((TF + SUBLANES, D_FF), F32)],
        compiler_params=params,
        name="ffn",
    )(h1, h1m, ffn_norm_w, w_up[0], ffn_conv_w[0], ffn_conv_b, w_down[0], final_norm_w.reshape(1, D))
    return out
```

```python
import numpy as np
import jax
import jax.numpy as jnp
from jax import lax
from jax.experimental import pallas as pl
from jax.experimental.pallas import tpu as pltpu

D_MODEL = 1024
N_META = 16
N_HEADS = 8
HEAD_DIM = 128
D_FF = 2816
N_IN_SECTIONS = 9
EPS = 1e-6

SUBLANES = 8
CHUNK = 128
GROUP_HEADS = 4
GROUP_WIDTH = GROUP_HEADS * HEAD_DIM
N_GROUPS = N_HEADS // GROUP_HEADS
MAX_SAFE_EXPONENT = 60.0
LOG2_E = 1.4426950408889634
MIXER_TILE = 512
FFN_TILE = 512
FFN_COL_BLOCK = 1408
WEIGHT_LOAD_ROWS = 32
W_DOWN_LOAD_ROWS = 352
VMEM_LIMIT_BYTES = 60 * 1024 * 1024

F32 = jnp.float32
BF16 = jnp.bfloat16

_NT = (((1,), (1,)), ((), ()))
_TN = (((0,), (0,)), ((), ()))


def _rms(x, w):
    ms = jnp.mean(x * x, axis=-1, keepdims=True)
    return x * lax.rsqrt(ms + EPS) * w


def _sigmoid(x):
    return 1.0 / (1.0 + jnp.exp2(x * -LOG2_E))


def _load_weight(src_hbm, dst_ref, stage_ref, sem_ref):
    rows, width = src_hbm.shape
    block = stage_ref.shape[1]
    assert rows % block == 0 and width <= stage_ref.shape[2]
    n = rows // block

    def copy(i):
        return pltpu.make_async_copy(src_hbm.at[pl.ds(i * block, block), :],
                                     stage_ref.at[i % 2, :, pl.ds(0, width)], sem_ref.at[i % 2])

    copy(0).start()
    for i in range(n):
        if i + 1 < n:
            copy(i + 1).start()
        copy(i).wait()
        dst_ref[i * block:(i + 1) * block, :] = stage_ref[i % 2, :, 0:width].astype(BF16)


def _first_step():
    return jnp.logical_and(pl.program_id(0) == 0, pl.program_id(1) == 0)


def _bcast_rows(ref, lanes, row0, block, offset, nrows):
    pieces = [jnp.broadcast_to(_row(ref, lanes, row0, i * block + offset), (block, HEAD_DIM))
              for i in range(nrows // block)]
    return pieces[0] if len(pieces) == 1 else jnp.concatenate(pieces, axis=0)


def _row(ref, lanes, row0, r):
    group = ref[pl.ds(row0 + r // SUBLANES * SUBLANES, SUBLANES), lanes]
    return group[r % SUBLANES:r % SUBLANES + 1, :]


def _hgrn_chunk_exact(row0, lanes, valid, st, q_ref, g_ref, k_ref, v_ref, rsel):
    rows = pl.ds(row0, CHUNK)
    q = q_ref[rows, lanes]
    G = g_ref[rows, lanes]
    k = k_ref[rows, lanes]
    v = v_ref[rows, lanes]
    row = lax.broadcasted_iota(jnp.int32, (CHUNK, HEAD_DIM), 0)

    sub = row & (SUBLANES - 1)
    ps = []
    for j in range(SUBLANES):
        gj = _bcast_rows(g_ref, lanes, row0, SUBLANES, j, CHUNK)
        kj = _bcast_rows(k_ref, lanes, row0, SUBLANES, j, CHUNK)
        p = q * kj * jnp.exp2(G - gj)
        ps.append(jnp.where(sub >= j, p, 0.0).astype(BF16))
    pcat = jnp.concatenate(ps, axis=1)
    a = jnp.dot(pcat, rsel, preferred_element_type=F32)

    tl_xor = (lax.broadcasted_iota(jnp.int32, (CHUNK, CHUNK), 0)
              ^ lax.broadcasted_iota(jnp.int32, (CHUNK, CHUNK), 1))
    a = jnp.where(tl_xor < SUBLANES, a, 0.0)

    b = 2 * SUBLANES
    while b <= CHUNK:
        gm = _bcast_rows(g_ref, lanes, row0, b, b // 2 - 1, CHUNK)
        e = jnp.exp2(-jnp.abs(G - gm))
        upper = (row & (b // 2)) != 0
        qt = jnp.where(upper, q * e, 0.0).astype(BF16)
        kt = jnp.where(upper, 0.0, k * e).astype(BF16)
        ab = lax.dot_general(qt, kt, _NT, preferred_element_type=F32)
        a = a + (ab if b == CHUNK else jnp.where(tl_xor < b, ab, 0.0))
        b *= 2

    qi = (q * jnp.exp2(G)).astype(BF16)
    o = lax.dot_general(qi, st.astype(BF16), _NT, preferred_element_type=F32)
    o = o + jnp.dot(a.astype(BF16), v, preferred_element_type=F32)

    glast = _row(g_ref, lanes, row0, valid - 1)
    kd = k * jnp.exp2(glast - G)
    if valid < CHUNK:
        kd = jnp.where(row < valid, kd, 0.0)
    st = st * jnp.exp2(glast) + lax.dot_general(v, kd.astype(BF16), _TN, preferred_element_type=F32)
    return o, st


def _hgrn_chunk_fast(row0, lanes, st, q_ref, g_ref, k_ref, v_ref):
    rows = pl.ds(row0, CHUNK)
    q = q_ref[rows, lanes]
    G = g_ref[rows, lanes]
    k = k_ref[rows, lanes]
    v = v_ref[rows, lanes]
    gmid = g_ref[pl.ds(row0 + CHUNK // 2 - 1, 1), lanes]
    glast = g_ref[pl.ds(row0 + CHUNK - 1, 1), lanes]
    d = G - gmid
    qh = q * jnp.exp2(d)
    kh = k * jnp.exp2(-d)
    a = lax.dot_general(qh.astype(BF16), kh.astype(BF16), _NT, preferred_element_type=F32)
    causal = (lax.broadcasted_iota(jnp.int32, (CHUNK, CHUNK), 1)
              <= lax.broadcasted_iota(jnp.int32, (CHUNK, CHUNK), 0))
    a = jnp.where(causal, a, 0.0)
    qi = (qh * jnp.exp2(gmid)).astype(BF16)
    o = lax.dot_general(qi, st.astype(BF16), _NT, preferred_element_type=F32)
    o = o + jnp.dot(a.astype(BF16), v, preferred_element_type=F32)
    kd = (kh * jnp.exp2(glast - gmid)).astype(BF16)
    st = st * jnp.exp2(glast) + lax.dot_general(v, kd, _TN, preferred_element_type=F32)
    return o, st


def _project_group(u, g, R, valid, prm, slot, recurrence_inputs):
    lb_ref, win_ref, cw_ref, zc_ref, safe_ref = prm
    q_ref, g_ref, k_ref, v_ref, gs_ref, sa_ref, mb_ref, zs_ref = slot
    c0 = g * GROUP_WIDTH
    cols = slice(c0, c0 + GROUP_WIDTH)

    def sec(i):
        w = win_ref[:, i * D_MODEL + c0:i * D_MODEL + c0 + GROUP_WIDTH]
        return jnp.dot(u, w, preferred_element_type=F32)

    if not recurrence_inputs:
        yield from _project_gates(sec, cols, R, valid, cw_ref, zc_ref, slot)
        return

    qv = sec(0)
    q_ref[0:R, :] = qv * _sigmoid(qv)
    yield

    f = lb_ref[0:1, cols] + lb_ref[1:2, cols] * _sigmoid(sec(1))
    k_ref[0:R, :] = 1.0 - f
    lf = jnp.log2(f)
    tri = (lax.broadcasted_iota(jnp.int32, (CHUNK, CHUNK), 1)
           <= lax.broadcasted_iota(jnp.int32, (CHUNK, CHUNK), 0)).astype(BF16)
    worst = jnp.zeros((1, GROUP_WIDTH), F32)
    for c in range(R // CHUNK):
        lfc = lf[c * CHUNK:(c + 1) * CHUNK, :]
        hi = lfc.astype(BF16)
        lo = (lfc - hi.astype(F32)).astype(BF16)
        gc = (jnp.dot(tri, hi, preferred_element_type=F32)
              + jnp.dot(tri, lo, preferred_element_type=F32))
        g_ref[c * CHUNK:(c + 1) * CHUNK, :] = gc
        gmid = gc[CHUNK // 2 - 1:CHUNK // 2, :]
        glast = gc[CHUNK - 1:CHUNK, :]
        worst = jnp.maximum(worst, jnp.maximum(-gmid, gmid - glast))
    safe_ref[g] = (jnp.max(worst) <= MAX_SAFE_EXPONENT * LOG2_E).astype(jnp.int32)
    yield

    v_ref[0:R, :] = sec(2).astype(BF16)
    yield


def _project_gates(sec, cols, R, valid, cw_ref, zc_ref, slot):
    gs_ref, sa_ref, mb_ref, zs_ref = slot[4:8]
    gv = sec(3)
    gs_ref[0:R, :] = gv * _sigmoid(gv)
    yield

    bg = sec(4)
    z = sec(5) * sec(6)
    zs_ref[0:SUBLANES, :] = zc_ref[:, cols]
    zs_ref[SUBLANES:SUBLANES + R, :] = z
    z1 = zs_ref[SUBLANES - 1:SUBLANES - 1 + R, :]
    z2 = zs_ref[SUBLANES - 2:SUBLANES - 2 + R, :]
    yb = bg * (cw_ref[0:1, cols] * z2 + cw_ref[1:2, cols] * z1 + cw_ref[2:3, cols] * z)
    zc_ref[:, cols] = zs_ref[valid:valid + SUBLANES, :]
    yield

    sa_ref[0:R, :] = _sigmoid(sec(7))
    yield
    mb_ref[0:R, :] = _sigmoid(sec(8)) * yb
    yield


def _finish_chunk(row0, h, lanes, o, hnw, mg_ref, slot):
    gs_ref, sa_ref, mb_ref = slot[4:7]
    rows = pl.ds(row0, CHUNK)
    m = _rms(o, hnw) * gs_ref[rows, lanes] * sa_ref[rows, lanes] + mb_ref[rows, lanes]
    mg_ref[rows, h * HEAD_DIM:(h + 1) * HEAD_DIM] = m.astype(BF16)


def _recur_group_fast(g, R, hnw, st_ref, st_old_ref, mg_ref, slot):
    q_ref, g_ref, k_ref, v_ref = slot[0:4]
    for hh in range(GROUP_HEADS):
        h = GROUP_HEADS * g + hh
        lanes = slice(hh * HEAD_DIM, (hh + 1) * HEAD_DIM)
        st = st_ref[h]
        st_old_ref[h] = st
        for c in range(R // CHUNK):
            o, st = _hgrn_chunk_fast(c * CHUNK, lanes, st, q_ref, g_ref, k_ref, v_ref)
            mg_ref[c * CHUNK:(c + 1) * CHUNK, h * HEAD_DIM:(h + 1) * HEAD_DIM] = _rms(o, hnw).astype(BF16)
            yield
        st_ref[h] = st


def _finish_group_fast(g, R, mg_ref, slot):
    gs_ref, sa_ref, mb_ref = slot[4:7]
    cols = slice(g * GROUP_WIDTH, (g + 1) * GROUP_WIDTH)
    m = mg_ref[0:R, cols].astype(F32) * gs_ref[0:R, :] * sa_ref[0:R, :] + mb_ref[0:R, :]
    mg_ref[0:R, cols] = m.astype(BF16)


def _recur_group_exact(g, R, valid, hnw, rsel, st_in_ref, st_ref, mg_ref, slot):
    q_ref, g_ref, k_ref, v_ref = slot[0:4]
    for hh in range(GROUP_HEADS):
        h = GROUP_HEADS * g + hh
        lanes = slice(hh * HEAD_DIM, (hh + 1) * HEAD_DIM)
        st = st_in_ref[h]
        if valid < R:
            o, st = _hgrn_chunk_exact(0, lanes, valid, st, q_ref, g_ref, k_ref, v_ref, rsel)
            _finish_chunk(0, h, lanes, o, hnw, mg_ref, slot)
        else:
            def chunk_body(c, st):
                row0 = pl.multiple_of(c * CHUNK, CHUNK)
                o, st = _hgrn_chunk_exact(row0, lanes, CHUNK, st, q_ref, g_ref, k_ref, v_ref, rsel)
                _finish_chunk(row0, h, lanes, o, hnw, mg_ref, slot)
                return st

            st = lax.fori_loop(0, R // CHUNK, chunk_body, st)
        st_ref[h] = st


def _interleave(*gens):
    gens = list(gens)
    while gens:
        for g in list(gens):
            try:
                next(g)
            except StopIteration:
                gens.remove(g)


def _out_partial(g, R, mg_ref, wout_ref):
    rows = slice(g * GROUP_WIDTH, (g + 1) * GROUP_WIDTH)
    return jnp.dot(mg_ref[0:R, rows], wout_ref[rows, :], preferred_element_type=F32)


def _mixer_tile(load_x, store_out, R, valid, is_meta, refs):
    (lbp_ref, anw_ref, win_ref, hnw_ref, cw_ref, wout_ref, rsel_ref,
     st_ref, st_old_ref, zc_ref, lb_ref, mg_ref, safe_ref, slots) = refs
    prm = (lb_ref, win_ref, cw_ref, zc_ref, safe_ref)
    u = _rms(load_x(), anw_ref[...]).astype(BF16)
    rsel = rsel_ref[...]
    hnw = hnw_ref[...]
    project = lambda g, first: _interleave(_project_group(u, g, R, valid, prm, slots[g], first))

    lbp = lbp_ref[...]
    mx = jnp.max(lbp, axis=0, keepdims=True)
    ex = jnp.exp(lbp - mx)
    lb = ex[0:1, :] / jnp.sum(ex, axis=0, keepdims=True)
    lb_ref[0:1, :] = lb
    lb_ref[1:2, :] = 1.0 - lb

    if is_meta:
        for g in range(N_GROUPS):
            project(g, True)
            project(g, False)
            _recur_group_exact(g, R, valid, hnw, rsel, st_ref, st_ref, mg_ref, slots[g])
        acc = load_x()
        for g in range(N_GROUPS):
            acc = acc + _out_partial(g, R, mg_ref, wout_ref)
        store_out(acc, False)
        return

    def store_early():
        acc = load_x()
        for g in range(N_GROUPS - 1):
            acc = acc + _out_partial(g, R, mg_ref, wout_ref)
        store_out(acc, False)

    for g in range(N_GROUPS):
        project(g, True)
        if g > 0:
            _finish_group_fast(g - 1, R, mg_ref, slots[g - 1])
            if g == N_GROUPS - 1:
                store_early()
        _interleave(_recur_group_fast(g, R, hnw, st_ref, st_old_ref, mg_ref, slots[g]))
        project(g, False)
    _finish_group_fast(N_GROUPS - 1, R, mg_ref, slots[N_GROUPS - 1])
    for g in range(N_GROUPS):
        def redo(g=g):
            _recur_group_exact(g, R, valid, hnw, rsel, st_old_ref, st_ref, mg_ref, slots[g])
            if g < N_GROUPS - 1:
                store_early()

        pl.when(safe_ref[g] == 0)(redo)
    store_out(_out_partial(N_GROUPS - 1, R, mg_ref, wout_ref), True)


def _mixer_kernel(x_ref, meta_ref, lbp_ref, anw_ref, win_hbm, hnw_ref, cw_ref, wout_hbm, rsel_ref,
                  h1_ref, h1m_ref, win_ref, wout_ref, stage_ref, sem_ref,
                  st_ref, st_old_ref, zc_ref, lb_ref, mg_ref, safe_ref, *slots):
    @pl.when(_first_step())
    def _():
        _load_weight(win_hbm, win_ref, stage_ref, sem_ref)
        _load_weight(wout_hbm, wout_ref, stage_ref, sem_ref)

    n = len(slots) // N_GROUPS
    refs = (lbp_ref, anw_ref, win_ref, hnw_ref, cw_ref, wout_ref, rsel_ref,
            st_ref, st_old_ref, zc_ref, lb_ref, mg_ref, safe_ref,
            [slots[i * n:(i + 1) * n] for i in range(N_GROUPS)])

    @pl.when(pl.program_id(1) == 0)
    def _():
        st_ref[...] = jnp.zeros_like(st_ref)
        zc_ref[...] = jnp.zeros_like(zc_ref)
        load_meta = lambda: jnp.concatenate(
            [meta_ref[...], jnp.zeros((CHUNK - N_META, D_MODEL), F32)], axis=0)

        def store_meta(val, accumulate):
            h1m_ref[0] = val[0:N_META, :]

        _mixer_tile(load_meta, store_meta, CHUNK, N_META, True, refs)

    def store_tile(val, accumulate):
        h1_ref[0] = h1_ref[0] + val if accumulate else val

    _mixer_tile(lambda: x_ref[0], store_tile, MIXER_TILE, MIXER_TILE, False, refs)


def _ffn_kernel(h1_ref, h1m_ref, fnw_ref, wup_hbm, fcw_ref, fcb_ref, wdn_hbm, finw_ref,
                out_ref, wup_ref, wdn_ref, stage_up_ref, stage_dn_ref, sem_ref, as_ref):
    T = FFN_TILE
    fnw = fnw_ref[...]

    @pl.when(_first_step())
    def _():
        _load_weight(wup_hbm, wup_ref, stage_up_ref, sem_ref)
        _load_weight(wdn_hbm, wdn_ref, stage_dn_ref, sem_ref)

    @pl.when(pl.program_id(1) == 0)
    def _():
        um = _rms(h1m_ref[0], fnw).astype(BF16)
        am = jnp.dot(um, wup_ref[:, 0:D_FF], preferred_element_type=F32)
        as_ref[0:SUBLANES, :] = am[N_META - SUBLANES:N_META, :]

    x = h1_ref[0]
    u = _rms(x, fnw).astype(BF16)
    y = x
    for cb in range(D_FF // FFN_COL_BLOCK):
        cols = slice(cb * FFN_COL_BLOCK, (cb + 1) * FFN_COL_BLOCK)
        a = jnp.dot(u, wup_ref[:, cols], preferred_element_type=F32)
        as_ref[SUBLANES:SUBLANES + T, cols] = a
        a1 = as_ref[SUBLANES - 1:SUBLANES - 1 + T, cols]
        a2 = as_ref[SUBLANES - 2:SUBLANES - 2 + T, cols]
        ac = (fcw_ref[0:1, cols] * a2 + fcw_ref[1:2, cols] * a1 + fcw_ref[2:3, cols] * a
              + fcb_ref[:, cols])
        as_ref[0:SUBLANES, cols] = as_ref[T:T + SUBLANES, cols]
        val = jnp.dot(u, wup_ref[:, D_FF + cb * FFN_COL_BLOCK:D_FF + (cb + 1) * FFN_COL_BLOCK],
                      preferred_element_type=F32)
        gated = (ac * _sigmoid(ac) * val).astype(BF16)
        y = y + jnp.dot(gated, wdn_ref[cols, :], preferred_element_type=F32)
    out_ref[0] = _rms(y, finw_ref[...])


def _resident(shape):
    return pl.BlockSpec(shape, lambda b, t: (0,) * len(shape), pipeline_mode=pl.Buffered(1))


def _rsel_matrix():
    r = np.arange(N_HEADS * HEAD_DIM)[:, None] // HEAD_DIM
    l = np.arange(CHUNK)[None, :] % SUBLANES
    return jnp.asarray(r == l, dtype=BF16)


def kernel(x, meta_tokens, lb_param, attn_norm_w, w_in, hgrn_norm_w, conv_w, w_out, ffn_norm_w, w_up,
           ffn_conv_w, ffn_conv_b, w_down, final_norm_w):
    B, L, D = x.shape
    assert D == D_MODEL and L % MIXER_TILE == 0 and L % FFN_TILE == 0
    assert w_in.shape == (1, D, N_IN_SECTIONS * D) and w_up.shape == (1, D, 2 * D_FF)
    assert SUBLANES * HEAD_DIM == D_MODEL and N_HEADS == SUBLANES

    tile = lambda T: pl.BlockSpec((1, T, D), lambda b, t: (b, t, 0))
    meta_tile = pl.BlockSpec((1, N_META, D), lambda b, t: (b, 0, 0))
    in_hbm = pl.BlockSpec(memory_space=pl.ANY)
    params = pltpu.CompilerParams(dimension_semantics=("arbitrary", "arbitrary"),
                                  vmem_limit_bytes=VMEM_LIMIT_BYTES)

    TM = MIXER_TILE
    slot = [pltpu.VMEM((TM, GROUP_WIDTH), F32)] * 3 + [pltpu.VMEM((TM, GROUP_WIDTH), BF16)] \
        + [pltpu.VMEM((TM, GROUP_WIDTH), F32)] * 3 + [pltpu.VMEM((TM + SUBLANES, GROUP_WIDTH), F32)]
    h1, h1m = pl.pallas_call(
        _mixer_kernel,
        grid=(B, L // TM),
        in_specs=[tile(TM), _resident((N_META, D)), _resident((2, D)), _resident((1, D)),
                  in_hbm, _resident((1, HEAD_DIM)), _resident((3, D)), in_hbm,
                  _resident((N_HEADS * HEAD_DIM, CHUNK))],
        out_specs=[tile(TM), meta_tile],
        out_shape=[jax.ShapeDtypeStruct((B, L, D), F32), jax.ShapeDtypeStruct((B, N_META, D), F32)],
        scratch_shapes=[pltpu.VMEM((D, N_IN_SECTIONS * D), BF16),
                        pltpu.VMEM((D, D), BF16),
                        pltpu.VMEM((2, WEIGHT_LOAD_ROWS, N_IN_SECTIONS * D), F32),
                        pltpu.SemaphoreType.DMA((2,)),
                        pltpu.VMEM((N_HEADS, HEAD_DIM, HEAD_DIM), F32),
                        pltpu.VMEM((N_HEADS, HEAD_DIM, HEAD_DIM), F32),
                        pltpu.VMEM((SUBLANES, D), F32),
                        pltpu.VMEM((SUBLANES, D), F32),
                        pltpu.VMEM((TM, D), BF16),
                        pltpu.SMEM((N_GROUPS,), jnp.int32),
                        *(slot * N_GROUPS)],
        compiler_params=params,
        name="mixer",
    )(x, meta_tokens, lb_param, attn_norm_w, w_in[0], hgrn_norm_w, conv_w[0], w_out[0], _rsel_matrix())

    TF = FFN_TILE
    out = pl.pallas_call(
        _ffn_kernel,
        grid=(B, L // TF),
        in_specs=[tile(TF), meta_tile, _resident((1, D)), in_hbm, _resident((3, D_FF)),
                  _resident((1, D_FF)), in_hbm, _resident((1, D))],
        out_specs=tile(TF),
        out_shape=jax.ShapeDtypeStruct((B, L, D), F32),
        scratch_shapes=[pltpu.VMEM((D, 2 * D_FF), BF16),
                        pltpu.VMEM((D_FF, D), BF16),
                        pltpu.VMEM((2, WEIGHT_LOAD_ROWS, 2 * D_FF), F32),
                        pltpu.VMEM((2, W_DOWN_LOAD_ROWS, D), F32),
                        pltpu.SemaphoreType.DMA((2,)),
                        pltpu.VMEM((TF + SUBLANES, D_FF), F32)],
        compiler_params=params,
        name="ffn",
    )(h1, h1m, ffn_norm_w, w_up[0], ffn_conv_w[0], ffn_conv_b, w_down[0], final_norm_w.reshape(1, D))
    return out
```

```python
import numpy as np
import jax
import jax.numpy as jnp
from jax import lax
from jax.experimental import pallas as pl
from jax.experimental.pallas import tpu as pltpu

D_MODEL = 1024
N_META = 16
N_HEADS = 8
HEAD_DIM = 128
D_FF = 2816
N_IN_SECTIONS = 9
EPS = 1e-6

SUBLANES = 8
CHUNK = 128
GROUP_HEADS = 4
GROUP_WIDTH = GROUP_HEADS * HEAD_DIM
N_GROUPS = N_HEADS // GROUP_HEADS
FAST_BLOCK = 2 * CHUNK
MAX_SAFE_EXPONENT = 80.0
LOG2_E = 1.4426950408889634
MIXER_TILE = 512
FFN_TILE = 512
FFN_COL_BLOCK = 1408
WEIGHT_LANE_PAD = 128
VMEM_LIMIT_BYTES = 60 * 1024 * 1024

F32 = jnp.float32
BF16 = jnp.bfloat16

_NT = (((1,), (1,)), ((), ()))
_TN = (((0,), (0,)), ((), ()))


def _rms(x, w):
    ms = jnp.mean(x * x, axis=-1, keepdims=True)
    return x * lax.rsqrt(ms + EPS) * w


def _sigmoid(x):
    return 1.0 / (1.0 + jnp.exp2(x * -LOG2_E))


def _bcast_rows(ref, lanes, row0, block, offset, nrows):
    pieces = [jnp.broadcast_to(_row(ref, lanes, row0, i * block + offset), (block, HEAD_DIM))
              for i in range(nrows // block)]
    return pieces[0] if len(pieces) == 1 else jnp.concatenate(pieces, axis=0)


def _row(ref, lanes, row0, r):
    group = ref[pl.ds(row0 + r // SUBLANES * SUBLANES, SUBLANES), lanes]
    return group[r % SUBLANES:r % SUBLANES + 1, :]


def _hgrn_chunk_exact(row0, lanes, valid, st, q_ref, g_ref, k_ref, v_ref, rsel):
    rows = pl.ds(row0, CHUNK)
    q = q_ref[rows, lanes]
    G = g_ref[rows, lanes]
    k = k_ref[rows, lanes]
    v = v_ref[rows, lanes]
    row = lax.broadcasted_iota(jnp.int32, (CHUNK, HEAD_DIM), 0)

    sub = row & (SUBLANES - 1)
    ps = []
    for j in range(SUBLANES):
        gj = _bcast_rows(g_ref, lanes, row0, SUBLANES, j, CHUNK)
        kj = _bcast_rows(k_ref, lanes, row0, SUBLANES, j, CHUNK)
        p = q * kj * jnp.exp2(G - gj)
        ps.append(jnp.where(sub >= j, p, 0.0).astype(BF16))
    pcat = jnp.concatenate(ps, axis=1)
    a = jnp.dot(pcat, rsel, preferred_element_type=F32)

    tl_xor = (lax.broadcasted_iota(jnp.int32, (CHUNK, CHUNK), 0)
              ^ lax.broadcasted_iota(jnp.int32, (CHUNK, CHUNK), 1))
    a = jnp.where(tl_xor < SUBLANES, a, 0.0)

    b = 2 * SUBLANES
    while b <= CHUNK:
        gm = _bcast_rows(g_ref, lanes, row0, b, b // 2 - 1, CHUNK)
        e = jnp.exp2(-jnp.abs(G - gm))
        upper = (row & (b // 2)) != 0
        qt = jnp.where(upper, q * e, 0.0).astype(BF16)
        kt = jnp.where(upper, 0.0, k * e).astype(BF16)
        ab = lax.dot_general(qt, kt, _NT, preferred_element_type=F32)
        a = a + (ab if b == CHUNK else jnp.where(tl_xor < b, ab, 0.0))
        b *= 2

    qi = (q * jnp.exp2(G)).astype(BF16)
    o = lax.dot_general(qi, st.astype(BF16), _NT, preferred_element_type=F32)
    o = o + jnp.dot(a.astype(BF16), v, preferred_element_type=F32)

    glast = _row(g_ref, lanes, row0, valid - 1)
    kd = k * jnp.exp2(glast - G)
    if valid < CHUNK:
        kd = jnp.where(row < valid, kd, 0.0)
    st = st * jnp.exp2(glast) + lax.dot_general(v, kd.astype(BF16), _TN, preferred_element_type=F32)
    return o, st


def _hgrn_chunk_fast(row0, lanes, st, q_ref, g_ref, k_ref, v_ref):
    rows = pl.ds(row0, FAST_BLOCK)
    q = q_ref[rows, lanes]
    k = k_ref[rows, lanes]
    v = v_ref[rows, lanes]
    g1last = g_ref[pl.ds(row0 + CHUNK - 1, 1), lanes]
    g2last = g_ref[pl.ds(row0 + FAST_BLOCK - 1, 1), lanes]
    d = jnp.concatenate([g_ref[pl.ds(row0, CHUNK), lanes] - g1last,
                         g_ref[pl.ds(row0 + CHUNK, CHUNK), lanes]], axis=0)
    qh = q * jnp.exp2(d)
    kh = k * jnp.exp2(-d)
    a = lax.dot_general(qh.astype(BF16), kh.astype(BF16), _NT, preferred_element_type=F32)
    causal = (lax.broadcasted_iota(jnp.int32, (FAST_BLOCK, FAST_BLOCK), 1)
              <= lax.broadcasted_iota(jnp.int32, (FAST_BLOCK, FAST_BLOCK), 0))
    a = jnp.where(causal, a, 0.0)
    qi = (qh * jnp.exp2(g1last)).astype(BF16)
    o = lax.dot_general(qi, st.astype(BF16), _NT, preferred_element_type=F32)
    o = o + jnp.dot(a.astype(BF16), v, preferred_element_type=F32)
    kd = (kh * jnp.exp2(g2last)).astype(BF16)
    st = st * jnp.exp2(g1last + g2last) + lax.dot_general(v, kd, _TN, preferred_element_type=F32)
    return o, st


def _project_group(u, g, R, valid, prm, slot, recurrence_inputs):
    lb_ref, win_ref, cw_ref, zc_ref, safe_ref = prm
    q_ref, g_ref, k_ref, v_ref, gs_ref, sa_ref, mb_ref, zs_ref = slot
    c0 = g * GROUP_WIDTH
    cols = slice(c0, c0 + GROUP_WIDTH)

    def sec(i):
        w = win_ref[:, i * D_MODEL + c0:i * D_MODEL + c0 + GROUP_WIDTH]
        return jnp.dot(u, w, preferred_element_type=F32)

    if not recurrence_inputs:
        yield from _project_gates(sec, cols, R, valid, cw_ref, zc_ref, slot)
        return

    qv = sec(0)
    q_ref[0:R, :] = qv * _sigmoid(qv)
    yield

    f = lb_ref[0:1, cols] + lb_ref[1:2, cols] * _sigmoid(sec(1))
    k_ref[0:R, :] = 1.0 - f
    lf = jnp.log2(f)
    tri = (lax.broadcasted_iota(jnp.int32, (CHUNK, CHUNK), 1)
           <= lax.broadcasted_iota(jnp.int32, (CHUNK, CHUNK), 0)).astype(BF16)
    worst = jnp.zeros((1, GROUP_WIDTH), F32)
    for c in range(R // CHUNK):
        lfc = lf[c * CHUNK:(c + 1) * CHUNK, :]
        hi = lfc.astype(BF16)
        lo = (lfc - hi.astype(F32)).astype(BF16)
        gc = (jnp.dot(tri, hi, preferred_element_type=F32)
              + jnp.dot(tri, lo, preferred_element_type=F32))
        g_ref[c * CHUNK:(c + 1) * CHUNK, :] = gc
        worst = jnp.maximum(worst, -gc[CHUNK - 1:CHUNK, :])
    safe_ref[g] = (jnp.max(worst) <= MAX_SAFE_EXPONENT * LOG2_E).astype(jnp.int32)
    yield

    v_ref[0:R, :] = sec(2).astype(BF16)
    yield


def _project_gates(sec, cols, R, valid, cw_ref, zc_ref, slot):
    gs_ref, sa_ref, mb_ref, zs_ref = slot[4:8]
    gv = sec(3)
    gs_ref[0:R, :] = gv * _sigmoid(gv)
    yield

    bg = sec(4)
    z = sec(5) * sec(6)
    zs_ref[0:SUBLANES, :] = zc_ref[:, cols]
    zs_ref[SUBLANES:SUBLANES + R, :] = z
    z1 = zs_ref[SUBLANES - 1:SUBLANES - 1 + R, :]
    z2 = zs_ref[SUBLANES - 2:SUBLANES - 2 + R, :]
    yb = bg * (cw_ref[0:1, cols] * z2 + cw_ref[1:2, cols] * z1 + cw_ref[2:3, cols] * z)
    zc_ref[:, cols] = zs_ref[valid:valid + SUBLANES, :]
    yield

    sa_ref[0:R, :] = _sigmoid(sec(7))
    yield
    mb_ref[0:R, :] = _sigmoid(sec(8)) * yb
    yield


def _finish_chunk(row0, h, lanes, o, hnw, mg_ref, slot):
    gs_ref, sa_ref, mb_ref = slot[4:7]
    rows = pl.ds(row0, CHUNK)
    m = _rms(o, hnw) * gs_ref[rows, lanes] * sa_ref[rows, lanes] + mb_ref[rows, lanes]
    mg_ref[rows, h * HEAD_DIM:(h + 1) * HEAD_DIM] = m.astype(BF16)


def _recur_group_fast(g, R, hnw, st_ref, st_old_ref, mg_ref, slot):
    q_ref, g_ref, k_ref, v_ref = slot[0:4]
    for hh in range(GROUP_HEADS):
        h = GROUP_HEADS * g + hh
        lanes = slice(hh * HEAD_DIM, (hh + 1) * HEAD_DIM)
        st = st_ref[h]
        st_old_ref[h] = st
        for c in range(R // FAST_BLOCK):
            rows = slice(c * FAST_BLOCK, (c + 1) * FAST_BLOCK)
            o, st = _hgrn_chunk_fast(c * FAST_BLOCK, lanes, st, q_ref, g_ref, k_ref, v_ref)
            mg_ref[rows, h * HEAD_DIM:(h + 1) * HEAD_DIM] = _rms(o, hnw).astype(BF16)
            yield
        st_ref[h] = st


def _finish_group_fast(g, R, mg_ref, slot):
    gs_ref, sa_ref, mb_ref = slot[4:7]
    cols = slice(g * GROUP_WIDTH, (g + 1) * GROUP_WIDTH)
    m = mg_ref[0:R, cols].astype(F32) * gs_ref[0:R, :] * sa_ref[0:R, :] + mb_ref[0:R, :]
    mg_ref[0:R, cols] = m.astype(BF16)


def _recur_group_exact(g, R, valid, hnw, rsel, st_in_ref, st_ref, mg_ref, slot):
    q_ref, g_ref, k_ref, v_ref = slot[0:4]
    for hh in range(GROUP_HEADS):
        h = GROUP_HEADS * g + hh
        lanes = slice(hh * HEAD_DIM, (hh + 1) * HEAD_DIM)
        st = st_in_ref[h]
        if valid < R:
            o, st = _hgrn_chunk_exact(0, lanes, valid, st, q_ref, g_ref, k_ref, v_ref, rsel)
            _finish_chunk(0, h, lanes, o, hnw, mg_ref, slot)
        else:
            def chunk_body(c, st):
                row0 = pl.multiple_of(c * CHUNK, CHUNK)
                o, st = _hgrn_chunk_exact(row0, lanes, CHUNK, st, q_ref, g_ref, k_ref, v_ref, rsel)
                _finish_chunk(row0, h, lanes, o, hnw, mg_ref, slot)
                return st

            st = lax.fori_loop(0, R // CHUNK, chunk_body, st)
        st_ref[h] = st


def _interleave(*gens):
    gens = list(gens)
    while gens:
        for g in list(gens):
            try:
                next(g)
            except StopIteration:
                gens.remove(g)


def _out_partial(g, R, mg_ref, wout_ref):
    rows = slice(g * GROUP_WIDTH, (g + 1) * GROUP_WIDTH)
    return jnp.dot(mg_ref[0:R, rows], wout_ref[rows, 0:D_MODEL], preferred_element_type=F32)


def _mixer_tile(load_x, store_out, R, valid, is_meta, refs):
    (lbp_ref, anw_ref, win_ref, hnw_ref, cw_ref, wout_ref, rsel_ref,
     st_ref, st_old_ref, zc_ref, lb_ref, mg_ref, safe_ref, slots) = refs
    prm = (lb_ref, win_ref, cw_ref, zc_ref, safe_ref)
    u = _rms(load_x(), anw_ref[...]).astype(BF16)
    rsel = rsel_ref[...]
    hnw = hnw_ref[...]
    project = lambda g, first: _interleave(_project_group(u, g, R, valid, prm, slots[g], first))

    lbp = lbp_ref[...]
    mx = jnp.max(lbp, axis=0, keepdims=True)
    ex = jnp.exp(lbp - mx)
    lb = ex[0:1, :] / jnp.sum(ex, axis=0, keepdims=True)
    lb_ref[0:1, :] = lb
    lb_ref[1:2, :] = 1.0 - lb

    if is_meta:
        for g in range(N_GROUPS):
            project(g, True)
            project(g, False)
            _recur_group_exact(g, R, valid, hnw, rsel, st_ref, st_ref, mg_ref, slots[g])
        acc = load_x()
        for g in range(N_GROUPS):
            acc = acc + _out_partial(g, R, mg_ref, wout_ref)
        store_out(acc, False)
        return

    def store_early():
        acc = load_x()
        for g in range(N_GROUPS - 1):
            acc = acc + _out_partial(g, R, mg_ref, wout_ref)
        store_out(acc, False)

    for g in range(N_GROUPS):
        project(g, True)
        if g > 0:
            _finish_group_fast(g - 1, R, mg_ref, slots[g - 1])
            if g == N_GROUPS - 1:
                store_early()
        _interleave(_recur_group_fast(g, R, hnw, st_ref, st_old_ref, mg_ref, slots[g]))
        project(g, False)
    _finish_group_fast(N_GROUPS - 1, R, mg_ref, slots[N_GROUPS - 1])
    for g in range(N_GROUPS):
        def redo(g=g):
            _recur_group_exact(g, R, valid, hnw, rsel, st_old_ref, st_ref, mg_ref, slots[g])
            if g < N_GROUPS - 1:
                store_early()

        pl.when(safe_ref[g] == 0)(redo)
    store_out(_out_partial(N_GROUPS - 1, R, mg_ref, wout_ref), True)


def _mixer_kernel(x_ref, meta_ref, lbp_ref, anw_ref, win_ref, hnw_ref, cw_ref, wout_ref, rsel_ref,
                  h1_ref, h1m_ref, st_ref, st_old_ref, zc_ref, lb_ref, mg_ref, safe_ref, *slots):
    n = len(slots) // N_GROUPS
    refs = (lbp_ref, anw_ref, win_ref, hnw_ref, cw_ref, wout_ref, rsel_ref,
            st_ref, st_old_ref, zc_ref, lb_ref, mg_ref, safe_ref,
            [slots[i * n:(i + 1) * n] for i in range(N_GROUPS)])

    @pl.when(pl.program_id(1) == 0)
    def _():
        st_ref[...] = jnp.zeros_like(st_ref)
        zc_ref[...] = jnp.zeros_like(zc_ref)
        load_meta = lambda: jnp.concatenate(
            [meta_ref[...], jnp.zeros((CHUNK - N_META, D_MODEL), F32)], axis=0)

        def store_meta(val, accumulate):
            h1m_ref[0] = val[0:N_META, :]

        _mixer_tile(load_meta, store_meta, CHUNK, N_META, True, refs)

    def store_tile(val, accumulate):
        h1_ref[0] = h1_ref[0] + val if accumulate else val

    _mixer_tile(lambda: x_ref[0], store_tile, MIXER_TILE, MIXER_TILE, False, refs)


def _ffn_kernel(h1_ref, h1m_ref, fnw_ref, wup_ref, fcw_ref, fcb_ref, wdn_ref, finw_ref,
                out_ref, as_ref):
    T = FFN_TILE
    fnw = fnw_ref[...]

    @pl.when(pl.program_id(1) == 0)
    def _():
        um = _rms(h1m_ref[0], fnw).astype(BF16)
        am = jnp.dot(um, wup_ref[:, 0:D_FF], preferred_element_type=F32)
        as_ref[0:SUBLANES, :] = am[N_META - SUBLANES:N_META, :]

    x = h1_ref[0]
    u = _rms(x, fnw).astype(BF16)
    y = x
    for cb in range(D_FF // FFN_COL_BLOCK):
        cols = slice(cb * FFN_COL_BLOCK, (cb + 1) * FFN_COL_BLOCK)
        a = jnp.dot(u, wup_ref[:, cols], preferred_element_type=F32)
        as_ref[SUBLANES:SUBLANES + T, cols] = a
        a1 = as_ref[SUBLANES - 1:SUBLANES - 1 + T, cols]
        a2 = as_ref[SUBLANES - 2:SUBLANES - 2 + T, cols]
        ac = (fcw_ref[0:1, cols] * a2 + fcw_ref[1:2, cols] * a1 + fcw_ref[2:3, cols] * a
              + fcb_ref[:, cols])
        as_ref[0:SUBLANES, cols] = as_ref[T:T + SUBLANES, cols]
        val = jnp.dot(u, wup_ref[:, D_FF + cb * FFN_COL_BLOCK:D_FF + (cb + 1) * FFN_COL_BLOCK],
                      preferred_element_type=F32)
        gated = (ac * _sigmoid(ac) * val).astype(BF16)
        y = y + jnp.dot(gated, wdn_ref[cols, 0:D_MODEL], preferred_element_type=F32)
    out_ref[0] = _rms(y, finw_ref[...])


def _resident(shape):
    return pl.BlockSpec(shape, lambda b, t: (0,) * len(shape), pipeline_mode=pl.Buffered(1))


def _weight(w):
    return jnp.pad(w.astype(BF16), ((0, 0), (0, WEIGHT_LANE_PAD)))


def _rsel_matrix():
    r = np.arange(N_HEADS * HEAD_DIM)[:, None] // HEAD_DIM
    l = np.arange(CHUNK)[None, :] % SUBLANES
    return jnp.asarray(r == l, dtype=BF16)


def kernel(x, meta_tokens, lb_param, attn_norm_w, w_in, hgrn_norm_w, conv_w, w_out, ffn_norm_w, w_up,
           ffn_conv_w, ffn_conv_b, w_down, final_norm_w):
    B, L, D = x.shape
    assert D == D_MODEL and L % MIXER_TILE == 0 and L % FFN_TILE == 0 and MIXER_TILE % FAST_BLOCK == 0
    assert w_in.shape == (1, D, N_IN_SECTIONS * D) and w_up.shape == (1, D, 2 * D_FF)
    assert SUBLANES * HEAD_DIM == D_MODEL and N_HEADS == SUBLANES

    tile = lambda T: pl.BlockSpec((1, T, D), lambda b, t: (b, t, 0))
    meta_tile = pl.BlockSpec((1, N_META, D), lambda b, t: (b, 0, 0))
    params = pltpu.CompilerParams(dimension_semantics=("arbitrary", "arbitrary"),
                                  vmem_limit_bytes=VMEM_LIMIT_BYTES)

    TM = MIXER_TILE
    slot = [pltpu.VMEM((TM, GROUP_WIDTH), F32)] * 3 + [pltpu.VMEM((TM, GROUP_WIDTH), BF16)] \
        + [pltpu.VMEM((TM, GROUP_WIDTH), F32)] * 3 + [pltpu.VMEM((TM + SUBLANES, GROUP_WIDTH), F32)]
    h1, h1m = pl.pallas_call(
        _mixer_kernel,
        grid=(B, L // TM),
        in_specs=[tile(TM), _resident((N_META, D)), _resident((2, D)), _resident((1, D)),
                  _resident((D, N_IN_SECTIONS * D + WEIGHT_LANE_PAD)), _resident((1, HEAD_DIM)),
                  _resident((3, D)), _resident((D, D + WEIGHT_LANE_PAD)),
                  _resident((N_HEADS * HEAD_DIM, CHUNK))],
        out_specs=[tile(TM), meta_tile],
        out_shape=[jax.ShapeDtypeStruct((B, L, D), F32), jax.ShapeDtypeStruct((B, N_META, D), F32)],
        scratch_shapes=[pltpu.VMEM((N_HEADS, HEAD_DIM, HEAD_DIM), F32),
                        pltpu.VMEM((N_HEADS, HEAD_DIM, HEAD_DIM), F32),
                        pltpu.VMEM((SUBLANES, D), F32),
                        pltpu.VMEM((SUBLANES, D), F32),
                        pltpu.VMEM((TM, D), BF16),
                        pltpu.SMEM((N_GROUPS,), jnp.int32),
                        *(slot * N_GROUPS)],
        compiler_params=params,
        name="mixer",
    )(x, meta_tokens, lb_param, attn_norm_w, _weight(w_in[0]), hgrn_norm_w, conv_w[0],
      _weight(w_out[0]), _rsel_matrix())

    TF = FFN_TILE
    out = pl.pallas_call(
        _ffn_kernel,
        grid=(B, L // TF),
        in_specs=[tile(TF), meta_tile, _resident((1, D)), _resident((D, 2 * D_FF)), _resident((3, D_FF)),
                  _resident((1, D_FF)), _resident((D_FF, D + WEIGHT_LANE_PAD)), _resident((1, D))],
        out_specs=tile(TF),
        out_shape=jax.ShapeDtypeStruct((B, L, D), F32),
        scratch_shapes=[pltpu.VMEM((TF + SUBLANES, D_FF), F32)],
        compiler_params=params,
        name="ffn",
    )(h1, h1m, ffn_norm_w, w_up[0].astype(BF16), ffn_conv_w[0], ffn_conv_b, _weight(w_down[0]),
      final_norm_w.reshape(1, D))
    return out
```

```python
import numpy as np
import jax
import jax.numpy as jnp
from jax import lax
from jax.experimental import pallas as pl
from jax.experimental.pallas import tpu as pltpu

D_MODEL = 1024
N_META = 16
N_HEADS = 8
HEAD_DIM = 128
D_FF = 2816
N_IN_SECTIONS = 9
EPS = 1e-6

SUBLANES = 8
CHUNK = 128
GROUP_HEADS = 4
GROUP_WIDTH = GROUP_HEADS * HEAD_DIM
N_GROUPS = N_HEADS // GROUP_HEADS
MAX_SAFE_EXPONENT = 60.0
LOG2_E = 1.4426950408889634
MIXER_TILE = 512
FFN_TILE = 512
FFN_COL_BLOCK = 1408
WEIGHT_LANE_PAD = 128
VMEM_LIMIT_BYTES = 60 * 1024 * 1024

F32 = jnp.float32
BF16 = jnp.bfloat16

_NT = (((1,), (1,)), ((), ()))
_TN = (((0,), (0,)), ((), ()))


def _rms(x, w):
    ms = jnp.mean(x * x, axis=-1, keepdims=True)
    return x * lax.rsqrt(ms + EPS) * w


def _sigmoid(x):
    return 1.0 / (1.0 + jnp.exp2(x * -LOG2_E))


def _bcast_rows(ref, lanes, row0, block, offset, nrows):
    pieces = [jnp.broadcast_to(_row(ref, lanes, row0, i * block + offset), (block, HEAD_DIM))
              for i in range(nrows // block)]
    return pieces[0] if len(pieces) == 1 else jnp.concatenate(pieces, axis=0)


def _row(ref, lanes, row0, r):
    group = ref[pl.ds(row0 + r // SUBLANES * SUBLANES, SUBLANES), lanes]
    return group[r % SUBLANES:r % SUBLANES + 1, :]


def _intra_plus_inter(a, qi, v, st):
    lhs = jnp.concatenate([a.astype(BF16), qi], axis=1)
    rhs = jnp.concatenate([v, st.astype(BF16)], axis=0)
    return jnp.dot(lhs, rhs, preferred_element_type=F32)


def _col_bcast(row):
    return jnp.transpose(jnp.broadcast_to(row, (HEAD_DIM, HEAD_DIM)))


def _hgrn_chunk_exact(row0, lanes, valid, st, q_ref, g_ref, k_ref, v_ref, rsel):
    rows = pl.ds(row0, CHUNK)
    q = q_ref[rows, lanes]
    G = g_ref[rows, lanes]
    k = k_ref[rows, lanes]
    v = v_ref[rows, lanes]
    row = lax.broadcasted_iota(jnp.int32, (CHUNK, HEAD_DIM), 0)

    sub = row & (SUBLANES - 1)
    ps = []
    for j in range(SUBLANES):
        gj = _bcast_rows(g_ref, lanes, row0, SUBLANES, j, CHUNK)
        kj = _bcast_rows(k_ref, lanes, row0, SUBLANES, j, CHUNK)
        p = q * kj * jnp.exp2(G - gj)
        ps.append(jnp.where(sub >= j, p, 0.0).astype(BF16))
    pcat = jnp.concatenate(ps, axis=1)
    a = jnp.dot(pcat, rsel, preferred_element_type=F32)

    tl_xor = (lax.broadcasted_iota(jnp.int32, (CHUNK, CHUNK), 0)
              ^ lax.broadcasted_iota(jnp.int32, (CHUNK, CHUNK), 1))
    a = jnp.where(tl_xor < SUBLANES, a, 0.0)

    b = 2 * SUBLANES
    while b <= CHUNK:
        gm = _bcast_rows(g_ref, lanes, row0, b, b // 2 - 1, CHUNK)
        e = jnp.exp2(-jnp.abs(G - gm))
        upper = (row & (b // 2)) != 0
        qt = jnp.where(upper, q * e, 0.0).astype(BF16)
        kt = jnp.where(upper, 0.0, k * e).astype(BF16)
        ab = lax.dot_general(qt, kt, _NT, preferred_element_type=F32)
        a = a + (ab if b == CHUNK else jnp.where(tl_xor < b, ab, 0.0))
        b *= 2

    qi = (q * jnp.exp2(G)).astype(BF16)
    o = _intra_plus_inter(a, qi, v, st)

    glast = _row(g_ref, lanes, row0, valid - 1)
    kd = k * jnp.exp2(glast - G)
    if valid < CHUNK:
        kd = jnp.where(row < valid, kd, 0.0)
    st = (_col_bcast(jnp.exp2(glast)) * st
          + lax.dot_general(kd.astype(BF16), v, _TN, preferred_element_type=F32))
    return o, st


def _hgrn_chunk_fast(row0, lanes, st, q_ref, g_ref, k_ref, v_ref):
    rows = pl.ds(row0, CHUNK)
    q = q_ref[rows, lanes]
    G = g_ref[rows, lanes]
    k = k_ref[rows, lanes]
    v = v_ref[rows, lanes]
    gmid = g_ref[pl.ds(row0 + CHUNK // 2 - 1, 1), lanes]
    glast = g_ref[pl.ds(row0 + CHUNK - 1, 1), lanes]
    d = G - gmid
    qh = q * jnp.exp2(d)
    kh = k * jnp.exp2(-d)
    a = lax.dot_general(qh.astype(BF16), kh.astype(BF16), _NT, preferred_element_type=F32)
    causal = (lax.broadcasted_iota(jnp.int32, (CHUNK, CHUNK), 1)
              <= lax.broadcasted_iota(jnp.int32, (CHUNK, CHUNK), 0))
    a = jnp.where(causal, a, 0.0)
    qi = (qh * jnp.exp2(gmid)).astype(BF16)
    o = _intra_plus_inter(a, qi, v, st)
    kd = (kh * jnp.exp2(glast - gmid)).astype(BF16)
    st = _col_bcast(jnp.exp2(glast)) * st + lax.dot_general(kd, v, _TN, preferred_element_type=F32)
    return o, st


def _project_group(u, g, R, valid, prm, slot, recurrence_inputs):
    lb_ref, win_ref, cw_ref, zc_ref, safe_ref = prm
    q_ref, g_ref, k_ref, v_ref, gs_ref, sa_ref, mb_ref, zs_ref = slot
    c0 = g * GROUP_WIDTH
    cols = slice(c0, c0 + GROUP_WIDTH)

    def sec(i):
        w = win_ref[:, i * D_MODEL + c0:i * D_MODEL + c0 + GROUP_WIDTH]
        return jnp.dot(u, w, preferred_element_type=F32)

    if not recurrence_inputs:
        yield from _project_gates(sec, cols, R, valid, cw_ref, zc_ref, slot)
        return

    qv = sec(0)
    q_ref[0:R, :] = qv * _sigmoid(qv)
    yield

    f = lb_ref[0:1, cols] + lb_ref[1:2, cols] * _sigmoid(sec(1))
    k_ref[0:R, :] = 1.0 - f
    lf = jnp.log2(f)
    tri = (lax.broadcasted_iota(jnp.int32, (CHUNK, CHUNK), 1)
           <= lax.broadcasted_iota(jnp.int32, (CHUNK, CHUNK), 0)).astype(BF16)
    tri2 = jnp.concatenate([tri, tri], axis=1)
    worst = jnp.zeros((1, GROUP_WIDTH), F32)
    for c in range(R // CHUNK):
        lfc = lf[c * CHUNK:(c + 1) * CHUNK, :]
        hi = lfc.astype(BF16)
        lo = (lfc - hi.astype(F32)).astype(BF16)
        gc = jnp.dot(tri2, jnp.concatenate([hi, lo], axis=0),
                     preferred_element_type=F32)
        g_ref[c * CHUNK:(c + 1) * CHUNK, :] = gc
        gmid = gc[CHUNK // 2 - 1:CHUNK // 2, :]
        glast = gc[CHUNK - 1:CHUNK, :]
        worst = jnp.maximum(worst, jnp.maximum(-gmid, gmid - glast))
    safe_ref[g] = (jnp.max(worst) <= MAX_SAFE_EXPONENT * LOG2_E).astype(jnp.int32)
    yield

    v_ref[0:R, :] = sec(2).astype(BF16)
    yield


def _project_gates(sec, cols, R, valid, cw_ref, zc_ref, slot):
    gs_ref, sa_ref, mb_ref, zs_ref = slot[4:8]
    gv = sec(3)
    gs_ref[0:R, :] = gv * _sigmoid(gv)
    yield

    bg = sec(4)
    z = sec(5) * sec(6)
    zs_ref[0:SUBLANES, :] = zc_ref[:, cols]
    zs_ref[SUBLANES:SUBLANES + R, :] = z
    z1 = zs_ref[SUBLANES - 1:SUBLANES - 1 + R, :]
    z2 = zs_ref[SUBLANES - 2:SUBLANES - 2 + R, :]
    yb = bg * (cw_ref[0:1, cols] * z2 + cw_ref[1:2, cols] * z1 + cw_ref[2:3, cols] * z)
    zc_ref[:, cols] = zs_ref[valid:valid + SUBLANES, :]
    yield

    sa_ref[0:R, :] = _sigmoid(sec(7))
    yield
    mb_ref[0:R, :] = _sigmoid(sec(8)) * yb
    yield


def _finish_chunk(row0, h, lanes, o, hnw, mg_ref, slot):
    gs_ref, sa_ref, mb_ref = slot[4:7]
    rows = pl.ds(row0, CHUNK)
    m = _rms(o, hnw) * gs_ref[rows, lanes] * sa_ref[rows, lanes] + mb_ref[rows, lanes]
    mg_ref[rows, h * HEAD_DIM:(h + 1) * HEAD_DIM] = m.astype(BF16)


def _recur_group_fast(g, R, hnw, st_ref, st_old_ref, mg_ref, slot):
    q_ref, g_ref, k_ref, v_ref = slot[0:4]
    for hh in range(GROUP_HEADS):
        h = GROUP_HEADS * g + hh
        lanes = slice(hh * HEAD_DIM, (hh + 1) * HEAD_DIM)
        st = st_ref[h]
        st_old_ref[h] = st
        for c in range(R // CHUNK):
            o, st = _hgrn_chunk_fast(c * CHUNK, lanes, st, q_ref, g_ref, k_ref, v_ref)
            mg_ref[c * CHUNK:(c + 1) * CHUNK, h * HEAD_DIM:(h + 1) * HEAD_DIM] = _rms(o, hnw).astype(BF16)
            yield
        st_ref[h] = st


def _finish_group_fast(g, R, mg_ref, slot):
    gs_ref, sa_ref, mb_ref = slot[4:7]
    cols = slice(g * GROUP_WIDTH, (g + 1) * GROUP_WIDTH)
    m = mg_ref[0:R, cols].astype(F32) * gs_ref[0:R, :] * sa_ref[0:R, :] + mb_ref[0:R, :]
    mg_ref[0:R, cols] = m.astype(BF16)


def _recur_group_exact(g, R, valid, hnw, rsel, st_in_ref, st_ref, mg_ref, slot):
    q_ref, g_ref, k_ref, v_ref = slot[0:4]
    for hh in range(GROUP_HEADS):
        h = GROUP_HEADS * g + hh
        lanes = slice(hh * HEAD_DIM, (hh + 1) * HEAD_DIM)
        st = st_in_ref[h]
        if valid < R:
            o, st = _hgrn_chunk_exact(0, lanes, valid, st, q_ref, g_ref, k_ref, v_ref, rsel)
            _finish_chunk(0, h, lanes, o, hnw, mg_ref, slot)
        else:
            def chunk_body(c, st):
                row0 = pl.multiple_of(c * CHUNK, CHUNK)
                o, st = _hgrn_chunk_exact(row0, lanes, CHUNK, st, q_ref, g_ref, k_ref, v_ref, rsel)
                _finish_chunk(row0, h, lanes, o, hnw, mg_ref, slot)
                return st

            st = lax.fori_loop(0, R // CHUNK, chunk_body, st)
        st_ref[h] = st


def _interleave(*gens):
    gens = list(gens)
    while gens:
        for g in list(gens):
            try:
                next(g)
            except StopIteration:
                gens.remove(g)


def _out_partial(g, R, mg_ref, wout_ref):
    rows = slice(g * GROUP_WIDTH, (g + 1) * GROUP_WIDTH)
    return jnp.dot(mg_ref[0:R, rows], wout_ref[rows, 0:D_MODEL], preferred_element_type=F32)


def _mixer_tile(load_x, store_out, R, valid, is_meta, refs):
    (lbp_ref, anw_ref, win_ref, hnw_ref, cw_ref, wout_ref, rsel_ref,
     st_ref, st_old_ref, zc_ref, lb_ref, mg_ref, safe_ref, slots) = refs
    prm = (lb_ref, win_ref, cw_ref, zc_ref, safe_ref)
    u = _rms(load_x(), anw_ref[...]).astype(BF16)
    rsel = rsel_ref[...]
    hnw = hnw_ref[...]
    project = lambda g, first: _interleave(_project_group(u, g, R, valid, prm, slots[g], first))

    lbp = lbp_ref[...]
    mx = jnp.max(lbp, axis=0, keepdims=True)
    ex = jnp.exp(lbp - mx)
    lb = ex[0:1, :] / jnp.sum(ex, axis=0, keepdims=True)
    lb_ref[0:1, :] = lb
    lb_ref[1:2, :] = 1.0 - lb

    if is_meta:
        for g in range(N_GROUPS):
            project(g, True)
            project(g, False)
            _recur_group_exact(g, R, valid, hnw, rsel, st_ref, st_ref, mg_ref, slots[g])
        acc = load_x()
        for g in range(N_GROUPS):
            acc = acc + _out_partial(g, R, mg_ref, wout_ref)
        store_out(acc, False)
        return

    def store_early():
        acc = load_x()
        for g in range(N_GROUPS - 1):
            acc = acc + _out_partial(g, R, mg_ref, wout_ref)
        store_out(acc, False)

    for g in range(N_GROUPS):
        project(g, True)
        if g > 0:
            _finish_group_fast(g - 1, R, mg_ref, slots[g - 1])
            if g == N_GROUPS - 1:
                store_early()
        _interleave(_recur_group_fast(g, R, hnw, st_ref, st_old_ref, mg_ref, slots[g]))
        project(g, False)
    _finish_group_fast(N_GROUPS - 1, R, mg_ref, slots[N_GROUPS - 1])
    for g in range(N_GROUPS):
        def redo(g=g):
            _recur_group_exact(g, R, valid, hnw, rsel, st_old_ref, st_ref, mg_ref, slots[g])
            if g < N_GROUPS - 1:
                store_early()

        pl.when(safe_ref[g] == 0)(redo)
    store_out(_out_partial(N_GROUPS - 1, R, mg_ref, wout_ref), True)


def _mixer_kernel(x_ref, meta_ref, lbp_ref, anw_ref, win_ref, hnw_ref, cw_ref, wout_ref, rsel_ref,
                  h1_ref, h1m_ref, st_ref, st_old_ref, zc_ref, lb_ref, mg_ref, safe_ref, *slots):
    n = len(slots) // N_GROUPS
    refs = (lbp_ref, anw_ref, win_ref, hnw_ref, cw_ref, wout_ref, rsel_ref,
            st_ref, st_old_ref, zc_ref, lb_ref, mg_ref, safe_ref,
            [slots[i * n:(i + 1) * n] for i in range(N_GROUPS)])

    @pl.when(pl.program_id(1) == 0)
    def _():
        st_ref[...] = jnp.zeros_like(st_ref)
        zc_ref[...] = jnp.zeros_like(zc_ref)
        load_meta = lambda: jnp.concatenate(
            [meta_ref[...], jnp.zeros((CHUNK - N_META, D_MODEL), F32)], axis=0)

        def store_meta(val, accumulate):
            h1m_ref[0] = val[0:N_META, :]

        _mixer_tile(load_meta, store_meta, CHUNK, N_META, True, refs)

    def store_tile(val, accumulate):
        h1_ref[0] = h1_ref[0] + val if accumulate else val

    _mixer_tile(lambda: x_ref[0], store_tile, MIXER_TILE, MIXER_TILE, False, refs)


def _ffn_kernel(h1_ref, h1m_ref, fnw_ref, wup_ref, fcw_ref, fcb_ref, wdn_ref, finw_ref,
                out_ref, as_ref):
    T = FFN_TILE
    fnw = fnw_ref[...]

    @pl.when(pl.program_id(1) == 0)
    def _():
        um = _rms(h1m_ref[0], fnw).astype(BF16)
        am = jnp.dot(um, wup_ref[:, 0:D_FF], preferred_element_type=F32)
        as_ref[0:SUBLANES, :] = am[N_META - SUBLANES:N_META, :]

    x = h1_ref[0]
    u = _rms(x, fnw).astype(BF16)
    y = x
    for cb in range(D_FF // FFN_COL_BLOCK):
        cols = slice(cb * FFN_COL_BLOCK, (cb + 1) * FFN_COL_BLOCK)
        a = jnp.dot(u, wup_ref[:, cols], preferred_element_type=F32)
        as_ref[SUBLANES:SUBLANES + T, cols] = a
        a1 = as_ref[SUBLANES - 1:SUBLANES - 1 + T, cols]
        a2 = as_ref[SUBLANES - 2:SUBLANES - 2 + T, cols]
        ac = (fcw_ref[0:1, cols] * a2 + fcw_ref[1:2, cols] * a1 + fcw_ref[2:3, cols] * a
              + fcb_ref[:, cols])
        as_ref[0:SUBLANES, cols] = as_ref[T:T + SUBLANES, cols]
        val = jnp.dot(u, wup_ref[:, D_FF + cb * FFN_COL_BLOCK:D_FF + (cb + 1) * FFN_COL_BLOCK],
                      preferred_element_type=F32)
        gated = (ac * _sigmoid(ac) * val).astype(BF16)
        y = y + jnp.dot(gated, wdn_ref[cols, 0:D_MODEL], preferred_element_type=F32)
    out_ref[0] = _rms(y, finw_ref[...])


def _resident(shape):
    return pl.BlockSpec(shape, lambda b, t: (0,) * len(shape), pipeline_mode=pl.Buffered(1))


def _weight(w):
    return jnp.pad(w.astype(BF16), ((0, 0), (0, WEIGHT_LANE_PAD)))


def _rsel_matrix():
    r = np.arange(N_HEADS * HEAD_DIM)[:, None] // HEAD_DIM
    l = np.arange(CHUNK)[None, :] % SUBLANES
    return jnp.asarray(r == l, dtype=BF16)


def kernel(x, meta_tokens, lb_param, attn_norm_w, w_in, hgrn_norm_w, conv_w, w_out, ffn_norm_w, w_up,
           ffn_conv_w, ffn_conv_b, w_down, final_norm_w):
    B, L, D = x.shape
    assert D == D_MODEL and L % MIXER_TILE == 0 and L % FFN_TILE == 0
    assert w_in.shape == (1, D, N_IN_SECTIONS * D) and w_up.shape == (1, D, 2 * D_FF)
    assert SUBLANES * HEAD_DIM == D_MODEL and N_HEADS == SUBLANES

    tile = lambda T: pl.BlockSpec((1, T, D), lambda b, t: (b, t, 0))
    meta_tile = pl.BlockSpec((1, N_META, D), lambda b, t: (b, 0, 0))
    params = pltpu.CompilerParams(dimension_semantics=("arbitrary", "arbitrary"),
                                  vmem_limit_bytes=VMEM_LIMIT_BYTES)

    TM = MIXER_TILE
    slot = [pltpu.VMEM((TM, GROUP_WIDTH), F32)] * 3 + [pltpu.VMEM((TM, GROUP_WIDTH), BF16)] \
        + [pltpu.VMEM((TM, GROUP_WIDTH), F32)] * 3 + [pltpu.VMEM((TM + SUBLANES, GROUP_WIDTH), F32)]
    h1, h1m = pl.pallas_call(
        _mixer_kernel,
        grid=(B, L // TM),
        in_specs=[tile(TM), _resident((N_META, D)), _resident((2, D)), _resident((1, D)),
                  _resident((D, N_IN_SECTIONS * D + WEIGHT_LANE_PAD)), _resident((1, HEAD_DIM)),
                  _resident((3, D)), _resident((D, D + WEIGHT_LANE_PAD)),
                  _resident((N_HEADS * HEAD_DIM, CHUNK))],
        out_specs=[tile(TM), meta_tile],
        out_shape=[jax.ShapeDtypeStruct((B, L, D), F32), jax.ShapeDtypeStruct((B, N_META, D), F32)],
        scratch_shapes=[pltpu.VMEM((N_HEADS, HEAD_DIM, HEAD_DIM), F32),
                        pltpu.VMEM((N_HEADS, HEAD_DIM, HEAD_DIM), F32),
                        pltpu.VMEM((SUBLANES, D), F32),
                        pltpu.VMEM((SUBLANES, D), F32),
                        pltpu.VMEM((TM, D), BF16),
                        pltpu.SMEM((N_GROUPS,), jnp.int32),
                        *(slot * N_GROUPS)],
        compiler_params=params,
        name="mixer",
    )(x, meta_tokens, lb_param, attn_norm_w, _weight(w_in[0]), hgrn_norm_w, conv_w[0],
      _weight(w_out[0]), _rsel_matrix())

    TF = FFN_TILE
    out = pl.pallas_call(
        _ffn_kernel,
        grid=(B, L // TF),
        in_specs=[tile(TF), meta_tile, _resident((1, D)), _resident((D, 2 * D_FF)), _resident((3, D_FF)),
                  _resident((1, D_FF)), _resident((D_FF, D + WEIGHT_LANE_PAD)), _resident((1, D))],
        out_specs=tile(TF),
        out_shape=jax.ShapeDtypeStruct((B, L, D), F32),
        scratch_shapes=[pltpu.VMEM((TF + SUBLANES, D_FF), F32)],
        compiler_params=params,
        name="ffn",
    )(h1, h1m, ffn_norm_w, w_up[0].astype(BF16), ffn_conv_w[0], ffn_conv_b, _weight(w_down[0]),
      final_norm_w.reshape(1, D))
    return out
```

```python
import numpy as np
import jax
import jax.numpy as jnp
from jax import lax
from jax.experimental import pallas as pl
from jax.experimental.pallas import tpu as pltpu

D_MODEL = 1024
N_META = 16
N_HEADS = 8
HEAD_DIM = 128
D_FF = 2816
N_IN_SECTIONS = 9
EPS = 1e-6

SUBLANES = 8
CHUNK = 128
GROUP_HEADS = 4
GROUP_WIDTH = GROUP_HEADS * HEAD_DIM
N_GROUPS = N_HEADS // GROUP_HEADS
MAX_SAFE_EXPONENT = 60.0
LOG2_E = 1.4426950408889634
MIXER_TILE = 512
FFN_TILE = 512
FFN_COL_BLOCK = 1408
WEIGHT_LANE_PAD = 128
VMEM_LIMIT_BYTES = 60 * 1024 * 1024

F32 = jnp.float32
BF16 = jnp.bfloat16

_NT = (((1,), (1,)), ((), ()))
_TN = (((0,), (0,)), ((), ()))


def _rms(x, w):
    ms = jnp.mean(x * x, axis=-1, keepdims=True)
    return x * lax.rsqrt(ms + EPS) * w


def _sigmoid(x):
    return 1.0 / (1.0 + jnp.exp2(x * -LOG2_E))


def _bcast_rows(ref, lanes, row0, block, offset, nrows):
    pieces = [jnp.broadcast_to(_row(ref, lanes, row0, i * block + offset), (block, HEAD_DIM))
              for i in range(nrows // block)]
    return pieces[0] if len(pieces) == 1 else jnp.concatenate(pieces, axis=0)


def _row(ref, lanes, row0, r):
    group = ref[pl.ds(row0 + r // SUBLANES * SUBLANES, SUBLANES), lanes]
    return group[r % SUBLANES:r % SUBLANES + 1, :]


def _hgrn_chunk_exact(row0, lanes, valid, st, q_ref, g_ref, k_ref, v_ref, rsel):
    rows = pl.ds(row0, CHUNK)
    q = q_ref[rows, lanes]
    G = g_ref[rows, lanes]
    k = k_ref[rows, lanes]
    v = v_ref[rows, lanes]
    row = lax.broadcasted_iota(jnp.int32, (CHUNK, HEAD_DIM), 0)

    sub = row & (SUBLANES - 1)
    ps = []
    for j in range(SUBLANES):
        gj = _bcast_rows(g_ref, lanes, row0, SUBLANES, j, CHUNK)
        kj = _bcast_rows(k_ref, lanes, row0, SUBLANES, j, CHUNK)
        p = q * kj * jnp.exp2(G - gj)
        ps.append(jnp.where(sub >= j, p, 0.0).astype(BF16))
    pcat = jnp.concatenate(ps, axis=1)
    a = jnp.dot(pcat, rsel, preferred_element_type=F32)

    tl_xor = (lax.broadcasted_iota(jnp.int32, (CHUNK, CHUNK), 0)
              ^ lax.broadcasted_iota(jnp.int32, (CHUNK, CHUNK), 1))
    a = jnp.where(tl_xor < SUBLANES, a, 0.0)

    b = 2 * SUBLANES
    while b <= CHUNK:
        gm = _bcast_rows(g_ref, lanes, row0, b, b // 2 - 1, CHUNK)
        e = jnp.exp2(-jnp.abs(G - gm))
        upper = (row & (b // 2)) != 0
        qt = jnp.where(upper, q * e, 0.0).astype(BF16)
        kt = jnp.where(upper, 0.0, k * e).astype(BF16)
        ab = lax.dot_general(qt, kt, _NT, preferred_element_type=F32)
        a = a + (ab if b == CHUNK else jnp.where(tl_xor < b, ab, 0.0))
        b *= 2

    qi = (q * jnp.exp2(G)).astype(BF16)
    o = lax.dot_general(qi, st.astype(BF16), _NT, preferred_element_type=F32)
    o = o + jnp.dot(a.astype(BF16), v, preferred_element_type=F32)

    glast = _row(g_ref, lanes, row0, valid - 1)
    kd = k * jnp.exp2(glast - G)
    if valid < CHUNK:
        kd = jnp.where(row < valid, kd, 0.0)
    st = st * jnp.exp2(glast) + lax.dot_general(v, kd.astype(BF16), _TN, preferred_element_type=F32)
    return o, st


def _hgrn_chunk_fast(row0, lanes, st, q_ref, g_ref, k_ref, v_ref):
    rows = pl.ds(row0, CHUNK)
    q = q_ref[rows, lanes]
    G = g_ref[rows, lanes]
    k = k_ref[rows, lanes]
    v = v_ref[rows, lanes]
    gmid = g_ref[pl.ds(row0 + CHUNK // 2 - 1, 1), lanes]
    glast = g_ref[pl.ds(row0 + CHUNK - 1, 1), lanes]
    d = G - gmid
    qh = q * jnp.exp2(d)
    kh = k * jnp.exp2(-d)
    a = lax.dot_general(qh.astype(BF16), kh.astype(BF16), _NT, preferred_element_type=F32)
    causal = (lax.broadcasted_iota(jnp.int32, (CHUNK, CHUNK), 1)
              <= lax.broadcasted_iota(jnp.int32, (CHUNK, CHUNK), 0))
    a = jnp.where(causal, a, 0.0)
    qi = (qh * jnp.exp2(gmid)).astype(BF16)
    o = lax.dot_general(qi, st.astype(BF16), _NT, preferred_element_type=F32)
    o = o + jnp.dot(a.astype(BF16), v, preferred_element_type=F32)
    kd = (kh * jnp.exp2(glast - gmid)).astype(BF16)
    st = st * jnp.exp2(glast) + lax.dot_general(v, kd, _TN, preferred_element_type=F32)
    return o, st


def _project_group(u, g, R, valid, prm, slot, recurrence_inputs):
    lb_ref, win_ref, cw_ref, zc_ref, safe_ref = prm
    q_ref, g_ref, k_ref, v_ref, gs_ref, sa_ref, mb_ref, zs_ref = slot
    c0 = g * GROUP_WIDTH
    cols = slice(c0, c0 + GROUP_WIDTH)

    def sec(i):
        w = win_ref[:, i * D_MODEL + c0:i * D_MODEL + c0 + GROUP_WIDTH]
        return jnp.dot(u, w, preferred_element_type=F32)

    if not recurrence_inputs:
        yield from _project_gates(sec, cols, R, valid, cw_ref, zc_ref, slot)
        return

    qv = sec(0)
    q_ref[0:R, :] = qv * _sigmoid(qv)
    yield

    f = lb_ref[0:1, cols] + lb_ref[1:2, cols] * _sigmoid(sec(1))
    k_ref[0:R, :] = 1.0 - f
    lf = jnp.log2(f)
    tri = (lax.broadcasted_iota(jnp.int32, (CHUNK, CHUNK), 1)
           <= lax.broadcasted_iota(jnp.int32, (CHUNK, CHUNK), 0)).astype(BF16)
    tri2 = jnp.concatenate([tri, tri], axis=1)
    worst = jnp.zeros((1, GROUP_WIDTH), F32)
    for c in range(R // CHUNK):
        lfc = lf[c * CHUNK:(c + 1) * CHUNK, :]
        hi = lfc.astype(BF16)
        lo = (lfc - hi.astype(F32)).astype(BF16)
        gc = jnp.dot(tri2, jnp.concatenate([hi, lo], axis=0),
                     preferred_element_type=F32)
        g_ref[c * CHUNK:(c + 1) * CHUNK, :] = gc
        gmid = gc[CHUNK // 2 - 1:CHUNK // 2, :]
        glast = gc[CHUNK - 1:CHUNK, :]
        worst = jnp.maximum(worst, jnp.maximum(-gmid, gmid - glast))
    safe_ref[g] = (jnp.max(worst) <= MAX_SAFE_EXPONENT * LOG2_E).astype(jnp.int32)
    yield

    v_ref[0:R, :] = sec(2).astype(BF16)
    yield


def _project_gates(sec, cols, R, valid, cw_ref, zc_ref, slot):
    gs_ref, sa_ref, mb_ref, zs_ref = slot[4:8]
    gv = sec(3)
    gs_ref[0:R, :] = gv * _sigmoid(gv)
    yield

    bg = sec(4)
    z = sec(5) * sec(6)
    zs_ref[0:SUBLANES, :] = zc_ref[:, cols]
    zs_ref[SUBLANES:SUBLANES + R, :] = z
    z1 = zs_ref[SUBLANES - 1:SUBLANES - 1 + R, :]
    z2 = zs_ref[SUBLANES - 2:SUBLANES - 2 + R, :]
    yb = bg * (cw_ref[0:1, cols] * z2 + cw_ref[1:2, cols] * z1 + cw_ref[2:3, cols] * z)
    zc_ref[:, cols] = zs_ref[valid:valid + SUBLANES, :]
    yield

    sa_ref[0:R, :] = _sigmoid(sec(7))
    yield
    mb_ref[0:R, :] = _sigmoid(sec(8)) * yb
    yield


def _finish_chunk(row0, h, lanes, o, hnw, mg_ref, slot):
    gs_ref, sa_ref, mb_ref = slot[4:7]
    rows = pl.ds(row0, CHUNK)
    m = _rms(o, hnw) * gs_ref[rows, lanes] * sa_ref[rows, lanes] + mb_ref[rows, lanes]
    mg_ref[rows, h * HEAD_DIM:(h + 1) * HEAD_DIM] = m.astype(BF16)


def _recur_group_fast(g, R, hnw, st_ref, st_old_ref, mg_ref, slot):
    q_ref, g_ref, k_ref, v_ref = slot[0:4]
    for hh in range(GROUP_HEADS):
        h = GROUP_HEADS * g + hh
        lanes = slice(hh * HEAD_DIM, (hh + 1) * HEAD_DIM)
        st = st_ref[h]
        st_old_ref[h] = st
        for c in range(R // CHUNK):
            o, st = _hgrn_chunk_fast(c * CHUNK, lanes, st, q_ref, g_ref, k_ref, v_ref)
            mg_ref[c * CHUNK:(c + 1) * CHUNK, h * HEAD_DIM:(h + 1) * HEAD_DIM] = _rms(o, hnw).astype(BF16)
            yield
        st_ref[h] = st


def _finish_group_fast(g, R, mg_ref, slot):
    gs_ref, sa_ref, mb_ref = slot[4:7]
    cols = slice(g * GROUP_WIDTH, (g + 1) * GROUP_WIDTH)
    m = mg_ref[0:R, cols].astype(F32) * gs_ref[0:R, :] * sa_ref[0:R, :] + mb_ref[0:R, :]
    mg_ref[0:R, cols] = m.astype(BF16)


def _recur_group_exact(g, R, valid, hnw, rsel, st_in_ref, st_ref, mg_ref, slot):
    q_ref, g_ref, k_ref, v_ref = slot[0:4]
    for hh in range(GROUP_HEADS):
        h = GROUP_HEADS * g + hh
        lanes = slice(hh * HEAD_DIM, (hh + 1) * HEAD_DIM)
        st = st_in_ref[h]
        if valid < R:
            o, st = _hgrn_chunk_exact(0, lanes, valid, st, q_ref, g_ref, k_ref, v_ref, rsel)
            _finish_chunk(0, h, lanes, o, hnw, mg_ref, slot)
        else:
            def chunk_body(c, st):
                row0 = pl.multiple_of(c * CHUNK, CHUNK)
                o, st = _hgrn_chunk_exact(row0, lanes, CHUNK, st, q_ref, g_ref, k_ref, v_ref, rsel)
                _finish_chunk(row0, h, lanes, o, hnw, mg_ref, slot)
                return st

            st = lax.fori_loop(0, R // CHUNK, chunk_body, st)
        st_ref[h] = st


def _interleave(*gens):
    gens = list(gens)
    while gens:
        for g in list(gens):
            try:
                next(g)
            except StopIteration:
                gens.remove(g)


def _out_partial(g, R, mg_ref, wout_ref):
    rows = slice(g * GROUP_WIDTH, (g + 1) * GROUP_WIDTH)
    return jnp.dot(mg_ref[0:R, rows], wout_ref[rows, 0:D_MODEL], preferred_element_type=F32)


def _mixer_tile(load_x, store_out, R, valid, is_meta, refs):
    (lbp_ref, anw_ref, win_ref, hnw_ref, cw_ref, wout_ref, rsel_ref,
     st_ref, st_old_ref, zc_ref, lb_ref, mg_ref, safe_ref, slots) = refs
    prm = (lb_ref, win_ref, cw_ref, zc_ref, safe_ref)
    u = _rms(load_x(), anw_ref[...]).astype(BF16)
    rsel = rsel_ref[...]
    hnw = hnw_ref[...]
    project = lambda g, first: _interleave(_project_group(u, g, R, valid, prm, slots[g], first))

    lbp = lbp_ref[...]
    mx = jnp.max(lbp, axis=0, keepdims=True)
    ex = jnp.exp(lbp - mx)
    lb = ex[0:1, :] / jnp.sum(ex, axis=0, keepdims=True)
    lb_ref[0:1, :] = lb
    lb_ref[1:2, :] = 1.0 - lb

    if is_meta:
        for g in range(N_GROUPS):
            project(g, True)
            project(g, False)
            _recur_group_exact(g, R, valid, hnw, rsel, st_ref, st_ref, mg_ref, slots[g])
        acc = load_x()
        for g in range(N_GROUPS):
            acc = acc + _out_partial(g, R, mg_ref, wout_ref)
        store_out(acc, False)
        return

    def store_early():
        acc = load_x()
        for g in range(N_GROUPS - 1):
            acc = acc + _out_partial(g, R, mg_ref, wout_ref)
        store_out(acc, False)

    for g in range(N_GROUPS):
        project(g, True)
        if g > 0:
            _finish_group_fast(g - 1, R, mg_ref, slots[g - 1])
            if g == N_GROUPS - 1:
                store_early()
        _interleave(_recur_group_fast(g, R, hnw, st_ref, st_old_ref, mg_ref, slots[g]))
        project(g, False)
    _finish_group_fast(N_GROUPS - 1, R, mg_ref, slots[N_GROUPS - 1])
    for g in range(N_GROUPS):
        def redo(g=g):
            _recur_group_exact(g, R, valid, hnw, rsel, st_old_ref, st_ref, mg_ref, slots[g])
            if g < N_GROUPS - 1:
                store_early()

        pl.when(safe_ref[g] == 0)(redo)
    store_out(_out_partial(N_GROUPS - 1, R, mg_ref, wout_ref), True)


def _mixer_kernel(x_ref, meta_ref, lbp_ref, anw_ref, win_ref, hnw_ref, cw_ref, wout_ref, rsel_ref,
                  h1_ref, h1m_ref, st_ref, st_old_ref, zc_ref, lb_ref, mg_ref, safe_ref, *slots):
    n = len(slots) // N_GROUPS
    refs = (lbp_ref, anw_ref, win_ref, hnw_ref, cw_ref, wout_ref, rsel_ref,
            st_ref, st_old_ref, zc_ref, lb_ref, mg_ref, safe_ref,
            [slots[i * n:(i + 1) * n] for i in range(N_GROUPS)])

    @pl.when(pl.program_id(1) == 0)
    def _():
        st_ref[...] = jnp.zeros_like(st_ref)
        zc_ref[...] = jnp.zeros_like(zc_ref)
        load_meta = lambda: jnp.concatenate(
            [meta_ref[...], jnp.zeros((CHUNK - N_META, D_MODEL), F32)], axis=0)

        def store_meta(val, accumulate):
            h1m_ref[0] = val[0:N_META, :]

        _mixer_tile(load_meta, store_meta, CHUNK, N_META, True, refs)

    def store_tile(val, accumulate):
        h1_ref[0] = h1_ref[0] + val if accumulate else val

    _mixer_tile(lambda: x_ref[0], store_tile, MIXER_TILE, MIXER_TILE, False, refs)


def _ffn_kernel(h1_ref, h1m_ref, fnw_ref, wup_ref, fcw_ref, fcb_ref, wdn_ref, finw_ref,
                out_ref, as_ref):
    T = FFN_TILE
    fnw = fnw_ref[...]

    @pl.when(pl.program_id(1) == 0)
    def _():
        um = _rms(h1m_ref[0], fnw).astype(BF16)
        am = jnp.dot(um, wup_ref[:, 0:D_FF], preferred_element_type=F32)
        as_ref[0:SUBLANES, :] = am[N_META - SUBLANES:N_META, :]

    x = h1_ref[0]
    u = _rms(x, fnw).astype(BF16)
    y = x
    for cb in range(D_FF // FFN_COL_BLOCK):
        cols = slice(cb * FFN_COL_BLOCK, (cb + 1) * FFN_COL_BLOCK)
        a = jnp.dot(u, wup_ref[:, cols], preferred_element_type=F32)
        as_ref[SUBLANES:SUBLANES + T, cols] = a
        a1 = as_ref[SUBLANES - 1:SUBLANES - 1 + T, cols]
        a2 = as_ref[SUBLANES - 2:SUBLANES - 2 + T, cols]
        ac = (fcw_ref[0:1, cols] * a2 + fcw_ref[1:2, cols] * a1 + fcw_ref[2:3, cols] * a
              + fcb_ref[:, cols])
        as_ref[0:SUBLANES, cols] = as_ref[T:T + SUBLANES, cols]
        val = jnp.dot(u, wup_ref[:, D_FF + cb * FFN_COL_BLOCK:D_FF + (cb + 1) * FFN_COL_BLOCK],
                      preferred_element_type=F32)
        gated = (ac * _sigmoid(ac) * val).astype(BF16)
        y = y + jnp.dot(gated, wdn_ref[cols, 0:D_MODEL], preferred_element_type=F32)
    out_ref[0] = _rms(y, finw_ref[...])


def _resident(shape):
    return pl.BlockSpec(shape, lambda b, t: (0,) * len(shape), pipeline_mode=pl.Buffered(1))


def _weight(w):
    return jnp.pad(w.astype(BF16), ((0, 0), (0, WEIGHT_LANE_PAD)))


def _rsel_matrix():
    r = np.arange(N_HEADS * HEAD_DIM)[:, None] // HEAD_DIM
    l = np.arange(CHUNK)[None, :] % SUBLANES
    return jnp.asarray(r == l, dtype=BF16)


def kernel(x, meta_tokens, lb_param, attn_norm_w, w_in, hgrn_norm_w, conv_w, w_out, ffn_norm_w, w_up,
           ffn_conv_w, ffn_conv_b, w_down, final_norm_w):
    B, L, D = x.shape
    assert D == D_MODEL and L % MIXER_TILE == 0 and L % FFN_TILE == 0
    assert w_in.shape == (1, D, N_IN_SECTIONS * D) and w_up.shape == (1, D, 2 * D_FF)
    assert SUBLANES * HEAD_DIM == D_MODEL and N_HEADS == SUBLANES

    tile = lambda T: pl.BlockSpec((1, T, D), lambda b, t: (b, t, 0))
    meta_tile = pl.BlockSpec((1, N_META, D), lambda b, t: (b, 0, 0))
    params = pltpu.CompilerParams(dimension_semantics=("arbitrary", "arbitrary"),
                                  vmem_limit_bytes=VMEM_LIMIT_BYTES)

    TM = MIXER_TILE
    slot = [pltpu.VMEM((TM, GROUP_WIDTH), F32)] * 3 + [pltpu.VMEM((TM, GROUP_WIDTH), BF16)] \
        + [pltpu.VMEM((TM, GROUP_WIDTH), F32)] * 3 + [pltpu.VMEM((TM + SUBLANES, GROUP_WIDTH), F32)]
    h1, h1m = pl.pallas_call(
        _mixer_kernel,
        grid=(B, L // TM),
        in_specs=[tile(TM), _resident((N_META, D)), _resident((2, D)), _resident((1, D)),
                  _resident((D, N_IN_SECTIONS * D + WEIGHT_LANE_PAD)), _resident((1, HEAD_DIM)),
                  _resident((3, D)), _resident((D, D + WEIGHT_LANE_PAD)),
                  _resident((N_HEADS * HEAD_DIM, CHUNK))],
        out_specs=[tile(TM), meta_tile],
        out_shape=[jax.ShapeDtypeStruct((B, L, D), F32), jax.ShapeDtypeStruct((B, N_META, D), F32)],
        scratch_shapes=[pltpu.VMEM((N_HEADS, HEAD_DIM, HEAD_DIM), F32),
                        pltpu.VMEM((N_HEADS, HEAD_DIM, HEAD_DIM), F32),
                        pltpu.VMEM((SUBLANES, D), F32),
                        pltpu.VMEM((SUBLANES, D), F32),
                        pltpu.VMEM((TM, D), BF16),
                        pltpu.SMEM((N_GROUPS,), jnp.int32),
                        *(slot * N_GROUPS)],
        compiler_params=params,
        name="mixer",
    )(x, meta_tokens, lb_param, attn_norm_w, _weight(w_in[0]), hgrn_norm_w, conv_w[0],
      _weight(w_out[0]), _rsel_matrix())

    TF = FFN_TILE
    out = pl.pallas_call(
        _ffn_kernel,
        grid=(B, L // TF),
        in_specs=[tile(TF), meta_tile, _resident((1, D)), _resident((D, 2 * D_FF)), _resident((3, D_FF)),
                  _resident((1, D_FF)), _resident((D_FF, D + WEIGHT_LANE_PAD)), _resident((1, D))],
        out_specs=tile(TF),
        out_shape=jax.ShapeDtypeStruct((B, L, D), F32),
        scratch_shapes=[pltpu.VMEM((TF + SUBLANES, D_FF), F32)],
        compiler_params=params,
        name="ffn",
    )(h1, h1m, ffn_norm_w, w_up[0].astype(BF16), ffn_conv_w[0], ffn_conv_b, _weight(w_down[0]),
      final_norm_w.reshape(1, D))
    return out
```

```python
import numpy as np
import jax
import jax.numpy as jnp
from jax import lax
from jax.experimental import pallas as pl
from jax.experimental.pallas import tpu as pltpu

D_MODEL = 1024
N_META = 16
N_HEADS = 8
HEAD_DIM = 128
D_FF = 2816
N_IN_SECTIONS = 9
EPS = 1e-6

SUBLANES = 8
CHUNK = 128
GROUP_HEADS = 4
GROUP_WIDTH = GROUP_HEADS * HEAD_DIM
N_GROUPS = N_HEADS // GROUP_HEADS
MAX_SAFE_EXPONENT = 60.0
LOG2_E = 1.4426950408889634
MIXER_TILE = 512
FFN_TILE = 512
FFN_COL_BLOCK = 1408
WEIGHT_LANE_PAD = 128
W_DOWN_ROWS_PER_STEP = 128
VMEM_LIMIT_BYTES = 60 * 1024 * 1024

F32 = jnp.float32
BF16 = jnp.bfloat16

_NT = (((1,), (1,)), ((), ()))
_TN = (((0,), (0,)), ((), ()))


def _rms(x, w):
    ms = jnp.mean(x * x, axis=-1, keepdims=True)
    return x * lax.rsqrt(ms + EPS) * w


def _sigmoid(x):
    return 1.0 / (1.0 + jnp.exp2(x * -LOG2_E))


def _bcast_rows(ref, lanes, row0, block, offset, nrows):
    pieces = [jnp.broadcast_to(_row(ref, lanes, row0, i * block + offset), (block, HEAD_DIM))
              for i in range(nrows // block)]
    return pieces[0] if len(pieces) == 1 else jnp.concatenate(pieces, axis=0)


def _row(ref, lanes, row0, r):
    group = ref[pl.ds(row0 + r // SUBLANES * SUBLANES, SUBLANES), lanes]
    return group[r % SUBLANES:r % SUBLANES + 1, :]


def _hgrn_chunk_exact(row0, lanes, valid, st, q_ref, g_ref, k_ref, v_ref, rsel):
    rows = pl.ds(row0, CHUNK)
    q = q_ref[rows, lanes]
    G = g_ref[rows, lanes]
    k = k_ref[rows, lanes]
    v = v_ref[rows, lanes]
    row = lax.broadcasted_iota(jnp.int32, (CHUNK, HEAD_DIM), 0)

    sub = row & (SUBLANES - 1)
    ps = []
    for j in range(SUBLANES):
        gj = _bcast_rows(g_ref, lanes, row0, SUBLANES, j, CHUNK)
        kj = _bcast_rows(k_ref, lanes, row0, SUBLANES, j, CHUNK)
        p = q * kj * jnp.exp2(G - gj)
        ps.append(jnp.where(sub >= j, p, 0.0).astype(BF16))
    pcat = jnp.concatenate(ps, axis=1)
    a = jnp.dot(pcat, rsel, preferred_element_type=F32)

    tl_xor = (lax.broadcasted_iota(jnp.int32, (CHUNK, CHUNK), 0)
              ^ lax.broadcasted_iota(jnp.int32, (CHUNK, CHUNK), 1))
    a = jnp.where(tl_xor < SUBLANES, a, 0.0)

    b = 2 * SUBLANES
    while b <= CHUNK:
        gm = _bcast_rows(g_ref, lanes, row0, b, b // 2 - 1, CHUNK)
        e = jnp.exp2(-jnp.abs(G - gm))
        upper = (row & (b // 2)) != 0
        qt = jnp.where(upper, q * e, 0.0).astype(BF16)
        kt = jnp.where(upper, 0.0, k * e).astype(BF16)
        ab = lax.dot_general(qt, kt, _NT, preferred_element_type=F32)
        a = a + (ab if b == CHUNK else jnp.where(tl_xor < b, ab, 0.0))
        b *= 2

    qi = (q * jnp.exp2(G)).astype(BF16)
    o = lax.dot_general(qi, st.astype(BF16), _NT, preferred_element_type=F32)
    o = o + jnp.dot(a.astype(BF16), v, preferred_element_type=F32)

    glast = _row(g_ref, lanes, row0, valid - 1)
    kd = k * jnp.exp2(glast - G)
    if valid < CHUNK:
        kd = jnp.where(row < valid, kd, 0.0)
    st = st * jnp.exp2(glast) + lax.dot_general(v, kd.astype(BF16), _TN, preferred_element_type=F32)
    return o, st


def _hgrn_chunk_fast(row0, lanes, st, q_ref, g_ref, k_ref, v_ref):
    rows = pl.ds(row0, CHUNK)
    q = q_ref[rows, lanes]
    G = g_ref[rows, lanes]
    k = k_ref[rows, lanes]
    v = v_ref[rows, lanes]
    gmid = g_ref[pl.ds(row0 + CHUNK // 2 - 1, 1), lanes]
    glast = g_ref[pl.ds(row0 + CHUNK - 1, 1), lanes]
    d = G - gmid
    qh = q * jnp.exp2(d)
    kh = k * jnp.exp2(-d)
    a = lax.dot_general(qh.astype(BF16), kh.astype(BF16), _NT, preferred_element_type=F32)
    causal = (lax.broadcasted_iota(jnp.int32, (CHUNK, CHUNK), 1)
              <= lax.broadcasted_iota(jnp.int32, (CHUNK, CHUNK), 0))
    a = jnp.where(causal, a, 0.0)
    qi = (qh * jnp.exp2(gmid)).astype(BF16)
    o = lax.dot_general(qi, st.astype(BF16), _NT, preferred_element_type=F32)
    o = o + jnp.dot(a.astype(BF16), v, preferred_element_type=F32)
    kd = (kh * jnp.exp2(glast - gmid)).astype(BF16)
    st = st * jnp.exp2(glast) + lax.dot_general(v, kd, _TN, preferred_element_type=F32)
    return o, st


def _project_group(u, g, R, valid, prm, slot, recurrence_inputs):
    lb_ref, win_ref, cw_ref, zc_ref, safe_ref = prm
    q_ref, g_ref, k_ref, v_ref, gs_ref, sa_ref, mb_ref, zs_ref = slot
    c0 = g * GROUP_WIDTH
    cols = slice(c0, c0 + GROUP_WIDTH)

    def sec(i):
        w = win_ref[:, i * D_MODEL + c0:i * D_MODEL + c0 + GROUP_WIDTH]
        return jnp.dot(u, w, preferred_element_type=F32)

    if not recurrence_inputs:
        yield from _project_gates(sec, cols, R, valid, cw_ref, zc_ref, slot)
        return

    qv = sec(0)
    q_ref[0:R, :] = qv * _sigmoid(qv)
    yield

    f = lb_ref[0:1, cols] + lb_ref[1:2, cols] * _sigmoid(sec(1))
    k_ref[0:R, :] = 1.0 - f
    lf = jnp.log2(f)
    tri = (lax.broadcasted_iota(jnp.int32, (CHUNK, CHUNK), 1)
           <= lax.broadcasted_iota(jnp.int32, (CHUNK, CHUNK), 0)).astype(BF16)
    tri2 = jnp.concatenate([tri, tri], axis=1)
    worst = jnp.zeros((1, GROUP_WIDTH), F32)
    for c in range(R // CHUNK):
        lfc = lf[c * CHUNK:(c + 1) * CHUNK, :]
        hi = lfc.astype(BF16)
        lo = (lfc - hi.astype(F32)).astype(BF16)
        gc = jnp.dot(tri2, jnp.concatenate([hi, lo], axis=0),
                     preferred_element_type=F32)
        g_ref[c * CHUNK:(c + 1) * CHUNK, :] = gc
        gmid = gc[CHUNK // 2 - 1:CHUNK // 2, :]
        glast = gc[CHUNK - 1:CHUNK, :]
        worst = jnp.maximum(worst, jnp.maximum(-gmid, gmid - glast))
    safe_ref[g] = (jnp.max(worst) <= MAX_SAFE_EXPONENT * LOG2_E).astype(jnp.int32)
    yield

    v_ref[0:R, :] = sec(2).astype(BF16)
    yield


def _project_gates(sec, cols, R, valid, cw_ref, zc_ref, slot):
    gs_ref, sa_ref, mb_ref, zs_ref = slot[4:8]
    gv = sec(3)
    gs_ref[0:R, :] = gv * _sigmoid(gv)
    yield

    bg = sec(4)
    z = sec(5) * sec(6)
    zs_ref[0:SUBLANES, :] = zc_ref[:, cols]
    zs_ref[SUBLANES:SUBLANES + R, :] = z
    z1 = zs_ref[SUBLANES - 1:SUBLANES - 1 + R, :]
    z2 = zs_ref[SUBLANES - 2:SUBLANES - 2 + R, :]
    yb = bg * (cw_ref[0:1, cols] * z2 + cw_ref[1:2, cols] * z1 + cw_ref[2:3, cols] * z)
    zc_ref[:, cols] = zs_ref[valid:valid + SUBLANES, :]
    yield

    sa_ref[0:R, :] = _sigmoid(sec(7))
    yield
    mb_ref[0:R, :] = _sigmoid(sec(8)) * yb
    yield


def _finish_chunk(row0, h, lanes, o, hnw, mg_ref, slot):
    gs_ref, sa_ref, mb_ref = slot[4:7]
    rows = pl.ds(row0, CHUNK)
    m = _rms(o, hnw) * gs_ref[rows, lanes] * sa_ref[rows, lanes] + mb_ref[rows, lanes]
    mg_ref[rows, h * HEAD_DIM:(h + 1) * HEAD_DIM] = m.astype(BF16)


def _recur_group_fast(g, R, hnw, st_ref, st_old_ref, mg_ref, slot):
    q_ref, g_ref, k_ref, v_ref = slot[0:4]
    for hh in range(GROUP_HEADS):
        h = GROUP_HEADS * g + hh
        lanes = slice(hh * HEAD_DIM, (hh + 1) * HEAD_DIM)
        st = st_ref[h]
        st_old_ref[h] = st
        for c in range(R // CHUNK):
            o, st = _hgrn_chunk_fast(c * CHUNK, lanes, st, q_ref, g_ref, k_ref, v_ref)
            mg_ref[c * CHUNK:(c + 1) * CHUNK, h * HEAD_DIM:(h + 1) * HEAD_DIM] = _rms(o, hnw).astype(BF16)
            yield
        st_ref[h] = st


def _finish_group_fast(g, R, mg_ref, slot):
    gs_ref, sa_ref, mb_ref = slot[4:7]
    cols = slice(g * GROUP_WIDTH, (g + 1) * GROUP_WIDTH)
    m = mg_ref[0:R, cols].astype(F32) * gs_ref[0:R, :] * sa_ref[0:R, :] + mb_ref[0:R, :]
    mg_ref[0:R, cols] = m.astype(BF16)


def _recur_group_exact(g, R, valid, hnw, rsel, st_in_ref, st_ref, mg_ref, slot):
    q_ref, g_ref, k_ref, v_ref = slot[0:4]
    for hh in range(GROUP_HEADS):
        h = GROUP_HEADS * g + hh
        lanes = slice(hh * HEAD_DIM, (hh + 1) * HEAD_DIM)
        st = st_in_ref[h]
        if valid < R:
            o, st = _hgrn_chunk_exact(0, lanes, valid, st, q_ref, g_ref, k_ref, v_ref, rsel)
            _finish_chunk(0, h, lanes, o, hnw, mg_ref, slot)
        else:
            def chunk_body(c, st):
                row0 = pl.multiple_of(c * CHUNK, CHUNK)
                o, st = _hgrn_chunk_exact(row0, lanes, CHUNK, st, q_ref, g_ref, k_ref, v_ref, rsel)
                _finish_chunk(row0, h, lanes, o, hnw, mg_ref, slot)
                return st

            st = lax.fori_loop(0, R // CHUNK, chunk_body, st)
        st_ref[h] = st


def _interleave(*gens):
    gens = list(gens)
    while gens:
        for g in list(gens):
            try:
                next(g)
            except StopIteration:
                gens.remove(g)


def _out_partial(g, R, mg_ref, wout_ref):
    rows = slice(g * GROUP_WIDTH, (g + 1) * GROUP_WIDTH)
    return jnp.dot(mg_ref[0:R, rows], wout_ref[rows, 0:D_MODEL], preferred_element_type=F32)


def _mixer_tile(load_x, store_out, R, valid, is_meta, refs):
    (lbp_ref, anw_ref, win_ref, hnw_ref, cw_ref, wout_ref, rsel_ref,
     st_ref, st_old_ref, zc_ref, lb_ref, mg_ref, safe_ref, slots) = refs
    prm = (lb_ref, win_ref, cw_ref, zc_ref, safe_ref)
    u = _rms(load_x(), anw_ref[...]).astype(BF16)
    rsel = rsel_ref[...]
    hnw = hnw_ref[...]
    project = lambda g, first: _interleave(_project_group(u, g, R, valid, prm, slots[g], first))

    lbp = lbp_ref[...]
    mx = jnp.max(lbp, axis=0, keepdims=True)
    ex = jnp.exp(lbp - mx)
    lb = ex[0:1, :] / jnp.sum(ex, axis=0, keepdims=True)
    lb_ref[0:1, :] = lb
    lb_ref[1:2, :] = 1.0 - lb

    if is_meta:
        for g in range(N_GROUPS):
            project(g, True)
            project(g, False)
            _recur_group_exact(g, R, valid, hnw, rsel, st_ref, st_ref, mg_ref, slots[g])
        acc = load_x()
        for g in range(N_GROUPS):
            acc = acc + _out_partial(g, R, mg_ref, wout_ref)
        store_out(acc, False)
        return

    def store_early():
        acc = load_x()
        for g in range(N_GROUPS - 1):
            acc = acc + _out_partial(g, R, mg_ref, wout_ref)
        store_out(acc, False)

    for g in range(N_GROUPS):
        project(g, True)
        if g > 0:
            _finish_group_fast(g - 1, R, mg_ref, slots[g - 1])
            if g == N_GROUPS - 1:
                store_early()
        _interleave(_recur_group_fast(g, R, hnw, st_ref, st_old_ref, mg_ref, slots[g]))
        project(g, False)
    _finish_group_fast(N_GROUPS - 1, R, mg_ref, slots[N_GROUPS - 1])
    for g in range(N_GROUPS):
        def redo(g=g):
            _recur_group_exact(g, R, valid, hnw, rsel, st_old_ref, st_ref, mg_ref, slots[g])
            if g < N_GROUPS - 1:
                store_early()

        pl.when(safe_ref[g] == 0)(redo)
    store_out(_out_partial(N_GROUPS - 1, R, mg_ref, wout_ref), True)


def _mixer_kernel(x_ref, meta_ref, lbp_ref, anw_ref, win_ref, hnw_ref, cw_ref, wout_ref, rsel_ref,
                  wup_f32_ref, wdn_f32_ref, h1_ref, h1m_ref, wup_ref, wdn_ref,
                  st_ref, st_old_ref, zc_ref, lb_ref, mg_ref, safe_ref, *slots):
    wup_ref[...] = wup_f32_ref[...].astype(BF16)
    wdn_ref[:, 0:D_MODEL] = wdn_f32_ref[...].astype(BF16)
    wdn_ref[:, D_MODEL:] = jnp.zeros((W_DOWN_ROWS_PER_STEP, WEIGHT_LANE_PAD), BF16)

    n = len(slots) // N_GROUPS
    refs = (lbp_ref, anw_ref, win_ref, hnw_ref, cw_ref, wout_ref, rsel_ref,
            st_ref, st_old_ref, zc_ref, lb_ref, mg_ref, safe_ref,
            [slots[i * n:(i + 1) * n] for i in range(N_GROUPS)])

    @pl.when(pl.program_id(1) == 0)
    def _():
        st_ref[...] = jnp.zeros_like(st_ref)
        zc_ref[...] = jnp.zeros_like(zc_ref)
        load_meta = lambda: jnp.concatenate(
            [meta_ref[...], jnp.zeros((CHUNK - N_META, D_MODEL), F32)], axis=0)

        def store_meta(val, accumulate):
            h1m_ref[0] = val[0:N_META, :]

        _mixer_tile(load_meta, store_meta, CHUNK, N_META, True, refs)

    def store_tile(val, accumulate):
        h1_ref[0] = h1_ref[0] + val if accumulate else val

    _mixer_tile(lambda: x_ref[0], store_tile, MIXER_TILE, MIXER_TILE, False, refs)


def _ffn_kernel(h1_ref, h1m_ref, fnw_ref, wup_ref, fcw_ref, fcb_ref, wdn_ref, finw_ref,
                out_ref, as_ref):
    T = FFN_TILE
    fnw = fnw_ref[...]

    @pl.when(pl.program_id(1) == 0)
    def _():
        um = _rms(h1m_ref[0], fnw).astype(BF16)
        am = jnp.dot(um, wup_ref[:, 0:D_FF], preferred_element_type=F32)
        as_ref[0:SUBLANES, :] = am[N_META - SUBLANES:N_META, :]

    x = h1_ref[0]
    u = _rms(x, fnw).astype(BF16)
    y = x
    for cb in range(D_FF // FFN_COL_BLOCK):
        cols = slice(cb * FFN_COL_BLOCK, (cb + 1) * FFN_COL_BLOCK)
        a = jnp.dot(u, wup_ref[:, cols], preferred_element_type=F32)
        as_ref[SUBLANES:SUBLANES + T, cols] = a
        a1 = as_ref[SUBLANES - 1:SUBLANES - 1 + T, cols]
        a2 = as_ref[SUBLANES - 2:SUBLANES - 2 + T, cols]
        ac = (fcw_ref[0:1, cols] * a2 + fcw_ref[1:2, cols] * a1 + fcw_ref[2:3, cols] * a
              + fcb_ref[:, cols])
        as_ref[0:SUBLANES, cols] = as_ref[T:T + SUBLANES, cols]
        val = jnp.dot(u, wup_ref[:, D_FF + cb * FFN_COL_BLOCK:D_FF + (cb + 1) * FFN_COL_BLOCK],
                      preferred_element_type=F32)
        gated = (ac * _sigmoid(ac) * val).astype(BF16)
        y = y + jnp.dot(gated, wdn_ref[cols, 0:D_MODEL], preferred_element_type=F32)
    out_ref[0] = _rms(y, finw_ref[...])


def _resident(shape):
    return pl.BlockSpec(shape, lambda b, t: (0,) * len(shape), pipeline_mode=pl.Buffered(1))


def _weight(w):
    return jnp.pad(w.astype(BF16), ((0, 0), (0, WEIGHT_LANE_PAD)))


def _rsel_matrix():
    r = np.arange(N_HEADS * HEAD_DIM)[:, None] // HEAD_DIM
    l = np.arange(CHUNK)[None, :] % SUBLANES
    return jnp.asarray(r == l, dtype=BF16)


def kernel(x, meta_tokens, lb_param, attn_norm_w, w_in, hgrn_norm_w, conv_w, w_out, ffn_norm_w, w_up,
           ffn_conv_w, ffn_conv_b, w_down, final_norm_w):
    B, L, D = x.shape
    assert D == D_MODEL and L % MIXER_TILE == 0 and L % FFN_TILE == 0
    assert w_in.shape == (1, D, N_IN_SECTIONS * D) and w_up.shape == (1, D, 2 * D_FF)
    assert SUBLANES * HEAD_DIM == D_MODEL and N_HEADS == SUBLANES

    tile = lambda T: pl.BlockSpec((1, T, D), lambda b, t: (b, t, 0))
    meta_tile = pl.BlockSpec((1, N_META, D), lambda b, t: (b, 0, 0))
    params = pltpu.CompilerParams(dimension_semantics=("arbitrary", "arbitrary"),
                                  vmem_limit_bytes=VMEM_LIMIT_BYTES)

    TM = MIXER_TILE
    n_steps = B * (L // TM)
    step = lambda b, t: b * (L // TM) + t
    assert D % n_steps == 0 and D_FF % W_DOWN_ROWS_PER_STEP == 0 and D_FF // W_DOWN_ROWS_PER_STEP <= n_steps
    wup_rows = D // n_steps
    wup_block = pl.BlockSpec((wup_rows, 2 * D_FF), lambda b, t: (step(b, t), 0))
    wdn_index = lambda b, t: (jnp.minimum(step(b, t), D_FF // W_DOWN_ROWS_PER_STEP - 1), 0)
    slot = [pltpu.VMEM((TM, GROUP_WIDTH), F32)] * 3 + [pltpu.VMEM((TM, GROUP_WIDTH), BF16)] \
        + [pltpu.VMEM((TM, GROUP_WIDTH), F32)] * 3 + [pltpu.VMEM((TM + SUBLANES, GROUP_WIDTH), F32)]
    h1, h1m, w_up_bf16, w_down_bf16 = pl.pallas_call(
        _mixer_kernel,
        grid=(B, L // TM),
        in_specs=[tile(TM), _resident((N_META, D)), _resident((2, D)), _resident((1, D)),
                  _resident((D, N_IN_SECTIONS * D + WEIGHT_LANE_PAD)), _resident((1, HEAD_DIM)),
                  _resident((3, D)), _resident((D, D + WEIGHT_LANE_PAD)),
                  _resident((N_HEADS * HEAD_DIM, CHUNK)), wup_block,
                  pl.BlockSpec((W_DOWN_ROWS_PER_STEP, D), wdn_index)],
        out_specs=[tile(TM), meta_tile, wup_block,
                   pl.BlockSpec((W_DOWN_ROWS_PER_STEP, D + WEIGHT_LANE_PAD), wdn_index)],
        out_shape=[jax.ShapeDtypeStruct((B, L, D), F32), jax.ShapeDtypeStruct((B, N_META, D), F32),
                   jax.ShapeDtypeStruct((D, 2 * D_FF), BF16),
                   jax.ShapeDtypeStruct((D_FF, D + WEIGHT_LANE_PAD), BF16)],
        scratch_shapes=[pltpu.VMEM((N_HEADS, HEAD_DIM, HEAD_DIM), F32),
                        pltpu.VMEM((N_HEADS, HEAD_DIM, HEAD_DIM), F32),
                        pltpu.VMEM((SUBLANES, D), F32),
                        pltpu.VMEM((SUBLANES, D), F32),
                        pltpu.VMEM((TM, D), BF16),
                        pltpu.SMEM((N_GROUPS,), jnp.int32),
                        *(slot * N_GROUPS)],
        compiler_params=params,
        name="mixer",
    )(x, meta_tokens, lb_param, attn_norm_w, _weight(w_in[0]), hgrn_norm_w, conv_w[0],
      _weight(w_out[0]), _rsel_matrix(), w_up[0], w_down[0])

    TF = FFN_TILE
    out = pl.pallas_call(
        _ffn_kernel,
        grid=(B, L // TF),
        in_specs=[tile(TF), meta_tile, _resident((1, D)), _resident((D, 2 * D_FF)), _resident((3, D_FF)),
                  _resident((1, D_FF)), _resident((D_FF, D + WEIGHT_LANE_PAD)), _resident((1, D))],
        out_specs=tile(TF),
        out_shape=jax.ShapeDtypeStruct((B, L, D), F32),
        scratch_shapes=[pltpu.VMEM((TF + SUBLANES, D_FF), F32)],
        compiler_params=params,
        name="ffn",
    )(h1, h1m, ffn_norm_w, w_up_bf16, ffn_conv_w[0], ffn_conv_b, w_down_bf16, final_norm_w.reshape(1, D))
    return out
```

```python
import numpy as np
import jax
import jax.numpy as jnp
from jax import lax
from jax.experimental import pallas as pl
from jax.experimental.pallas import tpu as pltpu

D_MODEL = 1024
N_META = 16
N_HEADS = 8
HEAD_DIM = 128
D_FF = 2816
N_IN_SECTIONS = 9
EPS = 1e-6

SUBLANES = 8
CHUNK = 128
GROUP_HEADS = 4
GROUP_WIDTH = GROUP_HEADS * HEAD_DIM
N_GROUPS = N_HEADS // GROUP_HEADS
MAX_SAFE_EXPONENT = 60.0
LOG2_E = 1.4426950408889634
MIXER_TILE = 512
FFN_TILE = 512
FFN_COL_BLOCK = 1408
W_IN_SPLIT = N_IN_SECTIONS * D_MODEL // 2
WEIGHT_LANE_PAD = 128
W_DOWN_ROWS_PER_STEP = 128
VMEM_LIMIT_BYTES = 60 * 1024 * 1024

F32 = jnp.float32
BF16 = jnp.bfloat16

_NT = (((1,), (1,)), ((), ()))
_TN = (((0,), (0,)), ((), ()))


def _rms(x, w):
    ms = jnp.mean(x * x, axis=-1, keepdims=True)
    return x * lax.rsqrt(ms + EPS) * w


def _sigmoid(x):
    return 1.0 / (1.0 + jnp.exp2(x * -LOG2_E))


def _bcast_rows(ref, lanes, row0, block, offset, nrows):
    pieces = [jnp.broadcast_to(_row(ref, lanes, row0, i * block + offset), (block, HEAD_DIM))
              for i in range(nrows // block)]
    return pieces[0] if len(pieces) == 1 else jnp.concatenate(pieces, axis=0)


def _row(ref, lanes, row0, r):
    group = ref[pl.ds(row0 + r // SUBLANES * SUBLANES, SUBLANES), lanes]
    return group[r % SUBLANES:r % SUBLANES + 1, :]


def _hgrn_chunk_exact(row0, lanes, valid, st, q_ref, g_ref, k_ref, v_ref, rsel):
    rows = pl.ds(row0, CHUNK)
    q = q_ref[rows, lanes]
    G = g_ref[rows, lanes]
    k = k_ref[rows, lanes]
    v = v_ref[rows, lanes]
    row = lax.broadcasted_iota(jnp.int32, (CHUNK, HEAD_DIM), 0)

    sub = row & (SUBLANES - 1)
    ps = []
    for j in range(SUBLANES):
        gj = _bcast_rows(g_ref, lanes, row0, SUBLANES, j, CHUNK)
        kj = _bcast_rows(k_ref, lanes, row0, SUBLANES, j, CHUNK)
        p = q * kj * jnp.exp2(G - gj)
        ps.append(jnp.where(sub >= j, p, 0.0).astype(BF16))
    pcat = jnp.concatenate(ps, axis=1)
    a = jnp.dot(pcat, rsel, preferred_element_type=F32)

    tl_xor = (lax.broadcasted_iota(jnp.int32, (CHUNK, CHUNK), 0)
              ^ lax.broadcasted_iota(jnp.int32, (CHUNK, CHUNK), 1))
    a = jnp.where(tl_xor < SUBLANES, a, 0.0)

    b = 2 * SUBLANES
    while b <= CHUNK:
        gm = _bcast_rows(g_ref, lanes, row0, b, b // 2 - 1, CHUNK)
        e = jnp.exp2(-jnp.abs(G - gm))
        upper = (row & (b // 2)) != 0
        qt = jnp.where(upper, q * e, 0.0).astype(BF16)
        kt = jnp.where(upper, 0.0, k * e).astype(BF16)
        ab = lax.dot_general(qt, kt, _NT, preferred_element_type=F32)
        a = a + (ab if b == CHUNK else jnp.where(tl_xor < b, ab, 0.0))
        b *= 2

    qi = (q * jnp.exp2(G)).astype(BF16)
    o = lax.dot_general(qi, st.astype(BF16), _NT, preferred_element_type=F32)
    o = o + jnp.dot(a.astype(BF16), v, preferred_element_type=F32)

    glast = _row(g_ref, lanes, row0, valid - 1)
    kd = k * jnp.exp2(glast - G)
    if valid < CHUNK:
        kd = jnp.where(row < valid, kd, 0.0)
    st = st * jnp.exp2(glast) + lax.dot_general(v, kd.astype(BF16), _TN, preferred_element_type=F32)
    return o, st


def _hgrn_chunk_fast(row0, lanes, st, q_ref, g_ref, k_ref, v_ref):
    rows = pl.ds(row0, CHUNK)
    q = q_ref[rows, lanes]
    G = g_ref[rows, lanes]
    k = k_ref[rows, lanes]
    v = v_ref[rows, lanes]
    gmid = g_ref[pl.ds(row0 + CHUNK // 2 - 1, 1), lanes]
    glast = g_ref[pl.ds(row0 + CHUNK - 1, 1), lanes]
    d = G - gmid
    qh = q * jnp.exp2(d)
    kh = k * jnp.exp2(-d)
    a = lax.dot_general(qh.astype(BF16), kh.astype(BF16), _NT, preferred_element_type=F32)
    causal = (lax.broadcasted_iota(jnp.int32, (CHUNK, CHUNK), 1)
              <= lax.broadcasted_iota(jnp.int32, (CHUNK, CHUNK), 0))
    a = jnp.where(causal, a, 0.0)
    qi = (qh * jnp.exp2(gmid)).astype(BF16)
    o = lax.dot_general(qi, st.astype(BF16), _NT, preferred_element_type=F32)
    o = o + jnp.dot(a.astype(BF16), v, preferred_element_type=F32)
    kd = (kh * jnp.exp2(glast - gmid)).astype(BF16)
    st = st * jnp.exp2(glast) + lax.dot_general(v, kd, _TN, preferred_element_type=F32)
    return o, st


def _project_group(u, g, R, valid, prm, slot, recurrence_inputs):
    lb_ref, win_ref, cw_ref, zc_ref, safe_ref = prm
    q_ref, g_ref, k_ref, v_ref, gs_ref, sa_ref, mb_ref, zs_ref = slot
    c0 = g * GROUP_WIDTH
    cols = slice(c0, c0 + GROUP_WIDTH)

    def sec(i):
        col = i * D_MODEL + c0
        half = win_ref[col // W_IN_SPLIT]
        w = half[:, col % W_IN_SPLIT:col % W_IN_SPLIT + GROUP_WIDTH]
        return jnp.dot(u, w, preferred_element_type=F32)

    if not recurrence_inputs:
        yield from _project_gates(sec, cols, R, valid, cw_ref, zc_ref, slot)
        return

    qv = sec(0)
    q_ref[0:R, :] = qv * _sigmoid(qv)
    yield

    f = lb_ref[0:1, cols] + lb_ref[1:2, cols] * _sigmoid(sec(1))
    k_ref[0:R, :] = 1.0 - f
    lf = jnp.log2(f)
    tri = (lax.broadcasted_iota(jnp.int32, (CHUNK, CHUNK), 1)
           <= lax.broadcasted_iota(jnp.int32, (CHUNK, CHUNK), 0)).astype(BF16)
    tri2 = jnp.concatenate([tri, tri], axis=1)
    worst = jnp.zeros((1, GROUP_WIDTH), F32)
    for c in range(R // CHUNK):
        lfc = lf[c * CHUNK:(c + 1) * CHUNK, :]
        hi = lfc.astype(BF16)
        lo = (lfc - hi.astype(F32)).astype(BF16)
        gc = jnp.dot(tri2, jnp.concatenate([hi, lo], axis=0),
                     preferred_element_type=F32)
        g_ref[c * CHUNK:(c + 1) * CHUNK, :] = gc
        gmid = gc[CHUNK // 2 - 1:CHUNK // 2, :]
        glast = gc[CHUNK - 1:CHUNK, :]
        worst = jnp.maximum(worst, jnp.maximum(-gmid, gmid - glast))
    safe_ref[g] = (jnp.max(worst) <= MAX_SAFE_EXPONENT * LOG2_E).astype(jnp.int32)
    yield

    v_ref[0:R, :] = sec(2).astype(BF16)
    yield


def _project_gates(sec, cols, R, valid, cw_ref, zc_ref, slot):
    gs_ref, sa_ref, mb_ref, zs_ref = slot[4:8]
    gv = sec(3)
    gs_ref[0:R, :] = gv * _sigmoid(gv)
    yield

    bg = sec(4)
    z = sec(5) * sec(6)
    zs_ref[0:SUBLANES, :] = zc_ref[:, cols]
    zs_ref[SUBLANES:SUBLANES + R, :] = z
    z1 = zs_ref[SUBLANES - 1:SUBLANES - 1 + R, :]
    z2 = zs_ref[SUBLANES - 2:SUBLANES - 2 + R, :]
    yb = bg * (cw_ref[0:1, cols] * z2 + cw_ref[1:2, cols] * z1 + cw_ref[2:3, cols] * z)
    zc_ref[:, cols] = zs_ref[valid:valid + SUBLANES, :]
    yield

    sa_ref[0:R, :] = _sigmoid(sec(7))
    yield
    mb_ref[0:R, :] = _sigmoid(sec(8)) * yb
    yield


def _finish_chunk(row0, h, lanes, o, hnw, mg_ref, slot):
    gs_ref, sa_ref, mb_ref = slot[4:7]
    rows = pl.ds(row0, CHUNK)
    m = _rms(o, hnw) * gs_ref[rows, lanes] * sa_ref[rows, lanes] + mb_ref[rows, lanes]
    mg_ref[rows, h * HEAD_DIM:(h + 1) * HEAD_DIM] = m.astype(BF16)


def _recur_group_fast(g, R, hnw, st_ref, st_old_ref, mg_ref, slot):
    q_ref, g_ref, k_ref, v_ref = slot[0:4]
    for hh in range(GROUP_HEADS):
        h = GROUP_HEADS * g + hh
        lanes = slice(hh * HEAD_DIM, (hh + 1) * HEAD_DIM)
        st = st_ref[h]
        st_old_ref[h] = st
        for c in range(R // CHUNK):
            o, st = _hgrn_chunk_fast(c * CHUNK, lanes, st, q_ref, g_ref, k_ref, v_ref)
            mg_ref[c * CHUNK:(c + 1) * CHUNK, h * HEAD_DIM:(h + 1) * HEAD_DIM] = _rms(o, hnw).astype(BF16)
            yield
        st_ref[h] = st


def _finish_group_fast(g, R, mg_ref, slot):
    gs_ref, sa_ref, mb_ref = slot[4:7]
    cols = slice(g * GROUP_WIDTH, (g + 1) * GROUP_WIDTH)
    m = mg_ref[0:R, cols].astype(F32) * gs_ref[0:R, :] * sa_ref[0:R, :] + mb_ref[0:R, :]
    mg_ref[0:R, cols] = m.astype(BF16)


def _recur_group_exact(g, R, valid, hnw, rsel, st_in_ref, st_ref, mg_ref, slot):
    q_ref, g_ref, k_ref, v_ref = slot[0:4]
    for hh in range(GROUP_HEADS):
        h = GROUP_HEADS * g + hh
        lanes = slice(hh * HEAD_DIM, (hh + 1) * HEAD_DIM)
        st = st_in_ref[h]
        if valid < R:
            o, st = _hgrn_chunk_exact(0, lanes, valid, st, q_ref, g_ref, k_ref, v_ref, rsel)
            _finish_chunk(0, h, lanes, o, hnw, mg_ref, slot)
        else:
            def chunk_body(c, st):
                row0 = pl.multiple_of(c * CHUNK, CHUNK)
                o, st = _hgrn_chunk_exact(row0, lanes, CHUNK, st, q_ref, g_ref, k_ref, v_ref, rsel)
                _finish_chunk(row0, h, lanes, o, hnw, mg_ref, slot)
                return st

            st = lax.fori_loop(0, R // CHUNK, chunk_body, st)
        st_ref[h] = st


def _interleave(*gens):
    gens = list(gens)
    while gens:
        for g in list(gens):
            try:
                next(g)
            except StopIteration:
                gens.remove(g)


def _out_partial(g, R, mg_ref, wout_ref):
    rows = slice(g * GROUP_WIDTH, (g + 1) * GROUP_WIDTH)
    return jnp.dot(mg_ref[0:R, rows], wout_ref[rows, 0:D_MODEL], preferred_element_type=F32)


def _mixer_tile(load_x, store_out, R, valid, is_meta, refs):
    (lbp_ref, anw_ref, win_ref, hnw_ref, cw_ref, wout_ref, rsel_ref,
     st_ref, st_old_ref, zc_ref, lb_ref, mg_ref, safe_ref, slots) = refs
    prm = (lb_ref, win_ref, cw_ref, zc_ref, safe_ref)
    u = _rms(load_x(), anw_ref[...]).astype(BF16)
    rsel = rsel_ref[...]
    hnw = hnw_ref[...]
    project = lambda g, first: _interleave(_project_group(u, g, R, valid, prm, slots[g], first))

    lbp = lbp_ref[...]
    mx = jnp.max(lbp, axis=0, keepdims=True)
    ex = jnp.exp(lbp - mx)
    lb = ex[0:1, :] / jnp.sum(ex, axis=0, keepdims=True)
    lb_ref[0:1, :] = lb
    lb_ref[1:2, :] = 1.0 - lb

    if is_meta:
        for g in range(N_GROUPS):
            project(g, True)
            project(g, False)
            _recur_group_exact(g, R, valid, hnw, rsel, st_ref, st_ref, mg_ref, slots[g])
        acc = load_x()
        for g in range(N_GROUPS):
            acc = acc + _out_partial(g, R, mg_ref, wout_ref)
        store_out(acc, False)
        return

    def store_early():
        acc = load_x()
        for g in range(N_GROUPS - 1):
            acc = acc + _out_partial(g, R, mg_ref, wout_ref)
        store_out(acc, False)

    for g in range(N_GROUPS):
        project(g, True)
        if g > 0:
            _finish_group_fast(g - 1, R, mg_ref, slots[g - 1])
            if g == N_GROUPS - 1:
                store_early()
        _interleave(_recur_group_fast(g, R, hnw, st_ref, st_old_ref, mg_ref, slots[g]))
        project(g, False)
    _finish_group_fast(N_GROUPS - 1, R, mg_ref, slots[N_GROUPS - 1])
    for g in range(N_GROUPS):
        def redo(g=g):
            _recur_group_exact(g, R, valid, hnw, rsel, st_old_ref, st_ref, mg_ref, slots[g])
            if g < N_GROUPS - 1:
                store_early()

        pl.when(safe_ref[g] == 0)(redo)
    store_out(_out_partial(N_GROUPS - 1, R, mg_ref, wout_ref), True)


def _mixer_kernel(x_ref, meta_ref, lbp_ref, anw_ref, win_lo_ref, win_hi_ref, hnw_ref, cw_ref, wout_ref, rsel_ref,
                  wup_f32_ref, wdn_f32_ref, h1_ref, h1m_ref, wup_ref, wdn_ref,
                  st_ref, st_old_ref, zc_ref, lb_ref, mg_ref, safe_ref, *slots):
    wup_ref[...] = wup_f32_ref[...].astype(BF16)
    wdn_ref[:, 0:D_MODEL] = wdn_f32_ref[...].astype(BF16)
    wdn_ref[:, D_MODEL:] = jnp.zeros((W_DOWN_ROWS_PER_STEP, WEIGHT_LANE_PAD), BF16)

    n = len(slots) // N_GROUPS
    refs = (lbp_ref, anw_ref, (win_lo_ref, win_hi_ref), hnw_ref, cw_ref, wout_ref, rsel_ref,
            st_ref, st_old_ref, zc_ref, lb_ref, mg_ref, safe_ref,
            [slots[i * n:(i + 1) * n] for i in range(N_GROUPS)])

    @pl.when(pl.program_id(1) == 0)
    def _():
        st_ref[...] = jnp.zeros_like(st_ref)
        zc_ref[...] = jnp.zeros_like(zc_ref)
        load_meta = lambda: jnp.concatenate(
            [meta_ref[...], jnp.zeros((CHUNK - N_META, D_MODEL), F32)], axis=0)

        def store_meta(val, accumulate):
            h1m_ref[0] = val[0:N_META, :]

        _mixer_tile(load_meta, store_meta, CHUNK, N_META, True, refs)

    def store_tile(val, accumulate):
        h1_ref[0] = h1_ref[0] + val if accumulate else val

    _mixer_tile(lambda: x_ref[0], store_tile, MIXER_TILE, MIXER_TILE, False, refs)


def _ffn_kernel(h1_ref, h1m_ref, fnw_ref, wup_ref, fcw_ref, fcb_ref, wdn_ref, finw_ref,
                out_ref, as_ref):
    T = FFN_TILE
    fnw = fnw_ref[...]

    @pl.when(pl.program_id(1) == 0)
    def _():
        um = _rms(h1m_ref[0], fnw).astype(BF16)
        am = jnp.dot(um, wup_ref[:, 0:D_FF], preferred_element_type=F32)
        as_ref[0:SUBLANES, :] = am[N_META - SUBLANES:N_META, :]

    x = h1_ref[0]
    u = _rms(x, fnw).astype(BF16)
    y = x
    for cb in range(D_FF // FFN_COL_BLOCK):
        cols = slice(cb * FFN_COL_BLOCK, (cb + 1) * FFN_COL_BLOCK)
        a = jnp.dot(u, wup_ref[:, cols], preferred_element_type=F32)
        as_ref[SUBLANES:SUBLANES + T, cols] = a
        a1 = as_ref[SUBLANES - 1:SUBLANES - 1 + T, cols]
        a2 = as_ref[SUBLANES - 2:SUBLANES - 2 + T, cols]
        ac = (fcw_ref[0:1, cols] * a2 + fcw_ref[1:2, cols] * a1 + fcw_ref[2:3, cols] * a
              + fcb_ref[:, cols])
        as_ref[0:SUBLANES, cols] = as_ref[T:T + SUBLANES, cols]
        val = jnp.dot(u, wup_ref[:, D_FF + cb * FFN_COL_BLOCK:D_FF + (cb + 1) * FFN_COL_BLOCK],
                      preferred_element_type=F32)
        gated = (ac * _sigmoid(ac) * val).astype(BF16)
        y = y + jnp.dot(gated, wdn_ref[cols, 0:D_MODEL], preferred_element_type=F32)
    out_ref[0] = _rms(y, finw_ref[...])


def _resident(shape):
    return pl.BlockSpec(shape, lambda b, t: (0,) * len(shape), pipeline_mode=pl.Buffered(1))


def _weight(w):
    return jnp.concatenate([w.astype(BF16), jnp.zeros((w.shape[0], WEIGHT_LANE_PAD), BF16)], axis=1)


def _rsel_matrix():
    r = np.arange(N_HEADS * HEAD_DIM)[:, None] // HEAD_DIM
    l = np.arange(CHUNK)[None, :] % SUBLANES
    return jnp.asarray(r == l, dtype=BF16)


def kernel(x, meta_tokens, lb_param, attn_norm_w, w_in, hgrn_norm_w, conv_w, w_out, ffn_norm_w, w_up,
           ffn_conv_w, ffn_conv_b, w_down, final_norm_w):
    B, L, D = x.shape
    assert D == D_MODEL and L % MIXER_TILE == 0 and L % FFN_TILE == 0
    assert w_in.shape == (1, D, N_IN_SECTIONS * D) and w_up.shape == (1, D, 2 * D_FF)
    assert SUBLANES * HEAD_DIM == D_MODEL and N_HEADS == SUBLANES and W_IN_SPLIT % GROUP_WIDTH == 0

    tile = lambda T: pl.BlockSpec((1, T, D), lambda b, t: (b, t, 0))
    meta_tile = pl.BlockSpec((1, N_META, D), lambda b, t: (b, 0, 0))
    params = pltpu.CompilerParams(dimension_semantics=("arbitrary", "arbitrary"),
                                  vmem_limit_bytes=VMEM_LIMIT_BYTES)

    TM = MIXER_TILE
    n_steps = B * (L // TM)
    step = lambda b, t: b * (L // TM) + t
    assert D % n_steps == 0 and D_FF % W_DOWN_ROWS_PER_STEP == 0 and D_FF // W_DOWN_ROWS_PER_STEP <= n_steps
    wup_rows = D // n_steps
    wup_block = pl.BlockSpec((wup_rows, 2 * D_FF), lambda b, t: (step(b, t), 0))
    wdn_index = lambda b, t: (jnp.minimum(step(b, t), D_FF // W_DOWN_ROWS_PER_STEP - 1), 0)
    slot = [pltpu.VMEM((TM, GROUP_WIDTH), F32)] * 3 + [pltpu.VMEM((TM, GROUP_WIDTH), BF16)] \
        + [pltpu.VMEM((TM, GROUP_WIDTH), F32)] * 3 + [pltpu.VMEM((TM + SUBLANES, GROUP_WIDTH), F32)]
    h1, h1m, w_up_bf16, w_down_bf16 = pl.pallas_call(
        _mixer_kernel,
        grid=(B, L // TM),
        in_specs=[tile(TM), _resident((N_META, D)), _resident((2, D)), _resident((1, D)),
                  _resident((D, W_IN_SPLIT)), _resident((D, W_IN_SPLIT)), _resident((1, HEAD_DIM)),
                  _resident((3, D)), _resident((D, D + WEIGHT_LANE_PAD)),
                  _resident((N_HEADS * HEAD_DIM, CHUNK)), wup_block,
                  pl.BlockSpec((W_DOWN_ROWS_PER_STEP, D), wdn_index)],
        out_specs=[tile(TM), meta_tile, wup_block,
                   pl.BlockSpec((W_DOWN_ROWS_PER_STEP, D + WEIGHT_LANE_PAD), wdn_index)],
        out_shape=[jax.ShapeDtypeStruct((B, L, D), F32), jax.ShapeDtypeStruct((B, N_META, D), F32),
                   jax.ShapeDtypeStruct((D, 2 * D_FF), BF16),
                   jax.ShapeDtypeStruct((D_FF, D + WEIGHT_LANE_PAD), BF16)],
        scratch_shapes=[pltpu.VMEM((N_HEADS, HEAD_DIM, HEAD_DIM), F32),
                        pltpu.VMEM((N_HEADS, HEAD_DIM, HEAD_DIM), F32),
                        pltpu.VMEM((SUBLANES, D), F32),
                        pltpu.VMEM((SUBLANES, D), F32),
                        pltpu.VMEM((TM, D), BF16),
                        pltpu.SMEM((N_GROUPS,), jnp.int32),
                        *(slot * N_GROUPS)],
        compiler_params=params,
        name="mixer",
    )(x, meta_tokens, lb_param, attn_norm_w, w_in[0][:, :W_IN_SPLIT].astype(BF16),
      w_in[0][:, W_IN_SPLIT:].astype(BF16), hgrn_norm_w, conv_w[0],
      _weight(w_out[0]), _rsel_matrix(), w_up[0], w_down[0])

    TF = FFN_TILE
    out = pl.pallas_call(
        _ffn_kernel,
        grid=(B, L // TF),
        in_specs=[tile(TF), meta_tile, _resident((1, D)), _resident((D, 2 * D_FF)), _resident((3, D_FF)),
                  _resident((1, D_FF)), _resident((D_FF, D + WEIGHT_LANE_PAD)), _resident((1, D))],
        out_specs=tile(TF),
        out_shape=jax.ShapeDtypeStruct((B, L, D), F32),
        scratch_shapes=[pltpu.VMEM((TF + SUBLANES, D_FF), F32)],
        compiler_params=params,
        name="ffn",
    )(h1, h1m, ffn_norm_w, w_up_bf16, ffn_conv_w[0], ffn_conv_b, w_down_bf16, final_norm_w.reshape(1, D))
    return out
```

```python
import numpy as np
import jax
import jax.numpy as jnp
from jax import lax
from jax.experimental import pallas as pl
from jax.experimental.pallas import tpu as pltpu

D_MODEL = 1024
N_META = 16
N_HEADS = 8
HEAD_DIM = 128
D_FF = 2816
N_IN_SECTIONS = 9
EPS = 1e-6

SUBLANES = 8
CHUNK = 128
GROUP_HEADS = 4
GROUP_WIDTH = GROUP_HEADS * HEAD_DIM
N_GROUPS = N_HEADS // GROUP_HEADS
MAX_SAFE_EXPONENT = 60.0
LOG2_E = 1.4426950408889634
MIXER_TILE = 512
FFN_TILE = 512
FFN_COL_BLOCK = 2816
WEIGHT_LANE_PAD = 128
W_DOWN_ROWS_PER_STEP = 128
VMEM_LIMIT_BYTES = 60 * 1024 * 1024

F32 = jnp.float32
BF16 = jnp.bfloat16

_NT = (((1,), (1,)), ((), ()))
_TN = (((0,), (0,)), ((), ()))


def _rms(x, w):
    ms = jnp.mean(x * x, axis=-1, keepdims=True)
    return x * lax.rsqrt(ms + EPS) * w


def _sigmoid(x):
    return 1.0 / (1.0 + jnp.exp2(x * -LOG2_E))


def _bcast_rows(ref, lanes, row0, block, offset, nrows):
    pieces = [jnp.broadcast_to(_row(ref, lanes, row0, i * block + offset), (block, HEAD_DIM))
              for i in range(nrows // block)]
    return pieces[0] if len(pieces) == 1 else jnp.concatenate(pieces, axis=0)


def _row(ref, lanes, row0, r):
    group = ref[pl.ds(row0 + r // SUBLANES * SUBLANES, SUBLANES), lanes]
    return group[r % SUBLANES:r % SUBLANES + 1, :]


def _hgrn_chunk_exact(row0, lanes, valid, st, q_ref, g_ref, k_ref, v_ref, rsel):
    rows = pl.ds(row0, CHUNK)
    q = q_ref[rows, lanes]
    G = g_ref[rows, lanes]
    k = k_ref[rows, lanes]
    v = v_ref[rows, lanes]
    row = lax.broadcasted_iota(jnp.int32, (CHUNK, HEAD_DIM), 0)

    sub = row & (SUBLANES - 1)
    ps = []
    for j in range(SUBLANES):
        gj = _bcast_rows(g_ref, lanes, row0, SUBLANES, j, CHUNK)
        kj = _bcast_rows(k_ref, lanes, row0, SUBLANES, j, CHUNK)
        p = q * kj * jnp.exp2(G - gj)
        ps.append(jnp.where(sub >= j, p, 0.0).astype(BF16))
    pcat = jnp.concatenate(ps, axis=1)
    a = jnp.dot(pcat, rsel, preferred_element_type=F32)

    tl_xor = (lax.broadcasted_iota(jnp.int32, (CHUNK, CHUNK), 0)
              ^ lax.broadcasted_iota(jnp.int32, (CHUNK, CHUNK), 1))
    a = jnp.where(tl_xor < SUBLANES, a, 0.0)

    b = 2 * SUBLANES
    while b <= CHUNK:
        gm = _bcast_rows(g_ref, lanes, row0, b, b // 2 - 1, CHUNK)
        e = jnp.exp2(-jnp.abs(G - gm))
        upper = (row & (b // 2)) != 0
        qt = jnp.where(upper, q * e, 0.0).astype(BF16)
        kt = jnp.where(upper, 0.0, k * e).astype(BF16)
        ab = lax.dot_general(qt, kt, _NT, preferred_element_type=F32)
        a = a + (ab if b == CHUNK else jnp.where(tl_xor < b, ab, 0.0))
        b *= 2

    qi = (q * jnp.exp2(G)).astype(BF16)
    o = lax.dot_general(qi, st.astype(BF16), _NT, preferred_element_type=F32)
    o = o + jnp.dot(a.astype(BF16), v, preferred_element_type=F32)

    glast = _row(g_ref, lanes, row0, valid - 1)
    kd = k * jnp.exp2(glast - G)
    if valid < CHUNK:
        kd = jnp.where(row < valid, kd, 0.0)
    st = st * jnp.exp2(glast) + lax.dot_general(v, kd.astype(BF16), _TN, preferred_element_type=F32)
    return o, st


def _hgrn_chunk_fast(row0, lanes, st, q_ref, g_ref, k_ref, v_ref):
    rows = pl.ds(row0, CHUNK)
    q = q_ref[rows, lanes]
    G = g_ref[rows, lanes]
    k = k_ref[rows, lanes]
    v = v_ref[rows, lanes]
    gmid = g_ref[pl.ds(row0 + CHUNK // 2 - 1, 1), lanes]
    glast = g_ref[pl.ds(row0 + CHUNK - 1, 1), lanes]
    d = G - gmid
    qh = q * jnp.exp2(d)
    kh = k * jnp.exp2(-d)
    a = lax.dot_general(qh.astype(BF16), kh.astype(BF16), _NT, preferred_element_type=F32)
    causal = (lax.broadcasted_iota(jnp.int32, (CHUNK, CHUNK), 1)
              <= lax.broadcasted_iota(jnp.int32, (CHUNK, CHUNK), 0))
    a = jnp.where(causal, a, 0.0)
    qi = (qh * jnp.exp2(gmid)).astype(BF16)
    o = lax.dot_general(qi, st.astype(BF16), _NT, preferred_element_type=F32)
    o = o + jnp.dot(a.astype(BF16), v, preferred_element_type=F32)
    kd = (kh * jnp.exp2(glast - gmid)).astype(BF16)
    st = st * jnp.exp2(glast) + lax.dot_general(v, kd, _TN, preferred_element_type=F32)
    return o, st


def _project_group(u, g, R, valid, prm, slot, recurrence_inputs):
    lb_ref, win_ref, cw_ref, zc_ref, safe_ref = prm
    q_ref, g_ref, k_ref, v_ref, gs_ref, sa_ref, mb_ref, zs_ref = slot
    c0 = g * GROUP_WIDTH
    cols = slice(c0, c0 + GROUP_WIDTH)

    def sec(i):
        w = win_ref[:, i * D_MODEL + c0:i * D_MODEL + c0 + GROUP_WIDTH]
        return jnp.dot(u, w, preferred_element_type=F32)

    if not recurrence_inputs:
        yield from _project_gates(sec, cols, R, valid, cw_ref, zc_ref, slot)
        return

    qv = sec(0)
    q_ref[0:R, :] = qv * _sigmoid(qv)
    yield

    f = lb_ref[0:1, cols] + lb_ref[1:2, cols] * _sigmoid(sec(1))
    k_ref[0:R, :] = 1.0 - f
    lf = jnp.log2(f)
    tri = (lax.broadcasted_iota(jnp.int32, (CHUNK, CHUNK), 1)
           <= lax.broadcasted_iota(jnp.int32, (CHUNK, CHUNK), 0)).astype(BF16)
    tri2 = jnp.concatenate([tri, tri], axis=1)
    worst = jnp.zeros((1, GROUP_WIDTH), F32)
    for c in range(R // CHUNK):
        lfc = lf[c * CHUNK:(c + 1) * CHUNK, :]
        hi = lfc.astype(BF16)
        lo = (lfc - hi.astype(F32)).astype(BF16)
        gc = jnp.dot(tri2, jnp.concatenate([hi, lo], axis=0),
                     preferred_element_type=F32)
        g_ref[c * CHUNK:(c + 1) * CHUNK, :] = gc
        gmid = gc[CHUNK // 2 - 1:CHUNK // 2, :]
        glast = gc[CHUNK - 1:CHUNK, :]
        worst = jnp.maximum(worst, jnp.maximum(-gmid, gmid - glast))
    safe_ref[g] = (jnp.max(worst) <= MAX_SAFE_EXPONENT * LOG2_E).astype(jnp.int32)
    yield

    v_ref[0:R, :] = sec(2).astype(BF16)
    yield


def _project_gates(sec, cols, R, valid, cw_ref, zc_ref, slot):
    gs_ref, sa_ref, mb_ref, zs_ref = slot[4:8]
    gv = sec(3)
    gs_ref[0:R, :] = gv * _sigmoid(gv)
    yield

    bg = sec(4)
    z = sec(5) * sec(6)
    zs_ref[0:SUBLANES, :] = zc_ref[:, cols]
    zs_ref[SUBLANES:SUBLANES + R, :] = z
    z1 = zs_ref[SUBLANES - 1:SUBLANES - 1 + R, :]
    z2 = zs_ref[SUBLANES - 2:SUBLANES - 2 + R, :]
    yb = bg * (cw_ref[0:1, cols] * z2 + cw_ref[1:2, cols] * z1 + cw_ref[2:3, cols] * z)
    zc_ref[:, cols] = zs_ref[valid:valid + SUBLANES, :]
    yield

    sa_ref[0:R, :] = _sigmoid(sec(7))
    yield
    mb_ref[0:R, :] = _sigmoid(sec(8)) * yb
    yield


def _finish_chunk(row0, h, lanes, o, hnw, mg_ref, slot):
    gs_ref, sa_ref, mb_ref = slot[4:7]
    rows = pl.ds(row0, CHUNK)
    m = _rms(o, hnw) * gs_ref[rows, lanes] * sa_ref[rows, lanes] + mb_ref[rows, lanes]
    mg_ref[rows, h * HEAD_DIM:(h + 1) * HEAD_DIM] = m.astype(BF16)


def _recur_group_fast(g, R, hnw, st_ref, st_old_ref, mg_ref, slot):
    q_ref, g_ref, k_ref, v_ref = slot[0:4]
    for hh in range(GROUP_HEADS):
        h = GROUP_HEADS * g + hh
        lanes = slice(hh * HEAD_DIM, (hh + 1) * HEAD_DIM)
        st = st_ref[h]
        st_old_ref[h] = st
        for c in range(R // CHUNK):
            o, st = _hgrn_chunk_fast(c * CHUNK, lanes, st, q_ref, g_ref, k_ref, v_ref)
            mg_ref[c * CHUNK:(c + 1) * CHUNK, h * HEAD_DIM:(h + 1) * HEAD_DIM] = _rms(o, hnw).astype(BF16)
            yield
        st_ref[h] = st


def _finish_group_fast(g, R, mg_ref, slot):
    gs_ref, sa_ref, mb_ref = slot[4:7]
    cols = slice(g * GROUP_WIDTH, (g + 1) * GROUP_WIDTH)
    m = mg_ref[0:R, cols].astype(F32) * gs_ref[0:R, :] * sa_ref[0:R, :] + mb_ref[0:R, :]
    mg_ref[0:R, cols] = m.astype(BF16)


def _recur_group_exact(g, R, valid, hnw, rsel, st_in_ref, st_ref, mg_ref, slot):
    q_ref, g_ref, k_ref, v_ref = slot[0:4]
    for hh in range(GROUP_HEADS):
        h = GROUP_HEADS * g + hh
        lanes = slice(hh * HEAD_DIM, (hh + 1) * HEAD_DIM)
        st = st_in_ref[h]
        if valid < R:
            o, st = _hgrn_chunk_exact(0, lanes, valid, st, q_ref, g_ref, k_ref, v_ref, rsel)
            _finish_chunk(0, h, lanes, o, hnw, mg_ref, slot)
        else:
            def chunk_body(c, st):
                row0 = pl.multiple_of(c * CHUNK, CHUNK)
                o, st = _hgrn_chunk_exact(row0, lanes, CHUNK, st, q_ref, g_ref, k_ref, v_ref, rsel)
                _finish_chunk(row0, h, lanes, o, hnw, mg_ref, slot)
                return st

            st = lax.fori_loop(0, R // CHUNK, chunk_body, st)
        st_ref[h] = st


def _interleave(*gens):
    gens = list(gens)
    while gens:
        for g in list(gens):
            try:
                next(g)
            except StopIteration:
                gens.remove(g)


def _out_partial(g, R, mg_ref, wout_ref):
    rows = slice(g * GROUP_WIDTH, (g + 1) * GROUP_WIDTH)
    return jnp.dot(mg_ref[0:R, rows], wout_ref[rows, 0:D_MODEL], preferred_element_type=F32)


def _mixer_tile(load_x, store_out, R, valid, is_meta, refs):
    (lbp_ref, anw_ref, win_ref, hnw_ref, cw_ref, wout_ref, rsel_ref,
     st_ref, st_old_ref, zc_ref, lb_ref, mg_ref, safe_ref, slots) = refs
    prm = (lb_ref, win_ref, cw_ref, zc_ref, safe_ref)
    u = _rms(load_x(), anw_ref[...]).astype(BF16)
    rsel = rsel_ref[...]
    hnw = hnw_ref[...]
    project = lambda g, first: _interleave(_project_group(u, g, R, valid, prm, slots[g], first))

    lbp = lbp_ref[...]
    mx = jnp.max(lbp, axis=0, keepdims=True)
    ex = jnp.exp(lbp - mx)
    lb = ex[0:1, :] / jnp.sum(ex, axis=0, keepdims=True)
    lb_ref[0:1, :] = lb
    lb_ref[1:2, :] = 1.0 - lb

    if is_meta:
        for g in range(N_GROUPS):
            project(g, True)
            project(g, False)
            _recur_group_exact(g, R, valid, hnw, rsel, st_ref, st_ref, mg_ref, slots[g])
        acc = load_x()
        for g in range(N_GROUPS):
            acc = acc + _out_partial(g, R, mg_ref, wout_ref)
        store_out(acc, False)
        return

    def store_early():
        acc = load_x()
        for g in range(N_GROUPS - 1):
            acc = acc + _out_partial(g, R, mg_ref, wout_ref)
        store_out(acc, False)

    for g in range(N_GROUPS):
        project(g, True)
        if g > 0:
            _finish_group_fast(g - 1, R, mg_ref, slots[g - 1])
            if g == N_GROUPS - 1:
                store_early()
        _interleave(_recur_group_fast(g, R, hnw, st_ref, st_old_ref, mg_ref, slots[g]))
        project(g, False)
    _finish_group_fast(N_GROUPS - 1, R, mg_ref, slots[N_GROUPS - 1])
    for g in range(N_GROUPS):
        def redo(g=g):
            _recur_group_exact(g, R, valid, hnw, rsel, st_old_ref, st_ref, mg_ref, slots[g])
            if g < N_GROUPS - 1:
                store_early()

        pl.when(safe_ref[g] == 0)(redo)
    store_out(_out_partial(N_GROUPS - 1, R, mg_ref, wout_ref), True)


def _mixer_kernel(x_ref, meta_ref, lbp_ref, anw_ref, win_ref, hnw_ref, cw_ref, wout_ref, rsel_ref,
                  wup_f32_ref, wdn_f32_ref, h1_ref, h1m_ref, wup_ref, wdn_ref,
                  st_ref, st_old_ref, zc_ref, lb_ref, mg_ref, safe_ref, *slots):
    wup_ref[...] = wup_f32_ref[...].astype(BF16)
    wdn_ref[:, 0:D_MODEL] = wdn_f32_ref[...].astype(BF16)
    wdn_ref[:, D_MODEL:] = jnp.zeros((W_DOWN_ROWS_PER_STEP, WEIGHT_LANE_PAD), BF16)

    n = len(slots) // N_GROUPS
    refs = (lbp_ref, anw_ref, win_ref, hnw_ref, cw_ref, wout_ref, rsel_ref,
            st_ref, st_old_ref, zc_ref, lb_ref, mg_ref, safe_ref,
            [slots[i * n:(i + 1) * n] for i in range(N_GROUPS)])

    @pl.when(pl.program_id(1) == 0)
    def _():
        st_ref[...] = jnp.zeros_like(st_ref)
        zc_ref[...] = jnp.zeros_like(zc_ref)
        load_meta = lambda: jnp.concatenate(
            [meta_ref[...], jnp.zeros((CHUNK - N_META, D_MODEL), F32)], axis=0)

        def store_meta(val, accumulate):
            h1m_ref[0] = val[0:N_META, :]

        _mixer_tile(load_meta, store_meta, CHUNK, N_META, True, refs)

    def store_tile(val, accumulate):
        h1_ref[0] = h1_ref[0] + val if accumulate else val

    _mixer_tile(lambda: x_ref[0], store_tile, MIXER_TILE, MIXER_TILE, False, refs)


def _ffn_kernel(h1_ref, h1m_ref, fnw_ref, wup_ref, fcw_ref, fcb_ref, wdn_ref, finw_ref,
                out_ref, as_ref):
    T = FFN_TILE
    fnw = fnw_ref[...]

    @pl.when(pl.program_id(1) == 0)
    def _():
        um = _rms(h1m_ref[0], fnw).astype(BF16)
        am = jnp.dot(um, wup_ref[:, 0:D_FF], preferred_element_type=F32)
        as_ref[0:SUBLANES, :] = am[N_META - SUBLANES:N_META, :]

    x = h1_ref[0]
    u = _rms(x, fnw).astype(BF16)
    y = x
    for cb in range(D_FF // FFN_COL_BLOCK):
        cols = slice(cb * FFN_COL_BLOCK, (cb + 1) * FFN_COL_BLOCK)
        a = jnp.dot(u, wup_ref[:, cols], preferred_element_type=F32)
        as_ref[SUBLANES:SUBLANES + T, cols] = a
        a1 = as_ref[SUBLANES - 1:SUBLANES - 1 + T, cols]
        a2 = as_ref[SUBLANES - 2:SUBLANES - 2 + T, cols]
        ac = (fcw_ref[0:1, cols] * a2 + fcw_ref[1:2, cols] * a1 + fcw_ref[2:3, cols] * a
              + fcb_ref[:, cols])
        as_ref[0:SUBLANES, cols] = as_ref[T:T + SUBLANES, cols]
        val = jnp.dot(u, wup_ref[:, D_FF + cb * FFN_COL_BLOCK:D_FF + (cb + 1) * FFN_COL_BLOCK],
                      preferred_element_type=F32)
        gated = (ac * _sigmoid(ac) * val).astype(BF16)
        y = y + jnp.dot(gated, wdn_ref[cols, 0:D_MODEL], preferred_element_type=F32)
    out_ref[0] = _rms(y, finw_ref[...])


def _resident(shape):
    return pl.BlockSpec(shape, lambda b, t: (0,) * len(shape), pipeline_mode=pl.Buffered(1))


def _weight(w):
    return jnp.pad(w.astype(BF16), ((0, 0), (0, WEIGHT_LANE_PAD)))


def _rsel_matrix():
    r = np.arange(N_HEADS * HEAD_DIM)[:, None] // HEAD_DIM
    l = np.arange(CHUNK)[None, :] % SUBLANES
    return jnp.asarray(r == l, dtype=BF16)


def kernel(x, meta_tokens, lb_param, attn_norm_w, w_in, hgrn_norm_w, conv_w, w_out, ffn_norm_w, w_up,
           ffn_conv_w, ffn_conv_b, w_down, final_norm_w):
    B, L, D = x.shape
    assert D == D_MODEL and L % MIXER_TILE == 0 and L % FFN_TILE == 0
    assert w_in.shape == (1, D, N_IN_SECTIONS * D) and w_up.shape == (1, D, 2 * D_FF)
    assert SUBLANES * HEAD_DIM == D_MODEL and N_HEADS == SUBLANES

    tile = lambda T: pl.BlockSpec((1, T, D), lambda b, t: (b, t, 0))
    meta_tile = pl.BlockSpec((1, N_META, D), lambda b, t: (b, 0, 0))
    params = pltpu.CompilerParams(dimension_semantics=("arbitrary", "arbitrary"),
                                  vmem_limit_bytes=VMEM_LIMIT_BYTES)

    TM = MIXER_TILE
    n_steps = B * (L // TM)
    step = lambda b, t: b * (L // TM) + t
    assert D % n_steps == 0 and D_FF % W_DOWN_ROWS_PER_STEP == 0 and D_FF // W_DOWN_ROWS_PER_STEP <= n_steps
    wup_rows = D // n_steps
    wup_block = pl.BlockSpec((wup_rows, 2 * D_FF), lambda b, t: (step(b, t), 0))
    wdn_index = lambda b, t: (jnp.minimum(step(b, t), D_FF // W_DOWN_ROWS_PER_STEP - 1), 0)
    slot = [pltpu.VMEM((TM, GROUP_WIDTH), F32)] * 3 + [pltpu.VMEM((TM, GROUP_WIDTH), BF16)] \
        + [pltpu.VMEM((TM, GROUP_WIDTH), F32)] * 3 + [pltpu.VMEM((TM + SUBLANES, GROUP_WIDTH), F32)]
    h1, h1m, w_up_bf16, w_down_bf16 = pl.pallas_call(
        _mixer_kernel,
        grid=(B, L // TM),
        in_specs=[tile(TM), _resident((N_META, D)), _resident((2, D)), _resident((1, D)),
                  _resident((D, N_IN_SECTIONS * D + WEIGHT_LANE_PAD)), _resident((1, HEAD_DIM)),
                  _resident((3, D)), _resident((D, D + WEIGHT_LANE_PAD)),
                  _resident((N_HEADS * HEAD_DIM, CHUNK)), wup_block,
                  pl.BlockSpec((W_DOWN_ROWS_PER_STEP, D), wdn_index)],
        out_specs=[tile(TM), meta_tile, wup_block,
                   pl.BlockSpec((W_DOWN_ROWS_PER_STEP, D + WEIGHT_LANE_PAD), wdn_index)],
        out_shape=[jax.ShapeDtypeStruct((B, L, D), F32), jax.ShapeDtypeStruct((B, N_META, D), F32),
                   jax.ShapeDtypeStruct((D, 2 * D_FF), BF16),
                   jax.ShapeDtypeStruct((D_FF, D + WEIGHT_LANE_PAD), BF16)],
        scratch_shapes=[pltpu.VMEM((N_HEADS, HEAD_DIM, HEAD_DIM), F32),
                        pltpu.VMEM((N_HEADS, HEAD_DIM, HEAD_DIM), F32),
                        pltpu.VMEM((SUBLANES, D), F32),
                        pltpu.VMEM((SUBLANES, D), F32),
                        pltpu.VMEM((TM, D), BF16),
                        pltpu.SMEM((N_GROUPS,), jnp.int32),
                        *(slot * N_GROUPS)],
        compiler_params=params,
        name="mixer",
    )(x, meta_tokens, lb_param, attn_norm_w, _weight(w_in[0]), hgrn_norm_w, conv_w[0],
      _weight(w_out[0]), _rsel_matrix(), w_up[0], w_down[0])

    TF = FFN_TILE
    out = pl.pallas_call(
        _ffn_kernel,
        grid=(B, L // TF),
        in_specs=[tile(TF), meta_tile, _resident((1, D)), _resident((D, 2 * D_FF)), _resident((3, D_FF)),
                  _resident((1, D_FF)), _resident((D_FF, D + WEIGHT_LANE_PAD)), _resident((1, D))],
        out_specs=tile(TF),
        out_shape=jax.ShapeDtypeStruct((B, L, D), F32),
        scratch_shapes=[pltpu.VMEM((TF + SUBLANES, D_FF), F32)],
        compiler_params=params,
        name="ffn",
    )(h1, h1m, ffn_norm_w, w_up_bf16, ffn_conv_w[0], ffn_conv_b, w_down_bf16, final_norm_w.reshape(1, D))
    return out
```

```python
import numpy as np
import jax
import jax.numpy as jnp
from jax import lax
from jax.experimental import pallas as pl
from jax.experimental.pallas import tpu as pltpu

D_MODEL = 1024
N_META = 16
N_HEADS = 8
HEAD_DIM = 128
D_FF = 2816
N_IN_SECTIONS = 9
EPS = 1e-6

SUBLANES = 8
CHUNK = 128
GROUP_HEADS = 4
GROUP_WIDTH = GROUP_HEADS * HEAD_DIM
N_GROUPS = N_HEADS // GROUP_HEADS
MAX_SAFE_EXPONENT = 60.0
LOG2_E = 1.4426950408889634
MIXER_TILE = 512
FFN_TILE = 512
FFN_COL_BLOCK = 2816
WEIGHT_LANE_PAD = 128
W_DOWN_ROWS_PER_STEP = 128
V7X_VMEM_BYTES = 64 * 1024 * 1024
VMEM_LIMIT_BYTES = V7X_VMEM_BYTES - 4 * 1024 * 1024

F32 = jnp.float32
BF16 = jnp.bfloat16

_NT = (((1,), (1,)), ((), ()))
_TN = (((0,), (0,)), ((), ()))


def _rms(x, w):
    ms = jnp.mean(x * x, axis=-1, keepdims=True)
    return x * lax.rsqrt(ms + EPS) * w


def _sigmoid(x):
    return 1.0 / (1.0 + jnp.exp2(x * -LOG2_E))


def _bcast_rows(ref, lanes, row0, block, offset, nrows):
    pieces = [jnp.broadcast_to(_row(ref, lanes, row0, i * block + offset), (block, HEAD_DIM))
              for i in range(nrows // block)]
    return pieces[0] if len(pieces) == 1 else jnp.concatenate(pieces, axis=0)


def _row(ref, lanes, row0, r):
    group = ref[pl.ds(row0 + r // SUBLANES * SUBLANES, SUBLANES), lanes]
    return group[r % SUBLANES:r % SUBLANES + 1, :]


def _hgrn_chunk_exact(row0, lanes, valid, st, q_ref, g_ref, k_ref, v_ref, rsel):
    rows = pl.ds(row0, CHUNK)
    q = q_ref[rows, lanes]
    G = g_ref[rows, lanes]
    k = k_ref[rows, lanes]
    v = v_ref[rows, lanes]
    row = lax.broadcasted_iota(jnp.int32, (CHUNK, HEAD_DIM), 0)

    sub = row & (SUBLANES - 1)
    ps = []
    for j in range(SUBLANES):
        gj = _bcast_rows(g_ref, lanes, row0, SUBLANES, j, CHUNK)
        kj = _bcast_rows(k_ref, lanes, row0, SUBLANES, j, CHUNK)
        p = q * kj * jnp.exp2(G - gj)
        ps.append(jnp.where(sub >= j, p, 0.0).astype(BF16))
    pcat = jnp.concatenate(ps, axis=1)
    a = jnp.dot(pcat, rsel, preferred_element_type=F32)

    tl_xor = (lax.broadcasted_iota(jnp.int32, (CHUNK, CHUNK), 0)
              ^ lax.broadcasted_iota(jnp.int32, (CHUNK, CHUNK), 1))
    a = jnp.where(tl_xor < SUBLANES, a, 0.0)

    b = 2 * SUBLANES
    while b <= CHUNK:
        gm = _bcast_rows(g_ref, lanes, row0, b, b // 2 - 1, CHUNK)
        e = jnp.exp2(-jnp.abs(G - gm))
        upper = (row & (b // 2)) != 0
        qt = jnp.where(upper, q * e, 0.0).astype(BF16)
        kt = jnp.where(upper, 0.0, k * e).astype(BF16)
        ab = lax.dot_general(qt, kt, _NT, preferred_element_type=F32)
        a = a + (ab if b == CHUNK else jnp.where(tl_xor < b, ab, 0.0))
        b *= 2

    qi = (q * jnp.exp2(G)).astype(BF16)
    o = lax.dot_general(qi, st.astype(BF16), _NT, preferred_element_type=F32)
    o = o + jnp.dot(a.astype(BF16), v, preferred_element_type=F32)

    glast = _row(g_ref, lanes, row0, valid - 1)
    kd = k * jnp.exp2(glast - G)
    if valid < CHUNK:
        kd = jnp.where(row < valid, kd, 0.0)
    st = st * jnp.exp2(glast) + lax.dot_general(v, kd.astype(BF16), _TN, preferred_element_type=F32)
    return o, st


def _hgrn_chunk_fast(row0, lanes, st, q_ref, g_ref, k_ref, v_ref):
    rows = pl.ds(row0, CHUNK)
    q = q_ref[rows, lanes]
    G = g_ref[rows, lanes]
    k = k_ref[rows, lanes]
    v = v_ref[rows, lanes]
    gmid = g_ref[pl.ds(row0 + CHUNK // 2 - 1, 1), lanes]
    glast = g_ref[pl.ds(row0 + CHUNK - 1, 1), lanes]
    d = G - gmid
    qh = q * jnp.exp2(d)
    kh = k * jnp.exp2(-d)
    a = lax.dot_general(qh.astype(BF16), kh.astype(BF16), _NT, preferred_element_type=F32)
    causal = (lax.broadcasted_iota(jnp.int32, (CHUNK, CHUNK), 1)
              <= lax.broadcasted_iota(jnp.int32, (CHUNK, CHUNK), 0))
    a = jnp.where(causal, a, 0.0)
    qi = (qh * jnp.exp2(gmid)).astype(BF16)
    o = lax.dot_general(qi, st.astype(BF16), _NT, preferred_element_type=F32)
    o = o + jnp.dot(a.astype(BF16), v, preferred_element_type=F32)
    kd = (kh * jnp.exp2(glast - gmid)).astype(BF16)
    st = st * jnp.exp2(glast) + lax.dot_general(v, kd, _TN, preferred_element_type=F32)
    return o, st


def _project_group(u, g, R, valid, prm, slot, recurrence_inputs):
    lb_ref, win_ref, cw_ref, zc_ref, safe_ref = prm
    q_ref, g_ref, k_ref, v_ref, gs_ref, sa_ref, mb_ref, zs_ref = slot
    c0 = g * GROUP_WIDTH
    cols = slice(c0, c0 + GROUP_WIDTH)

    def sec(i):
        w = win_ref[:, i * D_MODEL + c0:i * D_MODEL + c0 + GROUP_WIDTH]
        return jnp.dot(u, w, preferred_element_type=F32)

    if not recurrence_inputs:
        _project_gates(sec, cols, R, valid, cw_ref, zc_ref, slot)
        return

    qv = sec(0)
    q_ref[0:R, :] = qv * _sigmoid(qv)

    f = lb_ref[0:1, cols] + lb_ref[1:2, cols] * _sigmoid(sec(1))
    k_ref[0:R, :] = 1.0 - f
    lf = jnp.log2(f)
    tri = (lax.broadcasted_iota(jnp.int32, (CHUNK, CHUNK), 1)
           <= lax.broadcasted_iota(jnp.int32, (CHUNK, CHUNK), 0)).astype(BF16)
    tri2 = jnp.concatenate([tri, tri], axis=1)
    worst = jnp.zeros((1, GROUP_WIDTH), F32)
    for c in range(R // CHUNK):
        lfc = lf[c * CHUNK:(c + 1) * CHUNK, :]
        hi = lfc.astype(BF16)
        lo = (lfc - hi.astype(F32)).astype(BF16)
        gc = jnp.dot(tri2, jnp.concatenate([hi, lo], axis=0),
                     preferred_element_type=F32)
        g_ref[c * CHUNK:(c + 1) * CHUNK, :] = gc
        gmid = gc[CHUNK // 2 - 1:CHUNK // 2, :]
        glast = gc[CHUNK - 1:CHUNK, :]
        worst = jnp.maximum(worst, jnp.maximum(-gmid, gmid - glast))
    safe_ref[g] = (jnp.max(worst) <= MAX_SAFE_EXPONENT * LOG2_E).astype(jnp.int32)

    v_ref[0:R, :] = sec(2).astype(BF16)


def _project_gates(sec, cols, R, valid, cw_ref, zc_ref, slot):
    gs_ref, sa_ref, mb_ref, zs_ref = slot[4:8]
    gv = sec(3)
    gs_ref[0:R, :] = gv * _sigmoid(gv)

    bg = sec(4)
    z = sec(5) * sec(6)
    zs_ref[0:SUBLANES, :] = zc_ref[:, cols]
    zs_ref[SUBLANES:SUBLANES + R, :] = z
    z1 = zs_ref[SUBLANES - 1:SUBLANES - 1 + R, :]
    z2 = zs_ref[SUBLANES - 2:SUBLANES - 2 + R, :]
    yb = bg * (cw_ref[0:1, cols] * z2 + cw_ref[1:2, cols] * z1 + cw_ref[2:3, cols] * z)
    zc_ref[:, cols] = zs_ref[valid:valid + SUBLANES, :]

    sa_ref[0:R, :] = _sigmoid(sec(7))
    mb_ref[0:R, :] = _sigmoid(sec(8)) * yb


def _finish_chunk(row0, h, lanes, o, hnw, mg_ref, slot):
    gs_ref, sa_ref, mb_ref = slot[4:7]
    rows = pl.ds(row0, CHUNK)
    m = _rms(o, hnw) * gs_ref[rows, lanes] * sa_ref[rows, lanes] + mb_ref[rows, lanes]
    mg_ref[rows, h * HEAD_DIM:(h + 1) * HEAD_DIM] = m.astype(BF16)


def _recur_group_fast(g, R, hnw, st_ref, st_old_ref, mg_ref, slot):
    q_ref, g_ref, k_ref, v_ref = slot[0:4]
    for hh in range(GROUP_HEADS):
        h = GROUP_HEADS * g + hh
        lanes = slice(hh * HEAD_DIM, (hh + 1) * HEAD_DIM)
        st = st_ref[h]
        st_old_ref[h] = st
        for c in range(R // CHUNK):
            o, st = _hgrn_chunk_fast(c * CHUNK, lanes, st, q_ref, g_ref, k_ref, v_ref)
            mg_ref[c * CHUNK:(c + 1) * CHUNK, h * HEAD_DIM:(h + 1) * HEAD_DIM] = _rms(o, hnw).astype(BF16)
        st_ref[h] = st


def _finish_group_fast(g, R, mg_ref, slot):
    gs_ref, sa_ref, mb_ref = slot[4:7]
    cols = slice(g * GROUP_WIDTH, (g + 1) * GROUP_WIDTH)
    m = mg_ref[0:R, cols].astype(F32) * gs_ref[0:R, :] * sa_ref[0:R, :] + mb_ref[0:R, :]
    mg_ref[0:R, cols] = m.astype(BF16)


def _recur_group_exact(g, R, valid, hnw, rsel, st_in_ref, st_ref, mg_ref, slot):
    q_ref, g_ref, k_ref, v_ref = slot[0:4]
    for hh in range(GROUP_HEADS):
        h = GROUP_HEADS * g + hh
        lanes = slice(hh * HEAD_DIM, (hh + 1) * HEAD_DIM)
        st = st_in_ref[h]
        if valid < R:
            o, st = _hgrn_chunk_exact(0, lanes, valid, st, q_ref, g_ref, k_ref, v_ref, rsel)
            _finish_chunk(0, h, lanes, o, hnw, mg_ref, slot)
        else:
            def chunk_body(c, st):
                row0 = pl.multiple_of(c * CHUNK, CHUNK)
                o, st = _hgrn_chunk_exact(row0, lanes, CHUNK, st, q_ref, g_ref, k_ref, v_ref, rsel)
                _finish_chunk(row0, h, lanes, o, hnw, mg_ref, slot)
                return st

            st = lax.fori_loop(0, R // CHUNK, chunk_body, st)
        st_ref[h] = st


def _out_partial(g, R, mg_ref, wout_ref):
    rows = slice(g * GROUP_WIDTH, (g + 1) * GROUP_WIDTH)
    return jnp.dot(mg_ref[0:R, rows], wout_ref[rows, 0:D_MODEL], preferred_element_type=F32)


def _mixer_tile(load_x, store_out, R, valid, is_meta, refs):
    (lbp_ref, anw_ref, win_ref, hnw_ref, cw_ref, wout_ref, rsel_ref,
     st_ref, st_old_ref, zc_ref, lb_ref, mg_ref, safe_ref, slots) = refs
    prm = (lb_ref, win_ref, cw_ref, zc_ref, safe_ref)
    u = _rms(load_x(), anw_ref[...]).astype(BF16)
    rsel = rsel_ref[...]
    hnw = hnw_ref[...]
    project = lambda g, first: _project_group(u, g, R, valid, prm, slots[g], first)

    lbp = lbp_ref[...]
    mx = jnp.max(lbp, axis=0, keepdims=True)
    ex = jnp.exp(lbp - mx)
    lb = ex[0:1, :] / jnp.sum(ex, axis=0, keepdims=True)
    lb_ref[0:1, :] = lb
    lb_ref[1:2, :] = 1.0 - lb

    if is_meta:
        for g in range(N_GROUPS):
            project(g, True)
            project(g, False)
            _recur_group_exact(g, R, valid, hnw, rsel, st_ref, st_ref, mg_ref, slots[g])
        acc = load_x()
        for g in range(N_GROUPS):
            acc = acc + _out_partial(g, R, mg_ref, wout_ref)
        store_out(acc, False)
        return

    def store_early():
        acc = load_x()
        for g in range(N_GROUPS - 1):
            acc = acc + _out_partial(g, R, mg_ref, wout_ref)
        store_out(acc, False)

    for g in range(N_GROUPS):
        project(g, True)
        if g > 0:
            _finish_group_fast(g - 1, R, mg_ref, slots[g - 1])
            if g == N_GROUPS - 1:
                store_early()
        _recur_group_fast(g, R, hnw, st_ref, st_old_ref, mg_ref, slots[g])
        project(g, False)
    _finish_group_fast(N_GROUPS - 1, R, mg_ref, slots[N_GROUPS - 1])
    for g in range(N_GROUPS):
        def redo(g=g):
            _recur_group_exact(g, R, valid, hnw, rsel, st_old_ref, st_ref, mg_ref, slots[g])
            if g < N_GROUPS - 1:
                store_early()

        pl.when(safe_ref[g] == 0)(redo)
    store_out(_out_partial(N_GROUPS - 1, R, mg_ref, wout_ref), True)


def _mixer_kernel(x_ref, meta_ref, lbp_ref, anw_ref, win_ref, hnw_ref, cw_ref, wout_ref, rsel_ref,
                  wup_f32_ref, wdn_f32_ref, h1_ref, h1m_ref, wup_ref, wdn_ref,
                  st_ref, st_old_ref, zc_ref, lb_ref, mg_ref, safe_ref, *slots):
    wup_ref[...] = wup_f32_ref[...].astype(BF16)
    wdn_ref[:, 0:D_MODEL] = wdn_f32_ref[...].astype(BF16)
    wdn_ref[:, D_MODEL:] = jnp.zeros((W_DOWN_ROWS_PER_STEP, WEIGHT_LANE_PAD), BF16)

    n = len(slots) // N_GROUPS
    refs = (lbp_ref, anw_ref, win_ref, hnw_ref, cw_ref, wout_ref, rsel_ref,
            st_ref, st_old_ref, zc_ref, lb_ref, mg_ref, safe_ref,
            [slots[i * n:(i + 1) * n] for i in range(N_GROUPS)])

    @pl.when(pl.program_id(1) == 0)
    def _():
        st_ref[...] = jnp.zeros_like(st_ref)
        zc_ref[...] = jnp.zeros_like(zc_ref)
        load_meta = lambda: jnp.concatenate(
            [meta_ref[...], jnp.zeros((CHUNK - N_META, D_MODEL), F32)], axis=0)

        def store_meta(val, accumulate):
            h1m_ref[0] = val[0:N_META, :]

        _mixer_tile(load_meta, store_meta, CHUNK, N_META, True, refs)

    def store_tile(val, accumulate):
        h1_ref[0] = h1_ref[0] + val if accumulate else val

    _mixer_tile(lambda: x_ref[0], store_tile, MIXER_TILE, MIXER_TILE, False, refs)


def _ffn_kernel(h1_ref, h1m_ref, fnw_ref, wup_ref, fcw_ref, fcb_ref, wdn_ref, finw_ref,
                out_ref, as_ref):
    T = FFN_TILE
    fnw = fnw_ref[...]

    @pl.when(pl.program_id(1) == 0)
    def _():
        um = _rms(h1m_ref[0], fnw).astype(BF16)
        am = jnp.dot(um, wup_ref[:, 0:D_FF], preferred_element_type=F32)
        as_ref[0:SUBLANES, :] = am[N_META - SUBLANES:N_META, :]

    x = h1_ref[0]
    u = _rms(x, fnw).astype(BF16)
    y = x
    for cb in range(D_FF // FFN_COL_BLOCK):
        cols = slice(cb * FFN_COL_BLOCK, (cb + 1) * FFN_COL_BLOCK)
        a = jnp.dot(u, wup_ref[:, cols], preferred_element_type=F32)
        as_ref[SUBLANES:SUBLANES + T, cols] = a
        a1 = as_ref[SUBLANES - 1:SUBLANES - 1 + T, cols]
        a2 = as_ref[SUBLANES - 2:SUBLANES - 2 + T, cols]
        ac = (fcw_ref[0:1, cols] * a2 + fcw_ref[1:2, cols] * a1 + fcw_ref[2:3, cols] * a
              + fcb_ref[:, cols])
        as_ref[0:SUBLANES, cols] = as_ref[T:T + SUBLANES, cols]
        val = jnp.dot(u, wup_ref[:, D_FF + cb * FFN_COL_BLOCK:D_FF + (cb + 1) * FFN_COL_BLOCK],
                      preferred_element_type=F32)
        gated = (ac * _sigmoid(ac) * val).astype(BF16)
        y = y + jnp.dot(gated, wdn_ref[cols, 0:D_MODEL], preferred_element_type=F32)
    out_ref[0] = _rms(y, finw_ref[...])


def _resident(shape):
    return pl.BlockSpec(shape, lambda b, t: (0,) * len(shape), pipeline_mode=pl.Buffered(1))


def _weight(w):
    return jnp.pad(w.astype(BF16), ((0, 0), (0, WEIGHT_LANE_PAD)))


def _rsel_matrix():
    r = np.arange(N_HEADS * HEAD_DIM)[:, None] // HEAD_DIM
    l = np.arange(CHUNK)[None, :] % SUBLANES
    return jnp.asarray(r == l, dtype=BF16)


def kernel(x, meta_tokens, lb_param, attn_norm_w, w_in, hgrn_norm_w, conv_w, w_out, ffn_norm_w, w_up,
           ffn_conv_w, ffn_conv_b, w_down, final_norm_w):
    B, L, D = x.shape
    assert D == D_MODEL and L % MIXER_TILE == 0 and L % FFN_TILE == 0
    assert w_in.shape == (1, D, N_IN_SECTIONS * D) and w_up.shape == (1, D, 2 * D_FF)
    assert SUBLANES * HEAD_DIM == D_MODEL and N_HEADS == SUBLANES

    tile = lambda T: pl.BlockSpec((1, T, D), lambda b, t: (b, t, 0))
    meta_tile = pl.BlockSpec((1, N_META, D), lambda b, t: (b, 0, 0))
    params = pltpu.CompilerParams(dimension_semantics=("arbitrary", "arbitrary"),
                                  vmem_limit_bytes=VMEM_LIMIT_BYTES)

    TM = MIXER_TILE
    n_steps = B * (L // TM)
    step = lambda b, t: b * (L // TM) + t
    assert D % n_steps == 0 and D_FF % W_DOWN_ROWS_PER_STEP == 0 and D_FF // W_DOWN_ROWS_PER_STEP <= n_steps
    wup_rows = D // n_steps
    wup_block = pl.BlockSpec((wup_rows, 2 * D_FF), lambda b, t: (step(b, t), 0))
    wdn_index = lambda b, t: (jnp.minimum(step(b, t), D_FF // W_DOWN_ROWS_PER_STEP - 1), 0)
    slot = [pltpu.VMEM((TM, GROUP_WIDTH), F32)] * 3 + [pltpu.VMEM((TM, GROUP_WIDTH), BF16)] \
        + [pltpu.VMEM((TM, GROUP_WIDTH), F32)] * 3 + [pltpu.VMEM((TM + SUBLANES, GROUP_WIDTH), F32)]
    h1, h1m, w_up_bf16, w_down_bf16 = pl.pallas_call(
        _mixer_kernel,
        grid=(B, L // TM),
        in_specs=[tile(TM), _resident((N_META, D)), _resident((2, D)), _resident((1, D)),
                  _resident((D, N_IN_SECTIONS * D + WEIGHT_LANE_PAD)), _resident((1, HEAD_DIM)),
                  _resident((3, D)), _resident((D, D + WEIGHT_LANE_PAD)),
                  _resident((N_HEADS * HEAD_DIM, CHUNK)), wup_block,
                  pl.BlockSpec((W_DOWN_ROWS_PER_STEP, D), wdn_index)],
        out_specs=[tile(TM), meta_tile, wup_block,
                   pl.BlockSpec((W_DOWN_ROWS_PER_STEP, D + WEIGHT_LANE_PAD), wdn_index)],
        out_shape=[jax.ShapeDtypeStruct((B, L, D), F32), jax.ShapeDtypeStruct((B, N_META, D), F32),
                   jax.ShapeDtypeStruct((D, 2 * D_FF), BF16),
                   jax.ShapeDtypeStruct((D_FF, D + WEIGHT_LANE_PAD), BF16)],
        scratch_shapes=[pltpu.VMEM((N_HEADS, HEAD_DIM, HEAD_DIM), F32),
                        pltpu.VMEM((N_HEADS, HEAD_DIM, HEAD_DIM), F32),
                        pltpu.VMEM((SUBLANES, D), F32),
                        pltpu.VMEM((SUBLANES, D), F32),
                        pltpu.VMEM((TM, D), BF16),
                        pltpu.SMEM((N_GROUPS,), jnp.int32),
                        *(slot * N_GROUPS)],
        compiler_params=params,
        name="mixer",
    )(x, meta_tokens, lb_param, attn_norm_w, _weight(w_in[0]), hgrn_norm_w, conv_w[0],
      _weight(w_out[0]), _rsel_matrix(), w_up[0], w_down[0])

    TF = FFN_TILE
    out = pl.pallas_call(
        _ffn_kernel,
        grid=(B, L // TF),
        in_specs=[tile(TF), meta_tile, _resident((1, D)), _resident((D, 2 * D_FF)), _resident((3, D_FF)),
                  _resident((1, D_FF)), _resident((D_FF, D + WEIGHT_LANE_PAD)), _resident((1, D))],
        out_specs=tile(TF),
        out_shape=jax.ShapeDtypeStruct((B, L, D), F32),
        scratch_shapes=[pltpu.VMEM((TF + SUBLANES, D_FF), F32)],
        compiler_params=params,
        name="ffn",
    )(h1, h1m, ffn_norm_w, w_up_bf16, ffn_conv_w[0], ffn_conv_b, w_down_bf16, final_norm_w.reshape(1, D))
    return out
```

```python
import numpy as np
import jax
import jax.numpy as jnp
from jax import lax
from jax.experimental import pallas as pl
from jax.experimental.pallas import tpu as pltpu

D_MODEL = 1024
N_META = 16
N_HEADS = 8
HEAD_DIM = 128
D_FF = 2816
N_IN_SECTIONS = 9
EPS = 1e-6

SUBLANES = 8
CHUNK = 128
GROUP_HEADS = 4
GROUP_WIDTH = GROUP_HEADS * HEAD_DIM
N_GROUPS = N_HEADS // GROUP_HEADS
MAX_SAFE_EXPONENT = 60.0
LOG2_E = 1.4426950408889634
MIXER_TILE = 512
FFN_TILE = 512
FFN_COL_BLOCK = 2816
WEIGHT_LANE_PAD = 128
W_DOWN_ROWS_PER_STEP = 128
V7X_VMEM_BYTES = 64 * 1024 * 1024
VMEM_LIMIT_BYTES = V7X_VMEM_BYTES - 4 * 1024 * 1024

F32 = jnp.float32
BF16 = jnp.bfloat16

_NT = (((1,), (1,)), ((), ()))
_TN = (((0,), (0,)), ((), ()))


def _rms(x, w):
    ms = jnp.mean(x * x, axis=-1, keepdims=True)
    return x * lax.rsqrt(ms + EPS) * w


def _sigmoid(x):
    return 1.0 / (1.0 + jnp.exp2(x * -LOG2_E))


def _bcast_rows(ref, lanes, row0, block, offset, nrows):
    pieces = [jnp.broadcast_to(_row(ref, lanes, row0, i * block + offset), (block, HEAD_DIM))
              for i in range(nrows // block)]
    return pieces[0] if len(pieces) == 1 else jnp.concatenate(pieces, axis=0)


def _row(ref, lanes, row0, r):
    group = ref[pl.ds(row0 + r // SUBLANES * SUBLANES, SUBLANES), lanes]
    return group[r % SUBLANES:r % SUBLANES + 1, :]


def _hgrn_chunk_exact(row0, lanes, valid, st, q_ref, g_ref, k_ref, v_ref, rsel):
    rows = pl.ds(row0, CHUNK)
    q = q_ref[rows, lanes]
    G = g_ref[rows, lanes]
    k = k_ref[rows, lanes]
    v = v_ref[rows, lanes]
    row = lax.broadcasted_iota(jnp.int32, (CHUNK, HEAD_DIM), 0)

    sub = row & (SUBLANES - 1)
    ps = []
    for j in range(SUBLANES):
        gj = _bcast_rows(g_ref, lanes, row0, SUBLANES, j, CHUNK)
        kj = _bcast_rows(k_ref, lanes, row0, SUBLANES, j, CHUNK)
        p = q * kj * jnp.exp2(G - gj)
        ps.append(jnp.where(sub >= j, p, 0.0).astype(BF16))
    pcat = jnp.concatenate(ps, axis=1)
    a = jnp.dot(pcat, rsel, preferred_element_type=F32)

    tl_xor = (lax.broadcasted_iota(jnp.int32, (CHUNK, CHUNK), 0)
              ^ lax.broadcasted_iota(jnp.int32, (CHUNK, CHUNK), 1))
    a = jnp.where(tl_xor < SUBLANES, a, 0.0)

    b = 2 * SUBLANES
    while b <= CHUNK:
        gm = _bcast_rows(g_ref, lanes, row0, b, b // 2 - 1, CHUNK)
        e = jnp.exp2(-jnp.abs(G - gm))
        upper = (row & (b // 2)) != 0
        qt = jnp.where(upper, q * e, 0.0).astype(BF16)
        kt = jnp.where(upper, 0.0, k * e).astype(BF16)
        ab = lax.dot_general(qt, kt, _NT, preferred_element_type=F32)
        a = a + (ab if b == CHUNK else jnp.where(tl_xor < b, ab, 0.0))
        b *= 2

    qi = (q * jnp.exp2(G)).astype(BF16)
    o = lax.dot_general(qi, st.astype(BF16), _NT, preferred_element_type=F32)
    o = o + jnp.dot(a.astype(BF16), v, preferred_element_type=F32)

    glast = _row(g_ref, lanes, row0, valid - 1)
    kd = k * jnp.exp2(glast - G)
    if valid < CHUNK:
        kd = jnp.where(row < valid, kd, 0.0)
    st = st * jnp.exp2(glast) + lax.dot_general(v, kd.astype(BF16), _TN, preferred_element_type=F32)
    return o, st


def _hgrn_chunk_fast(row0, lanes, st, q_ref, g_ref, k_ref, v_ref):
    rows = pl.ds(row0, CHUNK)
    q = q_ref[rows, lanes]
    G = g_ref[rows, lanes]
    k = k_ref[rows, lanes]
    v = v_ref[rows, lanes]
    gmid = g_ref[pl.ds(row0 + CHUNK // 2 - 1, 1), lanes]
    glast = g_ref[pl.ds(row0 + CHUNK - 1, 1), lanes]
    d = G - gmid
    qh = q * jnp.exp2(d)
    kh = k * jnp.exp2(-d)
    a = lax.dot_general(qh.astype(BF16), kh.astype(BF16), _NT, preferred_element_type=F32)
    causal = (lax.broadcasted_iota(jnp.int32, (CHUNK, CHUNK), 1)
              <= lax.broadcasted_iota(jnp.int32, (CHUNK, CHUNK), 0))
    a = jnp.where(causal, a, 0.0)
    qi = (qh * jnp.exp2(gmid)).astype(BF16)
    o = lax.dot_general(qi, st.astype(BF16), _NT, preferred_element_type=F32)
    o = o + jnp.dot(a.astype(BF16), v, preferred_element_type=F32)
    kd = (kh * jnp.exp2(glast - gmid)).astype(BF16)
    st = st * jnp.exp2(glast) + lax.dot_general(v, kd, _TN, preferred_element_type=F32)
    return o, st


def _project_group(u, g, R, valid, prm, slot, recurrence_inputs):
    lb_ref, win_ref, cw_ref, zc_ref, safe_ref = prm
    q_ref, g_ref, k_ref, v_ref, gs_ref, sa_ref, mb_ref, zs_ref = slot
    c0 = g * GROUP_WIDTH
    cols = slice(c0, c0 + GROUP_WIDTH)

    def sec(i):
        w = win_ref[:, i * D_MODEL + c0:i * D_MODEL + c0 + GROUP_WIDTH]
        return jnp.dot(u, w, preferred_element_type=F32)

    if not recurrence_inputs:
        _project_gates(sec, cols, R, valid, cw_ref, zc_ref, slot)
        return

    qv = sec(0)
    q_ref[0:R, :] = qv * _sigmoid(qv)

    f = lb_ref[0:1, cols] + lb_ref[1:2, cols] * _sigmoid(sec(1))
    k_ref[0:R, :] = 1.0 - f
    lf = jnp.log2(f)
    tri = (lax.broadcasted_iota(jnp.int32, (CHUNK, CHUNK), 1)
           <= lax.broadcasted_iota(jnp.int32, (CHUNK, CHUNK), 0)).astype(BF16)
    tri2 = jnp.concatenate([tri, tri], axis=1)
    worst = jnp.zeros((1, GROUP_WIDTH), F32)
    for c in range(R // CHUNK):
        lfc = lf[c * CHUNK:(c + 1) * CHUNK, :]
        hi = lfc.astype(BF16)
        lo = (lfc - hi.astype(F32)).astype(BF16)
        gc = jnp.dot(tri2, jnp.concatenate([hi, lo], axis=0),
                     preferred_element_type=F32)
        g_ref[c * CHUNK:(c + 1) * CHUNK, :] = gc
        gmid = gc[CHUNK // 2 - 1:CHUNK // 2, :]
        glast = gc[CHUNK - 1:CHUNK, :]
        worst = jnp.maximum(worst, jnp.maximum(-gmid, gmid - glast))
    safe_ref[g] = (jnp.max(worst) <= MAX_SAFE_EXPONENT * LOG2_E).astype(jnp.int32)

    v_ref[0:R, :] = sec(2).astype(BF16)


def _project_gates(sec, cols, R, valid, cw_ref, zc_ref, slot):
    gs_ref, sa_ref, mb_ref, zs_ref = slot[4:8]
    gv = sec(3)
    gs_ref[0:R, :] = gv * _sigmoid(gv)

    bg = sec(4)
    z = sec(5) * sec(6)
    zs_ref[0:SUBLANES, :] = zc_ref[:, cols]
    zs_ref[SUBLANES:SUBLANES + R, :] = z
    z1 = zs_ref[SUBLANES - 1:SUBLANES - 1 + R, :]
    z2 = zs_ref[SUBLANES - 2:SUBLANES - 2 + R, :]
    yb = bg * (cw_ref[0:1, cols] * z2 + cw_ref[1:2, cols] * z1 + cw_ref[2:3, cols] * z)
    zc_ref[:, cols] = zs_ref[valid:valid + SUBLANES, :]

    sa_ref[0:R, :] = _sigmoid(sec(7))
    mb_ref[0:R, :] = _sigmoid(sec(8)) * yb


def _finish_chunk(row0, h, lanes, o, hnw, mg_ref, slot):
    gs_ref, sa_ref, mb_ref = slot[4:7]
    rows = pl.ds(row0, CHUNK)
    m = _rms(o, hnw) * gs_ref[rows, lanes] * sa_ref[rows, lanes] + mb_ref[rows, lanes]
    mg_ref[rows, h * HEAD_DIM:(h + 1) * HEAD_DIM] = m.astype(BF16)


def _recur_group_fast(g, R, hnw, st_ref, st_old_ref, mg_ref, slot):
    q_ref, g_ref, k_ref, v_ref = slot[0:4]
    for hh in range(GROUP_HEADS):
        h = GROUP_HEADS * g + hh
        lanes = slice(hh * HEAD_DIM, (hh + 1) * HEAD_DIM)
        st = st_ref[h]
        st_old_ref[h] = st
        for c in range(R // CHUNK):
            o, st = _hgrn_chunk_fast(c * CHUNK, lanes, st, q_ref, g_ref, k_ref, v_ref)
            mg_ref[c * CHUNK:(c + 1) * CHUNK, h * HEAD_DIM:(h + 1) * HEAD_DIM] = _rms(o, hnw).astype(BF16)
        st_ref[h] = st


def _finish_group_fast(g, R, mg_ref, slot):
    gs_ref, sa_ref, mb_ref = slot[4:7]
    cols = slice(g * GROUP_WIDTH, (g + 1) * GROUP_WIDTH)
    m = mg_ref[0:R, cols].astype(F32) * gs_ref[0:R, :] * sa_ref[0:R, :] + mb_ref[0:R, :]
    mg_ref[0:R, cols] = m.astype(BF16)


def _recur_group_exact(g, R, valid, hnw, rsel, st_in_ref, st_ref, mg_ref, slot):
    q_ref, g_ref, k_ref, v_ref = slot[0:4]
    for hh in range(GROUP_HEADS):
        h = GROUP_HEADS * g + hh
        lanes = slice(hh * HEAD_DIM, (hh + 1) * HEAD_DIM)
        st = st_in_ref[h]
        if valid < R:
            o, st = _hgrn_chunk_exact(0, lanes, valid, st, q_ref, g_ref, k_ref, v_ref, rsel)
            _finish_chunk(0, h, lanes, o, hnw, mg_ref, slot)
        else:
            def chunk_body(c, st):
                row0 = pl.multiple_of(c * CHUNK, CHUNK)
                o, st = _hgrn_chunk_exact(row0, lanes, CHUNK, st, q_ref, g_ref, k_ref, v_ref, rsel)
                _finish_chunk(row0, h, lanes, o, hnw, mg_ref, slot)
                return st

            st = lax.fori_loop(0, R // CHUNK, chunk_body, st)
        st_ref[h] = st


def _out_partial(g, R, mg_ref, wout_ref):
    rows = slice(g * GROUP_WIDTH, (g + 1) * GROUP_WIDTH)
    return jnp.dot(mg_ref[0:R, rows], wout_ref[rows, 0:D_MODEL], preferred_element_type=F32)


def _mixer_tile(load_x, store_out, R, valid, is_meta, refs):
    (lbp_ref, anw_ref, win_ref, hnw_ref, cw_ref, wout_ref, rsel_ref,
     st_ref, st_old_ref, zc_ref, lb_ref, mg_ref, safe_ref, slots) = refs
    prm = (lb_ref, win_ref, cw_ref, zc_ref, safe_ref)
    u = _rms(load_x(), anw_ref[...]).astype(BF16)
    rsel = rsel_ref[...]
    hnw = hnw_ref[...]
    project = lambda g, first: _project_group(u, g, R, valid, prm, slots[g], first)

    lbp = lbp_ref[...]
    mx = jnp.max(lbp, axis=0, keepdims=True)
    ex = jnp.exp(lbp - mx)
    lb = ex[0:1, :] / jnp.sum(ex, axis=0, keepdims=True)
    lb_ref[0:1, :] = lb
    lb_ref[1:2, :] = 1.0 - lb

    if is_meta:
        for g in range(N_GROUPS):
            project(g, True)
            project(g, False)
            _recur_group_exact(g, R, valid, hnw, rsel, st_ref, st_ref, mg_ref, slots[g])
        acc = load_x()
        for g in range(N_GROUPS):
            acc = acc + _out_partial(g, R, mg_ref, wout_ref)
        store_out(acc, False)
        return

    def store_early():
        acc = load_x()
        for g in range(N_GROUPS - 1):
            acc = acc + _out_partial(g, R, mg_ref, wout_ref)
        store_out(acc, False)

    for g in range(N_GROUPS):
        project(g, True)
        if g > 0:
            _finish_group_fast(g - 1, R, mg_ref, slots[g - 1])
            if g == N_GROUPS - 1:
                store_early()
        _recur_group_fast(g, R, hnw, st_ref, st_old_ref, mg_ref, slots[g])
        project(g, False)
    _finish_group_fast(N_GROUPS - 1, R, mg_ref, slots[N_GROUPS - 1])
    for g in range(N_GROUPS):
        def redo(g=g):
            _recur_group_exact(g, R, valid, hnw, rsel, st_old_ref, st_ref, mg_ref, slots[g])
            if g < N_GROUPS - 1:
                store_early()

        pl.when(safe_ref[g] == 0)(redo)
    store_out(_out_partial(N_GROUPS - 1, R, mg_ref, wout_ref), True)


def _mixer_kernel(x_ref, meta_ref, lbp_ref, anw_ref, win_ref, hnw_ref, cw_ref, wout_ref, rsel_ref,
                  wup_f32_ref, wdn_f32_ref, h1_ref, h1m_ref, wup_ref, wdn_ref,
                  st_ref, st_old_ref, zc_ref, lb_ref, mg_ref, safe_ref,
                  st_meta_ref, zc_meta_ref, h1m_meta_ref, *slots):
    wup_ref[...] = wup_f32_ref[...].astype(BF16)
    wdn_ref[:, 0:D_MODEL] = wdn_f32_ref[...].astype(BF16)
    wdn_ref[:, D_MODEL:] = jnp.zeros((W_DOWN_ROWS_PER_STEP, WEIGHT_LANE_PAD), BF16)

    n = len(slots) // N_GROUPS
    refs = (lbp_ref, anw_ref, win_ref, hnw_ref, cw_ref, wout_ref, rsel_ref,
            st_ref, st_old_ref, zc_ref, lb_ref, mg_ref, safe_ref,
            [slots[i * n:(i + 1) * n] for i in range(N_GROUPS)])

    first_tile = pl.program_id(1) == 0

    @pl.when(jnp.logical_and(first_tile, pl.program_id(0) == 0))
    def _():
        st_ref[...] = jnp.zeros_like(st_ref)
        zc_ref[...] = jnp.zeros_like(zc_ref)
        load_meta = lambda: jnp.concatenate(
            [meta_ref[...], jnp.zeros((CHUNK - N_META, D_MODEL), F32)], axis=0)

        def store_meta(val, accumulate):
            h1m_meta_ref[...] = val[0:N_META, :]

        _mixer_tile(load_meta, store_meta, CHUNK, N_META, True, refs)
        st_meta_ref[...] = st_ref[...]
        zc_meta_ref[...] = zc_ref[...]

    @pl.when(first_tile)
    def _():
        st_ref[...] = st_meta_ref[...]
        zc_ref[...] = zc_meta_ref[...]
        h1m_ref[0] = h1m_meta_ref[...]

    def store_tile(val, accumulate):
        h1_ref[0] = h1_ref[0] + val if accumulate else val

    _mixer_tile(lambda: x_ref[0], store_tile, MIXER_TILE, MIXER_TILE, False, refs)


def _ffn_kernel(h1_ref, h1m_ref, fnw_ref, wup_ref, fcw_ref, fcb_ref, wdn_ref, finw_ref,
                out_ref, as_ref):
    T = FFN_TILE
    fnw = fnw_ref[...]

    @pl.when(pl.program_id(1) == 0)
    def _():
        um = _rms(h1m_ref[0], fnw).astype(BF16)
        am = jnp.dot(um, wup_ref[:, 0:D_FF], preferred_element_type=F32)
        as_ref[0:SUBLANES, :] = am[N_META - SUBLANES:N_META, :]

    x = h1_ref[0]
    u = _rms(x, fnw).astype(BF16)
    y = x
    for cb in range(D_FF // FFN_COL_BLOCK):
        cols = slice(cb * FFN_COL_BLOCK, (cb + 1) * FFN_COL_BLOCK)
        a = jnp.dot(u, wup_ref[:, cols], preferred_element_type=F32)
        as_ref[SUBLANES:SUBLANES + T, cols] = a
        a1 = as_ref[SUBLANES - 1:SUBLANES - 1 + T, cols]
        a2 = as_ref[SUBLANES - 2:SUBLANES - 2 + T, cols]
        ac = (fcw_ref[0:1, cols] * a2 + fcw_ref[1:2, cols] * a1 + fcw_ref[2:3, cols] * a
              + fcb_ref[:, cols])
        as_ref[0:SUBLANES, cols] = as_ref[T:T + SUBLANES, cols]
        val = jnp.dot(u, wup_ref[:, D_FF + cb * FFN_COL_BLOCK:D_FF + (cb + 1) * FFN_COL_BLOCK],
                      preferred_element_type=F32)
        gated = (ac * _sigmoid(ac) * val).astype(BF16)
        y = y + jnp.dot(gated, wdn_ref[cols, 0:D_MODEL], preferred_element_type=F32)
    out_ref[0] = _rms(y, finw_ref[...])


def _resident(shape):
    return pl.BlockSpec(shape, lambda b, t: (0,) * len(shape), pipeline_mode=pl.Buffered(1))


def _weight(w):
    return jnp.pad(w.astype(BF16), ((0, 0), (0, WEIGHT_LANE_PAD)))


def _rsel_matrix():
    r = np.arange(N_HEADS * HEAD_DIM)[:, None] // HEAD_DIM
    l = np.arange(CHUNK)[None, :] % SUBLANES
    return jnp.asarray(r == l, dtype=BF16)


def kernel(x, meta_tokens, lb_param, attn_norm_w, w_in, hgrn_norm_w, conv_w, w_out, ffn_norm_w, w_up,
           ffn_conv_w, ffn_conv_b, w_down, final_norm_w):
    B, L, D = x.shape
    assert D == D_MODEL and L % MIXER_TILE == 0 and L % FFN_TILE == 0
    assert w_in.shape == (1, D, N_IN_SECTIONS * D) and w_up.shape == (1, D, 2 * D_FF)
    assert SUBLANES * HEAD_DIM == D_MODEL and N_HEADS == SUBLANES

    tile = lambda T: pl.BlockSpec((1, T, D), lambda b, t: (b, t, 0))
    meta_tile = pl.BlockSpec((1, N_META, D), lambda b, t: (b, 0, 0))
    params = pltpu.CompilerParams(dimension_semantics=("arbitrary", "arbitrary"),
                                  vmem_limit_bytes=VMEM_LIMIT_BYTES)

    TM = MIXER_TILE
    n_steps = B * (L // TM)
    step = lambda b, t: b * (L // TM) + t
    assert D % n_steps == 0 and D_FF % W_DOWN_ROWS_PER_STEP == 0 and D_FF // W_DOWN_ROWS_PER_STEP <= n_steps
    wup_rows = D // n_steps
    wup_block = pl.BlockSpec((wup_rows, 2 * D_FF), lambda b, t: (step(b, t), 0))
    wdn_index = lambda b, t: (jnp.minimum(step(b, t), D_FF // W_DOWN_ROWS_PER_STEP - 1), 0)
    slot = [pltpu.VMEM((TM, GROUP_WIDTH), F32)] * 3 + [pltpu.VMEM((TM, GROUP_WIDTH), BF16)] \
        + [pltpu.VMEM((TM, GROUP_WIDTH), F32)] * 3 + [pltpu.VMEM((TM + SUBLANES, GROUP_WIDTH), F32)]
    h1, h1m, w_up_bf16, w_down_bf16 = pl.pallas_call(
        _mixer_kernel,
        grid=(B, L // TM),
        in_specs=[tile(TM), _resident((N_META, D)), _resident((2, D)), _resident((1, D)),
                  _resident((D, N_IN_SECTIONS * D + WEIGHT_LANE_PAD)), _resident((1, HEAD_DIM)),
                  _resident((3, D)), _resident((D, D + WEIGHT_LANE_PAD)),
                  _resident((N_HEADS * HEAD_DIM, CHUNK)), wup_block,
                  pl.BlockSpec((W_DOWN_ROWS_PER_STEP, D), wdn_index)],
        out_specs=[tile(TM), meta_tile, wup_block,
                   pl.BlockSpec((W_DOWN_ROWS_PER_STEP, D + WEIGHT_LANE_PAD), wdn_index)],
        out_shape=[jax.ShapeDtypeStruct((B, L, D), F32), jax.ShapeDtypeStruct((B, N_META, D), F32),
                   jax.ShapeDtypeStruct((D, 2 * D_FF), BF16),
                   jax.ShapeDtypeStruct((D_FF, D + WEIGHT_LANE_PAD), BF16)],
        scratch_shapes=[pltpu.VMEM((N_HEADS, HEAD_DIM, HEAD_DIM), F32),
                        pltpu.VMEM((N_HEADS, HEAD_DIM, HEAD_DIM), F32),
                        pltpu.VMEM((SUBLANES, D), F32),
                        pltpu.VMEM((SUBLANES, D), F32),
                        pltpu.VMEM((TM, D), BF16),
                        pltpu.SMEM((N_GROUPS,), jnp.int32),
                        pltpu.VMEM((N_HEADS, HEAD_DIM, HEAD_DIM), F32),
                        pltpu.VMEM((SUBLANES, D), F32),
                        pltpu.VMEM((N_META, D), F32),
                        *(slot * N_GROUPS)],
        compiler_params=params,
        name="mixer",
    )(x, meta_tokens, lb_param, attn_norm_w, _weight(w_in[0]), hgrn_norm_w, conv_w[0],
      _weight(w_out[0]), _rsel_matrix(), w_up[0], w_down[0])

    TF = FFN_TILE
    out = pl.pallas_call(
        _ffn_kernel,
        grid=(B, L // TF),
        in_specs=[tile(TF), meta_tile, _resident((1, D)), _resident((D, 2 * D_FF)), _resident((3, D_FF)),
                  _resident((1, D_FF)), _resident((D_FF, D + WEIGHT_LANE_PAD)), _resident((1, D))],
        out_specs=tile(TF),
        out_shape=jax.ShapeDtypeStruct((B, L, D), F32),
        scratch_shapes=[pltpu.VMEM((TF + SUBLANES, D_FF), F32)],
        compiler_params=params,
        name="ffn",
    )(h1, h1m, ffn_norm_w, w_up_bf16, ffn_conv_w[0], ffn_conv_b, w_down_bf16, final_norm_w.reshape(1, D))
    return out
```

```python
import numpy as np
import jax
import jax.numpy as jnp
from jax import lax
from jax.experimental import pallas as pl
from jax.experimental.pallas import tpu as pltpu

D_MODEL = 1024
N_META = 16
N_HEADS = 8
HEAD_DIM = 128
D_FF = 2816
N_IN_SECTIONS = 9
EPS = 1e-6

SUBLANES = 8
CHUNK = 128
GROUP_HEADS = 4
GROUP_WIDTH = GROUP_HEADS * HEAD_DIM
N_GROUPS = N_HEADS // GROUP_HEADS
MAX_SAFE_EXPONENT = 60.0
LOG2_E = 1.4426950408889634
MIXER_TILE = 512
FFN_TILE = 512
FFN_COL_BLOCK = 2816
WEIGHT_LANE_PAD = 128
W_DOWN_ROWS_PER_STEP = 128
V7X_VMEM_BYTES = 64 * 1024 * 1024
VMEM_LIMIT_BYTES = V7X_VMEM_BYTES - 4 * 1024 * 1024

F32 = jnp.float32
BF16 = jnp.bfloat16

_NT = (((1,), (1,)), ((), ()))
_TN = (((0,), (0,)), ((), ()))


def _rms(x, w):
    ms = jnp.mean(x * x, axis=-1, keepdims=True)
    return x * lax.rsqrt(ms + EPS) * w


def _sigmoid(x):
    return 1.0 / (1.0 + jnp.exp2(x * -LOG2_E))


def _bcast_rows(ref, lanes, row0, block, offset, nrows):
    pieces = [jnp.broadcast_to(_row(ref, lanes, row0, i * block + offset), (block, HEAD_DIM))
              for i in range(nrows // block)]
    return pieces[0] if len(pieces) == 1 else jnp.concatenate(pieces, axis=0)


def _row(ref, lanes, row0, r):
    group = ref[pl.ds(row0 + r // SUBLANES * SUBLANES, SUBLANES), lanes]
    return group[r % SUBLANES:r % SUBLANES + 1, :]


def _hgrn_chunk_exact(row0, lanes, valid, st, q_ref, g_ref, k_ref, v_ref, rsel):
    rows = pl.ds(row0, CHUNK)
    q = q_ref[rows, lanes]
    G = g_ref[rows, lanes]
    k = k_ref[rows, lanes]
    v = v_ref[rows, lanes]
    row = lax.broadcasted_iota(jnp.int32, (CHUNK, HEAD_DIM), 0)

    sub = row & (SUBLANES - 1)
    ps = []
    for j in range(SUBLANES):
        gj = _bcast_rows(g_ref, lanes, row0, SUBLANES, j, CHUNK)
        kj = _bcast_rows(k_ref, lanes, row0, SUBLANES, j, CHUNK)
        p = q * kj * jnp.exp2(G - gj)
        ps.append(jnp.where(sub >= j, p, 0.0).astype(BF16))
    pcat = jnp.concatenate(ps, axis=1)
    a = jnp.dot(pcat, rsel, preferred_element_type=F32)

    tl_xor = (lax.broadcasted_iota(jnp.int32, (CHUNK, CHUNK), 0)
              ^ lax.broadcasted_iota(jnp.int32, (CHUNK, CHUNK), 1))
    a = jnp.where(tl_xor < SUBLANES, a, 0.0)

    b = 2 * SUBLANES
    while b <= CHUNK:
        gm = _bcast_rows(g_ref, lanes, row0, b, b // 2 - 1, CHUNK)
        e = jnp.exp2(-jnp.abs(G - gm))
        upper = (row & (b // 2)) != 0
        qt = jnp.where(upper, q * e, 0.0).astype(BF16)
        kt = jnp.where(upper, 0.0, k * e).astype(BF16)
        ab = lax.dot_general(qt, kt, _NT, preferred_element_type=F32)
        a = a + (ab if b == CHUNK else jnp.where(tl_xor < b, ab, 0.0))
        b *= 2

    qi = (q * jnp.exp2(G)).astype(BF16)
    o = lax.dot_general(qi, st.astype(BF16), _NT, preferred_element_type=F32)
    o = o + jnp.dot(a.astype(BF16), v, preferred_element_type=F32)

    glast = _row(g_ref, lanes, row0, valid - 1)
    kd = k * jnp.exp2(glast - G)
    if valid < CHUNK:
        kd = jnp.where(row < valid, kd, 0.0)
    st = st * jnp.exp2(glast) + lax.dot_general(v, kd.astype(BF16), _TN, preferred_element_type=F32)
    return o, st


def _hgrn_chunk_fast(row0, lanes, st, q_ref, g_ref, k_ref, v_ref):
    rows = pl.ds(row0, CHUNK)
    q = q_ref[rows, lanes]
    G = g_ref[rows, lanes]
    k = k_ref[rows, lanes]
    v = v_ref[rows, lanes]
    gmid = g_ref[pl.ds(row0 + CHUNK // 2 - 1, 1), lanes]
    glast = g_ref[pl.ds(row0 + CHUNK - 1, 1), lanes]
    d = G - gmid
    qh = q * jnp.exp2(d)
    kh = k * jnp.exp2(-d)
    a = lax.dot_general(qh.astype(BF16), kh.astype(BF16), _NT, preferred_element_type=F32)
    causal = (lax.broadcasted_iota(jnp.int32, (CHUNK, CHUNK), 1)
              <= lax.broadcasted_iota(jnp.int32, (CHUNK, CHUNK), 0))
    a = jnp.where(causal, a, 0.0)
    qi = (qh * jnp.exp2(gmid)).astype(BF16)
    o = lax.dot_general(qi, st.astype(BF16), _NT, preferred_element_type=F32)
    o = o + jnp.dot(a.astype(BF16), v, preferred_element_type=F32)
    kd = (kh * jnp.exp2(glast - gmid)).astype(BF16)
    st = st * jnp.exp2(glast) + lax.dot_general(v, kd, _TN, preferred_element_type=F32)
    return o, st


def _project_group(u, g, R, valid, prm, slot, recurrence_inputs):
    lb_ref, win_ref, cw_ref, zc_ref, safe_ref, tri2_ref = prm
    q_ref, g_ref, k_ref, v_ref, gs_ref, sa_ref, mb_ref, zs_ref = slot
    c0 = g * GROUP_WIDTH
    cols = slice(c0, c0 + GROUP_WIDTH)

    def sec(i):
        w = win_ref[:, i * D_MODEL + c0:i * D_MODEL + c0 + GROUP_WIDTH]
        return jnp.dot(u, w, preferred_element_type=F32)

    if not recurrence_inputs:
        _project_gates(sec, cols, R, valid, cw_ref, zc_ref, slot)
        return

    qv = sec(0)
    q_ref[0:R, :] = qv * _sigmoid(qv)

    f = lb_ref[0:1, cols] + lb_ref[1:2, cols] * _sigmoid(sec(1))
    k_ref[0:R, :] = 1.0 - f
    lf = jnp.log2(f)
    tri2 = tri2_ref[...]
    worst = jnp.zeros((1, GROUP_WIDTH), F32)
    for c in range(R // CHUNK):
        lfc = lf[c * CHUNK:(c + 1) * CHUNK, :]
        hi = lfc.astype(BF16)
        lo = (lfc - hi.astype(F32)).astype(BF16)
        gc = jnp.dot(tri2, jnp.concatenate([hi, lo], axis=0),
                     preferred_element_type=F32)
        g_ref[c * CHUNK:(c + 1) * CHUNK, :] = gc
        gmid = gc[CHUNK // 2 - 1:CHUNK // 2, :]
        glast = gc[CHUNK - 1:CHUNK, :]
        worst = jnp.maximum(worst, jnp.maximum(-gmid, gmid - glast))
    safe_ref[g] = (jnp.max(worst) <= MAX_SAFE_EXPONENT * LOG2_E).astype(jnp.int32)

    v_ref[0:R, :] = sec(2).astype(BF16)


def _project_gates(sec, cols, R, valid, cw_ref, zc_ref, slot):
    gs_ref, sa_ref, mb_ref, zs_ref = slot[4:8]
    gv = sec(3)
    gs_ref[0:R, :] = gv * _sigmoid(gv)

    bg = sec(4)
    z = sec(5) * sec(6)
    zs_ref[0:SUBLANES, :] = zc_ref[:, cols]
    zs_ref[SUBLANES:SUBLANES + R, :] = z
    z1 = zs_ref[SUBLANES - 1:SUBLANES - 1 + R, :]
    z2 = zs_ref[SUBLANES - 2:SUBLANES - 2 + R, :]
    yb = bg * (cw_ref[0:1, cols] * z2 + cw_ref[1:2, cols] * z1 + cw_ref[2:3, cols] * z)
    zc_ref[:, cols] = zs_ref[valid:valid + SUBLANES, :]

    sa_ref[0:R, :] = _sigmoid(sec(7))
    mb_ref[0:R, :] = _sigmoid(sec(8)) * yb


def _finish_chunk(row0, h, lanes, o, hnw, mg_ref, slot):
    gs_ref, sa_ref, mb_ref = slot[4:7]
    rows = pl.ds(row0, CHUNK)
    m = _rms(o, hnw) * gs_ref[rows, lanes] * sa_ref[rows, lanes] + mb_ref[rows, lanes]
    mg_ref[rows, h * HEAD_DIM:(h + 1) * HEAD_DIM] = m.astype(BF16)


def _recur_group_fast(g, R, hnw, st_ref, st_old_ref, mg_ref, slot):
    q_ref, g_ref, k_ref, v_ref = slot[0:4]
    for hh in range(GROUP_HEADS):
        h = GROUP_HEADS * g + hh
        lanes = slice(hh * HEAD_DIM, (hh + 1) * HEAD_DIM)
        st = st_ref[h]
        st_old_ref[h] = st
        for c in range(R // CHUNK):
            o, st = _hgrn_chunk_fast(c * CHUNK, lanes, st, q_ref, g_ref, k_ref, v_ref)
            mg_ref[c * CHUNK:(c + 1) * CHUNK, h * HEAD_DIM:(h + 1) * HEAD_DIM] = _rms(o, hnw).astype(BF16)
        st_ref[h] = st


def _finish_group_fast(g, R, mg_ref, slot):
    gs_ref, sa_ref, mb_ref = slot[4:7]
    cols = slice(g * GROUP_WIDTH, (g + 1) * GROUP_WIDTH)
    m = mg_ref[0:R, cols].astype(F32) * gs_ref[0:R, :] * sa_ref[0:R, :] + mb_ref[0:R, :]
    mg_ref[0:R, cols] = m.astype(BF16)


def _recur_group_exact(g, R, valid, hnw, rsel, st_in_ref, st_ref, mg_ref, slot):
    q_ref, g_ref, k_ref, v_ref = slot[0:4]
    for hh in range(GROUP_HEADS):
        h = GROUP_HEADS * g + hh
        lanes = slice(hh * HEAD_DIM, (hh + 1) * HEAD_DIM)
        st = st_in_ref[h]
        if valid < R:
            o, st = _hgrn_chunk_exact(0, lanes, valid, st, q_ref, g_ref, k_ref, v_ref, rsel)
            _finish_chunk(0, h, lanes, o, hnw, mg_ref, slot)
        else:
            def chunk_body(c, st):
                row0 = pl.multiple_of(c * CHUNK, CHUNK)
                o, st = _hgrn_chunk_exact(row0, lanes, CHUNK, st, q_ref, g_ref, k_ref, v_ref, rsel)
                _finish_chunk(row0, h, lanes, o, hnw, mg_ref, slot)
                return st

            st = lax.fori_loop(0, R // CHUNK, chunk_body, st)
        st_ref[h] = st


def _out_partial(g, R, mg_ref, wout_ref):
    rows = slice(g * GROUP_WIDTH, (g + 1) * GROUP_WIDTH)
    return jnp.dot(mg_ref[0:R, rows], wout_ref[rows, 0:D_MODEL], preferred_element_type=F32)


def _mixer_tile(load_x, store_out, R, valid, is_meta, refs):
    (lbp_ref, anw_ref, win_ref, hnw_ref, cw_ref, wout_ref, rsel_ref, tri2_ref,
     st_ref, st_old_ref, zc_ref, lb_ref, mg_ref, safe_ref, slots) = refs
    prm = (lb_ref, win_ref, cw_ref, zc_ref, safe_ref, tri2_ref)
    u = _rms(load_x(), anw_ref[...]).astype(BF16)
    rsel = rsel_ref[...]
    hnw = hnw_ref[...]
    project = lambda g, first: _project_group(u, g, R, valid, prm, slots[g], first)

    lbp = lbp_ref[...]
    mx = jnp.max(lbp, axis=0, keepdims=True)
    ex = jnp.exp(lbp - mx)
    lb = ex[0:1, :] / jnp.sum(ex, axis=0, keepdims=True)
    lb_ref[0:1, :] = lb
    lb_ref[1:2, :] = 1.0 - lb

    if is_meta:
        for g in range(N_GROUPS):
            project(g, True)
            project(g, False)
            _recur_group_exact(g, R, valid, hnw, rsel, st_ref, st_ref, mg_ref, slots[g])
        acc = load_x()
        for g in range(N_GROUPS):
            acc = acc + _out_partial(g, R, mg_ref, wout_ref)
        store_out(acc, False)
        return

    def store_early():
        acc = load_x()
        for g in range(N_GROUPS - 1):
            acc = acc + _out_partial(g, R, mg_ref, wout_ref)
        store_out(acc, False)

    for g in range(N_GROUPS):
        project(g, True)
        if g > 0:
            _finish_group_fast(g - 1, R, mg_ref, slots[g - 1])
            if g == N_GROUPS - 1:
                store_early()
        _recur_group_fast(g, R, hnw, st_ref, st_old_ref, mg_ref, slots[g])
        project(g, False)
    _finish_group_fast(N_GROUPS - 1, R, mg_ref, slots[N_GROUPS - 1])
    for g in range(N_GROUPS):
        def redo(g=g):
            _recur_group_exact(g, R, valid, hnw, rsel, st_old_ref, st_ref, mg_ref, slots[g])
            if g < N_GROUPS - 1:
                store_early()

        pl.when(safe_ref[g] == 0)(redo)
    store_out(_out_partial(N_GROUPS - 1, R, mg_ref, wout_ref), True)


def _mixer_kernel(x_ref, meta_ref, lbp_ref, anw_ref, win_ref, hnw_ref, cw_ref, wout_ref, rsel_ref, tri2_ref,
                  wup_f32_ref, wdn_f32_ref, h1_ref, h1m_ref, wup_ref, wdn_ref,
                  st_ref, st_old_ref, zc_ref, lb_ref, mg_ref, safe_ref,
                  st_meta_ref, zc_meta_ref, h1m_meta_ref, *slots):
    wup_ref[...] = wup_f32_ref[...].astype(BF16)
    wdn_ref[:, 0:D_MODEL] = wdn_f32_ref[...].astype(BF16)
    wdn_ref[:, D_MODEL:] = jnp.zeros((W_DOWN_ROWS_PER_STEP, WEIGHT_LANE_PAD), BF16)

    n = len(slots) // N_GROUPS
    refs = (lbp_ref, anw_ref, win_ref, hnw_ref, cw_ref, wout_ref, rsel_ref, tri2_ref,
            st_ref, st_old_ref, zc_ref, lb_ref, mg_ref, safe_ref,
            [slots[i * n:(i + 1) * n] for i in range(N_GROUPS)])

    first_tile = pl.program_id(1) == 0

    @pl.when(jnp.logical_and(first_tile, pl.program_id(0) == 0))
    def _():
        st_ref[...] = jnp.zeros_like(st_ref)
        zc_ref[...] = jnp.zeros_like(zc_ref)
        load_meta = lambda: jnp.concatenate(
            [meta_ref[...], jnp.zeros((CHUNK - N_META, D_MODEL), F32)], axis=0)

        def store_meta(val, accumulate):
            h1m_meta_ref[...] = val[0:N_META, :]

        _mixer_tile(load_meta, store_meta, CHUNK, N_META, True, refs)
        st_meta_ref[...] = st_ref[...]
        zc_meta_ref[...] = zc_ref[...]

    @pl.when(first_tile)
    def _():
        st_ref[...] = st_meta_ref[...]
        zc_ref[...] = zc_meta_ref[...]
        h1m_ref[0] = h1m_meta_ref[...]

    def store_tile(val, accumulate):
        h1_ref[0] = h1_ref[0] + val if accumulate else val

    _mixer_tile(lambda: x_ref[0], store_tile, MIXER_TILE, MIXER_TILE, False, refs)


def _ffn_kernel(h1_ref, h1m_ref, fnw_ref, wup_ref, fcw_ref, fcb_ref, wdn_ref, finw_ref,
                out_ref, as_ref):
    T = FFN_TILE
    fnw = fnw_ref[...]

    @pl.when(pl.program_id(1) == 0)
    def _():
        um = _rms(h1m_ref[0], fnw).astype(BF16)
        am = jnp.dot(um, wup_ref[:, 0:D_FF], preferred_element_type=F32)
        as_ref[0:SUBLANES, :] = am[N_META - SUBLANES:N_META, :]

    x = h1_ref[0]
    u = _rms(x, fnw).astype(BF16)
    y = x
    for cb in range(D_FF // FFN_COL_BLOCK):
        cols = slice(cb * FFN_COL_BLOCK, (cb + 1) * FFN_COL_BLOCK)
        a = jnp.dot(u, wup_ref[:, cols], preferred_element_type=F32)
        as_ref[SUBLANES:SUBLANES + T, cols] = a
        a1 = as_ref[SUBLANES - 1:SUBLANES - 1 + T, cols]
        a2 = as_ref[SUBLANES - 2:SUBLANES - 2 + T, cols]
        ac = (fcw_ref[0:1, cols] * a2 + fcw_ref[1:2, cols] * a1 + fcw_ref[2:3, cols] * a
              + fcb_ref[:, cols])
        as_ref[0:SUBLANES, cols] = as_ref[T:T + SUBLANES, cols]
        val = jnp.dot(u, wup_ref[:, D_FF + cb * FFN_COL_BLOCK:D_FF + (cb + 1) * FFN_COL_BLOCK],
                      preferred_element_type=F32)
        gated = (ac * _sigmoid(ac) * val).astype(BF16)
        y = y + jnp.dot(gated, wdn_ref[cols, 0:D_MODEL], preferred_element_type=F32)
    out_ref[0] = _rms(y, finw_ref[...])


def _resident(shape):
    return pl.BlockSpec(shape, lambda b, t: (0,) * len(shape), pipeline_mode=pl.Buffered(1))


def _weight(w):
    return jnp.pad(w.astype(BF16), ((0, 0), (0, WEIGHT_LANE_PAD)))


def _rsel_matrix():
    r = np.arange(N_HEADS * HEAD_DIM)[:, None] // HEAD_DIM
    l = np.arange(CHUNK)[None, :] % SUBLANES
    return jnp.asarray(r == l, dtype=BF16)


def _cumsum_matrix():
    tri = np.tril(np.ones((CHUNK, CHUNK), np.float32))
    return jnp.asarray(np.concatenate([tri, tri], axis=1), dtype=BF16)


def kernel(x, meta_tokens, lb_param, attn_norm_w, w_in, hgrn_norm_w, conv_w, w_out, ffn_norm_w, w_up,
           ffn_conv_w, ffn_conv_b, w_down, final_norm_w):
    B, L, D = x.shape
    assert D == D_MODEL and L % MIXER_TILE == 0 and L % FFN_TILE == 0
    assert w_in.shape == (1, D, N_IN_SECTIONS * D) and w_up.shape == (1, D, 2 * D_FF)
    assert SUBLANES * HEAD_DIM == D_MODEL and N_HEADS == SUBLANES

    tile = lambda T: pl.BlockSpec((1, T, D), lambda b, t: (b, t, 0))
    meta_tile = pl.BlockSpec((1, N_META, D), lambda b, t: (b, 0, 0))
    params = pltpu.CompilerParams(dimension_semantics=("arbitrary", "arbitrary"),
                                  vmem_limit_bytes=VMEM_LIMIT_BYTES)

    TM = MIXER_TILE
    n_steps = B * (L // TM)
    step = lambda b, t: b * (L // TM) + t
    assert D % n_steps == 0 and D_FF % W_DOWN_ROWS_PER_STEP == 0 and D_FF // W_DOWN_ROWS_PER_STEP <= n_steps
    wup_rows = D // n_steps
    wup_block = pl.BlockSpec((wup_rows, 2 * D_FF), lambda b, t: (step(b, t), 0))
    wdn_index = lambda b, t: (jnp.minimum(step(b, t), D_FF // W_DOWN_ROWS_PER_STEP - 1), 0)
    slot = [pltpu.VMEM((TM, GROUP_WIDTH), F32)] * 3 + [pltpu.VMEM((TM, GROUP_WIDTH), BF16)] \
        + [pltpu.VMEM((TM, GROUP_WIDTH), F32)] * 3 + [pltpu.VMEM((TM + SUBLANES, GROUP_WIDTH), F32)]
    h1, h1m, w_up_bf16, w_down_bf16 = pl.pallas_call(
        _mixer_kernel,
        grid=(B, L // TM),
        in_specs=[tile(TM), _resident((N_META, D)), _resident((2, D)), _resident((1, D)),
                  _resident((D, N_IN_SECTIONS * D + WEIGHT_LANE_PAD)), _resident((1, HEAD_DIM)),
                  _resident((3, D)), _resident((D, D + WEIGHT_LANE_PAD)),
                  _resident((N_HEADS * HEAD_DIM, CHUNK)), _resident((CHUNK, 2 * CHUNK)), wup_block,
                  pl.BlockSpec((W_DOWN_ROWS_PER_STEP, D), wdn_index)],
        out_specs=[tile(TM), meta_tile, wup_block,
                   pl.BlockSpec((W_DOWN_ROWS_PER_STEP, D + WEIGHT_LANE_PAD), wdn_index)],
        out_shape=[jax.ShapeDtypeStruct((B, L, D), F32), jax.ShapeDtypeStruct((B, N_META, D), F32),
                   jax.ShapeDtypeStruct((D, 2 * D_FF), BF16),
                   jax.ShapeDtypeStruct((D_FF, D + WEIGHT_LANE_PAD), BF16)],
        scratch_shapes=[pltpu.VMEM((N_HEADS, HEAD_DIM, HEAD_DIM), F32),
                        pltpu.VMEM((N_HEADS, HEAD_DIM, HEAD_DIM), F32),
                        pltpu.VMEM((SUBLANES, D), F32),
                        pltpu.VMEM((SUBLANES, D), F32),
                        pltpu.VMEM((TM, D), BF16),
                        pltpu.SMEM((N_GROUPS,), jnp.int32),
                        pltpu.VMEM((N_HEADS, HEAD_DIM, HEAD_DIM), F32),
                        pltpu.VMEM((SUBLANES, D), F32),
                        pltpu.VMEM((N_META, D), F32),
                        *(slot * N_GROUPS)],
        compiler_params=params,
        name="mixer",
    )(x, meta_tokens, lb_param, attn_norm_w, _weight(w_in[0]), hgrn_norm_w, conv_w[0],
      _weight(w_out[0]), _rsel_matrix(), _cumsum_matrix(), w_up[0], w_down[0])

    TF = FFN_TILE
    out = pl.pallas_call(
        _ffn_kernel,
        grid=(B, L // TF),
        in_specs=[tile(TF), meta_tile, _resident((1, D)), _resident((D, 2 * D_FF)), _resident((3, D_FF)),
                  _resident((1, D_FF)), _resident((D_FF, D + WEIGHT_LANE_PAD)), _resident((1, D))],
        out_specs=tile(TF),
        out_shape=jax.ShapeDtypeStruct((B, L, D), F32),
        scratch_shapes=[pltpu.VMEM((TF + SUBLANES, D_FF), F32)],
        compiler_params=params,
        name="ffn",
    )(h1, h1m, ffn_norm_w, w_up_bf16, ffn_conv_w[0], ffn_conv_b, w_down_bf16, final_norm_w.reshape(1, D))
    return out
```

```python
import numpy as np
import jax
import jax.numpy as jnp
from jax import lax
from jax.experimental import pallas as pl
from jax.experimental.pallas import tpu as pltpu

D_MODEL = 1024
N_META = 16
N_HEADS = 8
HEAD_DIM = 128
D_FF = 2816
N_IN_SECTIONS = 9
EPS = 1e-6

SUBLANES = 8
CHUNK = 128
GROUP_HEADS = 4
GROUP_WIDTH = GROUP_HEADS * HEAD_DIM
N_GROUPS = N_HEADS // GROUP_HEADS
MAX_SAFE_EXPONENT = 60.0
LOG2_E = 1.4426950408889634
MIXER_TILE = 512
FFN_TILE = 512
FFN_COL_BLOCK = 2816
WEIGHT_LANE_PAD = 128
W_DOWN_ROWS_PER_STEP = 128
V7X_VMEM_BYTES = 64 * 1024 * 1024
VMEM_LIMIT_BYTES = V7X_VMEM_BYTES - 4 * 1024 * 1024

F32 = jnp.float32
BF16 = jnp.bfloat16

_NT = (((1,), (1,)), ((), ()))
_TN = (((0,), (0,)), ((), ()))


def _rms(x, w):
    ms = jnp.mean(x * x, axis=-1, keepdims=True)
    return x * lax.rsqrt(ms + EPS) * w


def _sigmoid(x):
    return 1.0 / (1.0 + jnp.exp2(x * -LOG2_E))


def _bcast_rows(ref, lanes, row0, block, offset, nrows):
    pieces = [jnp.broadcast_to(_row(ref, lanes, row0, i * block + offset), (block, HEAD_DIM))
              for i in range(nrows // block)]
    return pieces[0] if len(pieces) == 1 else jnp.concatenate(pieces, axis=0)


def _row(ref, lanes, row0, r):
    group = ref[pl.ds(row0 + r // SUBLANES * SUBLANES, SUBLANES), lanes]
    return group[r % SUBLANES:r % SUBLANES + 1, :]


def _hgrn_chunk_exact(row0, lanes, valid, st, q_ref, g_ref, k_ref, v_ref, rsel):
    rows = pl.ds(row0, CHUNK)
    q = q_ref[rows, lanes]
    G = g_ref[rows, lanes]
    k = k_ref[rows, lanes]
    v = v_ref[rows, lanes]
    row = lax.broadcasted_iota(jnp.int32, (CHUNK, HEAD_DIM), 0)

    sub = row & (SUBLANES - 1)
    ps = []
    for j in range(SUBLANES):
        gj = _bcast_rows(g_ref, lanes, row0, SUBLANES, j, CHUNK)
        kj = _bcast_rows(k_ref, lanes, row0, SUBLANES, j, CHUNK)
        p = q * kj * jnp.exp2(G - gj)
        ps.append(jnp.where(sub >= j, p, 0.0).astype(BF16))
    pcat = jnp.concatenate(ps, axis=1)
    a = jnp.dot(pcat, rsel, preferred_element_type=F32)

    tl_xor = (lax.broadcasted_iota(jnp.int32, (CHUNK, CHUNK), 0)
              ^ lax.broadcasted_iota(jnp.int32, (CHUNK, CHUNK), 1))
    a = jnp.where(tl_xor < SUBLANES, a, 0.0)

    b = 2 * SUBLANES
    while b <= CHUNK:
        gm = _bcast_rows(g_ref, lanes, row0, b, b // 2 - 1, CHUNK)
        e = jnp.exp2(-jnp.abs(G - gm))
        upper = (row & (b // 2)) != 0
        qt = jnp.where(upper, q * e, 0.0).astype(BF16)
        kt = jnp.where(upper, 0.0, k * e).astype(BF16)
        ab = lax.dot_general(qt, kt, _NT, preferred_element_type=F32)
        a = a + (ab if b == CHUNK else jnp.where(tl_xor < b, ab, 0.0))
        b *= 2

    qi = (q * jnp.exp2(G)).astype(BF16)
    o = lax.dot_general(qi, st.astype(BF16), _NT, preferred_element_type=F32)
    o = o + jnp.dot(a.astype(BF16), v, preferred_element_type=F32)

    glast = _row(g_ref, lanes, row0, valid - 1)
    kd = k * jnp.exp2(glast - G)
    if valid < CHUNK:
        kd = jnp.where(row < valid, kd, 0.0)
    st = st * jnp.exp2(glast) + lax.dot_general(v, kd.astype(BF16), _TN, preferred_element_type=F32)
    return o, st


def _hgrn_chunk_fast(row0, lanes, st, q_ref, g_ref, k_ref, v_ref):
    rows = pl.ds(row0, CHUNK)
    q = q_ref[rows, lanes]
    G = g_ref[rows, lanes]
    k = k_ref[rows, lanes]
    v = v_ref[rows, lanes]
    gmid = g_ref[pl.ds(row0 + CHUNK // 2 - 1, 1), lanes]
    glast = g_ref[pl.ds(row0 + CHUNK - 1, 1), lanes]
    d = G - gmid
    qh = q * jnp.exp2(d)
    kh = k * jnp.exp2(-d)
    a = lax.dot_general(qh.astype(BF16), kh.astype(BF16), _NT, preferred_element_type=F32)
    causal = (lax.broadcasted_iota(jnp.int32, (CHUNK, CHUNK), 1)
              <= lax.broadcasted_iota(jnp.int32, (CHUNK, CHUNK), 0))
    a = jnp.where(causal, a, 0.0)
    qi = (qh * jnp.exp2(gmid)).astype(BF16)
    o = lax.dot_general(qi, st.astype(BF16), _NT, preferred_element_type=F32)
    o = o + jnp.dot(a.astype(BF16), v, preferred_element_type=F32)
    kd = (kh * jnp.exp2(glast - gmid)).astype(BF16)
    st = st * jnp.exp2(glast) + lax.dot_general(v, kd, _TN, preferred_element_type=F32)
    return o, st


def _project_group(u, g, R, valid, prm, slot, recurrence_inputs):
    lb_ref, win_ref, cw_ref, zc_ref, safe_ref = prm
    q_ref, g_ref, k_ref, v_ref, gs_ref, sa_ref, mb_ref, zs_ref = slot
    c0 = g * GROUP_WIDTH
    cols = slice(c0, c0 + GROUP_WIDTH)

    def sec(i):
        w = win_ref[:, i * D_MODEL + c0:i * D_MODEL + c0 + GROUP_WIDTH]
        return jnp.dot(u, w, preferred_element_type=F32)

    if not recurrence_inputs:
        _project_gates(sec, cols, R, valid, cw_ref, zc_ref, slot)
        return

    qv = sec(0)
    q_ref[0:R, :] = qv * _sigmoid(qv)

    f = lb_ref[0:1, cols] + lb_ref[1:2, cols] * _sigmoid(sec(1))
    k_ref[0:R, :] = 1.0 - f
    lf = jnp.log2(f)
    tri = (lax.broadcasted_iota(jnp.int32, (CHUNK, CHUNK), 1)
           <= lax.broadcasted_iota(jnp.int32, (CHUNK, CHUNK), 0)).astype(BF16)
    tri2 = jnp.concatenate([tri, tri], axis=1)
    worst = jnp.zeros((1, GROUP_WIDTH), F32)
    for c in range(R // CHUNK):
        lfc = lf[c * CHUNK:(c + 1) * CHUNK, :]
        hi = lfc.astype(BF16)
        lo = (lfc - hi.astype(F32)).astype(BF16)
        gc = jnp.dot(tri2, jnp.concatenate([hi, lo], axis=0),
                     preferred_element_type=F32)
        g_ref[c * CHUNK:(c + 1) * CHUNK, :] = gc
        gmid = gc[CHUNK // 2 - 1:CHUNK // 2, :]
        glast = gc[CHUNK - 1:CHUNK, :]
        worst = jnp.maximum(worst, jnp.maximum(-gmid, gmid - glast))
    safe_ref[g] = (jnp.max(worst) <= MAX_SAFE_EXPONENT * LOG2_E).astype(jnp.int32)

    v_ref[0:R, :] = sec(2).astype(BF16)


def _project_gates(sec, cols, R, valid, cw_ref, zc_ref, slot):
    gs_ref, sa_ref, mb_ref, zs_ref = slot[4:8]
    gv = sec(3)
    gs_ref[0:R, :] = gv * _sigmoid(gv)

    bg = sec(4)
    z = sec(5) * sec(6)
    zs_ref[0:SUBLANES, :] = zc_ref[:, cols]
    zs_ref[SUBLANES:SUBLANES + R, :] = z
    z1 = zs_ref[SUBLANES - 1:SUBLANES - 1 + R, :]
    z2 = zs_ref[SUBLANES - 2:SUBLANES - 2 + R, :]
    yb = bg * (cw_ref[0:1, cols] * z2 + cw_ref[1:2, cols] * z1 + cw_ref[2:3, cols] * z)
    zc_ref[:, cols] = zs_ref[valid:valid + SUBLANES, :]

    sa_ref[0:R, :] = _sigmoid(sec(7))
    mb_ref[0:R, :] = _sigmoid(sec(8)) * yb


def _finish_chunk(row0, h, lanes, o, hnw, mg_ref, slot):
    gs_ref, sa_ref, mb_ref = slot[4:7]
    rows = pl.ds(row0, CHUNK)
    m = _rms(o, hnw) * gs_ref[rows, lanes] * sa_ref[rows, lanes] + mb_ref[rows, lanes]
    mg_ref[rows, h * HEAD_DIM:(h + 1) * HEAD_DIM] = m.astype(BF16)


def _recur_group_fast(g, R, hnw, st_ref, st_old_ref, mg_ref, slot):
    q_ref, g_ref, k_ref, v_ref = slot[0:4]
    for hh in range(GROUP_HEADS):
        h = GROUP_HEADS * g + hh
        lanes = slice(hh * HEAD_DIM, (hh + 1) * HEAD_DIM)
        st = st_ref[h]
        st_old_ref[h] = st
        for c in range(R // CHUNK):
            o, st = _hgrn_chunk_fast(c * CHUNK, lanes, st, q_ref, g_ref, k_ref, v_ref)
            mg_ref[c * CHUNK:(c + 1) * CHUNK, h * HEAD_DIM:(h + 1) * HEAD_DIM] = _rms(o, hnw).astype(BF16)
        st_ref[h] = st


def _finish_group_fast(g, R, mg_ref, slot):
    gs_ref, sa_ref, mb_ref = slot[4:7]
    cols = slice(g * GROUP_WIDTH, (g + 1) * GROUP_WIDTH)
    m = mg_ref[0:R, cols].astype(F32) * gs_ref[0:R, :] * sa_ref[0:R, :] + mb_ref[0:R, :]
    mg_ref[0:R, cols] = m.astype(BF16)


def _recur_group_exact(g, R, valid, hnw, rsel, st_in_ref, st_ref, mg_ref, slot):
    q_ref, g_ref, k_ref, v_ref = slot[0:4]
    for hh in range(GROUP_HEADS):
        h = GROUP_HEADS * g + hh
        lanes = slice(hh * HEAD_DIM, (hh + 1) * HEAD_DIM)
        st = st_in_ref[h]
        if valid < R:
            o, st = _hgrn_chunk_exact(0, lanes, valid, st, q_ref, g_ref, k_ref, v_ref, rsel)
            _finish_chunk(0, h, lanes, o, hnw, mg_ref, slot)
        else:
            def chunk_body(c, st):
                row0 = pl.multiple_of(c * CHUNK, CHUNK)
                o, st = _hgrn_chunk_exact(row0, lanes, CHUNK, st, q_ref, g_ref, k_ref, v_ref, rsel)
                _finish_chunk(row0, h, lanes, o, hnw, mg_ref, slot)
                return st

            st = lax.fori_loop(0, R // CHUNK, chunk_body, st)
        st_ref[h] = st


def _out_partial(g, R, mg_ref, wout_ref):
    rows = slice(g * GROUP_WIDTH, (g + 1) * GROUP_WIDTH)
    return jnp.dot(mg_ref[0:R, rows], wout_ref[rows, 0:D_MODEL], preferred_element_type=F32)


def _mixer_tile(load_x, store_out, R, valid, is_meta, refs):
    (lbp_ref, anw_ref, win_ref, hnw_ref, cw_ref, wout_ref, rsel_ref,
     st_ref, st_old_ref, zc_ref, lb_ref, mg_ref, safe_ref, slots) = refs
    prm = (lb_ref, win_ref, cw_ref, zc_ref, safe_ref)
    u = _rms(load_x(), anw_ref[...]).astype(BF16)
    rsel = rsel_ref[...]
    hnw = hnw_ref[...]
    project = lambda g, first: _project_group(u, g, R, valid, prm, slots[g], first)

    lbp = lbp_ref[...]
    mx = jnp.max(lbp, axis=0, keepdims=True)
    ex = jnp.exp(lbp - mx)
    lb = ex[0:1, :] / jnp.sum(ex, axis=0, keepdims=True)
    lb_ref[0:1, :] = lb
    lb_ref[1:2, :] = 1.0 - lb

    if is_meta:
        for g in range(N_GROUPS):
            project(g, True)
            project(g, False)
            _recur_group_exact(g, R, valid, hnw, rsel, st_ref, st_ref, mg_ref, slots[g])
        acc = load_x()
        for g in range(N_GROUPS):
            acc = acc + _out_partial(g, R, mg_ref, wout_ref)
        store_out(acc, False)
        return

    def store_early():
        acc = load_x()
        for g in range(N_GROUPS - 1):
            acc = acc + _out_partial(g, R, mg_ref, wout_ref)
        store_out(acc, False)

    for g in range(N_GROUPS):
        project(g, True)
    for g in range(N_GROUPS):
        if g > 0:
            _finish_group_fast(g - 1, R, mg_ref, slots[g - 1])
            if g == N_GROUPS - 1:
                store_early()
        _recur_group_fast(g, R, hnw, st_ref, st_old_ref, mg_ref, slots[g])
        project(g, False)
    _finish_group_fast(N_GROUPS - 1, R, mg_ref, slots[N_GROUPS - 1])
    for g in range(N_GROUPS):
        def redo(g=g):
            _recur_group_exact(g, R, valid, hnw, rsel, st_old_ref, st_ref, mg_ref, slots[g])
            if g < N_GROUPS - 1:
                store_early()

        pl.when(safe_ref[g] == 0)(redo)
    store_out(_out_partial(N_GROUPS - 1, R, mg_ref, wout_ref), True)


def _mixer_kernel(x_ref, meta_ref, lbp_ref, anw_ref, win_ref, hnw_ref, cw_ref, wout_ref, rsel_ref,
                  wup_f32_ref, wdn_f32_ref, h1_ref, h1m_ref, wup_ref, wdn_ref,
                  st_ref, st_old_ref, zc_ref, lb_ref, mg_ref, safe_ref,
                  st_meta_ref, zc_meta_ref, h1m_meta_ref, *slots):
    wup_ref[...] = wup_f32_ref[...].astype(BF16)
    wdn_ref[:, 0:D_MODEL] = wdn_f32_ref[...].astype(BF16)
    wdn_ref[:, D_MODEL:] = jnp.zeros((W_DOWN_ROWS_PER_STEP, WEIGHT_LANE_PAD), BF16)

    n = len(slots) // N_GROUPS
    refs = (lbp_ref, anw_ref, win_ref, hnw_ref, cw_ref, wout_ref, rsel_ref,
            st_ref, st_old_ref, zc_ref, lb_ref, mg_ref, safe_ref,
            [slots[i * n:(i + 1) * n] for i in range(N_GROUPS)])

    first_tile = pl.program_id(1) == 0

    @pl.when(jnp.logical_and(first_tile, pl.program_id(0) == 0))
    def _():
        st_ref[...] = jnp.zeros_like(st_ref)
        zc_ref[...] = jnp.zeros_like(zc_ref)
        load_meta = lambda: jnp.concatenate(
            [meta_ref[...], jnp.zeros((CHUNK - N_META, D_MODEL), F32)], axis=0)

        def store_meta(val, accumulate):
            h1m_meta_ref[...] = val[0:N_META, :]

        _mixer_tile(load_meta, store_meta, CHUNK, N_META, True, refs)
        st_meta_ref[...] = st_ref[...]
        zc_meta_ref[...] = zc_ref[...]

    @pl.when(first_tile)
    def _():
        st_ref[...] = st_meta_ref[...]
        zc_ref[...] = zc_meta_ref[...]
        h1m_ref[0] = h1m_meta_ref[...]

    def store_tile(val, accumulate):
        h1_ref[0] = h1_ref[0] + val if accumulate else val

    _mixer_tile(lambda: x_ref[0], store_tile, MIXER_TILE, MIXER_TILE, False, refs)


def _ffn_kernel(h1_ref, h1m_ref, fnw_ref, wup_ref, fcw_ref, fcb_ref, wdn_ref, finw_ref,
                out_ref, as_ref):
    T = FFN_TILE
    fnw = fnw_ref[...]

    @pl.when(pl.program_id(1) == 0)
    def _():
        um = _rms(h1m_ref[0], fnw).astype(BF16)
        am = jnp.dot(um, wup_ref[:, 0:D_FF], preferred_element_type=F32)
        as_ref[0:SUBLANES, :] = am[N_META - SUBLANES:N_META, :]

    x = h1_ref[0]
    u = _rms(x, fnw).astype(BF16)
    y = x
    for cb in range(D_FF // FFN_COL_BLOCK):
        cols = slice(cb * FFN_COL_BLOCK, (cb + 1) * FFN_COL_BLOCK)
        a = jnp.dot(u, wup_ref[:, cols], preferred_element_type=F32)
        as_ref[SUBLANES:SUBLANES + T, cols] = a
        a1 = as_ref[SUBLANES - 1:SUBLANES - 1 + T, cols]
        a2 = as_ref[SUBLANES - 2:SUBLANES - 2 + T, cols]
        ac = (fcw_ref[0:1, cols] * a2 + fcw_ref[1:2, cols] * a1 + fcw_ref[2:3, cols] * a
              + fcb_ref[:, cols])
        as_ref[0:SUBLANES, cols] = as_ref[T:T + SUBLANES, cols]
        val = jnp.dot(u, wup_ref[:, D_FF + cb * FFN_COL_BLOCK:D_FF + (cb + 1) * FFN_COL_BLOCK],
                      preferred_element_type=F32)
        gated = (ac * _sigmoid(ac) * val).astype(BF16)
        y = y + jnp.dot(gated, wdn_ref[cols, 0:D_MODEL], preferred_element_type=F32)
    out_ref[0] = _rms(y, finw_ref[...])


def _resident(shape):
    return pl.BlockSpec(shape, lambda b, t: (0,) * len(shape), pipeline_mode=pl.Buffered(1))


def _weight(w):
    return jnp.pad(w.astype(BF16), ((0, 0), (0, WEIGHT_LANE_PAD)))


def _rsel_matrix():
    r = np.arange(N_HEADS * HEAD_DIM)[:, None] // HEAD_DIM
    l = np.arange(CHUNK)[None, :] % SUBLANES
    return jnp.asarray(r == l, dtype=BF16)


def kernel(x, meta_tokens, lb_param, attn_norm_w, w_in, hgrn_norm_w, conv_w, w_out, ffn_norm_w, w_up,
           ffn_conv_w, ffn_conv_b, w_down, final_norm_w):
    B, L, D = x.shape
    assert D == D_MODEL and L % MIXER_TILE == 0 and L % FFN_TILE == 0
    assert w_in.shape == (1, D, N_IN_SECTIONS * D) and w_up.shape == (1, D, 2 * D_FF)
    assert SUBLANES * HEAD_DIM == D_MODEL and N_HEADS == SUBLANES

    tile = lambda T: pl.BlockSpec((1, T, D), lambda b, t: (b, t, 0))
    meta_tile = pl.BlockSpec((1, N_META, D), lambda b, t: (b, 0, 0))
    params = pltpu.CompilerParams(dimension_semantics=("arbitrary", "arbitrary"),
                                  vmem_limit_bytes=VMEM_LIMIT_BYTES)

    TM = MIXER_TILE
    n_steps = B * (L // TM)
    step = lambda b, t: b * (L // TM) + t
    assert D % n_steps == 0 and D_FF % W_DOWN_ROWS_PER_STEP == 0 and D_FF // W_DOWN_ROWS_PER_STEP <= n_steps
    wup_rows = D // n_steps
    wup_block = pl.BlockSpec((wup_rows, 2 * D_FF), lambda b, t: (step(b, t), 0))
    wdn_index = lambda b, t: (jnp.minimum(step(b, t), D_FF // W_DOWN_ROWS_PER_STEP - 1), 0)
    slot = [pltpu.VMEM((TM, GROUP_WIDTH), F32)] * 3 + [pltpu.VMEM((TM, GROUP_WIDTH), BF16)] \
        + [pltpu.VMEM((TM, GROUP_WIDTH), F32)] * 3 + [pltpu.VMEM((TM + SUBLANES, GROUP_WIDTH), F32)]
    h1, h1m, w_up_bf16, w_down_bf16 = pl.pallas_call(
        _mixer_kernel,
        grid=(B, L // TM),
        in_specs=[tile(TM), _resident((N_META, D)), _resident((2, D)), _resident((1, D)),
                  _resident((D, N_IN_SECTIONS * D + WEIGHT_LANE_PAD)), _resident((1, HEAD_DIM)),
                  _resident((3, D)), _resident((D, D + WEIGHT_LANE_PAD)),
                  _resident((N_HEADS * HEAD_DIM, CHUNK)), wup_block,
                  pl.BlockSpec((W_DOWN_ROWS_PER_STEP, D), wdn_index)],
        out_specs=[tile(TM), meta_tile, wup_block,
                   pl.BlockSpec((W_DOWN_ROWS_PER_STEP, D + WEIGHT_LANE_PAD), wdn_index)],
        out_shape=[jax.ShapeDtypeStruct((B, L, D), F32), jax.ShapeDtypeStruct((B, N_META, D), F32),
                   jax.ShapeDtypeStruct((D, 2 * D_FF), BF16),
                   jax.ShapeDtypeStruct((D_FF, D + WEIGHT_LANE_PAD), BF16)],
        scratch_shapes=[pltpu.VMEM((N_HEADS, HEAD_DIM, HEAD_DIM), F32),
                        pltpu.VMEM((N_HEADS, HEAD_DIM, HEAD_DIM), F32),
                        pltpu.VMEM((SUBLANES, D), F32),
                        pltpu.VMEM((SUBLANES, D), F32),
                        pltpu.VMEM((TM, D), BF16),
                        pltpu.SMEM((N_GROUPS,), jnp.int32),
                        pltpu.VMEM((N_HEADS, HEAD_DIM, HEAD_DIM), F32),
                        pltpu.VMEM((SUBLANES, D), F32),
                        pltpu.VMEM((N_META, D), F32),
                        *(slot * N_GROUPS)],
        compiler_params=params,
        name="mixer",
    )(x, meta_tokens, lb_param, attn_norm_w, _weight(w_in[0]), hgrn_norm_w, conv_w[0],
      _weight(w_out[0]), _rsel_matrix(), w_up[0], w_down[0])

    TF = FFN_TILE
    out = pl.pallas_call(
        _ffn_kernel,
        grid=(B, L // TF),
        in_specs=[tile(TF), meta_tile, _resident((1, D)), _resident((D, 2 * D_FF)), _resident((3, D_FF)),
                  _resident((1, D_FF)), _resident((D_FF, D + WEIGHT_LANE_PAD)), _resident((1, D))],
        out_specs=tile(TF),
        out_shape=jax.ShapeDtypeStruct((B, L, D), F32),
        scratch_shapes=[pltpu.VMEM((TF + SUBLANES, D_FF), F32)],
        compiler_params=params,
        name="ffn",
    )(h1, h1m, ffn_norm_w, w_up_bf16, ffn_conv_w[0], ffn_conv_b, w_down_bf16, final_norm_w.reshape(1, D))
    return out
```

```python
import numpy as np
import jax
import jax.numpy as jnp
from jax import lax
from jax.experimental import pallas as pl
from jax.experimental.pallas import tpu as pltpu

D_MODEL = 1024
N_META = 16
N_HEADS = 8
HEAD_DIM = 128
D_FF = 2816
N_IN_SECTIONS = 9
EPS = 1e-6

SUBLANES = 8
CHUNK = 128
GROUP_HEADS = 4
GROUP_WIDTH = GROUP_HEADS * HEAD_DIM
N_GROUPS = N_HEADS // GROUP_HEADS
MAX_SAFE_EXPONENT = 60.0
LOG2_E = 1.4426950408889634
MIXER_TILE = 512
FFN_TILE = 512
FFN_COL_BLOCK = 2816
WEIGHT_LANE_PAD = 128
W_DOWN_ROWS_PER_STEP = 128
V7X_VMEM_BYTES = 64 * 1024 * 1024
VMEM_LIMIT_BYTES = V7X_VMEM_BYTES - 4 * 1024 * 1024

F32 = jnp.float32
BF16 = jnp.bfloat16

_NT = (((1,), (1,)), ((), ()))
_TN = (((0,), (0,)), ((), ()))


def _rms(x, w):
    ms = jnp.mean(x * x, axis=-1, keepdims=True)
    return x * lax.rsqrt(ms + EPS) * w


def _sigmoid(x):
    return 1.0 / (1.0 + jnp.exp2(x * -LOG2_E))


def _bcast_rows(ref, lanes, row0, block, offset, nrows):
    pieces = [jnp.broadcast_to(_row(ref, lanes, row0, i * block + offset), (block, HEAD_DIM))
              for i in range(nrows // block)]
    return pieces[0] if len(pieces) == 1 else jnp.concatenate(pieces, axis=0)


def _row(ref, lanes, row0, r):
    group = ref[pl.ds(row0 + r // SUBLANES * SUBLANES, SUBLANES), lanes]
    return group[r % SUBLANES:r % SUBLANES + 1, :]


def _hgrn_chunk_exact(row0, lanes, valid, st, q_ref, g_ref, k_ref, v_ref, rsel):
    rows = pl.ds(row0, CHUNK)
    q = q_ref[rows, lanes]
    G = g_ref[rows, lanes]
    k = k_ref[rows, lanes]
    v = v_ref[rows, lanes]
    row = lax.broadcasted_iota(jnp.int32, (CHUNK, HEAD_DIM), 0)

    sub = row & (SUBLANES - 1)
    ps = []
    for j in range(SUBLANES):
        gj = _bcast_rows(g_ref, lanes, row0, SUBLANES, j, CHUNK)
        kj = _bcast_rows(k_ref, lanes, row0, SUBLANES, j, CHUNK)
        p = q * kj * jnp.exp2(G - gj)
        ps.append(jnp.where(sub >= j, p, 0.0).astype(BF16))
    pcat = jnp.concatenate(ps, axis=1)
    a = jnp.dot(pcat, rsel, preferred_element_type=F32)

    tl_xor = (lax.broadcasted_iota(jnp.int32, (CHUNK, CHUNK), 0)
              ^ lax.broadcasted_iota(jnp.int32, (CHUNK, CHUNK), 1))
    a = jnp.where(tl_xor < SUBLANES, a, 0.0)

    b = 2 * SUBLANES
    while b <= CHUNK:
        gm = _bcast_rows(g_ref, lanes, row0, b, b // 2 - 1, CHUNK)
        e = jnp.exp2(-jnp.abs(G - gm))
        upper = (row & (b // 2)) != 0
        qt = jnp.where(upper, q * e, 0.0).astype(BF16)
        kt = jnp.where(upper, 0.0, k * e).astype(BF16)
        ab = lax.dot_general(qt, kt, _NT, preferred_element_type=F32)
        a = a + (ab if b == CHUNK else jnp.where(tl_xor < b, ab, 0.0))
        b *= 2

    qi = (q * jnp.exp2(G)).astype(BF16)
    o = lax.dot_general(qi, st.astype(BF16), _NT, preferred_element_type=F32)
    o = o + jnp.dot(a.astype(BF16), v, preferred_element_type=F32)

    glast = _row(g_ref, lanes, row0, valid - 1)
    kd = k * jnp.exp2(glast - G)
    if valid < CHUNK:
        kd = jnp.where(row < valid, kd, 0.0)
    st = st * jnp.exp2(glast) + lax.dot_general(v, kd.astype(BF16), _TN, preferred_element_type=F32)
    return o, st


def _hgrn_chunk_fast(row0, lanes, st, q_ref, g_ref, k_ref, v_ref):
    rows = pl.ds(row0, CHUNK)
    q = q_ref[rows, lanes]
    G = g_ref[rows, lanes]
    k = k_ref[rows, lanes]
    v = v_ref[rows, lanes]
    gmid = g_ref[pl.ds(row0 + CHUNK // 2 - 1, 1), lanes]
    glast = g_ref[pl.ds(row0 + CHUNK - 1, 1), lanes]
    d = G - gmid
    qh = q * jnp.exp2(d)
    kh = k * jnp.exp2(-d)
    a = lax.dot_general(qh.astype(BF16), kh.astype(BF16), _NT, preferred_element_type=F32)
    causal = (lax.broadcasted_iota(jnp.int32, (CHUNK, CHUNK), 1)
              <= lax.broadcasted_iota(jnp.int32, (CHUNK, CHUNK), 0))
    a = jnp.where(causal, a, 0.0)
    qi = (qh * jnp.exp2(gmid)).astype(BF16)
    o = lax.dot_general(qi, st.astype(BF16), _NT, preferred_element_type=F32)
    o = o + jnp.dot(a.astype(BF16), v, preferred_element_type=F32)
    kd = (kh * jnp.exp2(glast - gmid)).astype(BF16)
    st = st * jnp.exp2(glast) + lax.dot_general(v, kd, _TN, preferred_element_type=F32)
    return o, st


def _project_group(u, g, R, valid, prm, slot, recurrence_inputs):
    lb_ref, win_ref, cw_ref, zc_ref, safe_ref = prm
    q_ref, g_ref, k_ref, v_ref, gs_ref, sa_ref, mb_ref, zs_ref = slot
    c0 = g * GROUP_WIDTH
    cols = slice(c0, c0 + GROUP_WIDTH)

    def sec(i):
        w = win_ref[:, i * D_MODEL + c0:i * D_MODEL + c0 + GROUP_WIDTH]
        return jnp.dot(u, w, preferred_element_type=F32)

    if not recurrence_inputs:
        _project_gates(sec, cols, R, valid, cw_ref, zc_ref, slot)
        return

    qv = sec(0)
    q_ref[0:R, :] = qv * _sigmoid(qv)

    f = lb_ref[0:1, cols] + lb_ref[1:2, cols] * _sigmoid(sec(1))
    k_ref[0:R, :] = 1.0 - f
    lf = jnp.log2(f)
    tri = (lax.broadcasted_iota(jnp.int32, (CHUNK, CHUNK), 1)
           <= lax.broadcasted_iota(jnp.int32, (CHUNK, CHUNK), 0)).astype(BF16)
    tri2 = jnp.concatenate([tri, tri], axis=1)
    worst = jnp.zeros((1, GROUP_WIDTH), F32)
    for c in range(R // CHUNK):
        lfc = lf[c * CHUNK:(c + 1) * CHUNK, :]
        hi = lfc.astype(BF16)
        lo = (lfc - hi.astype(F32)).astype(BF16)
        gc = jnp.dot(tri2, jnp.concatenate([hi, lo], axis=0),
                     preferred_element_type=F32)
        g_ref[c * CHUNK:(c + 1) * CHUNK, :] = gc
        gmid = gc[CHUNK // 2 - 1:CHUNK // 2, :]
        glast = gc[CHUNK - 1:CHUNK, :]
        worst = jnp.maximum(worst, jnp.maximum(-gmid, gmid - glast))
    safe_ref[g] = (jnp.max(worst) <= MAX_SAFE_EXPONENT * LOG2_E).astype(jnp.int32)

    v_ref[0:R, :] = sec(2).astype(BF16)


def _project_gates(sec, cols, R, valid, cw_ref, zc_ref, slot):
    gs_ref, sa_ref, mb_ref, zs_ref = slot[4:8]
    gv = sec(3)
    gs_ref[0:R, :] = (gv * _sigmoid(gv)).astype(BF16)

    bg = sec(4)
    z = sec(5) * sec(6)
    zs_ref[0:SUBLANES, :] = zc_ref[:, cols]
    zs_ref[SUBLANES:SUBLANES + R, :] = z
    z1 = zs_ref[SUBLANES - 1:SUBLANES - 1 + R, :]
    z2 = zs_ref[SUBLANES - 2:SUBLANES - 2 + R, :]
    yb = bg * (cw_ref[0:1, cols] * z2 + cw_ref[1:2, cols] * z1 + cw_ref[2:3, cols] * z)
    zc_ref[:, cols] = zs_ref[valid:valid + SUBLANES, :]

    sa_ref[0:R, :] = _sigmoid(sec(7)).astype(BF16)
    mb_ref[0:R, :] = (_sigmoid(sec(8)) * yb).astype(BF16)


def _finish_chunk(row0, h, lanes, o, hnw, mg_ref, slot):
    gs_ref, sa_ref, mb_ref = slot[4:7]
    rows = pl.ds(row0, CHUNK)
    m = (_rms(o, hnw) * gs_ref[rows, lanes].astype(F32) * sa_ref[rows, lanes].astype(F32)
         + mb_ref[rows, lanes].astype(F32))
    mg_ref[rows, h * HEAD_DIM:(h + 1) * HEAD_DIM] = m.astype(BF16)


def _recur_group_fast(g, R, hnw, st_ref, st_old_ref, mg_ref, slot):
    q_ref, g_ref, k_ref, v_ref = slot[0:4]
    for hh in range(GROUP_HEADS):
        h = GROUP_HEADS * g + hh
        lanes = slice(hh * HEAD_DIM, (hh + 1) * HEAD_DIM)
        st = st_ref[h]
        st_old_ref[h] = st
        for c in range(R // CHUNK):
            o, st = _hgrn_chunk_fast(c * CHUNK, lanes, st, q_ref, g_ref, k_ref, v_ref)
            mg_ref[c * CHUNK:(c + 1) * CHUNK, h * HEAD_DIM:(h + 1) * HEAD_DIM] = _rms(o, hnw).astype(BF16)
        st_ref[h] = st


def _finish_group_fast(g, R, mg_ref, slot):
    gs_ref, sa_ref, mb_ref = slot[4:7]
    cols = slice(g * GROUP_WIDTH, (g + 1) * GROUP_WIDTH)
    mg_ref[0:R, cols] = mg_ref[0:R, cols] * gs_ref[0:R, :] * sa_ref[0:R, :] + mb_ref[0:R, :]


def _recur_group_exact(g, R, valid, hnw, rsel, st_in_ref, st_ref, mg_ref, slot):
    q_ref, g_ref, k_ref, v_ref = slot[0:4]
    for hh in range(GROUP_HEADS):
        h = GROUP_HEADS * g + hh
        lanes = slice(hh * HEAD_DIM, (hh + 1) * HEAD_DIM)
        st = st_in_ref[h]
        if valid < R:
            o, st = _hgrn_chunk_exact(0, lanes, valid, st, q_ref, g_ref, k_ref, v_ref, rsel)
            _finish_chunk(0, h, lanes, o, hnw, mg_ref, slot)
        else:
            def chunk_body(c, st):
                row0 = pl.multiple_of(c * CHUNK, CHUNK)
                o, st = _hgrn_chunk_exact(row0, lanes, CHUNK, st, q_ref, g_ref, k_ref, v_ref, rsel)
                _finish_chunk(row0, h, lanes, o, hnw, mg_ref, slot)
                return st

            st = lax.fori_loop(0, R // CHUNK, chunk_body, st)
        st_ref[h] = st


def _out_partial(g, R, mg_ref, wout_ref):
    rows = slice(g * GROUP_WIDTH, (g + 1) * GROUP_WIDTH)
    return jnp.dot(mg_ref[0:R, rows], wout_ref[rows, 0:D_MODEL], preferred_element_type=F32)


def _mixer_tile(load_x, store_out, R, valid, is_meta, refs):
    (lbp_ref, anw_ref, win_ref, hnw_ref, cw_ref, wout_ref, rsel_ref,
     st_ref, st_old_ref, zc_ref, lb_ref, mg_ref, safe_ref, slots) = refs
    prm = (lb_ref, win_ref, cw_ref, zc_ref, safe_ref)
    u = _rms(load_x(), anw_ref[...]).astype(BF16)
    rsel = rsel_ref[...]
    hnw = hnw_ref[...]
    project = lambda g, first: _project_group(u, g, R, valid, prm, slots[g], first)

    lbp = lbp_ref[...]
    mx = jnp.max(lbp, axis=0, keepdims=True)
    ex = jnp.exp(lbp - mx)
    lb = ex[0:1, :] / jnp.sum(ex, axis=0, keepdims=True)
    lb_ref[0:1, :] = lb
    lb_ref[1:2, :] = 1.0 - lb

    if is_meta:
        for g in range(N_GROUPS):
            project(g, True)
            project(g, False)
            _recur_group_exact(g, R, valid, hnw, rsel, st_ref, st_ref, mg_ref, slots[g])
        acc = load_x()
        for g in range(N_GROUPS):
            acc = acc + _out_partial(g, R, mg_ref, wout_ref)
        store_out(acc, False)
        return

    def store_early():
        acc = load_x()
        for g in range(N_GROUPS - 1):
            acc = acc + _out_partial(g, R, mg_ref, wout_ref)
        store_out(acc, False)

    for g in range(N_GROUPS):
        project(g, True)
        if g > 0:
            _finish_group_fast(g - 1, R, mg_ref, slots[g - 1])
            if g == N_GROUPS - 1:
                store_early()
        _recur_group_fast(g, R, hnw, st_ref, st_old_ref, mg_ref, slots[g])
        project(g, False)
    _finish_group_fast(N_GROUPS - 1, R, mg_ref, slots[N_GROUPS - 1])
    for g in range(N_GROUPS):
        def redo(g=g):
            _recur_group_exact(g, R, valid, hnw, rsel, st_old_ref, st_ref, mg_ref, slots[g])
            if g < N_GROUPS - 1:
                store_early()

        pl.when(safe_ref[g] == 0)(redo)
    store_out(_out_partial(N_GROUPS - 1, R, mg_ref, wout_ref), True)


def _mixer_kernel(x_ref, meta_ref, lbp_ref, anw_ref, win_ref, hnw_ref, cw_ref, wout_ref, rsel_ref,
                  wup_f32_ref, wdn_f32_ref, h1_ref, h1m_ref, wup_ref, wdn_ref,
                  st_ref, st_old_ref, zc_ref, lb_ref, mg_ref, safe_ref,
                  st_meta_ref, zc_meta_ref, h1m_meta_ref, *slots):
    wup_ref[...] = wup_f32_ref[...].astype(BF16)
    wdn_ref[:, 0:D_MODEL] = wdn_f32_ref[...].astype(BF16)
    wdn_ref[:, D_MODEL:] = jnp.zeros((W_DOWN_ROWS_PER_STEP, WEIGHT_LANE_PAD), BF16)

    n = len(slots) // N_GROUPS
    refs = (lbp_ref, anw_ref, win_ref, hnw_ref, cw_ref, wout_ref, rsel_ref,
            st_ref, st_old_ref, zc_ref, lb_ref, mg_ref, safe_ref,
            [slots[i * n:(i + 1) * n] for i in range(N_GROUPS)])

    first_tile = pl.program_id(1) == 0

    @pl.when(jnp.logical_and(first_tile, pl.program_id(0) == 0))
    def _():
        st_ref[...] = jnp.zeros_like(st_ref)
        zc_ref[...] = jnp.zeros_like(zc_ref)
        load_meta = lambda: jnp.concatenate(
            [meta_ref[...], jnp.zeros((CHUNK - N_META, D_MODEL), F32)], axis=0)

        def store_meta(val, accumulate):
            h1m_meta_ref[...] = val[0:N_META, :]

        _mixer_tile(load_meta, store_meta, CHUNK, N_META, True, refs)
        st_meta_ref[...] = st_ref[...]
        zc_meta_ref[...] = zc_ref[...]

    @pl.when(first_tile)
    def _():
        st_ref[...] = st_meta_ref[...]
        zc_ref[...] = zc_meta_ref[...]
        h1m_ref[0] = h1m_meta_ref[...]

    def store_tile(val, accumulate):
        h1_ref[0] = h1_ref[0] + val if accumulate else val

    _mixer_tile(lambda: x_ref[0], store_tile, MIXER_TILE, MIXER_TILE, False, refs)


def _ffn_kernel(h1_ref, h1m_ref, fnw_ref, wup_ref, fcw_ref, fcb_ref, wdn_ref, finw_ref,
                out_ref, as_ref):
    T = FFN_TILE
    fnw = fnw_ref[...]

    @pl.when(pl.program_id(1) == 0)
    def _():
        um = _rms(h1m_ref[0], fnw).astype(BF16)
        am = jnp.dot(um, wup_ref[:, 0:D_FF], preferred_element_type=F32)
        as_ref[0:SUBLANES, :] = am[N_META - SUBLANES:N_META, :]

    x = h1_ref[0]
    u = _rms(x, fnw).astype(BF16)
    y = x
    for cb in range(D_FF // FFN_COL_BLOCK):
        cols = slice(cb * FFN_COL_BLOCK, (cb + 1) * FFN_COL_BLOCK)
        a = jnp.dot(u, wup_ref[:, cols], preferred_element_type=F32)
        as_ref[SUBLANES:SUBLANES + T, cols] = a
        a1 = as_ref[SUBLANES - 1:SUBLANES - 1 + T, cols]
        a2 = as_ref[SUBLANES - 2:SUBLANES - 2 + T, cols]
        ac = (fcw_ref[0:1, cols] * a2 + fcw_ref[1:2, cols] * a1 + fcw_ref[2:3, cols] * a
              + fcb_ref[:, cols])
        as_ref[0:SUBLANES, cols] = as_ref[T:T + SUBLANES, cols]
        val = jnp.dot(u, wup_ref[:, D_FF + cb * FFN_COL_BLOCK:D_FF + (cb + 1) * FFN_COL_BLOCK],
                      preferred_element_type=F32)
        gated = (ac * _sigmoid(ac) * val).astype(BF16)
        y = y + jnp.dot(gated, wdn_ref[cols, 0:D_MODEL], preferred_element_type=F32)
    out_ref[0] = _rms(y, finw_ref[...])


def _resident(shape):
    return pl.BlockSpec(shape, lambda b, t: (0,) * len(shape), pipeline_mode=pl.Buffered(1))


def _weight(w):
    return jnp.pad(w.astype(BF16), ((0, 0), (0, WEIGHT_LANE_PAD)))


def _rsel_matrix():
    r = np.arange(N_HEADS * HEAD_DIM)[:, None] // HEAD_DIM
    l = np.arange(CHUNK)[None, :] % SUBLANES
    return jnp.asarray(r == l, dtype=BF16)


def kernel(x, meta_tokens, lb_param, attn_norm_w, w_in, hgrn_norm_w, conv_w, w_out, ffn_norm_w, w_up,
           ffn_conv_w, ffn_conv_b, w_down, final_norm_w):
    B, L, D = x.shape
    assert D == D_MODEL and L % MIXER_TILE == 0 and L % FFN_TILE == 0
    assert w_in.shape == (1, D, N_IN_SECTIONS * D) and w_up.shape == (1, D, 2 * D_FF)
    assert SUBLANES * HEAD_DIM == D_MODEL and N_HEADS == SUBLANES

    tile = lambda T: pl.BlockSpec((1, T, D), lambda b, t: (b, t, 0))
    meta_tile = pl.BlockSpec((1, N_META, D), lambda b, t: (b, 0, 0))
    params = pltpu.CompilerParams(dimension_semantics=("arbitrary", "arbitrary"),
                                  vmem_limit_bytes=VMEM_LIMIT_BYTES)

    TM = MIXER_TILE
    n_steps = B * (L // TM)
    step = lambda b, t: b * (L // TM) + t
    assert D % n_steps == 0 and D_FF % W_DOWN_ROWS_PER_STEP == 0 and D_FF // W_DOWN_ROWS_PER_STEP <= n_steps
    wup_rows = D // n_steps
    wup_block = pl.BlockSpec((wup_rows, 2 * D_FF), lambda b, t: (step(b, t), 0))
    wdn_index = lambda b, t: (jnp.minimum(step(b, t), D_FF // W_DOWN_ROWS_PER_STEP - 1), 0)
    slot = [pltpu.VMEM((TM, GROUP_WIDTH), F32)] * 3 + [pltpu.VMEM((TM, GROUP_WIDTH), BF16)] * 4 \
        + [pltpu.VMEM((TM + SUBLANES, GROUP_WIDTH), F32)]
    h1, h1m, w_up_bf16, w_down_bf16 = pl.pallas_call(
        _mixer_kernel,
        grid=(B, L // TM),
        in_specs=[tile(TM), _resident((N_META, D)), _resident((2, D)), _resident((1, D)),
                  _resident((D, N_IN_SECTIONS * D + WEIGHT_LANE_PAD)), _resident((1, HEAD_DIM)),
                  _resident((3, D)), _resident((D, D + WEIGHT_LANE_PAD)),
                  _resident((N_HEADS * HEAD_DIM, CHUNK)), wup_block,
                  pl.BlockSpec((W_DOWN_ROWS_PER_STEP, D), wdn_index)],
        out_specs=[tile(TM), meta_tile, wup_block,
                   pl.BlockSpec((W_DOWN_ROWS_PER_STEP, D + WEIGHT_LANE_PAD), wdn_index)],
        out_shape=[jax.ShapeDtypeStruct((B, L, D), F32), jax.ShapeDtypeStruct((B, N_META, D), F32),
                   jax.ShapeDtypeStruct((D, 2 * D_FF), BF16),
                   jax.ShapeDtypeStruct((D_FF, D + WEIGHT_LANE_PAD), BF16)],
        scratch_shapes=[pltpu.VMEM((N_HEADS, HEAD_DIM, HEAD_DIM), F32),
                        pltpu.VMEM((N_HEADS, HEAD_DIM, HEAD_DIM), F32),
                        pltpu.VMEM((SUBLANES, D), F32),
                        pltpu.VMEM((SUBLANES, D), F32),
                        pltpu.VMEM((TM, D), BF16),
                        pltpu.SMEM((N_GROUPS,), jnp.int32),
                        pltpu.VMEM((N_HEADS, HEAD_DIM, HEAD_DIM), F32),
                        pltpu.VMEM((SUBLANES, D), F32),
                        pltpu.VMEM((N_META, D), F32),
                        *(slot * N_GROUPS)],
        compiler_params=params,
        name="mixer",
    )(x, meta_tokens, lb_param, attn_norm_w, _weight(w_in[0]), hgrn_norm_w, conv_w[0],
      _weight(w_out[0]), _rsel_matrix(), w_up[0], w_down[0])

    TF = FFN_TILE
    out = pl.pallas_call(
        _ffn_kernel,
        grid=(B, L // TF),
        in_specs=[tile(TF), meta_tile, _resident((1, D)), _resident((D, 2 * D_FF)), _resident((3, D_FF)),
                  _resident((1, D_FF)), _resident((D_FF, D + WEIGHT_LANE_PAD)), _resident((1, D))],
        out_specs=tile(TF),
        out_shape=jax.ShapeDtypeStruct((B, L, D), F32),
        scratch_shapes=[pltpu.VMEM((TF + SUBLANES, D_FF), F32)],
        compiler_params=params,
        name="ffn",
    )(h1, h1m, ffn_norm_w, w_up_bf16, ffn_conv_w[0], ffn_conv_b, w_down_bf16, final_norm_w.reshape(1, D))
    return out
```

```python
import numpy as np
import jax
import jax.numpy as jnp
from jax import lax
from jax.experimental import pallas as pl
from jax.experimental.pallas import tpu as pltpu

D_MODEL = 1024
N_META = 16
N_HEADS = 8
HEAD_DIM = 128
D_FF = 2816
N_IN_SECTIONS = 9
EPS = 1e-6

SUBLANES = 8
CHUNK = 128
GROUP_HEADS = 4
GROUP_WIDTH = GROUP_HEADS * HEAD_DIM
N_GROUPS = N_HEADS // GROUP_HEADS
MAX_SAFE_EXPONENT = 60.0
LOG2_E = 1.4426950408889634
MIXER_TILE = 512
FFN_TILE = 512
FFN_COL_BLOCK = 2816
WEIGHT_LANE_PAD = 128
W_DOWN_ROWS_PER_STEP = 128
V7X_VMEM_BYTES = 64 * 1024 * 1024
VMEM_LIMIT_BYTES = V7X_VMEM_BYTES - 4 * 1024 * 1024

F32 = jnp.float32
BF16 = jnp.bfloat16

_NT = (((1,), (1,)), ((), ()))
_TN = (((0,), (0,)), ((), ()))


def _rms(x, w):
    ms = jnp.mean(x * x, axis=-1, keepdims=True)
    return x * lax.rsqrt(ms + EPS) * w


def _sigmoid(x):
    return 1.0 / (1.0 + jnp.exp2(x * -LOG2_E))


def _bcast_rows(ref, lanes, row0, block, offset, nrows):
    pieces = [jnp.broadcast_to(_row(ref, lanes, row0, i * block + offset), (block, HEAD_DIM))
              for i in range(nrows // block)]
    return pieces[0] if len(pieces) == 1 else jnp.concatenate(pieces, axis=0)


def _row(ref, lanes, row0, r):
    group = ref[pl.ds(row0 + r // SUBLANES * SUBLANES, SUBLANES), lanes]
    return group[r % SUBLANES:r % SUBLANES + 1, :]


def _hgrn_chunk_exact(row0, lanes, valid, st, q_ref, g_ref, k_ref, v_ref, rsel):
    rows = pl.ds(row0, CHUNK)
    q = q_ref[rows, lanes]
    G = g_ref[rows, lanes]
    k = k_ref[rows, lanes]
    v = v_ref[rows, lanes]
    row = lax.broadcasted_iota(jnp.int32, (CHUNK, HEAD_DIM), 0)

    sub = row & (SUBLANES - 1)
    ps = []
    for j in range(SUBLANES):
        gj = _bcast_rows(g_ref, lanes, row0, SUBLANES, j, CHUNK)
        kj = _bcast_rows(k_ref, lanes, row0, SUBLANES, j, CHUNK)
        p = q * kj * jnp.exp2(G - gj)
        ps.append(jnp.where(sub >= j, p, 0.0).astype(BF16))
    pcat = jnp.concatenate(ps, axis=1)
    a = jnp.dot(pcat, rsel, preferred_element_type=F32)

    tl_xor = (lax.broadcasted_iota(jnp.int32, (CHUNK, CHUNK), 0)
              ^ lax.broadcasted_iota(jnp.int32, (CHUNK, CHUNK), 1))
    a = jnp.where(tl_xor < SUBLANES, a, 0.0)

    b = 2 * SUBLANES
    while b <= CHUNK:
        gm = _bcast_rows(g_ref, lanes, row0, b, b // 2 - 1, CHUNK)
        e = jnp.exp2(-jnp.abs(G - gm))
        upper = (row & (b // 2)) != 0
        qt = jnp.where(upper, q * e, 0.0).astype(BF16)
        kt = jnp.where(upper, 0.0, k * e).astype(BF16)
        ab = lax.dot_general(qt, kt, _NT, preferred_element_type=F32)
        a = a + (ab if b == CHUNK else jnp.where(tl_xor < b, ab, 0.0))
        b *= 2

    qi = (q * jnp.exp2(G)).astype(BF16)
    o = lax.dot_general(qi, st.astype(BF16), _NT, preferred_element_type=F32)
    o = o + jnp.dot(a.astype(BF16), v, preferred_element_type=F32)

    glast = _row(g_ref, lanes, row0, valid - 1)
    kd = k * jnp.exp2(glast - G)
    if valid < CHUNK:
        kd = jnp.where(row < valid, kd, 0.0)
    st = st * jnp.exp2(glast) + lax.dot_general(v, kd.astype(BF16), _TN, preferred_element_type=F32)
    return o, st


def _hgrn_chunk_fast(row0, lanes, st, q_ref, g_ref, k_ref, v_ref):
    rows = pl.ds(row0, CHUNK)
    q = q_ref[rows, lanes]
    G = g_ref[rows, lanes]
    k = k_ref[rows, lanes]
    v = v_ref[rows, lanes]
    gmid = g_ref[pl.ds(row0 + CHUNK // 2 - 1, 1), lanes]
    glast = g_ref[pl.ds(row0 + CHUNK - 1, 1), lanes]
    d = G - gmid
    qh = q * jnp.exp2(d)
    kh = k * jnp.exp2(-d)
    a = lax.dot_general(qh.astype(BF16), kh.astype(BF16), _NT, preferred_element_type=F32)
    causal = (lax.broadcasted_iota(jnp.int32, (CHUNK, CHUNK), 1)
              <= lax.broadcasted_iota(jnp.int32, (CHUNK, CHUNK), 0))
    a = jnp.where(causal, a, 0.0)
    qi = (qh * jnp.exp2(gmid)).astype(BF16)
    o = lax.dot_general(qi, st.astype(BF16), _NT, preferred_element_type=F32)
    o = o + jnp.dot(a.astype(BF16), v, preferred_element_type=F32)
    kd = (kh * jnp.exp2(glast - gmid)).astype(BF16)
    st = st * jnp.exp2(glast) + lax.dot_general(v, kd, _TN, preferred_element_type=F32)
    return o, st


def _project_group(u, g, R, valid, prm, slot, recurrence_inputs):
    lb_ref, win_ref, cw_ref, zc_ref, safe_ref = prm
    q_ref, g_ref, k_ref, v_ref, gs_ref, sa_ref, mb_ref, zs_ref = slot
    c0 = g * GROUP_WIDTH
    cols = slice(c0, c0 + GROUP_WIDTH)

    def sec(i):
        w = win_ref[:, i * D_MODEL + c0:i * D_MODEL + c0 + GROUP_WIDTH]
        return jnp.dot(u, w, preferred_element_type=F32)

    if not recurrence_inputs:
        _project_gates(sec, cols, R, valid, cw_ref, zc_ref, slot)
        return

    qv = sec(0)
    q_ref[0:R, :] = qv * _sigmoid(qv)

    f = lb_ref[0:1, cols] + lb_ref[1:2, cols] * _sigmoid(sec(1))
    k_ref[0:R, :] = 1.0 - f
    lf = jnp.log2(f)
    tri = (lax.broadcasted_iota(jnp.int32, (CHUNK, CHUNK), 1)
           <= lax.broadcasted_iota(jnp.int32, (CHUNK, CHUNK), 0)).astype(BF16)
    worst = jnp.zeros((1, GROUP_WIDTH), F32)
    for c in range(R // CHUNK):
        lfc = lf[c * CHUNK:(c + 1) * CHUNK, :].astype(BF16)
        gc = jnp.dot(tri, lfc, preferred_element_type=F32)
        g_ref[c * CHUNK:(c + 1) * CHUNK, :] = gc
        gmid = gc[CHUNK // 2 - 1:CHUNK // 2, :]
        glast = gc[CHUNK - 1:CHUNK, :]
        worst = jnp.maximum(worst, jnp.maximum(-gmid, gmid - glast))
    safe_ref[g] = (jnp.max(worst) <= MAX_SAFE_EXPONENT * LOG2_E).astype(jnp.int32)

    v_ref[0:R, :] = sec(2).astype(BF16)


def _project_gates(sec, cols, R, valid, cw_ref, zc_ref, slot):
    gs_ref, sa_ref, mb_ref, zs_ref = slot[4:8]
    gv = sec(3)
    gs_ref[0:R, :] = (gv * _sigmoid(gv)).astype(BF16)

    bg = sec(4)
    z = sec(5) * sec(6)
    zs_ref[0:SUBLANES, :] = zc_ref[:, cols]
    zs_ref[SUBLANES:SUBLANES + R, :] = z
    z1 = zs_ref[SUBLANES - 1:SUBLANES - 1 + R, :]
    z2 = zs_ref[SUBLANES - 2:SUBLANES - 2 + R, :]
    yb = bg * (cw_ref[0:1, cols] * z2 + cw_ref[1:2, cols] * z1 + cw_ref[2:3, cols] * z)
    zc_ref[:, cols] = zs_ref[valid:valid + SUBLANES, :]

    sa_ref[0:R, :] = _sigmoid(sec(7)).astype(BF16)
    mb_ref[0:R, :] = (_sigmoid(sec(8)) * yb).astype(BF16)


def _finish_chunk(row0, h, lanes, o, hnw, mg_ref, slot):
    gs_ref, sa_ref, mb_ref = slot[4:7]
    rows = pl.ds(row0, CHUNK)
    m = (_rms(o, hnw) * gs_ref[rows, lanes].astype(F32) * sa_ref[rows, lanes].astype(F32)
         + mb_ref[rows, lanes].astype(F32))
    mg_ref[rows, h * HEAD_DIM:(h + 1) * HEAD_DIM] = m.astype(BF16)


def _recur_group_fast(g, R, hnw, st_ref, st_old_ref, mg_ref, slot):
    q_ref, g_ref, k_ref, v_ref = slot[0:4]
    for hh in range(GROUP_HEADS):
        h = GROUP_HEADS * g + hh
        lanes = slice(hh * HEAD_DIM, (hh + 1) * HEAD_DIM)
        st = st_ref[h]
        st_old_ref[h] = st
        for c in range(R // CHUNK):
            o, st = _hgrn_chunk_fast(c * CHUNK, lanes, st, q_ref, g_ref, k_ref, v_ref)
            mg_ref[c * CHUNK:(c + 1) * CHUNK, h * HEAD_DIM:(h + 1) * HEAD_DIM] = _rms(o, hnw).astype(BF16)
        st_ref[h] = st


def _finish_group_fast(g, R, mg_ref, slot):
    gs_ref, sa_ref, mb_ref = slot[4:7]
    cols = slice(g * GROUP_WIDTH, (g + 1) * GROUP_WIDTH)
    mg_ref[0:R, cols] = mg_ref[0:R, cols] * gs_ref[0:R, :] * sa_ref[0:R, :] + mb_ref[0:R, :]


def _recur_group_exact(g, R, valid, hnw, rsel, st_in_ref, st_ref, mg_ref, slot):
    q_ref, g_ref, k_ref, v_ref = slot[0:4]
    for hh in range(GROUP_HEADS):
        h = GROUP_HEADS * g + hh
        lanes = slice(hh * HEAD_DIM, (hh + 1) * HEAD_DIM)
        st = st_in_ref[h]
        if valid < R:
            o, st = _hgrn_chunk_exact(0, lanes, valid, st, q_ref, g_ref, k_ref, v_ref, rsel)
            _finish_chunk(0, h, lanes, o, hnw, mg_ref, slot)
        else:
            def chunk_body(c, st):
                row0 = pl.multiple_of(c * CHUNK, CHUNK)
                o, st = _hgrn_chunk_exact(row0, lanes, CHUNK, st, q_ref, g_ref, k_ref, v_ref, rsel)
                _finish_chunk(row0, h, lanes, o, hnw, mg_ref, slot)
                return st

            st = lax.fori_loop(0, R // CHUNK, chunk_body, st)
        st_ref[h] = st


def _out_partial(g, R, mg_ref, wout_ref):
    rows = slice(g * GROUP_WIDTH, (g + 1) * GROUP_WIDTH)
    return jnp.dot(mg_ref[0:R, rows], wout_ref[rows, 0:D_MODEL], preferred_element_type=F32)


def _mixer_tile(load_x, store_out, R, valid, is_meta, refs):
    (lbp_ref, anw_ref, win_ref, hnw_ref, cw_ref, wout_ref, rsel_ref,
     st_ref, st_old_ref, zc_ref, lb_ref, mg_ref, safe_ref, slots) = refs
    prm = (lb_ref, win_ref, cw_ref, zc_ref, safe_ref)
    u = _rms(load_x(), anw_ref[...]).astype(BF16)
    rsel = rsel_ref[...]
    hnw = hnw_ref[...]
    project = lambda g, first: _project_group(u, g, R, valid, prm, slots[g], first)

    lbp = lbp_ref[...]
    mx = jnp.max(lbp, axis=0, keepdims=True)
    ex = jnp.exp(lbp - mx)
    lb = ex[0:1, :] / jnp.sum(ex, axis=0, keepdims=True)
    lb_ref[0:1, :] = lb
    lb_ref[1:2, :] = 1.0 - lb

    if is_meta:
        for g in range(N_GROUPS):
            project(g, True)
            project(g, False)
            _recur_group_exact(g, R, valid, hnw, rsel, st_ref, st_ref, mg_ref, slots[g])
        acc = load_x()
        for g in range(N_GROUPS):
            acc = acc + _out_partial(g, R, mg_ref, wout_ref)
        store_out(acc, False)
        return

    def store_early():
        acc = load_x()
        for g in range(N_GROUPS - 1):
            acc = acc + _out_partial(g, R, mg_ref, wout_ref)
        store_out(acc, False)

    for g in range(N_GROUPS):
        project(g, True)
        if g > 0:
            _finish_group_fast(g - 1, R, mg_ref, slots[g - 1])
            if g == N_GROUPS - 1:
                store_early()
        _recur_group_fast(g, R, hnw, st_ref, st_old_ref, mg_ref, slots[g])
        project(g, False)
    _finish_group_fast(N_GROUPS - 1, R, mg_ref, slots[N_GROUPS - 1])
    for g in range(N_GROUPS):
        def redo(g=g):
            _recur_group_exact(g, R, valid, hnw, rsel, st_old_ref, st_ref, mg_ref, slots[g])
            if g < N_GROUPS - 1:
                store_early()

        pl.when(safe_ref[g] == 0)(redo)
    store_out(_out_partial(N_GROUPS - 1, R, mg_ref, wout_ref), True)


def _mixer_kernel(x_ref, meta_ref, lbp_ref, anw_ref, win_ref, hnw_ref, cw_ref, wout_ref, rsel_ref,
                  wup_f32_ref, wdn_f32_ref, h1_ref, h1m_ref, wup_ref, wdn_ref,
                  st_ref, st_old_ref, zc_ref, lb_ref, mg_ref, safe_ref,
                  st_meta_ref, zc_meta_ref, h1m_meta_ref, *slots):
    wup_ref[...] = wup_f32_ref[...].astype(BF16)
    wdn_ref[:, 0:D_MODEL] = wdn_f32_ref[...].astype(BF16)
    wdn_ref[:, D_MODEL:] = jnp.zeros((W_DOWN_ROWS_PER_STEP, WEIGHT_LANE_PAD), BF16)

    n = len(slots) // N_GROUPS
    refs = (lbp_ref, anw_ref, win_ref, hnw_ref, cw_ref, wout_ref, rsel_ref,
            st_ref, st_old_ref, zc_ref, lb_ref, mg_ref, safe_ref,
            [slots[i * n:(i + 1) * n] for i in range(N_GROUPS)])

    first_tile = pl.program_id(1) == 0

    @pl.when(jnp.logical_and(first_tile, pl.program_id(0) == 0))
    def _():
        st_ref[...] = jnp.zeros_like(st_ref)
        zc_ref[...] = jnp.zeros_like(zc_ref)
        load_meta = lambda: jnp.concatenate(
            [meta_ref[...], jnp.zeros((CHUNK - N_META, D_MODEL), F32)], axis=0)

        def store_meta(val, accumulate):
            h1m_meta_ref[...] = val[0:N_META, :]

        _mixer_tile(load_meta, store_meta, CHUNK, N_META, True, refs)
        st_meta_ref[...] = st_ref[...]
        zc_meta_ref[...] = zc_ref[...]

    @pl.when(first_tile)
    def _():
        st_ref[...] = st_meta_ref[...]
        zc_ref[...] = zc_meta_ref[...]
        h1m_ref[0] = h1m_meta_ref[...]

    def store_tile(val, accumulate):
        h1_ref[0] = h1_ref[0] + val if accumulate else val

    _mixer_tile(lambda: x_ref[0], store_tile, MIXER_TILE, MIXER_TILE, False, refs)


def _ffn_kernel(h1_ref, h1m_ref, fnw_ref, wup_ref, fcw_ref, fcb_ref, wdn_ref, finw_ref,
                out_ref, as_ref):
    T = FFN_TILE
    fnw = fnw_ref[...]

    @pl.when(pl.program_id(1) == 0)
    def _():
        um = _rms(h1m_ref[0], fnw).astype(BF16)
        am = jnp.dot(um, wup_ref[:, 0:D_FF], preferred_element_type=F32)
        as_ref[0:SUBLANES, :] = am[N_META - SUBLANES:N_META, :]

    x = h1_ref[0]
    u = _rms(x, fnw).astype(BF16)
    y = x
    for cb in range(D_FF // FFN_COL_BLOCK):
        cols = slice(cb * FFN_COL_BLOCK, (cb + 1) * FFN_COL_BLOCK)
        a = jnp.dot(u, wup_ref[:, cols], preferred_element_type=F32)
        as_ref[SUBLANES:SUBLANES + T, cols] = a
        a1 = as_ref[SUBLANES - 1:SUBLANES - 1 + T, cols]
        a2 = as_ref[SUBLANES - 2:SUBLANES - 2 + T, cols]
        ac = (fcw_ref[0:1, cols] * a2 + fcw_ref[1:2, cols] * a1 + fcw_ref[2:3, cols] * a
              + fcb_ref[:, cols])
        as_ref[0:SUBLANES, cols] = as_ref[T:T + SUBLANES, cols]
        val = jnp.dot(u, wup_ref[:, D_FF + cb * FFN_COL_BLOCK:D_FF + (cb + 1) * FFN_COL_BLOCK],
                      preferred_element_type=F32)
        gated = (ac * _sigmoid(ac) * val).astype(BF16)
        y = y + jnp.dot(gated, wdn_ref[cols, 0:D_MODEL], preferred_element_type=F32)
    out_ref[0] = _rms(y, finw_ref[...])


def _resident(shape):
    return pl.BlockSpec(shape, lambda b, t: (0,) * len(shape), pipeline_mode=pl.Buffered(1))


def _weight(w):
    return jnp.pad(w.astype(BF16), ((0, 0), (0, WEIGHT_LANE_PAD)))


def _rsel_matrix():
    r = np.arange(N_HEADS * HEAD_DIM)[:, None] // HEAD_DIM
    l = np.arange(CHUNK)[None, :] % SUBLANES
    return jnp.asarray(r == l, dtype=BF16)


def kernel(x, meta_tokens, lb_param, attn_norm_w, w_in, hgrn_norm_w, conv_w, w_out, ffn_norm_w, w_up,
           ffn_conv_w, ffn_conv_b, w_down, final_norm_w):
    B, L, D = x.shape
    assert D == D_MODEL and L % MIXER_TILE == 0 and L % FFN_TILE == 0
    assert w_in.shape == (1, D, N_IN_SECTIONS * D) and w_up.shape == (1, D, 2 * D_FF)
    assert SUBLANES * HEAD_DIM == D_MODEL and N_HEADS == SUBLANES

    tile = lambda T: pl.BlockSpec((1, T, D), lambda b, t: (b, t, 0))
    meta_tile = pl.BlockSpec((1, N_META, D), lambda b, t: (b, 0, 0))
    params = pltpu.CompilerParams(dimension_semantics=("arbitrary", "arbitrary"),
                                  vmem_limit_bytes=VMEM_LIMIT_BYTES)

    TM = MIXER_TILE
    n_steps = B * (L // TM)
    step = lambda b, t: b * (L // TM) + t
    assert D % n_steps == 0 and D_FF % W_DOWN_ROWS_PER_STEP == 0 and D_FF // W_DOWN_ROWS_PER_STEP <= n_steps
    wup_rows = D // n_steps
    wup_block = pl.BlockSpec((wup_rows, 2 * D_FF), lambda b, t: (step(b, t), 0))
    wdn_index = lambda b, t: (jnp.minimum(step(b, t), D_FF // W_DOWN_ROWS_PER_STEP - 1), 0)
    slot = [pltpu.VMEM((TM, GROUP_WIDTH), F32)] * 3 + [pltpu.VMEM((TM, GROUP_WIDTH), BF16)] * 4 \
        + [pltpu.VMEM((TM + SUBLANES, GROUP_WIDTH), F32)]
    h1, h1m, w_up_bf16, w_down_bf16 = pl.pallas_call(
        _mixer_kernel,
        grid=(B, L // TM),
        in_specs=[tile(TM), _resident((N_META, D)), _resident((2, D)), _resident((1, D)),
                  _resident((D, N_IN_SECTIONS * D + WEIGHT_LANE_PAD)), _resident((1, HEAD_DIM)),
                  _resident((3, D)), _resident((D, D + WEIGHT_LANE_PAD)),
                  _resident((N_HEADS * HEAD_DIM, CHUNK)), wup_block,
                  pl.BlockSpec((W_DOWN_ROWS_PER_STEP, D), wdn_index)],
        out_specs=[tile(TM), meta_tile, wup_block,
                   pl.BlockSpec((W_DOWN_ROWS_PER_STEP, D + WEIGHT_LANE_PAD), wdn_index)],
        out_shape=[jax.ShapeDtypeStruct((B, L, D), F32), jax.ShapeDtypeStruct((B, N_META, D), F32),
                   jax.ShapeDtypeStruct((D, 2 * D_FF), BF16),
                   jax.ShapeDtypeStruct((D_FF, D + WEIGHT_LANE_PAD), BF16)],
        scratch_shapes=[pltpu.VMEM((N_HEADS, HEAD_DIM, HEAD_DIM), F32),
                        pltpu.VMEM((N_HEADS, HEAD_DIM, HEAD_DIM), F32),
                        pltpu.VMEM((SUBLANES, D), F32),
                        pltpu.VMEM((SUBLANES, D), F32),
                        pltpu.VMEM((TM, D), BF16),
                        pltpu.SMEM((N_GROUPS,), jnp.int32),
                        pltpu.VMEM((N_HEADS, HEAD_DIM, HEAD_DIM), F32),
                        pltpu.VMEM((SUBLANES, D), F32),
                        pltpu.VMEM((N_META, D), F32),
                        *(slot * N_GROUPS)],
        compiler_params=params,
        name="mixer",
    )(x, meta_tokens, lb_param, attn_norm_w, _weight(w_in[0]), hgrn_norm_w, conv_w[0],
      _weight(w_out[0]), _rsel_matrix(), w_up[0], w_down[0])

    TF = FFN_TILE
    out = pl.pallas_call(
        _ffn_kernel,
        grid=(B, L // TF),
        in_specs=[tile(TF), meta_tile, _resident((1, D)), _resident((D, 2 * D_FF)), _resident((3, D_FF)),
                  _resident((1, D_FF)), _resident((D_FF, D + WEIGHT_LANE_PAD)), _resident((1, D))],
        out_specs=tile(TF),
        out_shape=jax.ShapeDtypeStruct((B, L, D), F32),
        scratch_shapes=[pltpu.VMEM((TF + SUBLANES, D_FF), F32)],
        compiler_params=params,
        name="ffn",
    )(h1, h1m, ffn_norm_w, w_up_bf16, ffn_conv_w[0], ffn_conv_b, w_down_bf16, final_norm_w.reshape(1, D))
    return out
```

```python
import numpy as np
import jax
import jax.numpy as jnp
from jax import lax
from jax.experimental import pallas as pl
from jax.experimental.pallas import tpu as pltpu

D_MODEL = 1024
N_META = 16
N_HEADS = 8
HEAD_DIM = 128
D_FF = 2816
N_IN_SECTIONS = 9
EPS = 1e-6

SUBLANES = 8
CHUNK = 128
GROUP_HEADS = 4
GROUP_WIDTH = GROUP_HEADS * HEAD_DIM
N_GROUPS = N_HEADS // GROUP_HEADS
MAX_SAFE_EXPONENT = 60.0
LOG2_E = 1.4426950408889634
MIXER_TILE = 512
FFN_TILE = 512
WEIGHT_LANE_PAD = 128
W_DOWN_ROWS_PER_STEP = 128
V7X_VMEM_BYTES = 64 * 1024 * 1024
VMEM_LIMIT_BYTES = V7X_VMEM_BYTES - 4 * 1024 * 1024

F32 = jnp.float32
BF16 = jnp.bfloat16

_NT = (((1,), (1,)), ((), ()))
_TN = (((0,), (0,)), ((), ()))


def _rms(x, w):
    ms = jnp.mean(x * x, axis=-1, keepdims=True)
    return x * lax.rsqrt(ms + EPS) * w


def _sigmoid(x):
    return 1.0 / (1.0 + jnp.exp2(x * -LOG2_E))


def _bcast_rows(ref, lanes, row0, block, offset, nrows):
    pieces = [jnp.broadcast_to(_row(ref, lanes, row0, i * block + offset), (block, HEAD_DIM))
              for i in range(nrows // block)]
    return pieces[0] if len(pieces) == 1 else jnp.concatenate(pieces, axis=0)


def _row(ref, lanes, row0, r):
    group = ref[pl.ds(row0 + r // SUBLANES * SUBLANES, SUBLANES), lanes]
    return group[r % SUBLANES:r % SUBLANES + 1, :]


def _hgrn_chunk_exact(row0, lanes, valid, st, q_ref, g_ref, k_ref, v_ref, rsel):
    rows = pl.ds(row0, CHUNK)
    q = q_ref[rows, lanes]
    G = g_ref[rows, lanes]
    k = k_ref[rows, lanes]
    v = v_ref[rows, lanes]
    row = lax.broadcasted_iota(jnp.int32, (CHUNK, HEAD_DIM), 0)

    sub = row & (SUBLANES - 1)
    ps = []
    for j in range(SUBLANES):
        gj = _bcast_rows(g_ref, lanes, row0, SUBLANES, j, CHUNK)
        kj = _bcast_rows(k_ref, lanes, row0, SUBLANES, j, CHUNK)
        p = q * kj * jnp.exp2(G - gj)
        ps.append(jnp.where(sub >= j, p, 0.0).astype(BF16))
    pcat = jnp.concatenate(ps, axis=1)
    a = jnp.dot(pcat, rsel, preferred_element_type=F32)

    tl_xor = (lax.broadcasted_iota(jnp.int32, (CHUNK, CHUNK), 0)
              ^ lax.broadcasted_iota(jnp.int32, (CHUNK, CHUNK), 1))
    a = jnp.where(tl_xor < SUBLANES, a, 0.0)

    b = 2 * SUBLANES
    while b <= CHUNK:
        gm = _bcast_rows(g_ref, lanes, row0, b, b // 2 - 1, CHUNK)
        e = jnp.exp2(-jnp.abs(G - gm))
        upper = (row & (b // 2)) != 0
        qt = jnp.where(upper, q * e, 0.0).astype(BF16)
        kt = jnp.where(upper, 0.0, k * e).astype(BF16)
        ab = lax.dot_general(qt, kt, _NT, preferred_element_type=F32)
        a = a + (ab if b == CHUNK else jnp.where(tl_xor < b, ab, 0.0))
        b *= 2

    qi = (q * jnp.exp2(G)).astype(BF16)
    o = lax.dot_general(qi, st.astype(BF16), _NT, preferred_element_type=F32)
    o = o + jnp.dot(a.astype(BF16), v, preferred_element_type=F32)

    glast = _row(g_ref, lanes, row0, valid - 1)
    kd = k * jnp.exp2(glast - G)
    if valid < CHUNK:
        kd = jnp.where(row < valid, kd, 0.0)
    st = st * jnp.exp2(glast) + lax.dot_general(v, kd.astype(BF16), _TN, preferred_element_type=F32)
    return o, st


def _hgrn_chunk_fast(row0, lanes, st, q_ref, g_ref, k_ref, v_ref):
    rows = pl.ds(row0, CHUNK)
    q = q_ref[rows, lanes]
    G = g_ref[rows, lanes]
    k = k_ref[rows, lanes]
    v = v_ref[rows, lanes]
    gmid = g_ref[pl.ds(row0 + CHUNK // 2 - 1, 1), lanes]
    glast = g_ref[pl.ds(row0 + CHUNK - 1, 1), lanes]
    d = G - gmid
    qh = q * jnp.exp2(d)
    kh = k * jnp.exp2(-d)
    a = lax.dot_general(qh.astype(BF16), kh.astype(BF16), _NT, preferred_element_type=F32)
    causal = (lax.broadcasted_iota(jnp.int32, (CHUNK, CHUNK), 1)
              <= lax.broadcasted_iota(jnp.int32, (CHUNK, CHUNK), 0))
    a = jnp.where(causal, a, 0.0)
    qi = (qh * jnp.exp2(gmid)).astype(BF16)
    o = lax.dot_general(qi, st.astype(BF16), _NT, preferred_element_type=F32)
    o = o + jnp.dot(a.astype(BF16), v, preferred_element_type=F32)
    kd = (kh * jnp.exp2(glast - gmid)).astype(BF16)
    st = st * jnp.exp2(glast) + lax.dot_general(v, kd, _TN, preferred_element_type=F32)
    return o, st


def _project_group(u, g, R, valid, prm, slot, recurrence_inputs):
    lb_ref, win_ref, cw_ref, zc_ref, safe_ref = prm
    q_ref, g_ref, k_ref, v_ref, gs_ref, sa_ref, mb_ref, zs_ref = slot
    c0 = g * GROUP_WIDTH
    cols = slice(c0, c0 + GROUP_WIDTH)

    def sec(i):
        w = win_ref[:, i * D_MODEL + c0:i * D_MODEL + c0 + GROUP_WIDTH]
        return jnp.dot(u, w, preferred_element_type=F32)

    if not recurrence_inputs:
        _project_gates(sec, cols, R, valid, cw_ref, zc_ref, slot)
        return

    qv = sec(0)
    q_ref[0:R, :] = qv * _sigmoid(qv)

    f = lb_ref[0:1, cols] + lb_ref[1:2, cols] * _sigmoid(sec(1))
    k_ref[0:R, :] = 1.0 - f
    lf = jnp.log2(f)
    tri = (lax.broadcasted_iota(jnp.int32, (CHUNK, CHUNK), 1)
           <= lax.broadcasted_iota(jnp.int32, (CHUNK, CHUNK), 0)).astype(BF16)
    tri2 = jnp.concatenate([tri, tri], axis=1)
    worst = jnp.zeros((1, GROUP_WIDTH), F32)
    for c in range(R // CHUNK):
        lfc = lf[c * CHUNK:(c + 1) * CHUNK, :]
        hi = lfc.astype(BF16)
        lo = (lfc - hi.astype(F32)).astype(BF16)
        gc = jnp.dot(tri2, jnp.concatenate([hi, lo], axis=0),
                     preferred_element_type=F32)
        g_ref[c * CHUNK:(c + 1) * CHUNK, :] = gc
        gmid = gc[CHUNK // 2 - 1:CHUNK // 2, :]
        glast = gc[CHUNK - 1:CHUNK, :]
        worst = jnp.maximum(worst, jnp.maximum(-gmid, gmid - glast))
    safe_ref[g] = (jnp.max(worst) <= MAX_SAFE_EXPONENT * LOG2_E).astype(jnp.int32)

    v_ref[0:R, :] = sec(2).astype(BF16)


def _project_gates(sec, cols, R, valid, cw_ref, zc_ref, slot):
    gs_ref, sa_ref, mb_ref, zs_ref = slot[4:8]
    gv = sec(3)
    gs_ref[0:R, :] = (gv * _sigmoid(gv)).astype(BF16)

    bg = sec(4)
    z = sec(5) * sec(6)
    zs_ref[0:SUBLANES, :] = zc_ref[:, cols]
    zs_ref[SUBLANES:SUBLANES + R, :] = z
    z1 = zs_ref[SUBLANES - 1:SUBLANES - 1 + R, :]
    z2 = zs_ref[SUBLANES - 2:SUBLANES - 2 + R, :]
    yb = bg * (cw_ref[0:1, cols] * z2 + cw_ref[1:2, cols] * z1 + cw_ref[2:3, cols] * z)
    zc_ref[:, cols] = zs_ref[valid:valid + SUBLANES, :]

    sa_ref[0:R, :] = _sigmoid(sec(7)).astype(BF16)
    mb_ref[0:R, :] = (_sigmoid(sec(8)) * yb).astype(BF16)


def _finish_chunk(row0, h, lanes, o, hnw, mg_ref, slot):
    gs_ref, sa_ref, mb_ref = slot[4:7]
    rows = pl.ds(row0, CHUNK)
    m = (_rms(o, hnw) * gs_ref[rows, lanes].astype(F32) * sa_ref[rows, lanes].astype(F32)
         + mb_ref[rows, lanes].astype(F32))
    mg_ref[rows, h * HEAD_DIM:(h + 1) * HEAD_DIM] = m.astype(BF16)


def _recur_group_fast(g, R, hnw, st_ref, st_old_ref, mg_ref, slot):
    q_ref, g_ref, k_ref, v_ref = slot[0:4]
    for hh in range(GROUP_HEADS):
        h = GROUP_HEADS * g + hh
        lanes = slice(hh * HEAD_DIM, (hh + 1) * HEAD_DIM)
        st = st_ref[h]
        st_old_ref[h] = st
        for c in range(R // CHUNK):
            o, st = _hgrn_chunk_fast(c * CHUNK, lanes, st, q_ref, g_ref, k_ref, v_ref)
            mg_ref[c * CHUNK:(c + 1) * CHUNK, h * HEAD_DIM:(h + 1) * HEAD_DIM] = _rms(o, hnw).astype(BF16)
        st_ref[h] = st


def _finish_group_fast(g, R, mg_ref, slot):
    gs_ref, sa_ref, mb_ref = slot[4:7]
    cols = slice(g * GROUP_WIDTH, (g + 1) * GROUP_WIDTH)
    mg_ref[0:R, cols] = mg_ref[0:R, cols] * gs_ref[0:R, :] * sa_ref[0:R, :] + mb_ref[0:R, :]


def _recur_group_exact(g, R, valid, hnw, rsel, st_in_ref, st_ref, mg_ref, slot):
    q_ref, g_ref, k_ref, v_ref = slot[0:4]
    for hh in range(GROUP_HEADS):
        h = GROUP_HEADS * g + hh
        lanes = slice(hh * HEAD_DIM, (hh + 1) * HEAD_DIM)
        st = st_in_ref[h]
        if valid < R:
            o, st = _hgrn_chunk_exact(0, lanes, valid, st, q_ref, g_ref, k_ref, v_ref, rsel)
            _finish_chunk(0, h, lanes, o, hnw, mg_ref, slot)
        else:
            def chunk_body(c, st):
                row0 = pl.multiple_of(c * CHUNK, CHUNK)
                o, st = _hgrn_chunk_exact(row0, lanes, CHUNK, st, q_ref, g_ref, k_ref, v_ref, rsel)
                _finish_chunk(row0, h, lanes, o, hnw, mg_ref, slot)
                return st

            st = lax.fori_loop(0, R // CHUNK, chunk_body, st)
        st_ref[h] = st


def _out_partial(g, R, mg_ref, wout_ref):
    rows = slice(g * GROUP_WIDTH, (g + 1) * GROUP_WIDTH)
    return jnp.dot(mg_ref[0:R, rows], wout_ref[rows, 0:D_MODEL], preferred_element_type=F32)


def _mixer_tile(load_x, store_out, R, valid, is_meta, refs):
    (lbp_ref, anw_ref, win_ref, hnw_ref, cw_ref, wout_ref, rsel_ref,
     st_ref, st_old_ref, zc_ref, lb_ref, mg_ref, safe_ref, slots) = refs
    prm = (lb_ref, win_ref, cw_ref, zc_ref, safe_ref)
    u = _rms(load_x(), anw_ref[...]).astype(BF16)
    rsel = rsel_ref[...]
    hnw = hnw_ref[...]
    project = lambda g, first: _project_group(u, g, R, valid, prm, slots[g], first)

    lbp = lbp_ref[...]
    mx = jnp.max(lbp, axis=0, keepdims=True)
    ex = jnp.exp(lbp - mx)
    lb = ex[0:1, :] / jnp.sum(ex, axis=0, keepdims=True)
    lb_ref[0:1, :] = lb
    lb_ref[1:2, :] = 1.0 - lb

    if is_meta:
        for g in range(N_GROUPS):
            project(g, True)
            project(g, False)
            _recur_group_exact(g, R, valid, hnw, rsel, st_ref, st_ref, mg_ref, slots[g])
        acc = load_x()
        for g in range(N_GROUPS):
            acc = acc + _out_partial(g, R, mg_ref, wout_ref)
        store_out(acc, False)
        return

    def store_early():
        acc = load_x()
        for g in range(N_GROUPS - 1):
            acc = acc + _out_partial(g, R, mg_ref, wout_ref)
        store_out(acc, False)

    for g in range(N_GROUPS):
        project(g, True)
        if g > 0:
            _finish_group_fast(g - 1, R, mg_ref, slots[g - 1])
            if g == N_GROUPS - 1:
                store_early()
        _recur_group_fast(g, R, hnw, st_ref, st_old_ref, mg_ref, slots[g])
        project(g, False)
    _finish_group_fast(N_GROUPS - 1, R, mg_ref, slots[N_GROUPS - 1])
    for g in range(N_GROUPS):
        def redo(g=g):
            _recur_group_exact(g, R, valid, hnw, rsel, st_old_ref, st_ref, mg_ref, slots[g])
            if g < N_GROUPS - 1:
                store_early()

        pl.when(safe_ref[g] == 0)(redo)
    store_out(_out_partial(N_GROUPS - 1, R, mg_ref, wout_ref), True)


def _mixer_kernel(x_ref, meta_ref, lbp_ref, anw_ref, win_ref, hnw_ref, cw_ref, wout_ref, rsel_ref,
                  wup_f32_ref, wdn_f32_ref, h1_ref, h1m_ref, wup_ref, wdn_ref,
                  st_ref, st_old_ref, zc_ref, lb_ref, mg_ref, safe_ref,
                  st_meta_ref, zc_meta_ref, h1m_meta_ref, *slots):
    wup_ref[...] = wup_f32_ref[...].astype(BF16)
    wdn_ref[:, 0:D_MODEL] = wdn_f32_ref[...].astype(BF16)
    wdn_ref[:, D_MODEL:] = jnp.zeros((W_DOWN_ROWS_PER_STEP, WEIGHT_LANE_PAD), BF16)

    n = len(slots) // N_GROUPS
    refs = (lbp_ref, anw_ref, win_ref, hnw_ref, cw_ref, wout_ref, rsel_ref,
            st_ref, st_old_ref, zc_ref, lb_ref, mg_ref, safe_ref,
            [slots[i * n:(i + 1) * n] for i in range(N_GROUPS)])

    first_tile = pl.program_id(1) == 0

    @pl.when(jnp.logical_and(first_tile, pl.program_id(0) == 0))
    def _():
        st_ref[...] = jnp.zeros_like(st_ref)
        zc_ref[...] = jnp.zeros_like(zc_ref)
        load_meta = lambda: jnp.concatenate(
            [meta_ref[...], jnp.zeros((CHUNK - N_META, D_MODEL), F32)], axis=0)

        def store_meta(val, accumulate):
            h1m_meta_ref[...] = val[0:N_META, :]

        _mixer_tile(load_meta, store_meta, CHUNK, N_META, True, refs)
        st_meta_ref[...] = st_ref[...]
        zc_meta_ref[...] = zc_ref[...]

    @pl.when(first_tile)
    def _():
        st_ref[...] = st_meta_ref[...]
        zc_ref[...] = zc_meta_ref[...]
        h1m_ref[0] = h1m_meta_ref[...]

    def store_tile(val, accumulate):
        h1_ref[0] = h1_ref[0] + val if accumulate else val

    _mixer_tile(lambda: x_ref[0], store_tile, MIXER_TILE, MIXER_TILE, False, refs)


def _ffn_kernel(h1_ref, h1m_ref, fnw_ref, wup_ref, fcw_ref, fcb_ref, wdn_ref, finw_ref,
                out_ref, as_ref):
    T = FFN_TILE
    fnw = fnw_ref[...]

    @pl.when(pl.program_id(1) == 0)
    def _():
        um = _rms(h1m_ref[0], fnw).astype(BF16)
        am = jnp.dot(um, wup_ref[:, 0:D_FF], preferred_element_type=F32)
        as_ref[0:SUBLANES, :] = am[N_META - SUBLANES:N_META, :]

    x = h1_ref[0]
    u = _rms(x, fnw).astype(BF16)
    up = jnp.dot(u, wup_ref[...], preferred_element_type=F32)
    a = up[:, 0:D_FF]
    as_ref[SUBLANES:SUBLANES + T, :] = a
    a1 = as_ref[SUBLANES - 1:SUBLANES - 1 + T, :]
    a2 = as_ref[SUBLANES - 2:SUBLANES - 2 + T, :]
    ac = fcw_ref[0:1, :] * a2 + fcw_ref[1:2, :] * a1 + fcw_ref[2:3, :] * a + fcb_ref[...]
    as_ref[0:SUBLANES, :] = as_ref[T:T + SUBLANES, :]
    gated = (ac * _sigmoid(ac) * up[:, D_FF:]).astype(BF16)
    y = x + jnp.dot(gated, wdn_ref[:, 0:D_MODEL], preferred_element_type=F32)
    out_ref[0] = _rms(y, finw_ref[...])


def _resident(shape):
    return pl.BlockSpec(shape, lambda b, t: (0,) * len(shape), pipeline_mode=pl.Buffered(1))


def _weight(w):
    return jnp.pad(w.astype(BF16), ((0, 0), (0, WEIGHT_LANE_PAD)))


def _rsel_matrix():
    r = np.arange(N_HEADS * HEAD_DIM)[:, None] // HEAD_DIM
    l = np.arange(CHUNK)[None, :] % SUBLANES
    return jnp.asarray(r == l, dtype=BF16)


def kernel(x, meta_tokens, lb_param, attn_norm_w, w_in, hgrn_norm_w, conv_w, w_out, ffn_norm_w, w_up,
           ffn_conv_w, ffn_conv_b, w_down, final_norm_w):
    B, L, D = x.shape
    assert D == D_MODEL and L % MIXER_TILE == 0 and L % FFN_TILE == 0
    assert w_in.shape == (1, D, N_IN_SECTIONS * D) and w_up.shape == (1, D, 2 * D_FF)
    assert SUBLANES * HEAD_DIM == D_MODEL and N_HEADS == SUBLANES

    tile = lambda T: pl.BlockSpec((1, T, D), lambda b, t: (b, t, 0))
    meta_tile = pl.BlockSpec((1, N_META, D), lambda b, t: (b, 0, 0))
    params = pltpu.CompilerParams(dimension_semantics=("arbitrary", "arbitrary"),
                                  vmem_limit_bytes=VMEM_LIMIT_BYTES)

    TM = MIXER_TILE
    n_steps = B * (L // TM)
    step = lambda b, t: b * (L // TM) + t
    assert D % n_steps == 0 and D_FF % W_DOWN_ROWS_PER_STEP == 0 and D_FF // W_DOWN_ROWS_PER_STEP <= n_steps
    wup_rows = D // n_steps
    wup_block = pl.BlockSpec((wup_rows, 2 * D_FF), lambda b, t: (step(b, t), 0))
    wdn_index = lambda b, t: (jnp.minimum(step(b, t), D_FF // W_DOWN_ROWS_PER_STEP - 1), 0)
    slot = [pltpu.VMEM((TM, GROUP_WIDTH), F32)] * 3 + [pltpu.VMEM((TM, GROUP_WIDTH), BF16)] * 4 \
        + [pltpu.VMEM((TM + SUBLANES, GROUP_WIDTH), F32)]
    h1, h1m, w_up_bf16, w_down_bf16 = pl.pallas_call(
        _mixer_kernel,
        grid=(B, L // TM),
        in_specs=[tile(TM), _resident((N_META, D)), _resident((2, D)), _resident((1, D)),
                  _resident((D, N_IN_SECTIONS * D + WEIGHT_LANE_PAD)), _resident((1, HEAD_DIM)),
                  _resident((3, D)), _resident((D, D + WEIGHT_LANE_PAD)),
                  _resident((N_HEADS * HEAD_DIM, CHUNK)), wup_block,
                  pl.BlockSpec((W_DOWN_ROWS_PER_STEP, D), wdn_index)],
        out_specs=[tile(TM), meta_tile, wup_block,
                   pl.BlockSpec((W_DOWN_ROWS_PER_STEP, D + WEIGHT_LANE_PAD), wdn_index)],
        out_shape=[jax.ShapeDtypeStruct((B, L, D), F32), jax.ShapeDtypeStruct((B, N_META, D), F32),
                   jax.ShapeDtypeStruct((D, 2 * D_FF), BF16),
                   jax.ShapeDtypeStruct((D_FF, D + WEIGHT_LANE_PAD), BF16)],
        scratch_shapes=[pltpu.VMEM((N_HEADS, HEAD_DIM, HEAD_DIM), F32),
                        pltpu.VMEM((N_HEADS, HEAD_DIM, HEAD_DIM), F32),
                        pltpu.VMEM((SUBLANES, D), F32),
                        pltpu.VMEM((SUBLANES, D), F32),
                        pltpu.VMEM((TM, D), BF16),
                        pltpu.SMEM((N_GROUPS,), jnp.int32),
                        pltpu.VMEM((N_HEADS, HEAD_DIM, HEAD_DIM), F32),
                        pltpu.VMEM((SUBLANES, D), F32),
                        pltpu.VMEM((N_META, D), F32),
                        *(slot * N_GROUPS)],
        compiler_params=params,
        name="mixer",
    )(x, meta_tokens, lb_param, attn_norm_w, _weight(w_in[0]), hgrn_norm_w, conv_w[0],
      _weight(w_out[0]), _rsel_matrix(), w_up[0], w_down[0])

    TF = FFN_TILE
    out = pl.pallas_call(
        _ffn_kernel,
        grid=(B, L // TF),
        in_specs=[tile(TF), meta_tile, _resident((1, D)), _resident((D, 2 * D_FF)), _resident((3, D_FF)),
                  _resident((1, D_FF)), _resident((D_FF, D + WEIGHT_LANE_PAD)), _resident((1, D))],
        out_specs=tile(TF),
        out_shape=jax.ShapeDtypeStruct((B, L, D), F32),
        scratch_shapes=[pltpu.VMEM((TF + SUBLANES, D_FF), F32)],
        compiler_params=params,
        name="ffn",
    )(h1, h1m, ffn_norm_w, w_up_bf16, ffn_conv_w[0], ffn_conv_b, w_down_bf16, final_norm_w.reshape(1, D))
    return out
```

```python
import numpy as np
import jax
import jax.numpy as jnp
from jax import lax
from jax.experimental import pallas as pl
from jax.experimental.pallas import tpu as pltpu

D_MODEL = 1024
N_META = 16
N_HEADS = 8
HEAD_DIM = 128
D_FF = 2816
N_IN_SECTIONS = 9
EPS = 1e-6

SUBLANES = 8
CHUNK = 128
GROUP_HEADS = 4
GROUP_WIDTH = GROUP_HEADS * HEAD_DIM
N_GROUPS = N_HEADS // GROUP_HEADS
MAX_SAFE_EXPONENT = 60.0
LOG2_E = 1.4426950408889634
MIXER_TILE = 512
FFN_TILE = 512
WEIGHT_LANE_PAD = 128
W_DOWN_ROWS_PER_STEP = 128
V7X_VMEM_BYTES = 64 * 1024 * 1024
VMEM_LIMIT_BYTES = V7X_VMEM_BYTES - 4 * 1024 * 1024

F32 = jnp.float32
BF16 = jnp.bfloat16

_NT = (((1,), (1,)), ((), ()))
_TN = (((0,), (0,)), ((), ()))


def _rms(x, w):
    ms = jnp.mean(x * x, axis=-1, keepdims=True)
    return x * lax.rsqrt(ms + EPS) * w


def _sigmoid(x):
    return 1.0 / (1.0 + jnp.exp2(x * -LOG2_E))


def _bcast_rows(ref, lanes, row0, block, offset, nrows):
    pieces = [jnp.broadcast_to(_row(ref, lanes, row0, i * block + offset), (block, HEAD_DIM))
              for i in range(nrows // block)]
    return pieces[0] if len(pieces) == 1 else jnp.concatenate(pieces, axis=0)


def _row(ref, lanes, row0, r):
    group = ref[pl.ds(row0 + r // SUBLANES * SUBLANES, SUBLANES), lanes]
    return group[r % SUBLANES:r % SUBLANES + 1, :]


def _hgrn_chunk_exact(row0, lanes, valid, st, q_ref, g_ref, k_ref, v_ref, rsel):
    rows = pl.ds(row0, CHUNK)
    q = q_ref[rows, lanes]
    G = g_ref[rows, lanes]
    k = k_ref[rows, lanes]
    v = v_ref[rows, lanes]
    row = lax.broadcasted_iota(jnp.int32, (CHUNK, HEAD_DIM), 0)

    sub = row & (SUBLANES - 1)
    ps = []
    for j in range(SUBLANES):
        gj = _bcast_rows(g_ref, lanes, row0, SUBLANES, j, CHUNK)
        kj = _bcast_rows(k_ref, lanes, row0, SUBLANES, j, CHUNK)
        p = q * kj * jnp.exp2(G - gj)
        ps.append(jnp.where(sub >= j, p, 0.0).astype(BF16))
    pcat = jnp.concatenate(ps, axis=1)
    a = jnp.dot(pcat, rsel, preferred_element_type=F32)

    tl_xor = (lax.broadcasted_iota(jnp.int32, (CHUNK, CHUNK), 0)
              ^ lax.broadcasted_iota(jnp.int32, (CHUNK, CHUNK), 1))
    a = jnp.where(tl_xor < SUBLANES, a, 0.0)

    b = 2 * SUBLANES
    while b <= CHUNK:
        gm = _bcast_rows(g_ref, lanes, row0, b, b // 2 - 1, CHUNK)
        e = jnp.exp2(-jnp.abs(G - gm))
        upper = (row & (b // 2)) != 0
        qt = jnp.where(upper, q * e, 0.0).astype(BF16)
        kt = jnp.where(upper, 0.0, k * e).astype(BF16)
        ab = lax.dot_general(qt, kt, _NT, preferred_element_type=F32)
        a = a + (ab if b == CHUNK else jnp.where(tl_xor < b, ab, 0.0))
        b *= 2

    qi = (q * jnp.exp2(G)).astype(BF16)
    o = lax.dot_general(qi, st.astype(BF16), _NT, preferred_element_type=F32)
    o = o + jnp.dot(a.astype(BF16), v, preferred_element_type=F32)

    glast = _row(g_ref, lanes, row0, valid - 1)
    kd = k * jnp.exp2(glast - G)
    if valid < CHUNK:
        kd = jnp.where(row < valid, kd, 0.0)
    st = st * jnp.exp2(glast) + lax.dot_general(v, kd.astype(BF16), _TN, preferred_element_type=F32)
    return o, st


def _hgrn_chunk_fast(row0, lanes, st, q_ref, g_ref, k_ref, v_ref):
    rows = pl.ds(row0, CHUNK)
    q = q_ref[rows, lanes]
    G = g_ref[rows, lanes]
    k = k_ref[rows, lanes]
    v = v_ref[rows, lanes]
    gmid = g_ref[pl.ds(row0 + CHUNK // 2 - 1, 1), lanes]
    glast = g_ref[pl.ds(row0 + CHUNK - 1, 1), lanes]
    d = G - gmid
    qh = q * jnp.exp2(d)
    kh = k * jnp.exp2(-d)
    a = lax.dot_general(qh.astype(BF16), kh.astype(BF16), _NT, preferred_element_type=F32)
    causal = (lax.broadcasted_iota(jnp.int32, (CHUNK, CHUNK), 1)
              <= lax.broadcasted_iota(jnp.int32, (CHUNK, CHUNK), 0))
    a = jnp.where(causal, a, 0.0)
    qi = (qh * jnp.exp2(gmid)).astype(BF16)
    o = lax.dot_general(qi, st.astype(BF16), _NT, preferred_element_type=F32)
    o = o + jnp.dot(a.astype(BF16), v, preferred_element_type=F32)
    kd = (kh * jnp.exp2(glast - gmid)).astype(BF16)
    st = st * jnp.exp2(glast) + lax.dot_general(v, kd, _TN, preferred_element_type=F32)
    return o, st


def _project_group(u, g, R, valid, prm, slot, recurrence_inputs):
    lb_ref, win_ref, cw_ref, zc_ref, safe_ref = prm
    q_ref, g_ref, k_ref, v_ref, gs_ref, sa_ref, mb_ref, zs_ref = slot
    c0 = g * GROUP_WIDTH
    cols = slice(c0, c0 + GROUP_WIDTH)

    def sec(i):
        w = win_ref[:, i * D_MODEL + c0:i * D_MODEL + c0 + GROUP_WIDTH]
        return jnp.dot(u, w, preferred_element_type=F32)

    if not recurrence_inputs:
        _project_gates(sec, cols, R, valid, cw_ref, zc_ref, slot)
        return

    qv = sec(0)
    q_ref[0:R, :] = qv * _sigmoid(qv)

    f = lb_ref[0:1, cols] + lb_ref[1:2, cols] * _sigmoid(sec(1))
    k_ref[0:R, :] = 1.0 - f
    lf = jnp.log2(f)
    tri = (lax.broadcasted_iota(jnp.int32, (CHUNK, CHUNK), 1)
           <= lax.broadcasted_iota(jnp.int32, (CHUNK, CHUNK), 0)).astype(BF16)
    tri2 = jnp.concatenate([tri, tri], axis=1)
    worst = jnp.zeros((1, GROUP_WIDTH), F32)
    for c in range(R // CHUNK):
        lfc = lf[c * CHUNK:(c + 1) * CHUNK, :]
        hi = lfc.astype(BF16)
        lo = (lfc - hi.astype(F32)).astype(BF16)
        gc = jnp.dot(tri2, jnp.concatenate([hi, lo], axis=0),
                     preferred_element_type=F32)
        g_ref[c * CHUNK:(c + 1) * CHUNK, :] = gc
        gmid = gc[CHUNK // 2 - 1:CHUNK // 2, :]
        glast = gc[CHUNK - 1:CHUNK, :]
        worst = jnp.maximum(worst, jnp.maximum(-gmid, gmid - glast))
    safe_ref[g] = (jnp.max(worst) <= MAX_SAFE_EXPONENT * LOG2_E).astype(jnp.int32)

    v_ref[0:R, :] = sec(2).astype(BF16)


def _project_gates(sec, cols, R, valid, cw_ref, zc_ref, slot):
    gs_ref, sa_ref, mb_ref, zs_ref = slot[4:8]
    gv = sec(3)
    gs_ref[0:R, :] = (gv * _sigmoid(gv)).astype(BF16)

    bg = sec(4)
    z = sec(5) * sec(6)
    zs_ref[0:SUBLANES, :] = zc_ref[:, cols]
    zs_ref[SUBLANES:SUBLANES + R, :] = z
    z1 = zs_ref[SUBLANES - 1:SUBLANES - 1 + R, :]
    z2 = zs_ref[SUBLANES - 2:SUBLANES - 2 + R, :]
    yb = bg * (cw_ref[0:1, cols] * z2 + cw_ref[1:2, cols] * z1 + cw_ref[2:3, cols] * z)
    zc_ref[:, cols] = zs_ref[valid:valid + SUBLANES, :]

    sa_ref[0:R, :] = _sigmoid(sec(7)).astype(BF16)
    mb_ref[0:R, :] = (_sigmoid(sec(8)) * yb).astype(BF16)


def _recur_group_fast(g, R, hnw, st_ref, st_old_ref, mg_ref, slot):
    q_ref, g_ref, k_ref, v_ref = slot[0:4]
    for hh in range(GROUP_HEADS):
        h = GROUP_HEADS * g + hh
        lanes = slice(hh * HEAD_DIM, (hh + 1) * HEAD_DIM)
        st = st_ref[h]
        st_old_ref[h] = st
        for c in range(R // CHUNK):
            o, st = _hgrn_chunk_fast(c * CHUNK, lanes, st, q_ref, g_ref, k_ref, v_ref)
            mg_ref[c * CHUNK:(c + 1) * CHUNK, h * HEAD_DIM:(h + 1) * HEAD_DIM] = _rms(o, hnw).astype(BF16)
        st_ref[h] = st


def _finish_group(g, rows, mg_ref, slot):
    gs_ref, sa_ref, mb_ref = slot[4:7]
    cols = slice(g * GROUP_WIDTH, (g + 1) * GROUP_WIDTH)
    mg_ref[rows, cols] = mg_ref[rows, cols] * gs_ref[rows, :] * sa_ref[rows, :] + mb_ref[rows, :]


def _recur_group_exact(g, R, valid, hnw, rsel, st_in_ref, st_ref, mg_ref, slot):
    q_ref, g_ref, k_ref, v_ref = slot[0:4]
    for hh in range(GROUP_HEADS):
        h = GROUP_HEADS * g + hh
        lanes = slice(hh * HEAD_DIM, (hh + 1) * HEAD_DIM)
        st = st_in_ref[h]
        if valid < R:
            o, st = _hgrn_chunk_exact(0, lanes, valid, st, q_ref, g_ref, k_ref, v_ref, rsel)
            mg_ref[0:CHUNK, h * HEAD_DIM:(h + 1) * HEAD_DIM] = _rms(o, hnw).astype(BF16)
        else:
            def chunk_body(c, st):
                row0 = pl.multiple_of(c * CHUNK, CHUNK)
                o, st = _hgrn_chunk_exact(row0, lanes, CHUNK, st, q_ref, g_ref, k_ref, v_ref, rsel)
                mg_ref[pl.ds(row0, CHUNK), h * HEAD_DIM:(h + 1) * HEAD_DIM] = _rms(o, hnw).astype(BF16)
                return st

            st = lax.fori_loop(0, R // CHUNK, chunk_body, st)
        st_ref[h] = st


def _out_partial(g, rows, mg_ref, wout_ref):
    cols = slice(g * GROUP_WIDTH, (g + 1) * GROUP_WIDTH)
    return jnp.dot(mg_ref[rows, cols], wout_ref[cols, 0:D_MODEL], preferred_element_type=F32)


def _mixer_tile(load_x, store_out, R, valid, is_meta, refs):
    (lbp_ref, anw_ref, win_ref, hnw_ref, cw_ref, wout_ref, rsel_ref,
     st_ref, st_old_ref, zc_ref, lb_ref, mg_ref, safe_ref, slots) = refs
    prm = (lb_ref, win_ref, cw_ref, zc_ref, safe_ref)
    u = _rms(load_x(), anw_ref[...]).astype(BF16)
    rsel = rsel_ref[...]
    hnw = hnw_ref[...]
    project = lambda g, first: _project_group(u, g, R, valid, prm, slots[g], first)
    all_rows = slice(0, R)

    lbp = lbp_ref[...]
    mx = jnp.max(lbp, axis=0, keepdims=True)
    ex = jnp.exp(lbp - mx)
    lb = ex[0:1, :] / jnp.sum(ex, axis=0, keepdims=True)
    lb_ref[0:1, :] = lb
    lb_ref[1:2, :] = 1.0 - lb

    if is_meta:
        for g in range(N_GROUPS):
            project(g, True)
            project(g, False)
            _recur_group_exact(g, R, valid, hnw, rsel, st_ref, st_ref, mg_ref, slots[g])
            _finish_group(g, all_rows, mg_ref, slots[g])
        acc = load_x()
        for g in range(N_GROUPS):
            acc = acc + _out_partial(g, all_rows, mg_ref, wout_ref)
        store_out(acc, all_rows, False)
        return

    def store_early():
        acc = load_x()
        for g in range(N_GROUPS - 1):
            acc = acc + _out_partial(g, all_rows, mg_ref, wout_ref)
        store_out(acc, all_rows, False)

    for g in range(N_GROUPS):
        project(g, True)
        if g > 0:
            _finish_group(g - 1, all_rows, mg_ref, slots[g - 1])
            if g == N_GROUPS - 1:
                store_early()
        _recur_group_fast(g, R, hnw, st_ref, st_old_ref, mg_ref, slots[g])
        project(g, False)
    for g in range(N_GROUPS):
        def redo(g=g):
            _recur_group_exact(g, R, valid, hnw, rsel, st_old_ref, st_ref, mg_ref, slots[g])
            if g < N_GROUPS - 1:
                _finish_group(g, all_rows, mg_ref, slots[g])
                store_early()

        pl.when(safe_ref[g] == 0)(redo)
    last = N_GROUPS - 1
    for rows in (slice(0, R // 2), slice(R // 2, R)):
        _finish_group(last, rows, mg_ref, slots[last])
        store_out(_out_partial(last, rows, mg_ref, wout_ref), rows, True)


def _mixer_kernel(x_ref, meta_ref, lbp_ref, anw_ref, win_ref, hnw_ref, cw_ref, wout_ref, rsel_ref,
                  wup_f32_ref, wdn_f32_ref, h1_ref, h1m_ref, wup_ref, wdn_ref,
                  st_ref, st_old_ref, zc_ref, lb_ref, mg_ref, safe_ref,
                  st_meta_ref, zc_meta_ref, h1m_meta_ref, *slots):
    wup_ref[...] = wup_f32_ref[...].astype(BF16)
    wdn_ref[:, 0:D_MODEL] = wdn_f32_ref[...].astype(BF16)
    wdn_ref[:, D_MODEL:] = jnp.zeros((W_DOWN_ROWS_PER_STEP, WEIGHT_LANE_PAD), BF16)

    n = len(slots) // N_GROUPS
    refs = (lbp_ref, anw_ref, win_ref, hnw_ref, cw_ref, wout_ref, rsel_ref,
            st_ref, st_old_ref, zc_ref, lb_ref, mg_ref, safe_ref,
            [slots[i * n:(i + 1) * n] for i in range(N_GROUPS)])

    first_tile = pl.program_id(1) == 0

    @pl.when(jnp.logical_and(first_tile, pl.program_id(0) == 0))
    def _():
        st_ref[...] = jnp.zeros_like(st_ref)
        zc_ref[...] = jnp.zeros_like(zc_ref)
        load_meta = lambda: jnp.concatenate(
            [meta_ref[...], jnp.zeros((CHUNK - N_META, D_MODEL), F32)], axis=0)

        def store_meta(val, rows, accumulate):
            h1m_meta_ref[...] = val[0:N_META, :]

        _mixer_tile(load_meta, store_meta, CHUNK, N_META, True, refs)
        st_meta_ref[...] = st_ref[...]
        zc_meta_ref[...] = zc_ref[...]

    @pl.when(first_tile)
    def _():
        st_ref[...] = st_meta_ref[...]
        zc_ref[...] = zc_meta_ref[...]
        h1m_ref[0] = h1m_meta_ref[...]

    def store_tile(val, rows, accumulate):
        h1_ref[0, rows, :] = h1_ref[0, rows, :] + val if accumulate else val

    _mixer_tile(lambda: x_ref[0], store_tile, MIXER_TILE, MIXER_TILE, False, refs)


def _ffn_kernel(h1_ref, h1m_ref, fnw_ref, wup_ref, fcw_ref, fcb_ref, wdn_ref, finw_ref,
                out_ref, as_ref):
    T = FFN_TILE
    fnw = fnw_ref[...]

    @pl.when(pl.program_id(1) == 0)
    def _():
        um = _rms(h1m_ref[0], fnw).astype(BF16)
        am = jnp.dot(um, wup_ref[:, 0:D_FF], preferred_element_type=F32)
        as_ref[0:SUBLANES, :] = am[N_META - SUBLANES:N_META, :]

    x = h1_ref[0]
    u = _rms(x, fnw).astype(BF16)
    up = jnp.dot(u, wup_ref[...], preferred_element_type=F32)
    a = up[:, 0:D_FF]
    as_ref[SUBLANES:SUBLANES + T, :] = a
    a1 = as_ref[SUBLANES - 1:SUBLANES - 1 + T, :]
    a2 = as_ref[SUBLANES - 2:SUBLANES - 2 + T, :]
    ac = fcw_ref[0:1, :] * a2 + fcw_ref[1:2, :] * a1 + fcw_ref[2:3, :] * a + fcb_ref[...]
    as_ref[0:SUBLANES, :] = as_ref[T:T + SUBLANES, :]
    gated = (ac * _sigmoid(ac) * up[:, D_FF:]).astype(BF16)
    y = x + jnp.dot(gated, wdn_ref[:, 0:D_MODEL], preferred_element_type=F32)
    out_ref[0] = _rms(y, finw_ref[...])


def _resident(shape):
    return pl.BlockSpec(shape, lambda b, t: (0,) * len(shape), pipeline_mode=pl.Buffered(1))


def _weight(w):
    return jnp.pad(w.astype(BF16), ((0, 0), (0, WEIGHT_LANE_PAD)))


def _rsel_matrix():
    r = np.arange(N_HEADS * HEAD_DIM)[:, None] // HEAD_DIM
    l = np.arange(CHUNK)[None, :] % SUBLANES
    return jnp.asarray(r == l, dtype=BF16)


def kernel(x, meta_tokens, lb_param, attn_norm_w, w_in, hgrn_norm_w, conv_w, w_out, ffn_norm_w, w_up,
           ffn_conv_w, ffn_conv_b, w_down, final_norm_w):
    B, L, D = x.shape
    assert D == D_MODEL and L % MIXER_TILE == 0 and L % FFN_TILE == 0
    assert w_in.shape == (1, D, N_IN_SECTIONS * D) and w_up.shape == (1, D, 2 * D_FF)
    assert SUBLANES * HEAD_DIM == D_MODEL and N_HEADS == SUBLANES

    tile = lambda T: pl.BlockSpec((1, T, D), lambda b, t: (b, t, 0))
    meta_tile = pl.BlockSpec((1, N_META, D), lambda b, t: (b, 0, 0))
    params = pltpu.CompilerParams(dimension_semantics=("arbitrary", "arbitrary"),
                                  vmem_limit_bytes=VMEM_LIMIT_BYTES)

    TM = MIXER_TILE
    n_steps = B * (L // TM)
    step = lambda b, t: b * (L // TM) + t
    assert D % n_steps == 0 and D_FF % W_DOWN_ROWS_PER_STEP == 0 and D_FF // W_DOWN_ROWS_PER_STEP <= n_steps
    wup_rows = D // n_steps
    wup_block = pl.BlockSpec((wup_rows, 2 * D_FF), lambda b, t: (step(b, t), 0))
    wdn_index = lambda b, t: (jnp.minimum(step(b, t), D_FF // W_DOWN_ROWS_PER_STEP - 1), 0)
    slot = [pltpu.VMEM((TM, GROUP_WIDTH), F32)] * 3 + [pltpu.VMEM((TM, GROUP_WIDTH), BF16)] * 4 \
        + [pltpu.VMEM((TM + SUBLANES, GROUP_WIDTH), F32)]
    h1, h1m, w_up_bf16, w_down_bf16 = pl.pallas_call(
        _mixer_kernel,
        grid=(B, L // TM),
        in_specs=[tile(TM), _resident((N_META, D)), _resident((2, D)), _resident((1, D)),
                  _resident((D, N_IN_SECTIONS * D + WEIGHT_LANE_PAD)), _resident((1, HEAD_DIM)),
                  _resident((3, D)), _resident((D, D + WEIGHT_LANE_PAD)),
                  _resident((N_HEADS * HEAD_DIM, CHUNK)), wup_block,
                  pl.BlockSpec((W_DOWN_ROWS_PER_STEP, D), wdn_index)],
        out_specs=[tile(TM), meta_tile, wup_block,
                   pl.BlockSpec((W_DOWN_ROWS_PER_STEP, D + WEIGHT_LANE_PAD), wdn_index)],
        out_shape=[jax.ShapeDtypeStruct((B, L, D), F32), jax.ShapeDtypeStruct((B, N_META, D), F32),
                   jax.ShapeDtypeStruct((D, 2 * D_FF), BF16),
                   jax.ShapeDtypeStruct((D_FF, D + WEIGHT_LANE_PAD), BF16)],
        scratch_shapes=[pltpu.VMEM((N_HEADS, HEAD_DIM, HEAD_DIM), F32),
                        pltpu.VMEM((N_HEADS, HEAD_DIM, HEAD_DIM), F32),
                        pltpu.VMEM((SUBLANES, D), F32),
                        pltpu.VMEM((SUBLANES, D), F32),
                        pltpu.VMEM((TM, D), BF16),
                        pltpu.SMEM((N_GROUPS,), jnp.int32),
                        pltpu.VMEM((N_HEADS, HEAD_DIM, HEAD_DIM), F32),
                        pltpu.VMEM((SUBLANES, D), F32),
                        pltpu.VMEM((N_META, D), F32),
                        *(slot * N_GROUPS)],
        compiler_params=params,
        name="mixer",
    )(x, meta_tokens, lb_param, attn_norm_w, _weight(w_in[0]), hgrn_norm_w, conv_w[0],
      _weight(w_out[0]), _rsel_matrix(), w_up[0], w_down[0])

    TF = FFN_TILE
    out = pl.pallas_call(
        _ffn_kernel,
        grid=(B, L // TF),
        in_specs=[tile(TF), meta_tile, _resident((1, D)), _resident((D, 2 * D_FF)), _resident((3, D_FF)),
                  _resident((1, D_FF)), _resident((D_FF, D + WEIGHT_LANE_PAD)), _resident((1, D))],
        out_specs=tile(TF),
        out_shape=jax.ShapeDtypeStruct((B, L, D), F32),
        scratch_shapes=[pltpu.VMEM((TF + SUBLANES, D_FF), F32)],
        compiler_params=params,
        name="ffn",
    )(h1, h1m, ffn_norm_w, w_up_bf16, ffn_conv_w[0], ffn_conv_b, w_down_bf16, final_norm_w.reshape(1, D))
    return out
```

```python
import numpy as np
import jax
import jax.numpy as jnp
from jax import lax
from jax.experimental import pallas as pl
from jax.experimental.pallas import tpu as pltpu

D_MODEL = 1024
N_META = 16
N_HEADS = 8
HEAD_DIM = 128
D_FF = 2816
N_IN_SECTIONS = 9
EPS = 1e-6

SUBLANES = 8
CHUNK = 128
GROUP_HEADS = 8
GROUP_WIDTH = GROUP_HEADS * HEAD_DIM
N_GROUPS = N_HEADS // GROUP_HEADS
MAX_SAFE_EXPONENT = 60.0
LOG2_E = 1.4426950408889634
MIXER_TILE = 256
FFN_TILE = 512
WEIGHT_LANE_PAD = 128
W_DOWN_ROWS_PER_STEP = 128
V7X_VMEM_BYTES = 64 * 1024 * 1024
VMEM_LIMIT_BYTES = V7X_VMEM_BYTES - 4 * 1024 * 1024

F32 = jnp.float32
BF16 = jnp.bfloat16

_NT = (((1,), (1,)), ((), ()))
_TN = (((0,), (0,)), ((), ()))


def _rms(x, w):
    ms = jnp.mean(x * x, axis=-1, keepdims=True)
    return x * lax.rsqrt(ms + EPS) * w


def _sigmoid(x):
    return 1.0 / (1.0 + jnp.exp2(x * -LOG2_E))


def _bcast_rows(ref, lanes, row0, block, offset, nrows):
    pieces = [jnp.broadcast_to(_row(ref, lanes, row0, i * block + offset), (block, HEAD_DIM))
              for i in range(nrows // block)]
    return pieces[0] if len(pieces) == 1 else jnp.concatenate(pieces, axis=0)


def _row(ref, lanes, row0, r):
    group = ref[pl.ds(row0 + r // SUBLANES * SUBLANES, SUBLANES), lanes]
    return group[r % SUBLANES:r % SUBLANES + 1, :]


def _hgrn_chunk_exact(row0, lanes, valid, st, q_ref, g_ref, k_ref, v_ref, rsel):
    rows = pl.ds(row0, CHUNK)
    q = q_ref[rows, lanes]
    G = g_ref[rows, lanes]
    k = k_ref[rows, lanes]
    v = v_ref[rows, lanes]
    row = lax.broadcasted_iota(jnp.int32, (CHUNK, HEAD_DIM), 0)

    sub = row & (SUBLANES - 1)
    ps = []
    for j in range(SUBLANES):
        gj = _bcast_rows(g_ref, lanes, row0, SUBLANES, j, CHUNK)
        kj = _bcast_rows(k_ref, lanes, row0, SUBLANES, j, CHUNK)
        p = q * kj * jnp.exp2(G - gj)
        ps.append(jnp.where(sub >= j, p, 0.0).astype(BF16))
    pcat = jnp.concatenate(ps, axis=1)
    a = jnp.dot(pcat, rsel, preferred_element_type=F32)

    tl_xor = (lax.broadcasted_iota(jnp.int32, (CHUNK, CHUNK), 0)
              ^ lax.broadcasted_iota(jnp.int32, (CHUNK, CHUNK), 1))
    a = jnp.where(tl_xor < SUBLANES, a, 0.0)

    b = 2 * SUBLANES
    while b <= CHUNK:
        gm = _bcast_rows(g_ref, lanes, row0, b, b // 2 - 1, CHUNK)
        e = jnp.exp2(-jnp.abs(G - gm))
        upper = (row & (b // 2)) != 0
        qt = jnp.where(upper, q * e, 0.0).astype(BF16)
        kt = jnp.where(upper, 0.0, k * e).astype(BF16)
        ab = lax.dot_general(qt, kt, _NT, preferred_element_type=F32)
        a = a + (ab if b == CHUNK else jnp.where(tl_xor < b, ab, 0.0))
        b *= 2

    qi = (q * jnp.exp2(G)).astype(BF16)
    o = lax.dot_general(qi, st.astype(BF16), _NT, preferred_element_type=F32)
    o = o + jnp.dot(a.astype(BF16), v, preferred_element_type=F32)

    glast = _row(g_ref, lanes, row0, valid - 1)
    kd = k * jnp.exp2(glast - G)
    if valid < CHUNK:
        kd = jnp.where(row < valid, kd, 0.0)
    st = st * jnp.exp2(glast) + lax.dot_general(v, kd.astype(BF16), _TN, preferred_element_type=F32)
    return o, st


def _hgrn_chunk_fast(row0, lanes, st, q_ref, g_ref, k_ref, v_ref):
    rows = pl.ds(row0, CHUNK)
    q = q_ref[rows, lanes]
    G = g_ref[rows, lanes]
    k = k_ref[rows, lanes]
    v = v_ref[rows, lanes]
    gmid = g_ref[pl.ds(row0 + CHUNK // 2 - 1, 1), lanes]
    glast = g_ref[pl.ds(row0 + CHUNK - 1, 1), lanes]
    d = G - gmid
    qh = q * jnp.exp2(d)
    kh = k * jnp.exp2(-d)
    a = lax.dot_general(qh.astype(BF16), kh.astype(BF16), _NT, preferred_element_type=F32)
    causal = (lax.broadcasted_iota(jnp.int32, (CHUNK, CHUNK), 1)
              <= lax.broadcasted_iota(jnp.int32, (CHUNK, CHUNK), 0))
    a = jnp.where(causal, a, 0.0)
    qi = (qh * jnp.exp2(gmid)).astype(BF16)
    o = lax.dot_general(qi, st.astype(BF16), _NT, preferred_element_type=F32)
    o = o + jnp.dot(a.astype(BF16), v, preferred_element_type=F32)
    kd = (kh * jnp.exp2(glast - gmid)).astype(BF16)
    st = st * jnp.exp2(glast) + lax.dot_general(v, kd, _TN, preferred_element_type=F32)
    return o, st


def _project_group(u, g, R, valid, prm, slot, recurrence_inputs):
    lb_ref, win_ref, cw_ref, zc_ref, safe_ref = prm
    q_ref, g_ref, k_ref, v_ref, gs_ref, sa_ref, mb_ref, zs_ref = slot
    c0 = g * GROUP_WIDTH
    cols = slice(c0, c0 + GROUP_WIDTH)

    def sec(i):
        w = win_ref[:, i * D_MODEL + c0:i * D_MODEL + c0 + GROUP_WIDTH]
        return jnp.dot(u, w, preferred_element_type=F32)

    if not recurrence_inputs:
        _project_gates(sec, cols, R, valid, cw_ref, zc_ref, slot)
        return

    qv = sec(0)
    q_ref[0:R, :] = qv * _sigmoid(qv)

    f = lb_ref[0:1, cols] + lb_ref[1:2, cols] * _sigmoid(sec(1))
    k_ref[0:R, :] = 1.0 - f
    lf = jnp.log2(f)
    tri = (lax.broadcasted_iota(jnp.int32, (CHUNK, CHUNK), 1)
           <= lax.broadcasted_iota(jnp.int32, (CHUNK, CHUNK), 0)).astype(BF16)
    tri2 = jnp.concatenate([tri, tri], axis=1)
    worst = jnp.zeros((1, GROUP_WIDTH), F32)
    for c in range(R // CHUNK):
        lfc = lf[c * CHUNK:(c + 1) * CHUNK, :]
        hi = lfc.astype(BF16)
        lo = (lfc - hi.astype(F32)).astype(BF16)
        gc = jnp.dot(tri2, jnp.concatenate([hi, lo], axis=0),
                     preferred_element_type=F32)
        g_ref[c * CHUNK:(c + 1) * CHUNK, :] = gc
        gmid = gc[CHUNK // 2 - 1:CHUNK // 2, :]
        glast = gc[CHUNK - 1:CHUNK, :]
        worst = jnp.maximum(worst, jnp.maximum(-gmid, gmid - glast))
    safe_ref[g] = (jnp.max(worst) <= MAX_SAFE_EXPONENT * LOG2_E).astype(jnp.int32)

    v_ref[0:R, :] = sec(2).astype(BF16)


def _project_gates(sec, cols, R, valid, cw_ref, zc_ref, slot):
    gs_ref, sa_ref, mb_ref, zs_ref = slot[4:8]
    gv = sec(3)
    gs_ref[0:R, :] = (gv * _sigmoid(gv)).astype(BF16)

    bg = sec(4)
    z = sec(5) * sec(6)
    zs_ref[0:SUBLANES, :] = zc_ref[:, cols]
    zs_ref[SUBLANES:SUBLANES + R, :] = z
    z1 = zs_ref[SUBLANES - 1:SUBLANES - 1 + R, :]
    z2 = zs_ref[SUBLANES - 2:SUBLANES - 2 + R, :]
    yb = bg * (cw_ref[0:1, cols] * z2 + cw_ref[1:2, cols] * z1 + cw_ref[2:3, cols] * z)
    zc_ref[:, cols] = zs_ref[valid:valid + SUBLANES, :]

    sa_ref[0:R, :] = _sigmoid(sec(7)).astype(BF16)
    mb_ref[0:R, :] = (_sigmoid(sec(8)) * yb).astype(BF16)


def _finish_chunk(row0, h, lanes, o, hnw, mg_ref, slot):
    gs_ref, sa_ref, mb_ref = slot[4:7]
    rows = pl.ds(row0, CHUNK)
    m = (_rms(o, hnw) * gs_ref[rows, lanes].astype(F32) * sa_ref[rows, lanes].astype(F32)
         + mb_ref[rows, lanes].astype(F32))
    mg_ref[rows, h * HEAD_DIM:(h + 1) * HEAD_DIM] = m.astype(BF16)


def _recur_group_fast(g, R, hnw, st_ref, st_old_ref, mg_ref, slot):
    q_ref, g_ref, k_ref, v_ref = slot[0:4]
    for hh in range(GROUP_HEADS):
        h = GROUP_HEADS * g + hh
        lanes = slice(hh * HEAD_DIM, (hh + 1) * HEAD_DIM)
        st = st_ref[h]
        st_old_ref[h] = st
        for c in range(R // CHUNK):
            o, st = _hgrn_chunk_fast(c * CHUNK, lanes, st, q_ref, g_ref, k_ref, v_ref)
            mg_ref[c * CHUNK:(c + 1) * CHUNK, h * HEAD_DIM:(h + 1) * HEAD_DIM] = _rms(o, hnw).astype(BF16)
        st_ref[h] = st


def _finish_group_fast(g, R, mg_ref, slot):
    gs_ref, sa_ref, mb_ref = slot[4:7]
    cols = slice(g * GROUP_WIDTH, (g + 1) * GROUP_WIDTH)
    mg_ref[0:R, cols] = mg_ref[0:R, cols] * gs_ref[0:R, :] * sa_ref[0:R, :] + mb_ref[0:R, :]


def _recur_group_exact(g, R, valid, hnw, rsel, st_in_ref, st_ref, mg_ref, slot):
    q_ref, g_ref, k_ref, v_ref = slot[0:4]
    for hh in range(GROUP_HEADS):
        h = GROUP_HEADS * g + hh
        lanes = slice(hh * HEAD_DIM, (hh + 1) * HEAD_DIM)
        st = st_in_ref[h]
        if valid < R:
            o, st = _hgrn_chunk_exact(0, lanes, valid, st, q_ref, g_ref, k_ref, v_ref, rsel)
            _finish_chunk(0, h, lanes, o, hnw, mg_ref, slot)
        else:
            def chunk_body(c, st):
                row0 = pl.multiple_of(c * CHUNK, CHUNK)
                o, st = _hgrn_chunk_exact(row0, lanes, CHUNK, st, q_ref, g_ref, k_ref, v_ref, rsel)
                _finish_chunk(row0, h, lanes, o, hnw, mg_ref, slot)
                return st

            st = lax.fori_loop(0, R // CHUNK, chunk_body, st)
        st_ref[h] = st


def _out_partial(g, R, mg_ref, wout_ref):
    rows = slice(g * GROUP_WIDTH, (g + 1) * GROUP_WIDTH)
    return jnp.dot(mg_ref[0:R, rows], wout_ref[rows, 0:D_MODEL], preferred_element_type=F32)


def _mixer_tile(load_x, store_out, R, valid, is_meta, refs):
    (lbp_ref, anw_ref, win_ref, hnw_ref, cw_ref, wout_ref, rsel_ref,
     st_ref, st_old_ref, zc_ref, lb_ref, mg_ref, safe_ref, slots) = refs
    prm = (lb_ref, win_ref, cw_ref, zc_ref, safe_ref)
    u = _rms(load_x(), anw_ref[...]).astype(BF16)
    rsel = rsel_ref[...]
    hnw = hnw_ref[...]
    project = lambda g, first: _project_group(u, g, R, valid, prm, slots[g], first)

    lbp = lbp_ref[...]
    mx = jnp.max(lbp, axis=0, keepdims=True)
    ex = jnp.exp(lbp - mx)
    lb = ex[0:1, :] / jnp.sum(ex, axis=0, keepdims=True)
    lb_ref[0:1, :] = lb
    lb_ref[1:2, :] = 1.0 - lb

    if is_meta:
        for g in range(N_GROUPS):
            project(g, True)
            project(g, False)
            _recur_group_exact(g, R, valid, hnw, rsel, st_ref, st_ref, mg_ref, slots[g])
        acc = load_x()
        for g in range(N_GROUPS):
            acc = acc + _out_partial(g, R, mg_ref, wout_ref)
        store_out(acc, False)
        return

    def store_early():
        acc = load_x()
        for g in range(N_GROUPS - 1):
            acc = acc + _out_partial(g, R, mg_ref, wout_ref)
        store_out(acc, False)

    for g in range(N_GROUPS):
        project(g, True)
        if g > 0:
            _finish_group_fast(g - 1, R, mg_ref, slots[g - 1])
        if g == N_GROUPS - 1:
            store_early()
        _recur_group_fast(g, R, hnw, st_ref, st_old_ref, mg_ref, slots[g])
        project(g, False)
    _finish_group_fast(N_GROUPS - 1, R, mg_ref, slots[N_GROUPS - 1])
    for g in range(N_GROUPS):
        def redo(g=g):
            _recur_group_exact(g, R, valid, hnw, rsel, st_old_ref, st_ref, mg_ref, slots[g])
            if g < N_GROUPS - 1:
                store_early()

        pl.when(safe_ref[g] == 0)(redo)
    store_out(_out_partial(N_GROUPS - 1, R, mg_ref, wout_ref), True)


def _mixer_kernel(x_ref, meta_ref, lbp_ref, anw_ref, win_ref, hnw_ref, cw_ref, wout_ref, rsel_ref,
                  wup_f32_ref, wdn_f32_ref, h1_ref, h1m_ref, wup_ref, wdn_ref,
                  st_ref, st_old_ref, zc_ref, lb_ref, mg_ref, safe_ref,
                  st_meta_ref, zc_meta_ref, h1m_meta_ref, *slots):
    wup_ref[...] = wup_f32_ref[...].astype(BF16)
    wdn_ref[:, 0:D_MODEL] = wdn_f32_ref[...].astype(BF16)
    wdn_ref[:, D_MODEL:] = jnp.zeros((W_DOWN_ROWS_PER_STEP, WEIGHT_LANE_PAD), BF16)

    n = len(slots) // N_GROUPS
    refs = (lbp_ref, anw_ref, win_ref, hnw_ref, cw_ref, wout_ref, rsel_ref,
            st_ref, st_old_ref, zc_ref, lb_ref, mg_ref, safe_ref,
            [slots[i * n:(i + 1) * n] for i in range(N_GROUPS)])

    first_tile = pl.program_id(1) == 0

    @pl.when(jnp.logical_and(first_tile, pl.program_id(0) == 0))
    def _():
        st_ref[...] = jnp.zeros_like(st_ref)
        zc_ref[...] = jnp.zeros_like(zc_ref)
        load_meta = lambda: jnp.concatenate(
            [meta_ref[...], jnp.zeros((CHUNK - N_META, D_MODEL), F32)], axis=0)

        def store_meta(val, accumulate):
            h1m_meta_ref[...] = val[0:N_META, :]

        _mixer_tile(load_meta, store_meta, CHUNK, N_META, True, refs)
        st_meta_ref[...] = st_ref[...]
        zc_meta_ref[...] = zc_ref[...]

    @pl.when(first_tile)
    def _():
        st_ref[...] = st_meta_ref[...]
        zc_ref[...] = zc_meta_ref[...]
        h1m_ref[0] = h1m_meta_ref[...]

    def store_tile(val, accumulate):
        h1_ref[0] = h1_ref[0] + val if accumulate else val

    _mixer_tile(lambda: x_ref[0], store_tile, MIXER_TILE, MIXER_TILE, False, refs)


def _ffn_kernel(h1_ref, h1m_ref, fnw_ref, wup_ref, fcw_ref, fcb_ref, wdn_ref, finw_ref,
                out_ref, as_ref):
    T = FFN_TILE
    fnw = fnw_ref[...]

    @pl.when(pl.program_id(1) == 0)
    def _():
        um = _rms(h1m_ref[0], fnw).astype(BF16)
        am = jnp.dot(um, wup_ref[:, 0:D_FF], preferred_element_type=F32)
        as_ref[0:SUBLANES, :] = am[N_META - SUBLANES:N_META, :]

    x = h1_ref[0]
    u = _rms(x, fnw).astype(BF16)
    up = jnp.dot(u, wup_ref[...], preferred_element_type=F32)
    a = up[:, 0:D_FF]
    as_ref[SUBLANES:SUBLANES + T, :] = a
    a1 = as_ref[SUBLANES - 1:SUBLANES - 1 + T, :]
    a2 = as_ref[SUBLANES - 2:SUBLANES - 2 + T, :]
    ac = fcw_ref[0:1, :] * a2 + fcw_ref[1:2, :] * a1 + fcw_ref[2:3, :] * a + fcb_ref[...]
    as_ref[0:SUBLANES, :] = as_ref[T:T + SUBLANES, :]
    gated = (ac * _sigmoid(ac) * up[:, D_FF:]).astype(BF16)
    y = x + jnp.dot(gated, wdn_ref[:, 0:D_MODEL], preferred_element_type=F32)
    out_ref[0] = _rms(y, finw_ref[...])


def _resident(shape):
    return pl.BlockSpec(shape, lambda b, t: (0,) * len(shape), pipeline_mode=pl.Buffered(1))


def _weight(w):
    return jnp.pad(w.astype(BF16), ((0, 0), (0, WEIGHT_LANE_PAD)))


def _rsel_matrix():
    r = np.arange(N_HEADS * HEAD_DIM)[:, None] // HEAD_DIM
    l = np.arange(CHUNK)[None, :] % SUBLANES
    return jnp.asarray(r == l, dtype=BF16)


def kernel(x, meta_tokens, lb_param, attn_norm_w, w_in, hgrn_norm_w, conv_w, w_out, ffn_norm_w, w_up,
           ffn_conv_w, ffn_conv_b, w_down, final_norm_w):
    B, L, D = x.shape
    assert D == D_MODEL and L % MIXER_TILE == 0 and L % FFN_TILE == 0
    assert w_in.shape == (1, D, N_IN_SECTIONS * D) and w_up.shape == (1, D, 2 * D_FF)
    assert SUBLANES * HEAD_DIM == D_MODEL and N_HEADS == SUBLANES

    tile = lambda T: pl.BlockSpec((1, T, D), lambda b, t: (b, t, 0))
    meta_tile = pl.BlockSpec((1, N_META, D), lambda b, t: (b, 0, 0))
    params = pltpu.CompilerParams(dimension_semantics=("arbitrary", "arbitrary"),
                                  vmem_limit_bytes=VMEM_LIMIT_BYTES)

    TM = MIXER_TILE
    n_steps = B * (L // TM)
    step = lambda b, t: b * (L // TM) + t
    assert D % n_steps == 0 and D_FF % W_DOWN_ROWS_PER_STEP == 0 and D_FF // W_DOWN_ROWS_PER_STEP <= n_steps
    wup_rows = D // n_steps
    wup_block = pl.BlockSpec((wup_rows, 2 * D_FF), lambda b, t: (step(b, t), 0))
    wdn_index = lambda b, t: (jnp.minimum(step(b, t), D_FF // W_DOWN_ROWS_PER_STEP - 1), 0)
    slot = [pltpu.VMEM((TM, GROUP_WIDTH), F32)] * 3 + [pltpu.VMEM((TM, GROUP_WIDTH), BF16)] * 4 \
        + [pltpu.VMEM((TM + SUBLANES, GROUP_WIDTH), F32)]
    h1, h1m, w_up_bf16, w_down_bf16 = pl.pallas_call(
        _mixer_kernel,
        grid=(B, L // TM),
        in_specs=[tile(TM), _resident((N_META, D)), _resident((2, D)), _resident((1, D)),
                  _resident((D, N_IN_SECTIONS * D + WEIGHT_LANE_PAD)), _resident((1, HEAD_DIM)),
                  _resident((3, D)), _resident((D, D + WEIGHT_LANE_PAD)),
                  _resident((N_HEADS * HEAD_DIM, CHUNK)), wup_block,
                  pl.BlockSpec((W_DOWN_ROWS_PER_STEP, D), wdn_index)],
        out_specs=[tile(TM), meta_tile, wup_block,
                   pl.BlockSpec((W_DOWN_ROWS_PER_STEP, D + WEIGHT_LANE_PAD), wdn_index)],
        out_shape=[jax.ShapeDtypeStruct((B, L, D), F32), jax.ShapeDtypeStruct((B, N_META, D), F32),
                   jax.ShapeDtypeStruct((D, 2 * D_FF), BF16),
                   jax.ShapeDtypeStruct((D_FF, D + WEIGHT_LANE_PAD), BF16)],
        scratch_shapes=[pltpu.VMEM((N_HEADS, HEAD_DIM, HEAD_DIM), F32),
                        pltpu.VMEM((N_HEADS, HEAD_DIM, HEAD_DIM), F32),
                        pltpu.VMEM((SUBLANES, D), F32),
                        pltpu.VMEM((SUBLANES, D), F32),
                        pltpu.VMEM((TM, D), BF16),
                        pltpu.SMEM((N_GROUPS,), jnp.int32),
                        pltpu.VMEM((N_HEADS, HEAD_DIM, HEAD_DIM), F32),
                        pltpu.VMEM((SUBLANES, D), F32),
                        pltpu.VMEM((N_META, D), F32),
                        *(slot * N_GROUPS)],
        compiler_params=params,
        name="mixer",
    )(x, meta_tokens, lb_param, attn_norm_w, _weight(w_in[0]), hgrn_norm_w, conv_w[0],
      _weight(w_out[0]), _rsel_matrix(), w_up[0], w_down[0])

    TF = FFN_TILE
    out = pl.pallas_call(
        _ffn_kernel,
        grid=(B, L // TF),
        in_specs=[tile(TF), meta_tile, _resident((1, D)), _resident((D, 2 * D_FF)), _resident((3, D_FF)),
                  _resident((1, D_FF)), _resident((D_FF, D + WEIGHT_LANE_PAD)), _resident((1, D))],
        out_specs=tile(TF),
        out_shape=jax.ShapeDtypeStruct((B, L, D), F32),
        scratch_shapes=[pltpu.VMEM((TF + SUBLANES, D_FF), F32)],
        compiler_params=params,
        name="ffn",
    )(h1, h1m, ffn_norm_w, w_up_bf16, ffn_conv_w[0], ffn_conv_b, w_down_bf16, final_norm_w.reshape(1, D))
    return out
```

```python
import numpy as np
import jax
import jax.numpy as jnp
from jax import lax
from jax.experimental import pallas as pl
from jax.experimental.pallas import tpu as pltpu

D_MODEL = 1024
N_META = 16
N_HEADS = 8
HEAD_DIM = 128
D_FF = 2816
N_IN_SECTIONS = 9
EPS = 1e-6

SUBLANES = 8
CHUNK = 128
GROUP_HEADS = 8
GROUP_WIDTH = GROUP_HEADS * HEAD_DIM
N_GROUPS = N_HEADS // GROUP_HEADS
MAX_SAFE_EXPONENT = 60.0
LOG2_E = 1.4426950408889634
MIXER_TILE = 512
FFN_TILE = 512
WEIGHT_LANE_PAD = 128
W_DOWN_ROWS_PER_STEP = 128
V7X_VMEM_BYTES = 64 * 1024 * 1024
VMEM_LIMIT_BYTES = V7X_VMEM_BYTES - 4 * 1024 * 1024

F32 = jnp.float32
BF16 = jnp.bfloat16

_NT = (((1,), (1,)), ((), ()))
_TN = (((0,), (0,)), ((), ()))


def _rms(x, w):
    ms = jnp.mean(x * x, axis=-1, keepdims=True)
    return x * lax.rsqrt(ms + EPS) * w


def _sigmoid(x):
    return 1.0 / (1.0 + jnp.exp2(x * -LOG2_E))


def _bcast_rows(ref, lanes, row0, block, offset, nrows):
    pieces = [jnp.broadcast_to(_row(ref, lanes, row0, i * block + offset), (block, HEAD_DIM))
              for i in range(nrows // block)]
    return pieces[0] if len(pieces) == 1 else jnp.concatenate(pieces, axis=0)


def _row(ref, lanes, row0, r):
    group = ref[pl.ds(row0 + r // SUBLANES * SUBLANES, SUBLANES), lanes]
    return group[r % SUBLANES:r % SUBLANES + 1, :]


def _hgrn_chunk_exact(row0, lanes, valid, st, q_ref, g_ref, k_ref, v_ref, rsel):
    rows = pl.ds(row0, CHUNK)
    q = q_ref[rows, lanes]
    G = g_ref[rows, lanes]
    k = k_ref[rows, lanes]
    v = v_ref[rows, lanes]
    row = lax.broadcasted_iota(jnp.int32, (CHUNK, HEAD_DIM), 0)

    sub = row & (SUBLANES - 1)
    ps = []
    for j in range(SUBLANES):
        gj = _bcast_rows(g_ref, lanes, row0, SUBLANES, j, CHUNK)
        kj = _bcast_rows(k_ref, lanes, row0, SUBLANES, j, CHUNK)
        p = q * kj * jnp.exp2(G - gj)
        ps.append(jnp.where(sub >= j, p, 0.0).astype(BF16))
    pcat = jnp.concatenate(ps, axis=1)
    a = jnp.dot(pcat, rsel, preferred_element_type=F32)

    tl_xor = (lax.broadcasted_iota(jnp.int32, (CHUNK, CHUNK), 0)
              ^ lax.broadcasted_iota(jnp.int32, (CHUNK, CHUNK), 1))
    a = jnp.where(tl_xor < SUBLANES, a, 0.0)

    b = 2 * SUBLANES
    while b <= CHUNK:
        gm = _bcast_rows(g_ref, lanes, row0, b, b // 2 - 1, CHUNK)
        e = jnp.exp2(-jnp.abs(G - gm))
        upper = (row & (b // 2)) != 0
        qt = jnp.where(upper, q * e, 0.0).astype(BF16)
        kt = jnp.where(upper, 0.0, k * e).astype(BF16)
        ab = lax.dot_general(qt, kt, _NT, preferred_element_type=F32)
        a = a + (ab if b == CHUNK else jnp.where(tl_xor < b, ab, 0.0))
        b *= 2

    qi = (q * jnp.exp2(G)).astype(BF16)
    o = lax.dot_general(qi, st.astype(BF16), _NT, preferred_element_type=F32)
    o = o + jnp.dot(a.astype(BF16), v, preferred_element_type=F32)

    glast = _row(g_ref, lanes, row0, valid - 1)
    kd = k * jnp.exp2(glast - G)
    if valid < CHUNK:
        kd = jnp.where(row < valid, kd, 0.0)
    st = st * jnp.exp2(glast) + lax.dot_general(v, kd.astype(BF16), _TN, preferred_element_type=F32)
    return o, st


def _hgrn_chunk_fast(row0, lanes, st, q_ref, g_ref, k_ref, v_ref):
    rows = pl.ds(row0, CHUNK)
    q = q_ref[rows, lanes]
    G = g_ref[rows, lanes]
    k = k_ref[rows, lanes]
    v = v_ref[rows, lanes]
    gmid = g_ref[pl.ds(row0 + CHUNK // 2 - 1, 1), lanes]
    glast = g_ref[pl.ds(row0 + CHUNK - 1, 1), lanes]
    d = G - gmid
    qh = q * jnp.exp2(d)
    kh = k * jnp.exp2(-d)
    a = lax.dot_general(qh.astype(BF16), kh.astype(BF16), _NT, preferred_element_type=F32)
    causal = (lax.broadcasted_iota(jnp.int32, (CHUNK, CHUNK), 1)
              <= lax.broadcasted_iota(jnp.int32, (CHUNK, CHUNK), 0))
    a = jnp.where(causal, a, 0.0)
    qi = (qh * jnp.exp2(gmid)).astype(BF16)
    o = lax.dot_general(qi, st.astype(BF16), _NT, preferred_element_type=F32)
    o = o + jnp.dot(a.astype(BF16), v, preferred_element_type=F32)
    kd = (kh * jnp.exp2(glast - gmid)).astype(BF16)
    st = st * jnp.exp2(glast) + lax.dot_general(v, kd, _TN, preferred_element_type=F32)
    return o, st


def _project_group(u, g, R, valid, prm, slot, recurrence_inputs):
    lb_ref, win_ref, cw_ref, zc_ref, safe_ref = prm
    q_ref, g_ref, k_ref, v_ref, gs_ref, sa_ref, mb_ref, zs_ref = slot
    c0 = g * GROUP_WIDTH
    cols = slice(c0, c0 + GROUP_WIDTH)

    def sec(i):
        w = win_ref[:, i * D_MODEL + c0:i * D_MODEL + c0 + GROUP_WIDTH]
        return jnp.dot(u, w, preferred_element_type=F32)

    if not recurrence_inputs:
        _project_gates(sec, cols, R, valid, cw_ref, zc_ref, slot)
        return

    qv = sec(0)
    q_ref[0:R, :] = qv * _sigmoid(qv)

    f = lb_ref[0:1, cols] + lb_ref[1:2, cols] * _sigmoid(sec(1))
    k_ref[0:R, :] = 1.0 - f
    lf = jnp.log2(f)
    tri = (lax.broadcasted_iota(jnp.int32, (CHUNK, CHUNK), 1)
           <= lax.broadcasted_iota(jnp.int32, (CHUNK, CHUNK), 0)).astype(BF16)
    tri2 = jnp.concatenate([tri, tri], axis=1)
    worst = jnp.zeros((1, GROUP_WIDTH), F32)
    for c in range(R // CHUNK):
        lfc = lf[c * CHUNK:(c + 1) * CHUNK, :]
        hi = lfc.astype(BF16)
        lo = (lfc - hi.astype(F32)).astype(BF16)
        gc = jnp.dot(tri2, jnp.concatenate([hi, lo], axis=0),
                     preferred_element_type=F32)
        g_ref[c * CHUNK:(c + 1) * CHUNK, :] = gc
        gmid = gc[CHUNK // 2 - 1:CHUNK // 2, :]
        glast = gc[CHUNK - 1:CHUNK, :]
        worst = jnp.maximum(worst, jnp.maximum(-gmid, gmid - glast))
    safe_ref[g] = (jnp.max(worst) <= MAX_SAFE_EXPONENT * LOG2_E).astype(jnp.int32)

    v_ref[0:R, :] = sec(2).astype(BF16)


def _project_gates(sec, cols, R, valid, cw_ref, zc_ref, slot):
    gs_ref, sa_ref, mb_ref, zs_ref = slot[4:8]
    gv = sec(3)
    gs_ref[0:R, :] = (gv * _sigmoid(gv)).astype(BF16)

    bg = sec(4)
    z = sec(5) * sec(6)
    zs_ref[0:SUBLANES, :] = zc_ref[:, cols]
    zs_ref[SUBLANES:SUBLANES + R, :] = z
    z1 = zs_ref[SUBLANES - 1:SUBLANES - 1 + R, :]
    z2 = zs_ref[SUBLANES - 2:SUBLANES - 2 + R, :]
    yb = bg * (cw_ref[0:1, cols] * z2 + cw_ref[1:2, cols] * z1 + cw_ref[2:3, cols] * z)
    zc_ref[:, cols] = zs_ref[valid:valid + SUBLANES, :]

    sa_ref[0:R, :] = _sigmoid(sec(7)).astype(BF16)
    mb_ref[0:R, :] = (_sigmoid(sec(8)) * yb).astype(BF16)


def _finish_chunk(row0, h, lanes, o, hnw, mg_ref, slot):
    gs_ref, sa_ref, mb_ref = slot[4:7]
    rows = pl.ds(row0, CHUNK)
    m = (_rms(o, hnw) * gs_ref[rows, lanes].astype(F32) * sa_ref[rows, lanes].astype(F32)
         + mb_ref[rows, lanes].astype(F32))
    mg_ref[rows, h * HEAD_DIM:(h + 1) * HEAD_DIM] = m.astype(BF16)


def _recur_group_fast(g, R, hnw, st_ref, st_old_ref, mg_ref, slot):
    q_ref, g_ref, k_ref, v_ref = slot[0:4]
    for hh in range(GROUP_HEADS):
        h = GROUP_HEADS * g + hh
        lanes = slice(hh * HEAD_DIM, (hh + 1) * HEAD_DIM)
        st = st_ref[h]
        st_old_ref[h] = st
        for c in range(R // CHUNK):
            o, st = _hgrn_chunk_fast(c * CHUNK, lanes, st, q_ref, g_ref, k_ref, v_ref)
            mg_ref[c * CHUNK:(c + 1) * CHUNK, h * HEAD_DIM:(h + 1) * HEAD_DIM] = _rms(o, hnw).astype(BF16)
        st_ref[h] = st


def _finish_group_fast(g, R, mg_ref, slot):
    gs_ref, sa_ref, mb_ref = slot[4:7]
    cols = slice(g * GROUP_WIDTH, (g + 1) * GROUP_WIDTH)
    mg_ref[0:R, cols] = mg_ref[0:R, cols] * gs_ref[0:R, :] * sa_ref[0:R, :] + mb_ref[0:R, :]


def _recur_group_exact(g, R, valid, hnw, rsel, st_in_ref, st_ref, mg_ref, slot):
    q_ref, g_ref, k_ref, v_ref = slot[0:4]
    for hh in range(GROUP_HEADS):
        h = GROUP_HEADS * g + hh
        lanes = slice(hh * HEAD_DIM, (hh + 1) * HEAD_DIM)
        st = st_in_ref[h]
        if valid < R:
            o, st = _hgrn_chunk_exact(0, lanes, valid, st, q_ref, g_ref, k_ref, v_ref, rsel)
            _finish_chunk(0, h, lanes, o, hnw, mg_ref, slot)
        else:
            def chunk_body(c, st):
                row0 = pl.multiple_of(c * CHUNK, CHUNK)
                o, st = _hgrn_chunk_exact(row0, lanes, CHUNK, st, q_ref, g_ref, k_ref, v_ref, rsel)
                _finish_chunk(row0, h, lanes, o, hnw, mg_ref, slot)
                return st

            st = lax.fori_loop(0, R // CHUNK, chunk_body, st)
        st_ref[h] = st


def _out_partial(g, R, mg_ref, wout_ref):
    rows = slice(g * GROUP_WIDTH, (g + 1) * GROUP_WIDTH)
    return jnp.dot(mg_ref[0:R, rows], wout_ref[rows, 0:D_MODEL], preferred_element_type=F32)


def _mixer_tile(load_x, store_out, R, valid, is_meta, refs):
    (lbp_ref, anw_ref, win_ref, hnw_ref, cw_ref, wout_ref, rsel_ref,
     st_ref, st_old_ref, zc_ref, lb_ref, mg_ref, safe_ref, slots) = refs
    prm = (lb_ref, win_ref, cw_ref, zc_ref, safe_ref)
    u = _rms(load_x(), anw_ref[...]).astype(BF16)
    rsel = rsel_ref[...]
    hnw = hnw_ref[...]
    project = lambda g, first: _project_group(u, g, R, valid, prm, slots[g], first)

    lbp = lbp_ref[...]
    mx = jnp.max(lbp, axis=0, keepdims=True)
    ex = jnp.exp(lbp - mx)
    lb = ex[0:1, :] / jnp.sum(ex, axis=0, keepdims=True)
    lb_ref[0:1, :] = lb
    lb_ref[1:2, :] = 1.0 - lb

    if is_meta:
        for g in range(N_GROUPS):
            project(g, True)
            project(g, False)
            _recur_group_exact(g, R, valid, hnw, rsel, st_ref, st_ref, mg_ref, slots[g])
        acc = load_x()
        for g in range(N_GROUPS):
            acc = acc + _out_partial(g, R, mg_ref, wout_ref)
        store_out(acc, False)
        return

    def store_early():
        acc = load_x()
        for g in range(N_GROUPS - 1):
            acc = acc + _out_partial(g, R, mg_ref, wout_ref)
        store_out(acc, False)

    for g in range(N_GROUPS):
        project(g, True)
        if g > 0:
            _finish_group_fast(g - 1, R, mg_ref, slots[g - 1])
        if g == N_GROUPS - 1:
            store_early()
        _recur_group_fast(g, R, hnw, st_ref, st_old_ref, mg_ref, slots[g])
        project(g, False)
    _finish_group_fast(N_GROUPS - 1, R, mg_ref, slots[N_GROUPS - 1])
    for g in range(N_GROUPS):
        def redo(g=g):
            _recur_group_exact(g, R, valid, hnw, rsel, st_old_ref, st_ref, mg_ref, slots[g])
            if g < N_GROUPS - 1:
                store_early()

        pl.when(safe_ref[g] == 0)(redo)
    store_out(_out_partial(N_GROUPS - 1, R, mg_ref, wout_ref), True)


def _mixer_kernel(x_ref, meta_ref, lbp_ref, anw_ref, win_ref, hnw_ref, cw_ref, wout_ref, rsel_ref,
                  wup_f32_ref, wdn_f32_ref, h1_ref, h1m_ref, wup_ref, wdn_ref,
                  st_ref, st_old_ref, zc_ref, lb_ref, mg_ref, safe_ref,
                  st_meta_ref, zc_meta_ref, h1m_meta_ref, *slots):
    wup_ref[...] = wup_f32_ref[...].astype(BF16)
    wdn_ref[:, 0:D_MODEL] = wdn_f32_ref[...].astype(BF16)
    wdn_ref[:, D_MODEL:] = jnp.zeros((W_DOWN_ROWS_PER_STEP, WEIGHT_LANE_PAD), BF16)

    n = len(slots) // N_GROUPS
    refs = (lbp_ref, anw_ref, win_ref, hnw_ref, cw_ref, wout_ref, rsel_ref,
            st_ref, st_old_ref, zc_ref, lb_ref, mg_ref, safe_ref,
            [slots[i * n:(i + 1) * n] for i in range(N_GROUPS)])

    first_tile = pl.program_id(1) == 0

    @pl.when(jnp.logical_and(first_tile, pl.program_id(0) == 0))
    def _():
        st_ref[...] = jnp.zeros_like(st_ref)
        zc_ref[...] = jnp.zeros_like(zc_ref)
        load_meta = lambda: jnp.concatenate(
            [meta_ref[...], jnp.zeros((CHUNK - N_META, D_MODEL), F32)], axis=0)

        def store_meta(val, accumulate):
            h1m_meta_ref[...] = val[0:N_META, :]

        _mixer_tile(load_meta, store_meta, CHUNK, N_META, True, refs)
        st_meta_ref[...] = st_ref[...]
        zc_meta_ref[...] = zc_ref[...]

    @pl.when(first_tile)
    def _():
        st_ref[...] = st_meta_ref[...]
        zc_ref[...] = zc_meta_ref[...]
        h1m_ref[0] = h1m_meta_ref[...]

    def store_tile(val, accumulate):
        h1_ref[0] = h1_ref[0] + val if accumulate else val

    _mixer_tile(lambda: x_ref[0], store_tile, MIXER_TILE, MIXER_TILE, False, refs)


def _ffn_kernel(h1_ref, h1m_ref, fnw_ref, wup_ref, fcw_ref, fcb_ref, wdn_ref, finw_ref,
                out_ref, as_ref):
    T = FFN_TILE
    fnw = fnw_ref[...]

    @pl.when(pl.program_id(1) == 0)
    def _():
        um = _rms(h1m_ref[0], fnw).astype(BF16)
        am = jnp.dot(um, wup_ref[:, 0:D_FF], preferred_element_type=F32)
        as_ref[0:SUBLANES, :] = am[N_META - SUBLANES:N_META, :]

    x = h1_ref[0]
    u = _rms(x, fnw).astype(BF16)
    up = jnp.dot(u, wup_ref[...], preferred_element_type=F32)
    a = up[:, 0:D_FF]
    as_ref[SUBLANES:SUBLANES + T, :] = a
    a1 = as_ref[SUBLANES - 1:SUBLANES - 1 + T, :]
    a2 = as_ref[SUBLANES - 2:SUBLANES - 2 + T, :]
    ac = fcw_ref[0:1, :] * a2 + fcw_ref[1:2, :] * a1 + fcw_ref[2:3, :] * a + fcb_ref[...]
    as_ref[0:SUBLANES, :] = as_ref[T:T + SUBLANES, :]
    gated = (ac * _sigmoid(ac) * up[:, D_FF:]).astype(BF16)
    y = x + jnp.dot(gated, wdn_ref[:, 0:D_MODEL], preferred_element_type=F32)
    out_ref[0] = _rms(y, finw_ref[...])


def _resident(shape):
    return pl.BlockSpec(shape, lambda b, t: (0,) * len(shape), pipeline_mode=pl.Buffered(1))


def _weight(w):
    return jnp.pad(w.astype(BF16), ((0, 0), (0, WEIGHT_LANE_PAD)))


def _rsel_matrix():
    r = np.arange(N_HEADS * HEAD_DIM)[:, None] // HEAD_DIM
    l = np.arange(CHUNK)[None, :] % SUBLANES
    return jnp.asarray(r == l, dtype=BF16)


def kernel(x, meta_tokens, lb_param, attn_norm_w, w_in, hgrn_norm_w, conv_w, w_out, ffn_norm_w, w_up,
           ffn_conv_w, ffn_conv_b, w_down, final_norm_w):
    B, L, D = x.shape
    assert D == D_MODEL and L % MIXER_TILE == 0 and L % FFN_TILE == 0
    assert w_in.shape == (1, D, N_IN_SECTIONS * D) and w_up.shape == (1, D, 2 * D_FF)
    assert SUBLANES * HEAD_DIM == D_MODEL and N_HEADS == SUBLANES

    tile = lambda T: pl.BlockSpec((1, T, D), lambda b, t: (b, t, 0))
    meta_tile = pl.BlockSpec((1, N_META, D), lambda b, t: (b, 0, 0))
    params = pltpu.CompilerParams(dimension_semantics=("arbitrary", "arbitrary"),
                                  vmem_limit_bytes=VMEM_LIMIT_BYTES)

    TM = MIXER_TILE
    n_steps = B * (L // TM)
    step = lambda b, t: b * (L // TM) + t
    assert D % n_steps == 0 and D_FF % W_DOWN_ROWS_PER_STEP == 0 and D_FF // W_DOWN_ROWS_PER_STEP <= n_steps
    wup_rows = D // n_steps
    wup_block = pl.BlockSpec((wup_rows, 2 * D_FF), lambda b, t: (step(b, t), 0))
    wdn_index = lambda b, t: (jnp.minimum(step(b, t), D_FF // W_DOWN_ROWS_PER_STEP - 1), 0)
    slot = [pltpu.VMEM((TM, GROUP_WIDTH), F32)] * 3 + [pltpu.VMEM((TM, GROUP_WIDTH), BF16)] * 4 \
        + [pltpu.VMEM((TM + SUBLANES, GROUP_WIDTH), F32)]
    h1, h1m, w_up_bf16, w_down_bf16 = pl.pallas_call(
        _mixer_kernel,
        grid=(B, L // TM),
        in_specs=[tile(TM), _resident((N_META, D)), _resident((2, D)), _resident((1, D)),
                  _resident((D, N_IN_SECTIONS * D + WEIGHT_LANE_PAD)), _resident((1, HEAD_DIM)),
                  _resident((3, D)), _resident((D, D + WEIGHT_LANE_PAD)),
                  _resident((N_HEADS * HEAD_DIM, CHUNK)), wup_block,
                  pl.BlockSpec((W_DOWN_ROWS_PER_STEP, D), wdn_index)],
        out_specs=[tile(TM), meta_tile, wup_block,
                   pl.BlockSpec((W_DOWN_ROWS_PER_STEP, D + WEIGHT_LANE_PAD), wdn_index)],
        out_shape=[jax.ShapeDtypeStruct((B, L, D), F32), jax.ShapeDtypeStruct((B, N_META, D), F32),
                   jax.ShapeDtypeStruct((D, 2 * D_FF), BF16),
                   jax.ShapeDtypeStruct((D_FF, D + WEIGHT_LANE_PAD), BF16)],
        scratch_shapes=[pltpu.VMEM((N_HEADS, HEAD_DIM, HEAD_DIM), F32),
                        pltpu.VMEM((N_HEADS, HEAD_DIM, HEAD_DIM), F32),
                        pltpu.VMEM((SUBLANES, D), F32),
                        pltpu.VMEM((SUBLANES, D), F32),
                        pltpu.VMEM((TM, D), BF16),
                        pltpu.SMEM((N_GROUPS,), jnp.int32),
                        pltpu.VMEM((N_HEADS, HEAD_DIM, HEAD_DIM), F32),
                        pltpu.VMEM((SUBLANES, D), F32),
                        pltpu.VMEM((N_META, D), F32),
                        *(slot * N_GROUPS)],
        compiler_params=params,
        name="mixer",
    )(x, meta_tokens, lb_param, attn_norm_w, _weight(w_in[0]), hgrn_norm_w, conv_w[0],
      _weight(w_out[0]), _rsel_matrix(), w_up[0], w_down[0])

    TF = FFN_TILE
    out = pl.pallas_call(
        _ffn_kernel,
        grid=(B, L // TF),
        in_specs=[tile(TF), meta_tile, _resident((1, D)), _resident((D, 2 * D_FF)), _resident((3, D_FF)),
                  _resident((1, D_FF)), _resident((D_FF, D + WEIGHT_LANE_PAD)), _resident((1, D))],
        out_specs=tile(TF),
        out_shape=jax.ShapeDtypeStruct((B, L, D), F32),
        scratch_shapes=[pltpu.VMEM((TF + SUBLANES, D_FF), F32)],
        compiler_params=params,
        name="ffn",
    )(h1, h1m, ffn_norm_w, w_up_bf16, ffn_conv_w[0], ffn_conv_b, w_down_bf16, final_norm_w.reshape(1, D))
    return out
```

```python
import numpy as np
import jax
import jax.numpy as jnp
from jax import lax
from jax.experimental import pallas as pl
from jax.experimental.pallas import tpu as pltpu

D_MODEL = 1024
N_META = 16
N_HEADS = 8
HEAD_DIM = 128
D_FF = 2816
N_IN_SECTIONS = 9
EPS = 1e-6

SUBLANES = 8
CHUNK = 128
GROUP_HEADS = 8
GROUP_WIDTH = GROUP_HEADS * HEAD_DIM
N_GROUPS = N_HEADS // GROUP_HEADS
MAX_SAFE_EXPONENT = 60.0
LOG2_E = 1.4426950408889634
MIXER_TILE = 512
FFN_TILE = 512
WEIGHT_LANE_PAD = 128
W_DOWN_ROWS_PER_STEP = 128
V7X_VMEM_BYTES = 64 * 1024 * 1024
VMEM_LIMIT_BYTES = V7X_VMEM_BYTES - 4 * 1024 * 1024

F32 = jnp.float32
BF16 = jnp.bfloat16

_NT = (((1,), (1,)), ((), ()))
_TN = (((0,), (0,)), ((), ()))


def _rms(x, w):
    ms = jnp.mean(x * x, axis=-1, keepdims=True)
    return x * lax.rsqrt(ms + EPS) * w


def _sigmoid(x):
    return 1.0 / (1.0 + jnp.exp2(x * -LOG2_E))


def _bcast_rows(ref, lanes, row0, block, offset, nrows):
    pieces = [jnp.broadcast_to(_row(ref, lanes, row0, i * block + offset), (block, HEAD_DIM))
              for i in range(nrows // block)]
    return pieces[0] if len(pieces) == 1 else jnp.concatenate(pieces, axis=0)


def _row(ref, lanes, row0, r):
    group = ref[pl.ds(row0 + r // SUBLANES * SUBLANES, SUBLANES), lanes]
    return group[r % SUBLANES:r % SUBLANES + 1, :]


def _hgrn_chunk_exact(row0, lanes, valid, st, q_ref, g_ref, k_ref, v_ref, rsel):
    rows = pl.ds(row0, CHUNK)
    q = q_ref[rows, lanes]
    G = g_ref[rows, lanes]
    k = k_ref[rows, lanes]
    v = v_ref[rows, lanes]
    row = lax.broadcasted_iota(jnp.int32, (CHUNK, HEAD_DIM), 0)

    sub = row & (SUBLANES - 1)
    ps = []
    for j in range(SUBLANES):
        gj = _bcast_rows(g_ref, lanes, row0, SUBLANES, j, CHUNK)
        kj = _bcast_rows(k_ref, lanes, row0, SUBLANES, j, CHUNK)
        p = q * kj * jnp.exp2(G - gj)
        ps.append(jnp.where(sub >= j, p, 0.0).astype(BF16))
    pcat = jnp.concatenate(ps, axis=1)
    a = jnp.dot(pcat, rsel, preferred_element_type=F32)

    tl_xor = (lax.broadcasted_iota(jnp.int32, (CHUNK, CHUNK), 0)
              ^ lax.broadcasted_iota(jnp.int32, (CHUNK, CHUNK), 1))
    a = jnp.where(tl_xor < SUBLANES, a, 0.0)

    b = 2 * SUBLANES
    while b <= CHUNK:
        gm = _bcast_rows(g_ref, lanes, row0, b, b // 2 - 1, CHUNK)
        e = jnp.exp2(-jnp.abs(G - gm))
        upper = (row & (b // 2)) != 0
        qt = jnp.where(upper, q * e, 0.0).astype(BF16)
        kt = jnp.where(upper, 0.0, k * e).astype(BF16)
        ab = lax.dot_general(qt, kt, _NT, preferred_element_type=F32)
        a = a + (ab if b == CHUNK else jnp.where(tl_xor < b, ab, 0.0))
        b *= 2

    qi = (q * jnp.exp2(G)).astype(BF16)
    o = lax.dot_general(qi, st.astype(BF16), _NT, preferred_element_type=F32)
    o = o + jnp.dot(a.astype(BF16), v, preferred_element_type=F32)

    glast = _row(g_ref, lanes, row0, valid - 1)
    kd = k * jnp.exp2(glast - G)
    if valid < CHUNK:
        kd = jnp.where(row < valid, kd, 0.0)
    st = st * jnp.exp2(glast) + lax.dot_general(v, kd.astype(BF16), _TN, preferred_element_type=F32)
    return o, st


def _hgrn_chunk_fast(row0, lanes, st, q_ref, g_ref, k_ref, v_ref):
    rows = pl.ds(row0, CHUNK)
    q = q_ref[rows, lanes]
    G = g_ref[rows, lanes]
    k = k_ref[rows, lanes]
    v = v_ref[rows, lanes]
    gmid = g_ref[pl.ds(row0 + CHUNK // 2 - 1, 1), lanes]
    glast = g_ref[pl.ds(row0 + CHUNK - 1, 1), lanes]
    d = G - gmid
    qh = q * jnp.exp2(d)
    kh = k * jnp.exp2(-d)
    a = lax.dot_general(qh.astype(BF16), kh.astype(BF16), _NT, preferred_element_type=F32)
    causal = (lax.broadcasted_iota(jnp.int32, (CHUNK, CHUNK), 1)
              <= lax.broadcasted_iota(jnp.int32, (CHUNK, CHUNK), 0))
    a = jnp.where(causal, a, 0.0)
    qi = (qh * jnp.exp2(gmid)).astype(BF16)
    o = lax.dot_general(qi, st.astype(BF16), _NT, preferred_element_type=F32)
    o = o + jnp.dot(a.astype(BF16), v, preferred_element_type=F32)
    kd = (kh * jnp.exp2(glast - gmid)).astype(BF16)
    st = st * jnp.exp2(glast) + lax.dot_general(v, kd, _TN, preferred_element_type=F32)
    return o, st


def _project_group(u, g, R, valid, prm, slot, recurrence_inputs):
    lb_ref, win_ref, cw_ref, zc_ref, safe_ref = prm
    q_ref, g_ref, k_ref, v_ref, gs_ref, sa_ref, mb_ref, zs_ref = slot
    c0 = g * GROUP_WIDTH
    cols = slice(c0, c0 + GROUP_WIDTH)

    def sec(i):
        w = win_ref[:, i * D_MODEL + c0:i * D_MODEL + c0 + GROUP_WIDTH]
        return jnp.dot(u, w, preferred_element_type=F32)

    if not recurrence_inputs:
        _project_gates(sec, cols, R, valid, cw_ref, zc_ref, slot)
        return

    qv = sec(0)
    q_ref[0:R, :] = qv * _sigmoid(qv)

    f = lb_ref[0:1, cols] + lb_ref[1:2, cols] * _sigmoid(sec(1))
    k_ref[0:R, :] = 1.0 - f
    lf = jnp.log2(f)
    tri = (lax.broadcasted_iota(jnp.int32, (CHUNK, CHUNK), 1)
           <= lax.broadcasted_iota(jnp.int32, (CHUNK, CHUNK), 0)).astype(BF16)
    tri2 = jnp.concatenate([tri, tri], axis=1)
    worst = jnp.zeros((1, GROUP_WIDTH), F32)
    for c in range(R // CHUNK):
        lfc = lf[c * CHUNK:(c + 1) * CHUNK, :]
        hi = lfc.astype(BF16)
        lo = (lfc - hi.astype(F32)).astype(BF16)
        gc = jnp.dot(tri2, jnp.concatenate([hi, lo], axis=0),
                     preferred_element_type=F32)
        g_ref[c * CHUNK:(c + 1) * CHUNK, :] = gc
        gmid = gc[CHUNK // 2 - 1:CHUNK // 2, :]
        glast = gc[CHUNK - 1:CHUNK, :]
        worst = jnp.maximum(worst, jnp.maximum(-gmid, gmid - glast))
    safe_ref[g] = (jnp.max(worst) <= MAX_SAFE_EXPONENT * LOG2_E).astype(jnp.int32)

    v_ref[0:R, :] = sec(2).astype(BF16)


def _project_gates(sec, cols, R, valid, cw_ref, zc_ref, slot):
    gs_ref, sa_ref, mb_ref, zs_ref = slot[4:8]
    gv = sec(3)
    gs_ref[0:R, :] = (gv * _sigmoid(gv)).astype(BF16)

    bg = sec(4)
    z = sec(5) * sec(6)
    zs_ref[0:SUBLANES, :] = zc_ref[:, cols]
    zs_ref[SUBLANES:SUBLANES + R, :] = z
    z1 = zs_ref[SUBLANES - 1:SUBLANES - 1 + R, :]
    z2 = zs_ref[SUBLANES - 2:SUBLANES - 2 + R, :]
    yb = bg * (cw_ref[0:1, cols] * z2 + cw_ref[1:2, cols] * z1 + cw_ref[2:3, cols] * z)
    zc_ref[:, cols] = zs_ref[valid:valid + SUBLANES, :]

    sa_ref[0:R, :] = _sigmoid(sec(7)).astype(BF16)
    mb_ref[0:R, :] = (_sigmoid(sec(8)) * yb).astype(BF16)


def _finish_chunk(row0, h, lanes, o, hnw, mg_ref, slot):
    gs_ref, sa_ref, mb_ref = slot[4:7]
    rows = pl.ds(row0, CHUNK)
    m = (_rms(o, hnw) * gs_ref[rows, lanes].astype(F32) * sa_ref[rows, lanes].astype(F32)
         + mb_ref[rows, lanes].astype(F32))
    mg_ref[rows, h * HEAD_DIM:(h + 1) * HEAD_DIM] = m.astype(BF16)


def _recur_group_fast(g, R, hnw, st_ref, st_old_ref, mg_ref, slot):
    q_ref, g_ref, k_ref, v_ref = slot[0:4]
    for hh in range(GROUP_HEADS):
        h = GROUP_HEADS * g + hh
        lanes = slice(hh * HEAD_DIM, (hh + 1) * HEAD_DIM)
        st = st_ref[h]
        st_old_ref[h] = st
        for c in range(R // CHUNK):
            o, st = _hgrn_chunk_fast(c * CHUNK, lanes, st, q_ref, g_ref, k_ref, v_ref)
            mg_ref[c * CHUNK:(c + 1) * CHUNK, h * HEAD_DIM:(h + 1) * HEAD_DIM] = _rms(o, hnw).astype(BF16)
        st_ref[h] = st


def _finish_group_fast(g, R, mg_ref, slot):
    gs_ref, sa_ref, mb_ref = slot[4:7]
    cols = slice(g * GROUP_WIDTH, (g + 1) * GROUP_WIDTH)
    mg_ref[0:R, cols] = mg_ref[0:R, cols] * gs_ref[0:R, :] * sa_ref[0:R, :] + mb_ref[0:R, :]


def _recur_group_exact(g, R, valid, hnw, rsel, st_in_ref, st_ref, mg_ref, slot):
    q_ref, g_ref, k_ref, v_ref = slot[0:4]
    for hh in range(GROUP_HEADS):
        h = GROUP_HEADS * g + hh
        lanes = slice(hh * HEAD_DIM, (hh + 1) * HEAD_DIM)
        st = st_in_ref[h]
        if valid < R:
            o, st = _hgrn_chunk_exact(0, lanes, valid, st, q_ref, g_ref, k_ref, v_ref, rsel)
            _finish_chunk(0, h, lanes, o, hnw, mg_ref, slot)
        else:
            def chunk_body(c, st):
                row0 = pl.multiple_of(c * CHUNK, CHUNK)
                o, st = _hgrn_chunk_exact(row0, lanes, CHUNK, st, q_ref, g_ref, k_ref, v_ref, rsel)
                _finish_chunk(row0, h, lanes, o, hnw, mg_ref, slot)
                return st

            st = lax.fori_loop(0, R // CHUNK, chunk_body, st)
        st_ref[h] = st


def _out_partial(g, R, mg_ref, wout_ref):
    rows = slice(g * GROUP_WIDTH, (g + 1) * GROUP_WIDTH)
    return jnp.dot(mg_ref[0:R, rows], wout_ref[rows, 0:D_MODEL], preferred_element_type=F32)


def _mixer_tile(load_x, store_out, R, valid, is_meta, refs):
    (lbp_ref, anw_ref, win_ref, hnw_ref, cw_ref, wout_ref, rsel_ref,
     st_ref, st_old_ref, zc_ref, lb_ref, mg_ref, safe_ref, slots) = refs
    prm = (lb_ref, win_ref, cw_ref, zc_ref, safe_ref)
    u = _rms(load_x(), anw_ref[...]).astype(BF16)
    rsel = rsel_ref[...]
    hnw = hnw_ref[...]
    project = lambda g, first: _project_group(u, g, R, valid, prm, slots[g], first)

    lbp = lbp_ref[...]
    mx = jnp.max(lbp, axis=0, keepdims=True)
    ex = jnp.exp(lbp - mx)
    lb = ex[0:1, :] / jnp.sum(ex, axis=0, keepdims=True)
    lb_ref[0:1, :] = lb
    lb_ref[1:2, :] = 1.0 - lb

    if is_meta:
        for g in range(N_GROUPS):
            project(g, True)
            project(g, False)
            _recur_group_exact(g, R, valid, hnw, rsel, st_ref, st_ref, mg_ref, slots[g])
        acc = load_x()
        for g in range(N_GROUPS):
            acc = acc + _out_partial(g, R, mg_ref, wout_ref)
        store_out(acc, False)
        return

    def store_partials(n_groups):
        acc = load_x()
        for g in range(n_groups):
            acc = acc + _out_partial(g, R, mg_ref, wout_ref)
        store_out(acc, False)

    for g in range(N_GROUPS):
        project(g, True)
        if g > 0:
            _finish_group_fast(g - 1, R, mg_ref, slots[g - 1])
        if g == N_GROUPS - 1:
            store_partials(N_GROUPS - 1)
        _recur_group_fast(g, R, hnw, st_ref, st_old_ref, mg_ref, slots[g])
        project(g, False)
    _finish_group_fast(N_GROUPS - 1, R, mg_ref, slots[N_GROUPS - 1])
    store_out(_out_partial(N_GROUPS - 1, R, mg_ref, wout_ref), True)
    for g in range(N_GROUPS):
        def redo(g=g):
            _recur_group_exact(g, R, valid, hnw, rsel, st_old_ref, st_ref, mg_ref, slots[g])
            store_partials(N_GROUPS)

        pl.when(safe_ref[g] == 0)(redo)


def _mixer_kernel(x_ref, meta_ref, lbp_ref, anw_ref, win_ref, hnw_ref, cw_ref, wout_ref, rsel_ref,
                  wup_f32_ref, wdn_f32_ref, h1_ref, h1m_ref, wup_ref, wdn_ref,
                  st_ref, st_old_ref, zc_ref, lb_ref, mg_ref, safe_ref,
                  st_meta_ref, zc_meta_ref, h1m_meta_ref, *slots):
    wup_ref[...] = wup_f32_ref[...].astype(BF16)
    wdn_ref[:, 0:D_MODEL] = wdn_f32_ref[...].astype(BF16)
    wdn_ref[:, D_MODEL:] = jnp.zeros((W_DOWN_ROWS_PER_STEP, WEIGHT_LANE_PAD), BF16)

    n = len(slots) // N_GROUPS
    refs = (lbp_ref, anw_ref, win_ref, hnw_ref, cw_ref, wout_ref, rsel_ref,
            st_ref, st_old_ref, zc_ref, lb_ref, mg_ref, safe_ref,
            [slots[i * n:(i + 1) * n] for i in range(N_GROUPS)])

    first_tile = pl.program_id(1) == 0

    @pl.when(jnp.logical_and(first_tile, pl.program_id(0) == 0))
    def _():
        st_ref[...] = jnp.zeros_like(st_ref)
        zc_ref[...] = jnp.zeros_like(zc_ref)
        load_meta = lambda: jnp.concatenate(
            [meta_ref[...], jnp.zeros((CHUNK - N_META, D_MODEL), F32)], axis=0)

        def store_meta(val, accumulate):
            h1m_meta_ref[...] = val[0:N_META, :]

        _mixer_tile(load_meta, store_meta, CHUNK, N_META, True, refs)
        st_meta_ref[...] = st_ref[...]
        zc_meta_ref[...] = zc_ref[...]

    @pl.when(first_tile)
    def _():
        st_ref[...] = st_meta_ref[...]
        zc_ref[...] = zc_meta_ref[...]
        h1m_ref[0] = h1m_meta_ref[...]

    def store_tile(val, accumulate):
        h1_ref[0] = h1_ref[0] + val if accumulate else val

    _mixer_tile(lambda: x_ref[0], store_tile, MIXER_TILE, MIXER_TILE, False, refs)


def _ffn_kernel(h1_ref, h1m_ref, fnw_ref, wup_ref, fcw_ref, fcb_ref, wdn_ref, finw_ref,
                out_ref, as_ref):
    T = FFN_TILE
    fnw = fnw_ref[...]

    @pl.when(pl.program_id(1) == 0)
    def _():
        um = _rms(h1m_ref[0], fnw).astype(BF16)
        am = jnp.dot(um, wup_ref[:, 0:D_FF], preferred_element_type=F32)
        as_ref[0:SUBLANES, :] = am[N_META - SUBLANES:N_META, :]

    x = h1_ref[0]
    u = _rms(x, fnw).astype(BF16)
    up = jnp.dot(u, wup_ref[...], preferred_element_type=F32)
    a = up[:, 0:D_FF]
    as_ref[SUBLANES:SUBLANES + T, :] = a
    a1 = as_ref[SUBLANES - 1:SUBLANES - 1 + T, :]
    a2 = as_ref[SUBLANES - 2:SUBLANES - 2 + T, :]
    ac = fcw_ref[0:1, :] * a2 + fcw_ref[1:2, :] * a1 + fcw_ref[2:3, :] * a + fcb_ref[...]
    as_ref[0:SUBLANES, :] = as_ref[T:T + SUBLANES, :]
    gated = (ac * _sigmoid(ac) * up[:, D_FF:]).astype(BF16)
    y = x + jnp.dot(gated, wdn_ref[:, 0:D_MODEL], preferred_element_type=F32)
    out_ref[0] = _rms(y, finw_ref[...])


def _resident(shape):
    return pl.BlockSpec(shape, lambda b, t: (0,) * len(shape), pipeline_mode=pl.Buffered(1))


def _weight(w):
    return jnp.pad(w.astype(BF16), ((0, 0), (0, WEIGHT_LANE_PAD)))


def _rsel_matrix():
    r = np.arange(N_HEADS * HEAD_DIM)[:, None] // HEAD_DIM
    l = np.arange(CHUNK)[None, :] % SUBLANES
    return jnp.asarray(r == l, dtype=BF16)


def kernel(x, meta_tokens, lb_param, attn_norm_w, w_in, hgrn_norm_w, conv_w, w_out, ffn_norm_w, w_up,
           ffn_conv_w, ffn_conv_b, w_down, final_norm_w):
    B, L, D = x.shape
    assert D == D_MODEL and L % MIXER_TILE == 0 and L % FFN_TILE == 0
    assert w_in.shape == (1, D, N_IN_SECTIONS * D) and w_up.shape == (1, D, 2 * D_FF)
    assert SUBLANES * HEAD_DIM == D_MODEL and N_HEADS == SUBLANES

    tile = lambda T: pl.BlockSpec((1, T, D), lambda b, t: (b, t, 0))
    meta_tile = pl.BlockSpec((1, N_META, D), lambda b, t: (b, 0, 0))
    params = pltpu.CompilerParams(dimension_semantics=("arbitrary", "arbitrary"),
                                  vmem_limit_bytes=VMEM_LIMIT_BYTES)

    TM = MIXER_TILE
    n_steps = B * (L // TM)
    step = lambda b, t: b * (L // TM) + t
    assert D % n_steps == 0 and D_FF % W_DOWN_ROWS_PER_STEP == 0 and D_FF // W_DOWN_ROWS_PER_STEP <= n_steps
    wup_rows = D // n_steps
    wup_block = pl.BlockSpec((wup_rows, 2 * D_FF), lambda b, t: (step(b, t), 0))
    wdn_index = lambda b, t: (jnp.minimum(step(b, t), D_FF // W_DOWN_ROWS_PER_STEP - 1), 0)
    slot = [pltpu.VMEM((TM, GROUP_WIDTH), F32)] * 3 + [pltpu.VMEM((TM, GROUP_WIDTH), BF16)] * 4 \
        + [pltpu.VMEM((TM + SUBLANES, GROUP_WIDTH), F32)]
    h1, h1m, w_up_bf16, w_down_bf16 = pl.pallas_call(
        _mixer_kernel,
        grid=(B, L // TM),
        in_specs=[tile(TM), _resident((N_META, D)), _resident((2, D)), _resident((1, D)),
                  _resident((D, N_IN_SECTIONS * D + WEIGHT_LANE_PAD)), _resident((1, HEAD_DIM)),
                  _resident((3, D)), _resident((D, D + WEIGHT_LANE_PAD)),
                  _resident((N_HEADS * HEAD_DIM, CHUNK)), wup_block,
                  pl.BlockSpec((W_DOWN_ROWS_PER_STEP, D), wdn_index)],
        out_specs=[tile(TM), meta_tile, wup_block,
                   pl.BlockSpec((W_DOWN_ROWS_PER_STEP, D + WEIGHT_LANE_PAD), wdn_index)],
        out_shape=[jax.ShapeDtypeStruct((B, L, D), F32), jax.ShapeDtypeStruct((B, N_META, D), F32),
                   jax.ShapeDtypeStruct((D, 2 * D_FF), BF16),
                   jax.ShapeDtypeStruct((D_FF, D + WEIGHT_LANE_PAD), BF16)],
        scratch_shapes=[pltpu.VMEM((N_HEADS, HEAD_DIM, HEAD_DIM), F32),
                        pltpu.VMEM((N_HEADS, HEAD_DIM, HEAD_DIM), F32),
                        pltpu.VMEM((SUBLANES, D), F32),
                        pltpu.VMEM((SUBLANES, D), F32),
                        pltpu.VMEM((TM, D), BF16),
                        pltpu.SMEM((N_GROUPS,), jnp.int32),
                        pltpu.VMEM((N_HEADS, HEAD_DIM, HEAD_DIM), F32),
                        pltpu.VMEM((SUBLANES, D), F32),
                        pltpu.VMEM((N_META, D), F32),
                        *(slot * N_GROUPS)],
        compiler_params=params,
        name="mixer",
    )(x, meta_tokens, lb_param, attn_norm_w, _weight(w_in[0]), hgrn_norm_w, conv_w[0],
      _weight(w_out[0]), _rsel_matrix(), w_up[0], w_down[0])

    TF = FFN_TILE
    out = pl.pallas_call(
        _ffn_kernel,
        grid=(B, L // TF),
        in_specs=[tile(TF), meta_tile, _resident((1, D)), _resident((D, 2 * D_FF)), _resident((3, D_FF)),
                  _resident((1, D_FF)), _resident((D_FF, D + WEIGHT_LANE_PAD)), _resident((1, D))],
        out_specs=tile(TF),
        out_shape=jax.ShapeDtypeStruct((B, L, D), F32),
        scratch_shapes=[pltpu.VMEM((TF + SUBLANES, D_FF), F32)],
        compiler_params=params,
        name="ffn",
    )(h1, h1m, ffn_norm_w, w_up_bf16, ffn_conv_w[0], ffn_conv_b, w_down_bf16, final_norm_w.reshape(1, D))
    return out
```

```python
import numpy as np
import jax
import jax.numpy as jnp
from jax import lax
from jax.experimental import pallas as pl
from jax.experimental.pallas import tpu as pltpu

D_MODEL = 1024
N_META = 16
N_HEADS = 8
HEAD_DIM = 128
D_FF = 2816
N_IN_SECTIONS = 9
N_RECURRENCE_SECTIONS = 3
EPS = 1e-6

SUBLANES = 8
CHUNK = 128
GROUP_HEADS = 8
GROUP_WIDTH = GROUP_HEADS * HEAD_DIM
N_GROUPS = N_HEADS // GROUP_HEADS
MAX_SAFE_EXPONENT = 60.0
LOG2_E = 1.4426950408889634
MIXER_TILE = 512
FFN_TILE = 512
WEIGHT_LANE_PAD = 128
W_DOWN_ROWS_PER_STEP = 128
V7X_VMEM_BYTES = 64 * 1024 * 1024
VMEM_LIMIT_BYTES = V7X_VMEM_BYTES - 4 * 1024 * 1024

F32 = jnp.float32
BF16 = jnp.bfloat16

_NT = (((1,), (1,)), ((), ()))
_TN = (((0,), (0,)), ((), ()))


def _rms(x, w):
    ms = jnp.mean(x * x, axis=-1, keepdims=True)
    return x * lax.rsqrt(ms + EPS) * w


def _sigmoid(x):
    return 1.0 / (1.0 + jnp.exp2(x * -LOG2_E))


def _bcast_rows(ref, lanes, row0, block, offset, nrows):
    pieces = [jnp.broadcast_to(_row(ref, lanes, row0, i * block + offset), (block, HEAD_DIM))
              for i in range(nrows // block)]
    return pieces[0] if len(pieces) == 1 else jnp.concatenate(pieces, axis=0)


def _row(ref, lanes, row0, r):
    group = ref[pl.ds(row0 + r // SUBLANES * SUBLANES, SUBLANES), lanes]
    return group[r % SUBLANES:r % SUBLANES + 1, :]


def _hgrn_chunk_exact(row0, lanes, valid, st, q_ref, g_ref, k_ref, v_ref, rsel):
    rows = pl.ds(row0, CHUNK)
    q = q_ref[rows, lanes]
    G = g_ref[rows, lanes]
    k = k_ref[rows, lanes]
    v = v_ref[rows, lanes]
    row = lax.broadcasted_iota(jnp.int32, (CHUNK, HEAD_DIM), 0)

    sub = row & (SUBLANES - 1)
    ps = []
    for j in range(SUBLANES):
        gj = _bcast_rows(g_ref, lanes, row0, SUBLANES, j, CHUNK)
        kj = _bcast_rows(k_ref, lanes, row0, SUBLANES, j, CHUNK)
        p = q * kj * jnp.exp2(G - gj)
        ps.append(jnp.where(sub >= j, p, 0.0).astype(BF16))
    pcat = jnp.concatenate(ps, axis=1)
    a = jnp.dot(pcat, rsel, preferred_element_type=F32)

    tl_xor = (lax.broadcasted_iota(jnp.int32, (CHUNK, CHUNK), 0)
              ^ lax.broadcasted_iota(jnp.int32, (CHUNK, CHUNK), 1))
    a = jnp.where(tl_xor < SUBLANES, a, 0.0)

    b = 2 * SUBLANES
    while b <= CHUNK:
        gm = _bcast_rows(g_ref, lanes, row0, b, b // 2 - 1, CHUNK)
        e = jnp.exp2(-jnp.abs(G - gm))
        upper = (row & (b // 2)) != 0
        qt = jnp.where(upper, q * e, 0.0).astype(BF16)
        kt = jnp.where(upper, 0.0, k * e).astype(BF16)
        ab = lax.dot_general(qt, kt, _NT, preferred_element_type=F32)
        a = a + (ab if b == CHUNK else jnp.where(tl_xor < b, ab, 0.0))
        b *= 2

    qi = (q * jnp.exp2(G)).astype(BF16)
    o = lax.dot_general(qi, st.astype(BF16), _NT, preferred_element_type=F32)
    o = o + jnp.dot(a.astype(BF16), v, preferred_element_type=F32)

    glast = _row(g_ref, lanes, row0, valid - 1)
    kd = k * jnp.exp2(glast - G)
    if valid < CHUNK:
        kd = jnp.where(row < valid, kd, 0.0)
    st = st * jnp.exp2(glast) + lax.dot_general(v, kd.astype(BF16), _TN, preferred_element_type=F32)
    return o, st


def _hgrn_chunk_fast(row0, lanes, st, q_ref, g_ref, k_ref, v_ref):
    rows = pl.ds(row0, CHUNK)
    q = q_ref[rows, lanes]
    G = g_ref[rows, lanes]
    k = k_ref[rows, lanes]
    v = v_ref[rows, lanes]
    gmid = g_ref[pl.ds(row0 + CHUNK // 2 - 1, 1), lanes]
    glast = g_ref[pl.ds(row0 + CHUNK - 1, 1), lanes]
    d = G - gmid
    qh = q * jnp.exp2(d)
    kh = k * jnp.exp2(-d)
    a = lax.dot_general(qh.astype(BF16), kh.astype(BF16), _NT, preferred_element_type=F32)
    causal = (lax.broadcasted_iota(jnp.int32, (CHUNK, CHUNK), 1)
              <= lax.broadcasted_iota(jnp.int32, (CHUNK, CHUNK), 0))
    a = jnp.where(causal, a, 0.0)
    qi = (qh * jnp.exp2(gmid)).astype(BF16)
    o = lax.dot_general(qi, st.astype(BF16), _NT, preferred_element_type=F32)
    o = o + jnp.dot(a.astype(BF16), v, preferred_element_type=F32)
    kd = (kh * jnp.exp2(glast - gmid)).astype(BF16)
    st = st * jnp.exp2(glast) + lax.dot_general(v, kd, _TN, preferred_element_type=F32)
    return o, st


def _project_sections(u, win_ref, c0, first, last):
    if GROUP_WIDTH == D_MODEL:
        p = jnp.dot(u, win_ref[:, first * D_MODEL:last * D_MODEL], preferred_element_type=F32)
        return [p[:, i * D_MODEL:(i + 1) * D_MODEL] for i in range(last - first)]
    return [jnp.dot(u, win_ref[:, i * D_MODEL + c0:i * D_MODEL + c0 + GROUP_WIDTH], preferred_element_type=F32)
            for i in range(first, last)]


def _project_group(u, g, R, valid, prm, slot, recurrence_inputs):
    lb_ref, win_ref, cw_ref, zc_ref, safe_ref = prm
    q_ref, g_ref, k_ref, v_ref, gs_ref, sa_ref, mb_ref, zs_ref = slot
    c0 = g * GROUP_WIDTH
    cols = slice(c0, c0 + GROUP_WIDTH)

    sec = lambda first, last: _project_sections(u, win_ref, c0, first, last)

    if not recurrence_inputs:
        _project_gates(sec(N_RECURRENCE_SECTIONS, N_IN_SECTIONS), cols, R, valid, cw_ref, zc_ref, slot)
        return

    qv, fv, iv = sec(0, N_RECURRENCE_SECTIONS)
    q_ref[0:R, :] = qv * _sigmoid(qv)

    f = lb_ref[0:1, cols] + lb_ref[1:2, cols] * _sigmoid(fv)
    k_ref[0:R, :] = 1.0 - f
    lf = jnp.log2(f)
    tri = (lax.broadcasted_iota(jnp.int32, (CHUNK, CHUNK), 1)
           <= lax.broadcasted_iota(jnp.int32, (CHUNK, CHUNK), 0)).astype(BF16)
    tri2 = jnp.concatenate([tri, tri], axis=1)
    worst = jnp.zeros((1, GROUP_WIDTH), F32)
    for c in range(R // CHUNK):
        lfc = lf[c * CHUNK:(c + 1) * CHUNK, :]
        hi = lfc.astype(BF16)
        lo = (lfc - hi.astype(F32)).astype(BF16)
        gc = jnp.dot(tri2, jnp.concatenate([hi, lo], axis=0),
                     preferred_element_type=F32)
        g_ref[c * CHUNK:(c + 1) * CHUNK, :] = gc
        gmid = gc[CHUNK // 2 - 1:CHUNK // 2, :]
        glast = gc[CHUNK - 1:CHUNK, :]
        worst = jnp.maximum(worst, jnp.maximum(-gmid, gmid - glast))
    safe_ref[g] = (jnp.max(worst) <= MAX_SAFE_EXPONENT * LOG2_E).astype(jnp.int32)

    v_ref[0:R, :] = iv.astype(BF16)


def _project_gates(sections, cols, R, valid, cw_ref, zc_ref, slot):
    gs_ref, sa_ref, mb_ref, zs_ref = slot[4:8]
    gv, bg, cg, hc, ga, gb = sections
    gs_ref[0:R, :] = (gv * _sigmoid(gv)).astype(BF16)

    z = cg * hc
    zs_ref[0:SUBLANES, :] = zc_ref[:, cols]
    zs_ref[SUBLANES:SUBLANES + R, :] = z
    z1 = zs_ref[SUBLANES - 1:SUBLANES - 1 + R, :]
    z2 = zs_ref[SUBLANES - 2:SUBLANES - 2 + R, :]
    yb = bg * (cw_ref[0:1, cols] * z2 + cw_ref[1:2, cols] * z1 + cw_ref[2:3, cols] * z)
    zc_ref[:, cols] = zs_ref[valid:valid + SUBLANES, :]

    sa_ref[0:R, :] = _sigmoid(ga).astype(BF16)
    mb_ref[0:R, :] = (_sigmoid(gb) * yb).astype(BF16)


def _finish_chunk(row0, h, lanes, o, hnw, mg_ref, slot):
    gs_ref, sa_ref, mb_ref = slot[4:7]
    rows = pl.ds(row0, CHUNK)
    m = (_rms(o, hnw) * gs_ref[rows, lanes].astype(F32) * sa_ref[rows, lanes].astype(F32)
         + mb_ref[rows, lanes].astype(F32))
    mg_ref[rows, h * HEAD_DIM:(h + 1) * HEAD_DIM] = m.astype(BF16)


def _recur_group_fast(g, R, hnw, st_ref, st_old_ref, mg_ref, slot):
    q_ref, g_ref, k_ref, v_ref = slot[0:4]
    for hh in range(GROUP_HEADS):
        h = GROUP_HEADS * g + hh
        lanes = slice(hh * HEAD_DIM, (hh + 1) * HEAD_DIM)
        st = st_ref[h]
        st_old_ref[h] = st
        for c in range(R // CHUNK):
            o, st = _hgrn_chunk_fast(c * CHUNK, lanes, st, q_ref, g_ref, k_ref, v_ref)
            mg_ref[c * CHUNK:(c + 1) * CHUNK, h * HEAD_DIM:(h + 1) * HEAD_DIM] = _rms(o, hnw).astype(BF16)
        st_ref[h] = st


def _finish_group_fast(g, R, mg_ref, slot):
    gs_ref, sa_ref, mb_ref = slot[4:7]
    cols = slice(g * GROUP_WIDTH, (g + 1) * GROUP_WIDTH)
    mg_ref[0:R, cols] = mg_ref[0:R, cols] * gs_ref[0:R, :] * sa_ref[0:R, :] + mb_ref[0:R, :]


def _recur_group_exact(g, R, valid, hnw, rsel, st_in_ref, st_ref, mg_ref, slot):
    q_ref, g_ref, k_ref, v_ref = slot[0:4]
    for hh in range(GROUP_HEADS):
        h = GROUP_HEADS * g + hh
        lanes = slice(hh * HEAD_DIM, (hh + 1) * HEAD_DIM)
        st = st_in_ref[h]
        if valid < R:
            o, st = _hgrn_chunk_exact(0, lanes, valid, st, q_ref, g_ref, k_ref, v_ref, rsel)
            _finish_chunk(0, h, lanes, o, hnw, mg_ref, slot)
        else:
            def chunk_body(c, st):
                row0 = pl.multiple_of(c * CHUNK, CHUNK)
                o, st = _hgrn_chunk_exact(row0, lanes, CHUNK, st, q_ref, g_ref, k_ref, v_ref, rsel)
                _finish_chunk(row0, h, lanes, o, hnw, mg_ref, slot)
                return st

            st = lax.fori_loop(0, R // CHUNK, chunk_body, st)
        st_ref[h] = st


def _out_partial(g, R, mg_ref, wout_ref):
    rows = slice(g * GROUP_WIDTH, (g + 1) * GROUP_WIDTH)
    return jnp.dot(mg_ref[0:R, rows], wout_ref[rows, 0:D_MODEL], preferred_element_type=F32)


def _mixer_tile(load_x, store_out, R, valid, is_meta, refs):
    (lbp_ref, anw_ref, win_ref, hnw_ref, cw_ref, wout_ref, rsel_ref,
     st_ref, st_old_ref, zc_ref, lb_ref, mg_ref, safe_ref, slots) = refs
    prm = (lb_ref, win_ref, cw_ref, zc_ref, safe_ref)
    u = _rms(load_x(), anw_ref[...]).astype(BF16)
    rsel = rsel_ref[...]
    hnw = hnw_ref[...]
    project = lambda g, first: _project_group(u, g, R, valid, prm, slots[g], first)

    lbp = lbp_ref[...]
    mx = jnp.max(lbp, axis=0, keepdims=True)
    ex = jnp.exp(lbp - mx)
    lb = ex[0:1, :] / jnp.sum(ex, axis=0, keepdims=True)
    lb_ref[0:1, :] = lb
    lb_ref[1:2, :] = 1.0 - lb

    if is_meta:
        for g in range(N_GROUPS):
            project(g, True)
            project(g, False)
            _recur_group_exact(g, R, valid, hnw, rsel, st_ref, st_ref, mg_ref, slots[g])
        acc = load_x()
        for g in range(N_GROUPS):
            acc = acc + _out_partial(g, R, mg_ref, wout_ref)
        store_out(acc, False)
        return

    def store_early():
        acc = load_x()
        for g in range(N_GROUPS - 1):
            acc = acc + _out_partial(g, R, mg_ref, wout_ref)
        store_out(acc, False)

    for g in range(N_GROUPS):
        project(g, True)
        if g > 0:
            _finish_group_fast(g - 1, R, mg_ref, slots[g - 1])
        if g == N_GROUPS - 1:
            store_early()
        _recur_group_fast(g, R, hnw, st_ref, st_old_ref, mg_ref, slots[g])
        project(g, False)
    _finish_group_fast(N_GROUPS - 1, R, mg_ref, slots[N_GROUPS - 1])
    for g in range(N_GROUPS):
        def redo(g=g):
            _recur_group_exact(g, R, valid, hnw, rsel, st_old_ref, st_ref, mg_ref, slots[g])
            if g < N_GROUPS - 1:
                store_early()

        pl.when(safe_ref[g] == 0)(redo)
    store_out(_out_partial(N_GROUPS - 1, R, mg_ref, wout_ref), True)


def _mixer_kernel(x_ref, meta_ref, lbp_ref, anw_ref, win_ref, hnw_ref, cw_ref, wout_ref, rsel_ref,
                  wup_f32_ref, wdn_f32_ref, h1_ref, h1m_ref, wup_ref, wdn_ref,
                  st_ref, st_old_ref, zc_ref, lb_ref, mg_ref, safe_ref,
                  st_meta_ref, zc_meta_ref, h1m_meta_ref, *slots):
    wup_ref[...] = wup_f32_ref[...].astype(BF16)
    wdn_ref[:, 0:D_MODEL] = wdn_f32_ref[...].astype(BF16)
    wdn_ref[:, D_MODEL:] = jnp.zeros((W_DOWN_ROWS_PER_STEP, WEIGHT_LANE_PAD), BF16)

    n = len(slots) // N_GROUPS
    refs = (lbp_ref, anw_ref, win_ref, hnw_ref, cw_ref, wout_ref, rsel_ref,
            st_ref, st_old_ref, zc_ref, lb_ref, mg_ref, safe_ref,
            [slots[i * n:(i + 1) * n] for i in range(N_GROUPS)])

    first_tile = pl.program_id(1) == 0

    @pl.when(jnp.logical_and(first_tile, pl.program_id(0) == 0))
    def _():
        st_ref[...] = jnp.zeros_like(st_ref)
        zc_ref[...] = jnp.zeros_like(zc_ref)
        load_meta = lambda: jnp.concatenate(
            [meta_ref[...], jnp.zeros((CHUNK - N_META, D_MODEL), F32)], axis=0)

        def store_meta(val, accumulate):
            h1m_meta_ref[...] = val[0:N_META, :]

        _mixer_tile(load_meta, store_meta, CHUNK, N_META, True, refs)
        st_meta_ref[...] = st_ref[...]
        zc_meta_ref[...] = zc_ref[...]

    @pl.when(first_tile)
    def _():
        st_ref[...] = st_meta_ref[...]
        zc_ref[...] = zc_meta_ref[...]
        h1m_ref[0] = h1m_meta_ref[...]

    def store_tile(val, accumulate):
        h1_ref[0] = h1_ref[0] + val if accumulate else val

    _mixer_tile(lambda: x_ref[0], store_tile, MIXER_TILE, MIXER_TILE, False, refs)


def _ffn_kernel(h1_ref, h1m_ref, fnw_ref, wup_ref, fcw_ref, fcb_ref, wdn_ref, finw_ref,
                out_ref, as_ref):
    T = FFN_TILE
    fnw = fnw_ref[...]

    @pl.when(pl.program_id(1) == 0)
    def _():
        um = _rms(h1m_ref[0], fnw).astype(BF16)
        am = jnp.dot(um, wup_ref[:, 0:D_FF], preferred_element_type=F32)
        as_ref[0:SUBLANES, :] = am[N_META - SUBLANES:N_META, :]

    x = h1_ref[0]
    u = _rms(x, fnw).astype(BF16)
    up = jnp.dot(u, wup_ref[...], preferred_element_type=F32)
    a = up[:, 0:D_FF]
    as_ref[SUBLANES:SUBLANES + T, :] = a
    a1 = as_ref[SUBLANES - 1:SUBLANES - 1 + T, :]
    a2 = as_ref[SUBLANES - 2:SUBLANES - 2 + T, :]
    ac = fcw_ref[0:1, :] * a2 + fcw_ref[1:2, :] * a1 + fcw_ref[2:3, :] * a + fcb_ref[...]
    as_ref[0:SUBLANES, :] = as_ref[T:T + SUBLANES, :]
    gated = (ac * _sigmoid(ac) * up[:, D_FF:]).astype(BF16)
    y = x + jnp.dot(gated, wdn_ref[:, 0:D_MODEL], preferred_element_type=F32)
    out_ref[0] = _rms(y, finw_ref[...])


def _resident(shape):
    return pl.BlockSpec(shape, lambda b, t: (0,) * len(shape), pipeline_mode=pl.Buffered(1))


def _weight(w):
    return jnp.pad(w.astype(BF16), ((0, 0), (0, WEIGHT_LANE_PAD)))


def _rsel_matrix():
    r = np.arange(N_HEADS * HEAD_DIM)[:, None] // HEAD_DIM
    l = np.arange(CHUNK)[None, :] % SUBLANES
    return jnp.asarray(r == l, dtype=BF16)


def kernel(x, meta_tokens, lb_param, attn_norm_w, w_in, hgrn_norm_w, conv_w, w_out, ffn_norm_w, w_up,
           ffn_conv_w, ffn_conv_b, w_down, final_norm_w):
    B, L, D = x.shape
    assert D == D_MODEL and L % MIXER_TILE == 0 and L % FFN_TILE == 0
    assert w_in.shape == (1, D, N_IN_SECTIONS * D) and w_up.shape == (1, D, 2 * D_FF)
    assert SUBLANES * HEAD_DIM == D_MODEL and N_HEADS == SUBLANES

    tile = lambda T: pl.BlockSpec((1, T, D), lambda b, t: (b, t, 0))
    meta_tile = pl.BlockSpec((1, N_META, D), lambda b, t: (b, 0, 0))
    params = pltpu.CompilerParams(dimension_semantics=("arbitrary", "arbitrary"),
                                  vmem_limit_bytes=VMEM_LIMIT_BYTES)

    TM = MIXER_TILE
    n_steps = B * (L // TM)
    step = lambda b, t: b * (L // TM) + t
    assert D % n_steps == 0 and D_FF % W_DOWN_ROWS_PER_STEP == 0 and D_FF // W_DOWN_ROWS_PER_STEP <= n_steps
    wup_rows = D // n_steps
    wup_block = pl.BlockSpec((wup_rows, 2 * D_FF), lambda b, t: (step(b, t), 0))
    wdn_index = lambda b, t: (jnp.minimum(step(b, t), D_FF // W_DOWN_ROWS_PER_STEP - 1), 0)
    slot = [pltpu.VMEM((TM, GROUP_WIDTH), F32)] * 3 + [pltpu.VMEM((TM, GROUP_WIDTH), BF16)] * 4 \
        + [pltpu.VMEM((TM + SUBLANES, GROUP_WIDTH), F32)]
    h1, h1m, w_up_bf16, w_down_bf16 = pl.pallas_call(
        _mixer_kernel,
        grid=(B, L // TM),
        in_specs=[tile(TM), _resident((N_META, D)), _resident((2, D)), _resident((1, D)),
                  _resident((D, N_IN_SECTIONS * D + WEIGHT_LANE_PAD)), _resident((1, HEAD_DIM)),
                  _resident((3, D)), _resident((D, D + WEIGHT_LANE_PAD)),
                  _resident((N_HEADS * HEAD_DIM, CHUNK)), wup_block,
                  pl.BlockSpec((W_DOWN_ROWS_PER_STEP, D), wdn_index)],
        out_specs=[tile(TM), meta_tile, wup_block,
                   pl.BlockSpec((W_DOWN_ROWS_PER_STEP, D + WEIGHT_LANE_PAD), wdn_index)],
        out_shape=[jax.ShapeDtypeStruct((B, L, D), F32), jax.ShapeDtypeStruct((B, N_META, D), F32),
                   jax.ShapeDtypeStruct((D, 2 * D_FF), BF16),
                   jax.ShapeDtypeStruct((D_FF, D + WEIGHT_LANE_PAD), BF16)],
        scratch_shapes=[pltpu.VMEM((N_HEADS, HEAD_DIM, HEAD_DIM), F32),
                        pltpu.VMEM((N_HEADS, HEAD_DIM, HEAD_DIM), F32),
                        pltpu.VMEM((SUBLANES, D), F32),
                        pltpu.VMEM((SUBLANES, D), F32),
                        pltpu.VMEM((TM, D), BF16),
                        pltpu.SMEM((N_GROUPS,), jnp.int32),
                        pltpu.VMEM((N_HEADS, HEAD_DIM, HEAD_DIM), F32),
                        pltpu.VMEM((SUBLANES, D), F32),
                        pltpu.VMEM((N_META, D), F32),
                        *(slot * N_GROUPS)],
        compiler_params=params,
        name="mixer",
    )(x, meta_tokens, lb_param, attn_norm_w, _weight(w_in[0]), hgrn_norm_w, conv_w[0],
      _weight(w_out[0]), _rsel_matrix(), w_up[0], w_down[0])

    TF = FFN_TILE
    out = pl.pallas_call(
        _ffn_kernel,
        grid=(B, L // TF),
        in_specs=[tile(TF), meta_tile, _resident((1, D)), _resident((D, 2 * D_FF)), _resident((3, D_FF)),
                  _resident((1, D_FF)), _resident((D_FF, D + WEIGHT_LANE_PAD)), _resident((1, D))],
        out_specs=tile(TF),
        out_shape=jax.ShapeDtypeStruct((B, L, D), F32),
        scratch_shapes=[pltpu.VMEM((TF + SUBLANES, D_FF), F32)],
        compiler_params=params,
        name="ffn",
    )(h1, h1m, ffn_norm_w, w_up_bf16, ffn_conv_w[0], ffn_conv_b, w_down_bf16, final_norm_w.reshape(1, D))
    return out
```

```python
import numpy as np
import jax
import jax.numpy as jnp
from jax import lax
from jax.experimental import pallas as pl
from jax.experimental.pallas import tpu as pltpu

D_MODEL = 1024
N_META = 16
N_HEADS = 8
HEAD_DIM = 128
D_FF = 2816
N_IN_SECTIONS = 9
N_RECURRENCE_SECTIONS = 3
EPS = 1e-6

SUBLANES = 8
CHUNK = 128
GROUP_HEADS = 8
GROUP_WIDTH = GROUP_HEADS * HEAD_DIM
N_GROUPS = N_HEADS // GROUP_HEADS
MAX_SAFE_EXPONENT = 60.0
LOG2_E = 1.4426950408889634
MIXER_TILE = 512
FFN_TILE = 512
WEIGHT_LANE_PAD = 128
WEIGHT_CONVERT_ROWS = 128
W_DOWN_ROWS_PER_STEP = 128
V7X_VMEM_BYTES = 64 * 1024 * 1024
VMEM_LIMIT_BYTES = V7X_VMEM_BYTES - 4 * 1024 * 1024

F32 = jnp.float32
BF16 = jnp.bfloat16

_NT = (((1,), (1,)), ((), ()))
_TN = (((0,), (0,)), ((), ()))


def _rms(x, w):
    ms = jnp.mean(x * x, axis=-1, keepdims=True)
    return x * lax.rsqrt(ms + EPS) * w


def _sigmoid(x):
    return 1.0 / (1.0 + jnp.exp2(x * -LOG2_E))


def _bcast_rows(ref, lanes, row0, block, offset, nrows):
    pieces = [jnp.broadcast_to(_row(ref, lanes, row0, i * block + offset), (block, HEAD_DIM))
              for i in range(nrows // block)]
    return pieces[0] if len(pieces) == 1 else jnp.concatenate(pieces, axis=0)


def _row(ref, lanes, row0, r):
    group = ref[pl.ds(row0 + r // SUBLANES * SUBLANES, SUBLANES), lanes]
    return group[r % SUBLANES:r % SUBLANES + 1, :]


def _hgrn_chunk_exact(row0, lanes, valid, st, q_ref, g_ref, k_ref, v_ref, rsel):
    rows = pl.ds(row0, CHUNK)
    q = q_ref[rows, lanes]
    G = g_ref[rows, lanes]
    k = k_ref[rows, lanes]
    v = v_ref[rows, lanes]
    row = lax.broadcasted_iota(jnp.int32, (CHUNK, HEAD_DIM), 0)

    sub = row & (SUBLANES - 1)
    ps = []
    for j in range(SUBLANES):
        gj = _bcast_rows(g_ref, lanes, row0, SUBLANES, j, CHUNK)
        kj = _bcast_rows(k_ref, lanes, row0, SUBLANES, j, CHUNK)
        p = q * kj * jnp.exp2(G - gj)
        ps.append(jnp.where(sub >= j, p, 0.0).astype(BF16))
    pcat = jnp.concatenate(ps, axis=1)
    a = jnp.dot(pcat, rsel, preferred_element_type=F32)

    tl_xor = (lax.broadcasted_iota(jnp.int32, (CHUNK, CHUNK), 0)
              ^ lax.broadcasted_iota(jnp.int32, (CHUNK, CHUNK), 1))
    a = jnp.where(tl_xor < SUBLANES, a, 0.0)

    b = 2 * SUBLANES
    while b <= CHUNK:
        gm = _bcast_rows(g_ref, lanes, row0, b, b // 2 - 1, CHUNK)
        e = jnp.exp2(-jnp.abs(G - gm))
        upper = (row & (b // 2)) != 0
        qt = jnp.where(upper, q * e, 0.0).astype(BF16)
        kt = jnp.where(upper, 0.0, k * e).astype(BF16)
        ab = lax.dot_general(qt, kt, _NT, preferred_element_type=F32)
        a = a + (ab if b == CHUNK else jnp.where(tl_xor < b, ab, 0.0))
        b *= 2

    qi = (q * jnp.exp2(G)).astype(BF16)
    o = lax.dot_general(qi, st.astype(BF16), _NT, preferred_element_type=F32)
    o = o + jnp.dot(a.astype(BF16), v, preferred_element_type=F32)

    glast = _row(g_ref, lanes, row0, valid - 1)
    kd = k * jnp.exp2(glast - G)
    if valid < CHUNK:
        kd = jnp.where(row < valid, kd, 0.0)
    st = st * jnp.exp2(glast) + lax.dot_general(v, kd.astype(BF16), _TN, preferred_element_type=F32)
    return o, st


def _hgrn_chunk_fast(row0, lanes, st, q_ref, g_ref, k_ref, v_ref):
    rows = pl.ds(row0, CHUNK)
    q = q_ref[rows, lanes]
    G = g_ref[rows, lanes]
    k = k_ref[rows, lanes]
    v = v_ref[rows, lanes]
    gmid = g_ref[pl.ds(row0 + CHUNK // 2 - 1, 1), lanes]
    glast = g_ref[pl.ds(row0 + CHUNK - 1, 1), lanes]
    d = G - gmid
    qh = q * jnp.exp2(d)
    kh = k * jnp.exp2(-d)
    a = lax.dot_general(qh.astype(BF16), kh.astype(BF16), _NT, preferred_element_type=F32)
    causal = (lax.broadcasted_iota(jnp.int32, (CHUNK, CHUNK), 1)
              <= lax.broadcasted_iota(jnp.int32, (CHUNK, CHUNK), 0))
    a = jnp.where(causal, a, 0.0)
    qi = (qh * jnp.exp2(gmid)).astype(BF16)
    o = lax.dot_general(qi, st.astype(BF16), _NT, preferred_element_type=F32)
    o = o + jnp.dot(a.astype(BF16), v, preferred_element_type=F32)
    kd = (kh * jnp.exp2(glast - gmid)).astype(BF16)
    st = st * jnp.exp2(glast) + lax.dot_general(v, kd, _TN, preferred_element_type=F32)
    return o, st


def _project_sections(u, win_ref, c0, first, last):
    if GROUP_WIDTH == D_MODEL:
        p = jnp.dot(u, win_ref[:, first * D_MODEL:last * D_MODEL], preferred_element_type=F32)
        return [p[:, i * D_MODEL:(i + 1) * D_MODEL] for i in range(last - first)]
    return [jnp.dot(u, win_ref[:, i * D_MODEL + c0:i * D_MODEL + c0 + GROUP_WIDTH], preferred_element_type=F32)
            for i in range(first, last)]


def _project_group(u, g, R, valid, prm, slot, recurrence_inputs):
    lb_ref, win_ref, cw_ref, zc_ref, safe_ref = prm
    q_ref, g_ref, k_ref, v_ref, gs_ref, sa_ref, mb_ref, zs_ref = slot
    c0 = g * GROUP_WIDTH
    cols = slice(c0, c0 + GROUP_WIDTH)

    sec = lambda first, last: _project_sections(u, win_ref, c0, first, last)

    if not recurrence_inputs:
        _project_gates(sec(N_RECURRENCE_SECTIONS, N_IN_SECTIONS), cols, R, valid, cw_ref, zc_ref, slot)
        return

    qv, fv, iv = sec(0, N_RECURRENCE_SECTIONS)
    q_ref[0:R, :] = qv * _sigmoid(qv)

    f = lb_ref[0:1, cols] + lb_ref[1:2, cols] * _sigmoid(fv)
    k_ref[0:R, :] = 1.0 - f
    lf = jnp.log2(f)
    tri = (lax.broadcasted_iota(jnp.int32, (CHUNK, CHUNK), 1)
           <= lax.broadcasted_iota(jnp.int32, (CHUNK, CHUNK), 0)).astype(BF16)
    tri2 = jnp.concatenate([tri, tri], axis=1)
    worst = jnp.zeros((1, GROUP_WIDTH), F32)
    for c in range(R // CHUNK):
        lfc = lf[c * CHUNK:(c + 1) * CHUNK, :]
        hi = lfc.astype(BF16)
        lo = (lfc - hi.astype(F32)).astype(BF16)
        gc = jnp.dot(tri2, jnp.concatenate([hi, lo], axis=0),
                     preferred_element_type=F32)
        g_ref[c * CHUNK:(c + 1) * CHUNK, :] = gc
        gmid = gc[CHUNK // 2 - 1:CHUNK // 2, :]
        glast = gc[CHUNK - 1:CHUNK, :]
        worst = jnp.maximum(worst, jnp.maximum(-gmid, gmid - glast))
    safe_ref[g] = (jnp.max(worst) <= MAX_SAFE_EXPONENT * LOG2_E).astype(jnp.int32)

    v_ref[0:R, :] = iv.astype(BF16)


def _project_gates(sections, cols, R, valid, cw_ref, zc_ref, slot):
    gs_ref, sa_ref, mb_ref, zs_ref = slot[4:8]
    gv, bg, cg, hc, ga, gb = sections
    gs_ref[0:R, :] = (gv * _sigmoid(gv)).astype(BF16)

    z = cg * hc
    zs_ref[0:SUBLANES, :] = zc_ref[:, cols]
    zs_ref[SUBLANES:SUBLANES + R, :] = z
    z1 = zs_ref[SUBLANES - 1:SUBLANES - 1 + R, :]
    z2 = zs_ref[SUBLANES - 2:SUBLANES - 2 + R, :]
    yb = bg * (cw_ref[0:1, cols] * z2 + cw_ref[1:2, cols] * z1 + cw_ref[2:3, cols] * z)
    zc_ref[:, cols] = zs_ref[valid:valid + SUBLANES, :]

    sa_ref[0:R, :] = _sigmoid(ga).astype(BF16)
    mb_ref[0:R, :] = (_sigmoid(gb) * yb).astype(BF16)


def _finish_chunk(row0, h, lanes, o, hnw, mg_ref, slot):
    gs_ref, sa_ref, mb_ref = slot[4:7]
    rows = pl.ds(row0, CHUNK)
    m = (_rms(o, hnw) * gs_ref[rows, lanes].astype(F32) * sa_ref[rows, lanes].astype(F32)
         + mb_ref[rows, lanes].astype(F32))
    mg_ref[rows, h * HEAD_DIM:(h + 1) * HEAD_DIM] = m.astype(BF16)


def _recur_group_fast(g, R, hnw, st_ref, st_old_ref, mg_ref, slot):
    q_ref, g_ref, k_ref, v_ref = slot[0:4]
    for hh in range(GROUP_HEADS):
        h = GROUP_HEADS * g + hh
        lanes = slice(hh * HEAD_DIM, (hh + 1) * HEAD_DIM)
        st = st_ref[h]
        st_old_ref[h] = st
        for c in range(R // CHUNK):
            o, st = _hgrn_chunk_fast(c * CHUNK, lanes, st, q_ref, g_ref, k_ref, v_ref)
            mg_ref[c * CHUNK:(c + 1) * CHUNK, h * HEAD_DIM:(h + 1) * HEAD_DIM] = _rms(o, hnw).astype(BF16)
        st_ref[h] = st


def _finish_group_fast(g, R, mg_ref, slot):
    gs_ref, sa_ref, mb_ref = slot[4:7]
    cols = slice(g * GROUP_WIDTH, (g + 1) * GROUP_WIDTH)
    mg_ref[0:R, cols] = mg_ref[0:R, cols] * gs_ref[0:R, :] * sa_ref[0:R, :] + mb_ref[0:R, :]


def _recur_group_exact(g, R, valid, hnw, rsel, st_in_ref, st_ref, mg_ref, slot):
    q_ref, g_ref, k_ref, v_ref = slot[0:4]
    for hh in range(GROUP_HEADS):
        h = GROUP_HEADS * g + hh
        lanes = slice(hh * HEAD_DIM, (hh + 1) * HEAD_DIM)
        st = st_in_ref[h]
        if valid < R:
            o, st = _hgrn_chunk_exact(0, lanes, valid, st, q_ref, g_ref, k_ref, v_ref, rsel)
            _finish_chunk(0, h, lanes, o, hnw, mg_ref, slot)
        else:
            def chunk_body(c, st):
                row0 = pl.multiple_of(c * CHUNK, CHUNK)
                o, st = _hgrn_chunk_exact(row0, lanes, CHUNK, st, q_ref, g_ref, k_ref, v_ref, rsel)
                _finish_chunk(row0, h, lanes, o, hnw, mg_ref, slot)
                return st

            st = lax.fori_loop(0, R // CHUNK, chunk_body, st)
        st_ref[h] = st


def _out_partial(g, R, mg_ref, wout_ref):
    rows = slice(g * GROUP_WIDTH, (g + 1) * GROUP_WIDTH)
    return jnp.dot(mg_ref[0:R, rows], wout_ref[rows, 0:D_MODEL], preferred_element_type=F32)


def _mixer_tile(load_x, store_out, R, valid, is_meta, refs):
    (lbp_ref, anw_ref, win_ref, hnw_ref, cw_ref, wout_ref, rsel_ref,
     st_ref, st_old_ref, zc_ref, lb_ref, mg_ref, safe_ref, slots) = refs
    prm = (lb_ref, win_ref, cw_ref, zc_ref, safe_ref)
    u = _rms(load_x(), anw_ref[...]).astype(BF16)
    rsel = rsel_ref[...]
    hnw = hnw_ref[...]
    project = lambda g, first: _project_group(u, g, R, valid, prm, slots[g], first)

    lbp = lbp_ref[...]
    mx = jnp.max(lbp, axis=0, keepdims=True)
    ex = jnp.exp(lbp - mx)
    lb = ex[0:1, :] / jnp.sum(ex, axis=0, keepdims=True)
    lb_ref[0:1, :] = lb
    lb_ref[1:2, :] = 1.0 - lb

    if is_meta:
        for g in range(N_GROUPS):
            project(g, True)
            project(g, False)
            _recur_group_exact(g, R, valid, hnw, rsel, st_ref, st_ref, mg_ref, slots[g])
        acc = load_x()
        for g in range(N_GROUPS):
            acc = acc + _out_partial(g, R, mg_ref, wout_ref)
        store_out(acc, False)
        return

    def store_early():
        acc = load_x()
        for g in range(N_GROUPS - 1):
            acc = acc + _out_partial(g, R, mg_ref, wout_ref)
        store_out(acc, False)

    for g in range(N_GROUPS):
        project(g, True)
        if g > 0:
            _finish_group_fast(g - 1, R, mg_ref, slots[g - 1])
        if g == N_GROUPS - 1:
            store_early()
        _recur_group_fast(g, R, hnw, st_ref, st_old_ref, mg_ref, slots[g])
        project(g, False)
    _finish_group_fast(N_GROUPS - 1, R, mg_ref, slots[N_GROUPS - 1])
    for g in range(N_GROUPS):
        def redo(g=g):
            _recur_group_exact(g, R, valid, hnw, rsel, st_old_ref, st_ref, mg_ref, slots[g])
            if g < N_GROUPS - 1:
                store_early()

        pl.when(safe_ref[g] == 0)(redo)
    store_out(_out_partial(N_GROUPS - 1, R, mg_ref, wout_ref), True)


def _mixer_kernel(x_ref, meta_ref, lbp_ref, anw_ref, win_ref, hnw_ref, cw_ref, wout_ref, rsel_ref,
                  wup_f32_ref, wdn_f32_ref, h1_ref, h1m_ref, wup_ref, wdn_ref,
                  st_ref, st_old_ref, zc_ref, lb_ref, mg_ref, safe_ref,
                  st_meta_ref, zc_meta_ref, h1m_meta_ref, *slots):
    wup_ref[...] = wup_f32_ref[...].astype(BF16)
    wdn_ref[:, 0:D_MODEL] = wdn_f32_ref[...].astype(BF16)
    wdn_ref[:, D_MODEL:] = jnp.zeros((W_DOWN_ROWS_PER_STEP, WEIGHT_LANE_PAD), BF16)

    n = len(slots) // N_GROUPS
    refs = (lbp_ref, anw_ref, win_ref, hnw_ref, cw_ref, wout_ref, rsel_ref,
            st_ref, st_old_ref, zc_ref, lb_ref, mg_ref, safe_ref,
            [slots[i * n:(i + 1) * n] for i in range(N_GROUPS)])

    first_tile = pl.program_id(1) == 0

    @pl.when(jnp.logical_and(first_tile, pl.program_id(0) == 0))
    def _():
        st_ref[...] = jnp.zeros_like(st_ref)
        zc_ref[...] = jnp.zeros_like(zc_ref)
        load_meta = lambda: jnp.concatenate(
            [meta_ref[...], jnp.zeros((CHUNK - N_META, D_MODEL), F32)], axis=0)

        def store_meta(val, accumulate):
            h1m_meta_ref[...] = val[0:N_META, :]

        _mixer_tile(load_meta, store_meta, CHUNK, N_META, True, refs)
        st_meta_ref[...] = st_ref[...]
        zc_meta_ref[...] = zc_ref[...]

    @pl.when(first_tile)
    def _():
        st_ref[...] = st_meta_ref[...]
        zc_ref[...] = zc_meta_ref[...]
        h1m_ref[0] = h1m_meta_ref[...]

    def store_tile(val, accumulate):
        h1_ref[0] = h1_ref[0] + val if accumulate else val

    _mixer_tile(lambda: x_ref[0], store_tile, MIXER_TILE, MIXER_TILE, False, refs)


def _ffn_kernel(h1_ref, h1m_ref, fnw_ref, wup_ref, fcw_ref, fcb_ref, wdn_ref, finw_ref,
                out_ref, as_ref):
    T = FFN_TILE
    fnw = fnw_ref[...]

    @pl.when(pl.program_id(1) == 0)
    def _():
        um = _rms(h1m_ref[0], fnw).astype(BF16)
        am = jnp.dot(um, wup_ref[:, 0:D_FF], preferred_element_type=F32)
        as_ref[0:SUBLANES, :] = am[N_META - SUBLANES:N_META, :]

    x = h1_ref[0]
    u = _rms(x, fnw).astype(BF16)
    up = jnp.dot(u, wup_ref[...], preferred_element_type=F32)
    a = up[:, 0:D_FF]
    as_ref[SUBLANES:SUBLANES + T, :] = a
    a1 = as_ref[SUBLANES - 1:SUBLANES - 1 + T, :]
    a2 = as_ref[SUBLANES - 2:SUBLANES - 2 + T, :]
    ac = fcw_ref[0:1, :] * a2 + fcw_ref[1:2, :] * a1 + fcw_ref[2:3, :] * a + fcb_ref[...]
    as_ref[0:SUBLANES, :] = as_ref[T:T + SUBLANES, :]
    gated = (ac * _sigmoid(ac) * up[:, D_FF:]).astype(BF16)
    y = x + jnp.dot(gated, wdn_ref[:, 0:D_MODEL], preferred_element_type=F32)
    out_ref[0] = _rms(y, finw_ref[...])


def _resident(shape):
    return pl.BlockSpec(shape, lambda b, t: (0,) * len(shape), pipeline_mode=pl.Buffered(1))


def _to_bf16_kernel(w_ref, o_ref):
    width = w_ref.shape[1]
    o_ref[:, 0:width] = w_ref[...].astype(BF16)
    o_ref[:, width:] = jnp.zeros((o_ref.shape[0], o_ref.shape[1] - width), BF16)


def _weight(w):
    rows, width = w.shape
    assert rows % WEIGHT_CONVERT_ROWS == 0
    return pl.pallas_call(
        _to_bf16_kernel,
        grid=(rows // WEIGHT_CONVERT_ROWS,),
        in_specs=[pl.BlockSpec((WEIGHT_CONVERT_ROWS, width), lambda i: (i, 0))],
        out_specs=pl.BlockSpec((WEIGHT_CONVERT_ROWS, width + WEIGHT_LANE_PAD), lambda i: (i, 0)),
        out_shape=jax.ShapeDtypeStruct((rows, width + WEIGHT_LANE_PAD), BF16),
        name="to_bf16",
    )(w)


def _rsel_matrix():
    r = np.arange(N_HEADS * HEAD_DIM)[:, None] // HEAD_DIM
    l = np.arange(CHUNK)[None, :] % SUBLANES
    return jnp.asarray(r == l, dtype=BF16)


def kernel(x, meta_tokens, lb_param, attn_norm_w, w_in, hgrn_norm_w, conv_w, w_out, ffn_norm_w, w_up,
           ffn_conv_w, ffn_conv_b, w_down, final_norm_w):
    B, L, D = x.shape
    assert D == D_MODEL and L % MIXER_TILE == 0 and L % FFN_TILE == 0
    assert w_in.shape == (1, D, N_IN_SECTIONS * D) and w_up.shape == (1, D, 2 * D_FF)
    assert SUBLANES * HEAD_DIM == D_MODEL and N_HEADS == SUBLANES

    tile = lambda T: pl.BlockSpec((1, T, D), lambda b, t: (b, t, 0))
    meta_tile = pl.BlockSpec((1, N_META, D), lambda b, t: (b, 0, 0))
    params = pltpu.CompilerParams(dimension_semantics=("arbitrary", "arbitrary"),
                                  vmem_limit_bytes=VMEM_LIMIT_BYTES)

    TM = MIXER_TILE
    n_steps = B * (L // TM)
    step = lambda b, t: b * (L // TM) + t
    assert D % n_steps == 0 and D_FF % W_DOWN_ROWS_PER_STEP == 0 and D_FF // W_DOWN_ROWS_PER_STEP <= n_steps
    wup_rows = D // n_steps
    wup_block = pl.BlockSpec((wup_rows, 2 * D_FF), lambda b, t: (step(b, t), 0))
    wdn_index = lambda b, t: (jnp.minimum(step(b, t), D_FF // W_DOWN_ROWS_PER_STEP - 1), 0)
    slot = [pltpu.VMEM((TM, GROUP_WIDTH), F32)] * 3 + [pltpu.VMEM((TM, GROUP_WIDTH), BF16)] * 4 \
        + [pltpu.VMEM((TM + SUBLANES, GROUP_WIDTH), F32)]
    h1, h1m, w_up_bf16, w_down_bf16 = pl.pallas_call(
        _mixer_kernel,
        grid=(B, L // TM),
        in_specs=[tile(TM), _resident((N_META, D)), _resident((2, D)), _resident((1, D)),
                  _resident((D, N_IN_SECTIONS * D + WEIGHT_LANE_PAD)), _resident((1, HEAD_DIM)),
                  _resident((3, D)), _resident((D, D + WEIGHT_LANE_PAD)),
                  _resident((N_HEADS * HEAD_DIM, CHUNK)), wup_block,
                  pl.BlockSpec((W_DOWN_ROWS_PER_STEP, D), wdn_index)],
        out_specs=[tile(TM), meta_tile, wup_block,
                   pl.BlockSpec((W_DOWN_ROWS_PER_STEP, D + WEIGHT_LANE_PAD), wdn_index)],
        out_shape=[jax.ShapeDtypeStruct((B, L, D), F32), jax.ShapeDtypeStruct((B, N_META, D), F32),
                   jax.ShapeDtypeStruct((D, 2 * D_FF), BF16),
                   jax.ShapeDtypeStruct((D_FF, D + WEIGHT_LANE_PAD), BF16)],
        scratch_shapes=[pltpu.VMEM((N_HEADS, HEAD_DIM, HEAD_DIM), F32),
                        pltpu.VMEM((N_HEADS, HEAD_DIM, HEAD_DIM), F32),
                        pltpu.VMEM((SUBLANES, D), F32),
                        pltpu.VMEM((SUBLANES, D), F32),
                        pltpu.VMEM((TM, D), BF16),
                        pltpu.SMEM((N_GROUPS,), jnp.int32),
                        pltpu.VMEM((N_HEADS, HEAD_DIM, HEAD_DIM), F32),
                        pltpu.VMEM((SUBLANES, D), F32),
                        pltpu.VMEM((N_META, D), F32),
                        *(slot * N_GROUPS)],
        compiler_params=params,
        name="mixer",
    )(x, meta_tokens, lb_param, attn_norm_w, _weight(w_in[0]), hgrn_norm_w, conv_w[0],
      _weight(w_out[0]), _rsel_matrix(), w_up[0], w_down[0])

    TF = FFN_TILE
    out = pl.pallas_call(
        _ffn_kernel,
        grid=(B, L // TF),
        in_specs=[tile(TF), meta_tile, _resident((1, D)), _resident((D, 2 * D_FF)), _resident((3, D_FF)),
                  _resident((1, D_FF)), _resident((D_FF, D + WEIGHT_LANE_PAD)), _resident((1, D))],
        out_specs=tile(TF),
        out_shape=jax.ShapeDtypeStruct((B, L, D), F32),
        scratch_shapes=[pltpu.VMEM((TF + SUBLANES, D_FF), F32)],
        compiler_params=params,
        name="ffn",
    )(h1, h1m, ffn_norm_w, w_up_bf16, ffn_conv_w[0], ffn_conv_b, w_down_bf16, final_norm_w.reshape(1, D))
    return out
```

```python
import numpy as np
import jax
import jax.numpy as jnp
from jax import lax
from jax.experimental import pallas as pl
from jax.experimental.pallas import tpu as pltpu

D_MODEL = 1024
N_META = 16
N_HEADS = 8
HEAD_DIM = 128
D_FF = 2816
N_IN_SECTIONS = 9
N_RECURRENCE_SECTIONS = 3
EPS = 1e-6

SUBLANES = 8
CHUNK = 128
GROUP_HEADS = 8
GROUP_WIDTH = GROUP_HEADS * HEAD_DIM
N_GROUPS = N_HEADS // GROUP_HEADS
MAX_SAFE_EXPONENT = 60.0
LOG2_E = 1.4426950408889634
MIXER_TILE = 512
FFN_TILE = 512
WEIGHT_LANE_PAD = 128
W_DOWN_ROWS_PER_STEP = 128
V7X_VMEM_BYTES = 64 * 1024 * 1024
VMEM_LIMIT_BYTES = V7X_VMEM_BYTES - 2 * 1024 * 1024

F32 = jnp.float32
BF16 = jnp.bfloat16

_NT = (((1,), (1,)), ((), ()))
_TN = (((0,), (0,)), ((), ()))


def _rms(x, w):
    ms = jnp.mean(x * x, axis=-1, keepdims=True)
    return x * lax.rsqrt(ms + EPS) * w


def _sigmoid(x):
    return 1.0 / (1.0 + jnp.exp2(x * -LOG2_E))


def _bcast_rows(ref, lanes, row0, block, offset, nrows):
    pieces = [jnp.broadcast_to(_row(ref, lanes, row0, i * block + offset), (block, HEAD_DIM))
              for i in range(nrows // block)]
    return pieces[0] if len(pieces) == 1 else jnp.concatenate(pieces, axis=0)


def _row(ref, lanes, row0, r):
    group = ref[pl.ds(row0 + r // SUBLANES * SUBLANES, SUBLANES), lanes]
    return group[r % SUBLANES:r % SUBLANES + 1, :]


def _hgrn_chunk_exact(row0, lanes, valid, st, q_ref, g_ref, k_ref, v_ref, rsel):
    rows = pl.ds(row0, CHUNK)
    q = q_ref[rows, lanes]
    G = g_ref[rows, lanes]
    k = k_ref[rows, lanes]
    v = v_ref[rows, lanes]
    row = lax.broadcasted_iota(jnp.int32, (CHUNK, HEAD_DIM), 0)

    sub = row & (SUBLANES - 1)
    ps = []
    for j in range(SUBLANES):
        gj = _bcast_rows(g_ref, lanes, row0, SUBLANES, j, CHUNK)
        kj = _bcast_rows(k_ref, lanes, row0, SUBLANES, j, CHUNK)
        p = q * kj * jnp.exp2(G - gj)
        ps.append(jnp.where(sub >= j, p, 0.0).astype(BF16))
    pcat = jnp.concatenate(ps, axis=1)
    a = jnp.dot(pcat, rsel, preferred_element_type=F32)

    tl_xor = (lax.broadcasted_iota(jnp.int32, (CHUNK, CHUNK), 0)
              ^ lax.broadcasted_iota(jnp.int32, (CHUNK, CHUNK), 1))
    a = jnp.where(tl_xor < SUBLANES, a, 0.0)

    b = 2 * SUBLANES
    while b <= CHUNK:
        gm = _bcast_rows(g_ref, lanes, row0, b, b // 2 - 1, CHUNK)
        e = jnp.exp2(-jnp.abs(G - gm))
        upper = (row & (b // 2)) != 0
        qt = jnp.where(upper, q * e, 0.0).astype(BF16)
        kt = jnp.where(upper, 0.0, k * e).astype(BF16)
        ab = lax.dot_general(qt, kt, _NT, preferred_element_type=F32)
        a = a + (ab if b == CHUNK else jnp.where(tl_xor < b, ab, 0.0))
        b *= 2

    qi = (q * jnp.exp2(G)).astype(BF16)
    o = lax.dot_general(qi, st.astype(BF16), _NT, preferred_element_type=F32)
    o = o + jnp.dot(a.astype(BF16), v, preferred_element_type=F32)

    glast = _row(g_ref, lanes, row0, valid - 1)
    kd = k * jnp.exp2(glast - G)
    if valid < CHUNK:
        kd = jnp.where(row < valid, kd, 0.0)
    st = st * jnp.exp2(glast) + lax.dot_general(v, kd.astype(BF16), _TN, preferred_element_type=F32)
    return o, st


def _hgrn_chunk_fast(row0, lanes, st, q_ref, g_ref, k_ref, v_ref):
    rows = pl.ds(row0, CHUNK)
    q = q_ref[rows, lanes]
    G = g_ref[rows, lanes]
    k = k_ref[rows, lanes]
    v = v_ref[rows, lanes]
    gmid = g_ref[pl.ds(row0 + CHUNK // 2 - 1, 1), lanes]
    glast = g_ref[pl.ds(row0 + CHUNK - 1, 1), lanes]
    d = G - gmid
    qh = q * jnp.exp2(d)
    kh = k * jnp.exp2(-d)
    a = lax.dot_general(qh.astype(BF16), kh.astype(BF16), _NT, preferred_element_type=F32)
    causal = (lax.broadcasted_iota(jnp.int32, (CHUNK, CHUNK), 1)
              <= lax.broadcasted_iota(jnp.int32, (CHUNK, CHUNK), 0))
    a = jnp.where(causal, a, 0.0)
    qi = (qh * jnp.exp2(gmid)).astype(BF16)
    o = lax.dot_general(qi, st.astype(BF16), _NT, preferred_element_type=F32)
    o = o + jnp.dot(a.astype(BF16), v, preferred_element_type=F32)
    kd = (kh * jnp.exp2(glast - gmid)).astype(BF16)
    st = st * jnp.exp2(glast) + lax.dot_general(v, kd, _TN, preferred_element_type=F32)
    return o, st


def _project_sections(u, win_ref, c0, first, last):
    if GROUP_WIDTH == D_MODEL:
        p = jnp.dot(u, win_ref[:, first * D_MODEL:last * D_MODEL], preferred_element_type=F32)
        return [p[:, i * D_MODEL:(i + 1) * D_MODEL] for i in range(last - first)]
    return [jnp.dot(u, win_ref[:, i * D_MODEL + c0:i * D_MODEL + c0 + GROUP_WIDTH], preferred_element_type=F32)
            for i in range(first, last)]


def _project_group(u, g, R, valid, prm, slot, recurrence_inputs):
    lb_ref, win_ref, cw_ref, zc_ref, safe_ref = prm
    q_ref, g_ref, k_ref, v_ref, gs_ref, sa_ref, mb_ref, zs_ref = slot
    c0 = g * GROUP_WIDTH
    cols = slice(c0, c0 + GROUP_WIDTH)

    sec = lambda first, last: _project_sections(u, win_ref, c0, first, last)

    if not recurrence_inputs:
        _project_gates(sec(N_RECURRENCE_SECTIONS, N_IN_SECTIONS), cols, R, valid, cw_ref, zc_ref, slot)
        return

    qv, fv, iv = sec(0, N_RECURRENCE_SECTIONS)
    q_ref[0:R, :] = qv * _sigmoid(qv)

    f = lb_ref[0:1, cols] + lb_ref[1:2, cols] * _sigmoid(fv)
    k_ref[0:R, :] = 1.0 - f
    lf = jnp.log2(f)
    tri = (lax.broadcasted_iota(jnp.int32, (CHUNK, CHUNK), 1)
           <= lax.broadcasted_iota(jnp.int32, (CHUNK, CHUNK), 0)).astype(BF16)
    tri2 = jnp.concatenate([tri, tri], axis=1)
    worst = jnp.zeros((1, GROUP_WIDTH), F32)
    for c in range(R // CHUNK):
        lfc = lf[c * CHUNK:(c + 1) * CHUNK, :]
        hi = lfc.astype(BF16)
        lo = (lfc - hi.astype(F32)).astype(BF16)
        gc = jnp.dot(tri2, jnp.concatenate([hi, lo], axis=0),
                     preferred_element_type=F32)
        g_ref[c * CHUNK:(c + 1) * CHUNK, :] = gc
        gmid = gc[CHUNK // 2 - 1:CHUNK // 2, :]
        glast = gc[CHUNK - 1:CHUNK, :]
        worst = jnp.maximum(worst, jnp.maximum(-gmid, gmid - glast))
    safe_ref[g] = (jnp.max(worst) <= MAX_SAFE_EXPONENT * LOG2_E).astype(jnp.int32)

    v_ref[0:R, :] = iv.astype(BF16)


def _project_gates(sections, cols, R, valid, cw_ref, zc_ref, slot):
    gs_ref, sa_ref, mb_ref, zs_ref = slot[4:8]
    gv, bg, cg, hc, ga, gb = sections
    gs_ref[0:R, :] = (gv * _sigmoid(gv)).astype(BF16)

    z = cg * hc
    zs_ref[0:SUBLANES, :] = zc_ref[:, cols]
    zs_ref[SUBLANES:SUBLANES + R, :] = z
    z1 = zs_ref[SUBLANES - 1:SUBLANES - 1 + R, :]
    z2 = zs_ref[SUBLANES - 2:SUBLANES - 2 + R, :]
    yb = bg * (cw_ref[0:1, cols] * z2 + cw_ref[1:2, cols] * z1 + cw_ref[2:3, cols] * z)
    zc_ref[:, cols] = zs_ref[valid:valid + SUBLANES, :]

    sa_ref[0:R, :] = _sigmoid(ga).astype(BF16)
    mb_ref[0:R, :] = (_sigmoid(gb) * yb).astype(BF16)


def _finish_chunk(row0, h, lanes, o, hnw, mg_ref, slot):
    gs_ref, sa_ref, mb_ref = slot[4:7]
    rows = pl.ds(row0, CHUNK)
    m = (_rms(o, hnw) * gs_ref[rows, lanes].astype(F32) * sa_ref[rows, lanes].astype(F32)
         + mb_ref[rows, lanes].astype(F32))
    mg_ref[rows, h * HEAD_DIM:(h + 1) * HEAD_DIM] = m.astype(BF16)


def _recur_group_fast(g, R, hnw, st_ref, st_old_ref, mg_ref, slot):
    q_ref, g_ref, k_ref, v_ref = slot[0:4]
    for hh in range(GROUP_HEADS):
        h = GROUP_HEADS * g + hh
        lanes = slice(hh * HEAD_DIM, (hh + 1) * HEAD_DIM)
        st = st_ref[h]
        st_old_ref[h] = st
        for c in range(R // CHUNK):
            o, st = _hgrn_chunk_fast(c * CHUNK, lanes, st, q_ref, g_ref, k_ref, v_ref)
            mg_ref[c * CHUNK:(c + 1) * CHUNK, h * HEAD_DIM:(h + 1) * HEAD_DIM] = _rms(o, hnw).astype(BF16)
        st_ref[h] = st


def _finish_group_fast(g, R, mg_ref, slot):
    gs_ref, sa_ref, mb_ref = slot[4:7]
    cols = slice(g * GROUP_WIDTH, (g + 1) * GROUP_WIDTH)
    mg_ref[0:R, cols] = mg_ref[0:R, cols] * gs_ref[0:R, :] * sa_ref[0:R, :] + mb_ref[0:R, :]


def _recur_group_exact(g, R, valid, hnw, rsel, st_in_ref, st_ref, mg_ref, slot):
    q_ref, g_ref, k_ref, v_ref = slot[0:4]
    for hh in range(GROUP_HEADS):
        h = GROUP_HEADS * g + hh
        lanes = slice(hh * HEAD_DIM, (hh + 1) * HEAD_DIM)
        st = st_in_ref[h]
        if valid < R:
            o, st = _hgrn_chunk_exact(0, lanes, valid, st, q_ref, g_ref, k_ref, v_ref, rsel)
            _finish_chunk(0, h, lanes, o, hnw, mg_ref, slot)
        else:
            def chunk_body(c, st):
                row0 = pl.multiple_of(c * CHUNK, CHUNK)
                o, st = _hgrn_chunk_exact(row0, lanes, CHUNK, st, q_ref, g_ref, k_ref, v_ref, rsel)
                _finish_chunk(row0, h, lanes, o, hnw, mg_ref, slot)
                return st

            st = lax.fori_loop(0, R // CHUNK, chunk_body, st)
        st_ref[h] = st


def _out_partial(g, R, mg_ref, wout_ref):
    rows = slice(g * GROUP_WIDTH, (g + 1) * GROUP_WIDTH)
    return jnp.dot(mg_ref[0:R, rows], wout_ref[rows, 0:D_MODEL], preferred_element_type=F32)


def _mixer_tile(load_x, store_out, R, valid, is_meta, refs):
    (lbp_ref, anw_ref, win_ref, hnw_ref, cw_ref, wout_ref, rsel_ref,
     st_ref, st_old_ref, zc_ref, lb_ref, mg_ref, safe_ref, slots) = refs
    prm = (lb_ref, win_ref, cw_ref, zc_ref, safe_ref)
    u = _rms(load_x(), anw_ref[...]).astype(BF16)
    rsel = rsel_ref[...]
    hnw = hnw_ref[...]
    project = lambda g, first: _project_group(u, g, R, valid, prm, slots[g], first)

    lbp = lbp_ref[...]
    mx = jnp.max(lbp, axis=0, keepdims=True)
    ex = jnp.exp(lbp - mx)
    lb = ex[0:1, :] / jnp.sum(ex, axis=0, keepdims=True)
    lb_ref[0:1, :] = lb
    lb_ref[1:2, :] = 1.0 - lb

    if is_meta:
        for g in range(N_GROUPS):
            project(g, True)
            project(g, False)
            _recur_group_exact(g, R, valid, hnw, rsel, st_ref, st_ref, mg_ref, slots[g])
        acc = load_x()
        for g in range(N_GROUPS):
            acc = acc + _out_partial(g, R, mg_ref, wout_ref)
        store_out(acc, False)
        return

    def store_early():
        acc = load_x()
        for g in range(N_GROUPS - 1):
            acc = acc + _out_partial(g, R, mg_ref, wout_ref)
        store_out(acc, False)

    for g in range(N_GROUPS):
        project(g, True)
        if g > 0:
            _finish_group_fast(g - 1, R, mg_ref, slots[g - 1])
        if g == N_GROUPS - 1:
            store_early()
        _recur_group_fast(g, R, hnw, st_ref, st_old_ref, mg_ref, slots[g])
        project(g, False)
    _finish_group_fast(N_GROUPS - 1, R, mg_ref, slots[N_GROUPS - 1])
    for g in range(N_GROUPS):
        def redo(g=g):
            _recur_group_exact(g, R, valid, hnw, rsel, st_old_ref, st_ref, mg_ref, slots[g])
            if g < N_GROUPS - 1:
                store_early()

        pl.when(safe_ref[g] == 0)(redo)
    store_out(_out_partial(N_GROUPS - 1, R, mg_ref, wout_ref), True)


def _mixer_kernel(x_ref, meta_ref, lbp_ref, anw_ref, win_ref, hnw_ref, cw_ref, wout_ref, rsel_ref, fnw_ref,
                  wup_f32_ref, wdn_f32_ref, h1_ref, h1m_ref, u2_ref, wup_ref, wdn_ref,
                  st_ref, st_old_ref, zc_ref, lb_ref, mg_ref, safe_ref,
                  st_meta_ref, zc_meta_ref, h1m_meta_ref, *slots):
    wup_ref[...] = wup_f32_ref[...].astype(BF16)
    wdn_ref[:, 0:D_MODEL] = wdn_f32_ref[...].astype(BF16)
    wdn_ref[:, D_MODEL:] = jnp.zeros((W_DOWN_ROWS_PER_STEP, WEIGHT_LANE_PAD), BF16)

    n = len(slots) // N_GROUPS
    refs = (lbp_ref, anw_ref, win_ref, hnw_ref, cw_ref, wout_ref, rsel_ref,
            st_ref, st_old_ref, zc_ref, lb_ref, mg_ref, safe_ref,
            [slots[i * n:(i + 1) * n] for i in range(N_GROUPS)])

    first_tile = pl.program_id(1) == 0

    @pl.when(jnp.logical_and(first_tile, pl.program_id(0) == 0))
    def _():
        st_ref[...] = jnp.zeros_like(st_ref)
        zc_ref[...] = jnp.zeros_like(zc_ref)
        load_meta = lambda: jnp.concatenate(
            [meta_ref[...], jnp.zeros((CHUNK - N_META, D_MODEL), F32)], axis=0)

        def store_meta(val, accumulate):
            h1m_meta_ref[...] = val[0:N_META, :]

        _mixer_tile(load_meta, store_meta, CHUNK, N_META, True, refs)
        st_meta_ref[...] = st_ref[...]
        zc_meta_ref[...] = zc_ref[...]

    @pl.when(first_tile)
    def _():
        st_ref[...] = st_meta_ref[...]
        zc_ref[...] = zc_meta_ref[...]
        h1m_ref[0] = h1m_meta_ref[...]

    def store_tile(val, accumulate):
        if accumulate:
            h1 = h1_ref[0] + val
            h1_ref[0] = h1
            u2_ref[0] = _rms(h1, fnw_ref[...]).astype(BF16)
        else:
            h1_ref[0] = val

    _mixer_tile(lambda: x_ref[0], store_tile, MIXER_TILE, MIXER_TILE, False, refs)


def _ffn_kernel(h1_ref, u2_ref, h1m_ref, fnw_ref, wup_ref, fcw_ref, fcb_ref, wdn_ref, finw_ref,
                out_ref, as_ref):
    T = FFN_TILE
    fnw = fnw_ref[...]

    @pl.when(pl.program_id(1) == 0)
    def _():
        um = _rms(h1m_ref[0], fnw).astype(BF16)
        am = jnp.dot(um, wup_ref[:, 0:D_FF], preferred_element_type=F32)
        as_ref[0:SUBLANES, :] = am[N_META - SUBLANES:N_META, :]

    x = h1_ref[0]
    up = jnp.dot(u2_ref[0], wup_ref[...], preferred_element_type=F32)
    a = up[:, 0:D_FF]
    as_ref[SUBLANES:SUBLANES + T, :] = a
    a1 = as_ref[SUBLANES - 1:SUBLANES - 1 + T, :]
    a2 = as_ref[SUBLANES - 2:SUBLANES - 2 + T, :]
    ac = fcw_ref[0:1, :] * a2 + fcw_ref[1:2, :] * a1 + fcw_ref[2:3, :] * a + fcb_ref[...]
    as_ref[0:SUBLANES, :] = as_ref[T:T + SUBLANES, :]
    gated = (ac * _sigmoid(ac) * up[:, D_FF:]).astype(BF16)
    y = x + jnp.dot(gated, wdn_ref[:, 0:D_MODEL], preferred_element_type=F32)
    out_ref[0] = _rms(y, finw_ref[...])


def _resident(shape):
    return pl.BlockSpec(shape, lambda b, t: (0,) * len(shape), pipeline_mode=pl.Buffered(1))


def _weight(w):
    return jnp.pad(w.astype(BF16), ((0, 0), (0, WEIGHT_LANE_PAD)))


def _rsel_matrix():
    r = np.arange(N_HEADS * HEAD_DIM)[:, None] // HEAD_DIM
    l = np.arange(CHUNK)[None, :] % SUBLANES
    return jnp.asarray(r == l, dtype=BF16)


def kernel(x, meta_tokens, lb_param, attn_norm_w, w_in, hgrn_norm_w, conv_w, w_out, ffn_norm_w, w_up,
           ffn_conv_w, ffn_conv_b, w_down, final_norm_w):
    B, L, D = x.shape
    assert D == D_MODEL and L % MIXER_TILE == 0 and L % FFN_TILE == 0
    assert w_in.shape == (1, D, N_IN_SECTIONS * D) and w_up.shape == (1, D, 2 * D_FF)
    assert SUBLANES * HEAD_DIM == D_MODEL and N_HEADS == SUBLANES

    tile = lambda T: pl.BlockSpec((1, T, D), lambda b, t: (b, t, 0))
    meta_tile = pl.BlockSpec((1, N_META, D), lambda b, t: (b, 0, 0))
    params = pltpu.CompilerParams(dimension_semantics=("arbitrary", "arbitrary"),
                                  vmem_limit_bytes=VMEM_LIMIT_BYTES)

    TM = MIXER_TILE
    n_steps = B * (L // TM)
    step = lambda b, t: b * (L // TM) + t
    assert D % n_steps == 0 and D_FF % W_DOWN_ROWS_PER_STEP == 0 and D_FF // W_DOWN_ROWS_PER_STEP <= n_steps
    wup_rows = D // n_steps
    wup_block = pl.BlockSpec((wup_rows, 2 * D_FF), lambda b, t: (step(b, t), 0))
    wdn_index = lambda b, t: (jnp.minimum(step(b, t), D_FF // W_DOWN_ROWS_PER_STEP - 1), 0)
    slot = [pltpu.VMEM((TM, GROUP_WIDTH), F32)] * 3 + [pltpu.VMEM((TM, GROUP_WIDTH), BF16)] * 4 \
        + [pltpu.VMEM((TM + SUBLANES, GROUP_WIDTH), F32)]
    h1, h1m, u2, w_up_bf16, w_down_bf16 = pl.pallas_call(
        _mixer_kernel,
        grid=(B, L // TM),
        in_specs=[tile(TM), _resident((N_META, D)), _resident((2, D)), _resident((1, D)),
                  _resident((D, N_IN_SECTIONS * D + WEIGHT_LANE_PAD)), _resident((1, HEAD_DIM)),
                  _resident((3, D)), _resident((D, D + WEIGHT_LANE_PAD)),
                  _resident((N_HEADS * HEAD_DIM, CHUNK)), _resident((1, D)), wup_block,
                  pl.BlockSpec((W_DOWN_ROWS_PER_STEP, D), wdn_index)],
        out_specs=[tile(TM), meta_tile, tile(TM), wup_block,
                   pl.BlockSpec((W_DOWN_ROWS_PER_STEP, D + WEIGHT_LANE_PAD), wdn_index)],
        out_shape=[jax.ShapeDtypeStruct((B, L, D), F32), jax.ShapeDtypeStruct((B, N_META, D), F32),
                   jax.ShapeDtypeStruct((B, L, D), BF16),
                   jax.ShapeDtypeStruct((D, 2 * D_FF), BF16),
                   jax.ShapeDtypeStruct((D_FF, D + WEIGHT_LANE_PAD), BF16)],
        scratch_shapes=[pltpu.VMEM((N_HEADS, HEAD_DIM, HEAD_DIM), F32),
                        pltpu.VMEM((N_HEADS, HEAD_DIM, HEAD_DIM), F32),
                        pltpu.VMEM((SUBLANES, D), F32),
                        pltpu.VMEM((SUBLANES, D), F32),
                        pltpu.VMEM((TM, D), BF16),
                        pltpu.SMEM((N_GROUPS,), jnp.int32),
                        pltpu.VMEM((N_HEADS, HEAD_DIM, HEAD_DIM), F32),
                        pltpu.VMEM((SUBLANES, D), F32),
                        pltpu.VMEM((N_META, D), F32),
                        *(slot * N_GROUPS)],
        compiler_params=params,
        name="mixer",
    )(x, meta_tokens, lb_param, attn_norm_w, _weight(w_in[0]), hgrn_norm_w, conv_w[0],
      _weight(w_out[0]), _rsel_matrix(), ffn_norm_w, w_up[0], w_down[0])

    TF = FFN_TILE
    out = pl.pallas_call(
        _ffn_kernel,
        grid=(B, L // TF),
        in_specs=[tile(TF), tile(TF), meta_tile, _resident((1, D)), _resident((D, 2 * D_FF)), _resident((3, D_FF)),
                  _resident((1, D_FF)), _resident((D_FF, D + WEIGHT_LANE_PAD)), _resident((1, D))],
        out_specs=tile(TF),
        out_shape=jax.ShapeDtypeStruct((B, L, D), F32),
        scratch_shapes=[pltpu.VMEM((TF + SUBLANES, D_FF), F32)],
        compiler_params=params,
        name="ffn",
    )(h1, u2, h1m, ffn_norm_w, w_up_bf16, ffn_conv_w[0], ffn_conv_b, w_down_bf16, final_norm_w.reshape(1, D))
    return out
```

```python
import numpy as np
import jax
import jax.numpy as jnp
from jax import lax
from jax.experimental import pallas as pl
from jax.experimental.pallas import tpu as pltpu

D_MODEL = 1024
N_META = 16
N_HEADS = 8
HEAD_DIM = 128
D_FF = 2816
N_IN_SECTIONS = 9
N_RECURRENCE_SECTIONS = 3
EPS = 1e-6

SUBLANES = 8
CHUNK = 128
GROUP_HEADS = 8
GROUP_WIDTH = GROUP_HEADS * HEAD_DIM
N_GROUPS = N_HEADS // GROUP_HEADS
MAX_SAFE_EXPONENT = 60.0
LOG2_E = 1.4426950408889634
MIXER_TILE = 512
FFN_TILE = 1024
WEIGHT_LANE_PAD = 128
W_DOWN_ROWS_PER_STEP = 128
V7X_VMEM_BYTES = 64 * 1024 * 1024
VMEM_LIMIT_BYTES = V7X_VMEM_BYTES - 4 * 1024 * 1024

F32 = jnp.float32
BF16 = jnp.bfloat16

_NT = (((1,), (1,)), ((), ()))
_TN = (((0,), (0,)), ((), ()))


def _rms(x, w):
    ms = jnp.mean(x * x, axis=-1, keepdims=True)
    return x * lax.rsqrt(ms + EPS) * w


def _sigmoid(x):
    return 1.0 / (1.0 + jnp.exp2(x * -LOG2_E))


def _bcast_rows(ref, lanes, row0, block, offset, nrows):
    pieces = [jnp.broadcast_to(_row(ref, lanes, row0, i * block + offset), (block, HEAD_DIM))
              for i in range(nrows // block)]
    return pieces[0] if len(pieces) == 1 else jnp.concatenate(pieces, axis=0)


def _row(ref, lanes, row0, r):
    group = ref[pl.ds(row0 + r // SUBLANES * SUBLANES, SUBLANES), lanes]
    return group[r % SUBLANES:r % SUBLANES + 1, :]


def _hgrn_chunk_exact(row0, lanes, valid, st, q_ref, g_ref, k_ref, v_ref, rsel):
    rows = pl.ds(row0, CHUNK)
    q = q_ref[rows, lanes]
    G = g_ref[rows, lanes]
    k = k_ref[rows, lanes]
    v = v_ref[rows, lanes]
    row = lax.broadcasted_iota(jnp.int32, (CHUNK, HEAD_DIM), 0)

    sub = row & (SUBLANES - 1)
    ps = []
    for j in range(SUBLANES):
        gj = _bcast_rows(g_ref, lanes, row0, SUBLANES, j, CHUNK)
        kj = _bcast_rows(k_ref, lanes, row0, SUBLANES, j, CHUNK)
        p = q * kj * jnp.exp2(G - gj)
        ps.append(jnp.where(sub >= j, p, 0.0).astype(BF16))
    pcat = jnp.concatenate(ps, axis=1)
    a = jnp.dot(pcat, rsel, preferred_element_type=F32)

    tl_xor = (lax.broadcasted_iota(jnp.int32, (CHUNK, CHUNK), 0)
              ^ lax.broadcasted_iota(jnp.int32, (CHUNK, CHUNK), 1))
    a = jnp.where(tl_xor < SUBLANES, a, 0.0)

    b = 2 * SUBLANES
    while b <= CHUNK:
        gm = _bcast_rows(g_ref, lanes, row0, b, b // 2 - 1, CHUNK)
        e = jnp.exp2(-jnp.abs(G - gm))
        upper = (row & (b // 2)) != 0
        qt = jnp.where(upper, q * e, 0.0).astype(BF16)
        kt = jnp.where(upper, 0.0, k * e).astype(BF16)
        ab = lax.dot_general(qt, kt, _NT, preferred_element_type=F32)
        a = a + (ab if b == CHUNK else jnp.where(tl_xor < b, ab, 0.0))
        b *= 2

    qi = (q * jnp.exp2(G)).astype(BF16)
    o = lax.dot_general(qi, st.astype(BF16), _NT, preferred_element_type=F32)
    o = o + jnp.dot(a.astype(BF16), v, preferred_element_type=F32)

    glast = _row(g_ref, lanes, row0, valid - 1)
    kd = k * jnp.exp2(glast - G)
    if valid < CHUNK:
        kd = jnp.where(row < valid, kd, 0.0)
    st = st * jnp.exp2(glast) + lax.dot_general(v, kd.astype(BF16), _TN, preferred_element_type=F32)
    return o, st


def _hgrn_chunk_fast(row0, lanes, st, q_ref, g_ref, k_ref, v_ref):
    rows = pl.ds(row0, CHUNK)
    q = q_ref[rows, lanes]
    G = g_ref[rows, lanes]
    k = k_ref[rows, lanes]
    v = v_ref[rows, lanes]
    gmid = g_ref[pl.ds(row0 + CHUNK // 2 - 1, 1), lanes]
    glast = g_ref[pl.ds(row0 + CHUNK - 1, 1), lanes]
    d = G - gmid
    qh = q * jnp.exp2(d)
    kh = k * jnp.exp2(-d)
    a = lax.dot_general(qh.astype(BF16), kh.astype(BF16), _NT, preferred_element_type=F32)
    causal = (lax.broadcasted_iota(jnp.int32, (CHUNK, CHUNK), 1)
              <= lax.broadcasted_iota(jnp.int32, (CHUNK, CHUNK), 0))
    a = jnp.where(causal, a, 0.0)
    qi = (qh * jnp.exp2(gmid)).astype(BF16)
    o = lax.dot_general(qi, st.astype(BF16), _NT, preferred_element_type=F32)
    o = o + jnp.dot(a.astype(BF16), v, preferred_element_type=F32)
    kd = (kh * jnp.exp2(glast - gmid)).astype(BF16)
    st = st * jnp.exp2(glast) + lax.dot_general(v, kd, _TN, preferred_element_type=F32)
    return o, st


def _project_sections(u, win_ref, c0, first, last):
    if GROUP_WIDTH == D_MODEL:
        p = jnp.dot(u, win_ref[:, first * D_MODEL:last * D_MODEL], preferred_element_type=F32)
        return [p[:, i * D_MODEL:(i + 1) * D_MODEL] for i in range(last - first)]
    return [jnp.dot(u, win_ref[:, i * D_MODEL + c0:i * D_MODEL + c0 + GROUP_WIDTH], preferred_element_type=F32)
            for i in range(first, last)]


def _project_group(u, g, R, valid, prm, slot, recurrence_inputs):
    lb_ref, win_ref, cw_ref, zc_ref, safe_ref = prm
    q_ref, g_ref, k_ref, v_ref, gs_ref, sa_ref, mb_ref, zs_ref = slot
    c0 = g * GROUP_WIDTH
    cols = slice(c0, c0 + GROUP_WIDTH)

    sec = lambda first, last: _project_sections(u, win_ref, c0, first, last)

    if not recurrence_inputs:
        _project_gates(sec(N_RECURRENCE_SECTIONS, N_IN_SECTIONS), cols, R, valid, cw_ref, zc_ref, slot)
        return

    qv, fv, iv = sec(0, N_RECURRENCE_SECTIONS)
    q_ref[0:R, :] = qv * _sigmoid(qv)

    f = lb_ref[0:1, cols] + lb_ref[1:2, cols] * _sigmoid(fv)
    k_ref[0:R, :] = 1.0 - f
    lf = jnp.log2(f)
    tri = (lax.broadcasted_iota(jnp.int32, (CHUNK, CHUNK), 1)
           <= lax.broadcasted_iota(jnp.int32, (CHUNK, CHUNK), 0)).astype(BF16)
    tri2 = jnp.concatenate([tri, tri], axis=1)
    worst = jnp.zeros((1, GROUP_WIDTH), F32)
    for c in range(R // CHUNK):
        lfc = lf[c * CHUNK:(c + 1) * CHUNK, :]
        hi = lfc.astype(BF16)
        lo = (lfc - hi.astype(F32)).astype(BF16)
        gc = jnp.dot(tri2, jnp.concatenate([hi, lo], axis=0),
                     preferred_element_type=F32)
        g_ref[c * CHUNK:(c + 1) * CHUNK, :] = gc
        gmid = gc[CHUNK // 2 - 1:CHUNK // 2, :]
        glast = gc[CHUNK - 1:CHUNK, :]
        worst = jnp.maximum(worst, jnp.maximum(-gmid, gmid - glast))
    safe_ref[g] = (jnp.max(worst) <= MAX_SAFE_EXPONENT * LOG2_E).astype(jnp.int32)

    v_ref[0:R, :] = iv.astype(BF16)


def _project_gates(sections, cols, R, valid, cw_ref, zc_ref, slot):
    gs_ref, sa_ref, mb_ref, zs_ref = slot[4:8]
    gv, bg, cg, hc, ga, gb = sections
    gs_ref[0:R, :] = (gv * _sigmoid(gv)).astype(BF16)

    z = cg * hc
    zs_ref[0:SUBLANES, :] = zc_ref[:, cols]
    zs_ref[SUBLANES:SUBLANES + R, :] = z
    z1 = zs_ref[SUBLANES - 1:SUBLANES - 1 + R, :]
    z2 = zs_ref[SUBLANES - 2:SUBLANES - 2 + R, :]
    yb = bg * (cw_ref[0:1, cols] * z2 + cw_ref[1:2, cols] * z1 + cw_ref[2:3, cols] * z)
    zc_ref[:, cols] = zs_ref[valid:valid + SUBLANES, :]

    sa_ref[0:R, :] = _sigmoid(ga).astype(BF16)
    mb_ref[0:R, :] = (_sigmoid(gb) * yb).astype(BF16)


def _finish_chunk(row0, h, lanes, o, hnw, mg_ref, slot):
    gs_ref, sa_ref, mb_ref = slot[4:7]
    rows = pl.ds(row0, CHUNK)
    m = (_rms(o, hnw) * gs_ref[rows, lanes].astype(F32) * sa_ref[rows, lanes].astype(F32)
         + mb_ref[rows, lanes].astype(F32))
    mg_ref[rows, h * HEAD_DIM:(h + 1) * HEAD_DIM] = m.astype(BF16)


def _recur_group_fast(g, R, hnw, st_ref, st_old_ref, mg_ref, slot):
    q_ref, g_ref, k_ref, v_ref = slot[0:4]
    for hh in range(GROUP_HEADS):
        h = GROUP_HEADS * g + hh
        lanes = slice(hh * HEAD_DIM, (hh + 1) * HEAD_DIM)
        st = st_ref[h]
        st_old_ref[h] = st
        for c in range(R // CHUNK):
            o, st = _hgrn_chunk_fast(c * CHUNK, lanes, st, q_ref, g_ref, k_ref, v_ref)
            mg_ref[c * CHUNK:(c + 1) * CHUNK, h * HEAD_DIM:(h + 1) * HEAD_DIM] = _rms(o, hnw).astype(BF16)
        st_ref[h] = st


def _finish_group_fast(g, R, mg_ref, slot):
    gs_ref, sa_ref, mb_ref = slot[4:7]
    cols = slice(g * GROUP_WIDTH, (g + 1) * GROUP_WIDTH)
    mg_ref[0:R, cols] = mg_ref[0:R, cols] * gs_ref[0:R, :] * sa_ref[0:R, :] + mb_ref[0:R, :]


def _recur_group_exact(g, R, valid, hnw, rsel, st_in_ref, st_ref, mg_ref, slot):
    q_ref, g_ref, k_ref, v_ref = slot[0:4]
    for hh in range(GROUP_HEADS):
        h = GROUP_HEADS * g + hh
        lanes = slice(hh * HEAD_DIM, (hh + 1) * HEAD_DIM)
        st = st_in_ref[h]
        if valid < R:
            o, st = _hgrn_chunk_exact(0, lanes, valid, st, q_ref, g_ref, k_ref, v_ref, rsel)
            _finish_chunk(0, h, lanes, o, hnw, mg_ref, slot)
        else:
            def chunk_body(c, st):
                row0 = pl.multiple_of(c * CHUNK, CHUNK)
                o, st = _hgrn_chunk_exact(row0, lanes, CHUNK, st, q_ref, g_ref, k_ref, v_ref, rsel)
                _finish_chunk(row0, h, lanes, o, hnw, mg_ref, slot)
                return st

            st = lax.fori_loop(0, R // CHUNK, chunk_body, st)
        st_ref[h] = st


def _out_partial(g, R, mg_ref, wout_ref):
    rows = slice(g * GROUP_WIDTH, (g + 1) * GROUP_WIDTH)
    return jnp.dot(mg_ref[0:R, rows], wout_ref[rows, 0:D_MODEL], preferred_element_type=F32)


def _mixer_tile(load_x, store_out, R, valid, is_meta, refs):
    (lbp_ref, anw_ref, win_ref, hnw_ref, cw_ref, wout_ref, rsel_ref,
     st_ref, st_old_ref, zc_ref, lb_ref, mg_ref, safe_ref, slots) = refs
    prm = (lb_ref, win_ref, cw_ref, zc_ref, safe_ref)
    u = _rms(load_x(), anw_ref[...]).astype(BF16)
    rsel = rsel_ref[...]
    hnw = hnw_ref[...]
    project = lambda g, first: _project_group(u, g, R, valid, prm, slots[g], first)

    lbp = lbp_ref[...]
    mx = jnp.max(lbp, axis=0, keepdims=True)
    ex = jnp.exp(lbp - mx)
    lb = ex[0:1, :] / jnp.sum(ex, axis=0, keepdims=True)
    lb_ref[0:1, :] = lb
    lb_ref[1:2, :] = 1.0 - lb

    if is_meta:
        for g in range(N_GROUPS):
            project(g, True)
            project(g, False)
            _recur_group_exact(g, R, valid, hnw, rsel, st_ref, st_ref, mg_ref, slots[g])
        acc = load_x()
        for g in range(N_GROUPS):
            acc = acc + _out_partial(g, R, mg_ref, wout_ref)
        store_out(acc, False)
        return

    def store_early():
        acc = load_x()
        for g in range(N_GROUPS - 1):
            acc = acc + _out_partial(g, R, mg_ref, wout_ref)
        store_out(acc, False)

    for g in range(N_GROUPS):
        project(g, True)
        if g > 0:
            _finish_group_fast(g - 1, R, mg_ref, slots[g - 1])
        if g == N_GROUPS - 1:
            store_early()
        _recur_group_fast(g, R, hnw, st_ref, st_old_ref, mg_ref, slots[g])
        project(g, False)
    _finish_group_fast(N_GROUPS - 1, R, mg_ref, slots[N_GROUPS - 1])
    for g in range(N_GROUPS):
        def redo(g=g):
            _recur_group_exact(g, R, valid, hnw, rsel, st_old_ref, st_ref, mg_ref, slots[g])
            if g < N_GROUPS - 1:
                store_early()

        pl.when(safe_ref[g] == 0)(redo)
    store_out(_out_partial(N_GROUPS - 1, R, mg_ref, wout_ref), True)


def _mixer_kernel(x_ref, meta_ref, lbp_ref, anw_ref, win_ref, hnw_ref, cw_ref, wout_ref, rsel_ref,
                  wup_f32_ref, wdn_f32_ref, h1_ref, h1m_ref, wup_ref, wdn_ref,
                  st_ref, st_old_ref, zc_ref, lb_ref, mg_ref, safe_ref,
                  st_meta_ref, zc_meta_ref, h1m_meta_ref, *slots):
    wup_ref[...] = wup_f32_ref[...].astype(BF16)
    wdn_ref[:, 0:D_MODEL] = wdn_f32_ref[...].astype(BF16)
    wdn_ref[:, D_MODEL:] = jnp.zeros((W_DOWN_ROWS_PER_STEP, WEIGHT_LANE_PAD), BF16)

    n = len(slots) // N_GROUPS
    refs = (lbp_ref, anw_ref, win_ref, hnw_ref, cw_ref, wout_ref, rsel_ref,
            st_ref, st_old_ref, zc_ref, lb_ref, mg_ref, safe_ref,
            [slots[i * n:(i + 1) * n] for i in range(N_GROUPS)])

    first_tile = pl.program_id(1) == 0

    @pl.when(jnp.logical_and(first_tile, pl.program_id(0) == 0))
    def _():
        st_ref[...] = jnp.zeros_like(st_ref)
        zc_ref[...] = jnp.zeros_like(zc_ref)
        load_meta = lambda: jnp.concatenate(
            [meta_ref[...], jnp.zeros((CHUNK - N_META, D_MODEL), F32)], axis=0)

        def store_meta(val, accumulate):
            h1m_meta_ref[...] = val[0:N_META, :]

        _mixer_tile(load_meta, store_meta, CHUNK, N_META, True, refs)
        st_meta_ref[...] = st_ref[...]
        zc_meta_ref[...] = zc_ref[...]

    @pl.when(first_tile)
    def _():
        st_ref[...] = st_meta_ref[...]
        zc_ref[...] = zc_meta_ref[...]
        h1m_ref[0] = h1m_meta_ref[...]

    def store_tile(val, accumulate):
        h1_ref[0] = h1_ref[0] + val if accumulate else val

    _mixer_tile(lambda: x_ref[0], store_tile, MIXER_TILE, MIXER_TILE, False, refs)


def _ffn_kernel(h1_ref, h1m_ref, fnw_ref, wup_ref, fcw_ref, fcb_ref, wdn_ref, finw_ref,
                out_ref, as_ref):
    T = FFN_TILE
    fnw = fnw_ref[...]

    @pl.when(pl.program_id(1) == 0)
    def _():
        um = _rms(h1m_ref[0], fnw).astype(BF16)
        am = jnp.dot(um, wup_ref[:, 0:D_FF], preferred_element_type=F32)
        as_ref[0:SUBLANES, :] = am[N_META - SUBLANES:N_META, :]

    x = h1_ref[0]
    u = _rms(x, fnw).astype(BF16)
    up = jnp.dot(u, wup_ref[...], preferred_element_type=F32)
    a = up[:, 0:D_FF]
    as_ref[SUBLANES:SUBLANES + T, :] = a
    a1 = as_ref[SUBLANES - 1:SUBLANES - 1 + T, :]
    a2 = as_ref[SUBLANES - 2:SUBLANES - 2 + T, :]
    ac = fcw_ref[0:1, :] * a2 + fcw_ref[1:2, :] * a1 + fcw_ref[2:3, :] * a + fcb_ref[...]
    as_ref[0:SUBLANES, :] = as_ref[T:T + SUBLANES, :]
    gated = (ac * _sigmoid(ac) * up[:, D_FF:]).astype(BF16)
    y = x + jnp.dot(gated, wdn_ref[:, 0:D_MODEL], preferred_element_type=F32)
    out_ref[0] = _rms(y, finw_ref[...])


def _resident(shape):
    return pl.BlockSpec(shape, lambda b, t: (0,) * len(shape), pipeline_mode=pl.Buffered(1))


def _weight(w):
    return jnp.pad(w.astype(BF16), ((0, 0), (0, WEIGHT_LANE_PAD)))


def _rsel_matrix():
    r = np.arange(N_HEADS * HEAD_DIM)[:, None] // HEAD_DIM
    l = np.arange(CHUNK)[None, :] % SUBLANES
    return jnp.asarray(r == l, dtype=BF16)


def kernel(x, meta_tokens, lb_param, attn_norm_w, w_in, hgrn_norm_w, conv_w, w_out, ffn_norm_w, w_up,
           ffn_conv_w, ffn_conv_b, w_down, final_norm_w):
    B, L, D = x.shape
    assert D == D_MODEL and L % MIXER_TILE == 0 and L % FFN_TILE == 0
    assert w_in.shape == (1, D, N_IN_SECTIONS * D) and w_up.shape == (1, D, 2 * D_FF)
    assert SUBLANES * HEAD_DIM == D_MODEL and N_HEADS == SUBLANES

    tile = lambda T: pl.BlockSpec((1, T, D), lambda b, t: (b, t, 0))
    meta_tile = pl.BlockSpec((1, N_META, D), lambda b, t: (b, 0, 0))
    params = pltpu.CompilerParams(dimension_semantics=("arbitrary", "arbitrary"),
                                  vmem_limit_bytes=VMEM_LIMIT_BYTES)

    TM = MIXER_TILE
    n_steps = B * (L // TM)
    step = lambda b, t: b * (L // TM) + t
    assert D % n_steps == 0 and D_FF % W_DOWN_ROWS_PER_STEP == 0 and D_FF // W_DOWN_ROWS_PER_STEP <= n_steps
    wup_rows = D // n_steps
    wup_block = pl.BlockSpec((wup_rows, 2 * D_FF), lambda b, t: (step(b, t), 0))
    wdn_index = lambda b, t: (jnp.minimum(step(b, t), D_FF // W_DOWN_ROWS_PER_STEP - 1), 0)
    slot = [pltpu.VMEM((TM, GROUP_WIDTH), F32)] * 3 + [pltpu.VMEM((TM, GROUP_WIDTH), BF16)] * 4 \
        + [pltpu.VMEM((TM + SUBLANES, GROUP_WIDTH), F32)]
    h1, h1m, w_up_bf16, w_down_bf16 = pl.pallas_call(
        _mixer_kernel,
        grid=(B, L // TM),
        in_specs=[tile(TM), _resident((N_META, D)), _resident((2, D)), _resident((1, D)),
                  _resident((D, N_IN_SECTIONS * D + WEIGHT_LANE_PAD)), _resident((1, HEAD_DIM)),
                  _resident((3, D)), _resident((D, D + WEIGHT_LANE_PAD)),
                  _resident((N_HEADS * HEAD_DIM, CHUNK)), wup_block,
                  pl.BlockSpec((W_DOWN_ROWS_PER_STEP, D), wdn_index)],
        out_specs=[tile(TM), meta_tile, wup_block,
                   pl.BlockSpec((W_DOWN_ROWS_PER_STEP, D + WEIGHT_LANE_PAD), wdn_index)],
        out_shape=[jax.ShapeDtypeStruct((B, L, D), F32), jax.ShapeDtypeStruct((B, N_META, D), F32),
                   jax.ShapeDtypeStruct((D, 2 * D_FF), BF16),
                   jax.ShapeDtypeStruct((D_FF, D + WEIGHT_LANE_PAD), BF16)],
        scratch_shapes=[pltpu.VMEM((N_HEADS, HEAD_DIM, HEAD_DIM), F32),
                        pltpu.VMEM((N_HEADS, HEAD_DIM, HEAD_DIM), F32),
                        pltpu.VMEM((SUBLANES, D), F32),
                        pltpu.VMEM((SUBLANES, D), F32),
                        pltpu.VMEM((TM, D), BF16),
                        pltpu.SMEM((N_GROUPS,), jnp.int32),
                        pltpu.VMEM((N_HEADS, HEAD_DIM, HEAD_DIM), F32),
                        pltpu.VMEM((SUBLANES, D), F32),
                        pltpu.VMEM((N_META, D), F32),
                        *(slot * N_GROUPS)],
        compiler_params=params,
        name="mixer",
    )(x, meta_tokens, lb_param, attn_norm_w, _weight(w_in[0]), hgrn_norm_w, conv_w[0],
      _weight(w_out[0]), _rsel_matrix(), w_up[0], w_down[0])

    TF = FFN_TILE
    out = pl.pallas_call(
        _ffn_kernel,
        grid=(B, L // TF),
        in_specs=[tile(TF), meta_tile, _resident((1, D)), _resident((D, 2 * D_FF)), _resident((3, D_FF)),
                  _resident((1, D_FF)), _resident((D_FF, D + WEIGHT_LANE_PAD)), _resident((1, D))],
        out_specs=tile(TF),
        out_shape=jax.ShapeDtypeStruct((B, L, D), F32),
        scratch_shapes=[pltpu.VMEM((TF + SUBLANES, D_FF), F32)],
        compiler_params=params,
        name="ffn",
    )(h1, h1m, ffn_norm_w, w_up_bf16, ffn_conv_w[0], ffn_conv_b, w_down_bf16, final_norm_w.reshape(1, D))
    return out
```

```python
import numpy as np
import jax
import jax.numpy as jnp
from jax import lax
from jax.experimental import pallas as pl
from jax.experimental.pallas import tpu as pltpu

D_MODEL = 1024
N_META = 16
N_HEADS = 8
HEAD_DIM = 128
D_FF = 2816
N_IN_SECTIONS = 9
N_RECURRENCE_SECTIONS = 3
EPS = 1e-6

SUBLANES = 8
CHUNK = 128
GROUP_HEADS = 8
GROUP_WIDTH = GROUP_HEADS * HEAD_DIM
N_GROUPS = N_HEADS // GROUP_HEADS
MAX_SAFE_EXPONENT = 60.0
LOG2_E = 1.4426950408889634
MIXER_TILE = 512
FFN_TILE = 1024
WEIGHT_LANE_PAD = 128
W_DOWN_ROWS_PER_STEP = 128
V7X_VMEM_BYTES = 64 * 1024 * 1024
VMEM_LIMIT_BYTES = V7X_VMEM_BYTES - 4 * 1024 * 1024

F32 = jnp.float32
BF16 = jnp.bfloat16

_NT = (((1,), (1,)), ((), ()))
_TN = (((0,), (0,)), ((), ()))


def _rms(x, w):
    ms = jnp.mean(x * x, axis=-1, keepdims=True)
    return x * lax.rsqrt(ms + EPS) * w


def _sigmoid(x):
    return 1.0 / (1.0 + jnp.exp2(x * -LOG2_E))


def _bcast_rows(ref, lanes, row0, block, offset, nrows):
    pieces = [jnp.broadcast_to(_row(ref, lanes, row0, i * block + offset), (block, HEAD_DIM))
              for i in range(nrows // block)]
    return pieces[0] if len(pieces) == 1 else jnp.concatenate(pieces, axis=0)


def _row(ref, lanes, row0, r):
    group = ref[pl.ds(row0 + r // SUBLANES * SUBLANES, SUBLANES), lanes]
    return group[r % SUBLANES:r % SUBLANES + 1, :]


def _hgrn_chunk_exact(row0, lanes, valid, st, q_ref, g_ref, k_ref, v_ref, rsel):
    rows = pl.ds(row0, CHUNK)
    q = q_ref[rows, lanes]
    G = g_ref[rows, lanes]
    k = k_ref[rows, lanes]
    v = v_ref[rows, lanes]
    row = lax.broadcasted_iota(jnp.int32, (CHUNK, HEAD_DIM), 0)

    sub = row & (SUBLANES - 1)
    ps = []
    for j in range(SUBLANES):
        gj = _bcast_rows(g_ref, lanes, row0, SUBLANES, j, CHUNK)
        kj = _bcast_rows(k_ref, lanes, row0, SUBLANES, j, CHUNK)
        p = q * kj * jnp.exp2(G - gj)
        ps.append(jnp.where(sub >= j, p, 0.0).astype(BF16))
    pcat = jnp.concatenate(ps, axis=1)
    a = jnp.dot(pcat, rsel, preferred_element_type=F32)

    tl_xor = (lax.broadcasted_iota(jnp.int32, (CHUNK, CHUNK), 0)
              ^ lax.broadcasted_iota(jnp.int32, (CHUNK, CHUNK), 1))
    a = jnp.where(tl_xor < SUBLANES, a, 0.0)

    b = 2 * SUBLANES
    while b <= CHUNK:
        gm = _bcast_rows(g_ref, lanes, row0, b, b // 2 - 1, CHUNK)
        e = jnp.exp2(-jnp.abs(G - gm))
        upper = (row & (b // 2)) != 0
        qt = jnp.where(upper, q * e, 0.0).astype(BF16)
        kt = jnp.where(upper, 0.0, k * e).astype(BF16)
        ab = lax.dot_general(qt, kt, _NT, preferred_element_type=F32)
        a = a + (ab if b == CHUNK else jnp.where(tl_xor < b, ab, 0.0))
        b *= 2

    qi = (q * jnp.exp2(G)).astype(BF16)
    o = lax.dot_general(qi, st.astype(BF16), _NT, preferred_element_type=F32)
    o = o + jnp.dot(a.astype(BF16), v, preferred_element_type=F32)

    glast = _row(g_ref, lanes, row0, valid - 1)
    kd = k * jnp.exp2(glast - G)
    if valid < CHUNK:
        kd = jnp.where(row < valid, kd, 0.0)
    st = st * jnp.exp2(glast) + lax.dot_general(v, kd.astype(BF16), _TN, preferred_element_type=F32)
    return o, st


def _hgrn_chunk_fast(row0, lanes, st, q_ref, g_ref, k_ref, v_ref):
    rows = pl.ds(row0, CHUNK)
    q = q_ref[rows, lanes]
    G = g_ref[rows, lanes]
    k = k_ref[rows, lanes]
    v = v_ref[rows, lanes]
    gmid = g_ref[pl.ds(row0 + CHUNK // 2 - 1, 1), lanes]
    glast = g_ref[pl.ds(row0 + CHUNK - 1, 1), lanes]
    d = G - gmid
    qh = q * jnp.exp2(d)
    kh = k * jnp.exp2(-d)
    a = lax.dot_general(qh.astype(BF16), kh.astype(BF16), _NT, preferred_element_type=F32)
    causal = (lax.broadcasted_iota(jnp.int32, (CHUNK, CHUNK), 1)
              <= lax.broadcasted_iota(jnp.int32, (CHUNK, CHUNK), 0))
    a = jnp.where(causal, a, 0.0)
    qi = (qh * jnp.exp2(gmid)).astype(BF16)
    o = lax.dot_general(qi, st.astype(BF16), _NT, preferred_element_type=F32)
    o = o + jnp.dot(a.astype(BF16), v, preferred_element_type=F32)
    kd = (kh * jnp.exp2(glast - gmid)).astype(BF16)
    st = st * jnp.exp2(glast) + lax.dot_general(v, kd, _TN, preferred_element_type=F32)
    return o, st


def _project_sections(u, win_ref, c0, first, last):
    if GROUP_WIDTH == D_MODEL:
        p = jnp.dot(u, win_ref[:, first * D_MODEL:last * D_MODEL], preferred_element_type=F32)
        return [p[:, i * D_MODEL:(i + 1) * D_MODEL] for i in range(last - first)]
    return [jnp.dot(u, win_ref[:, i * D_MODEL + c0:i * D_MODEL + c0 + GROUP_WIDTH], preferred_element_type=F32)
            for i in range(first, last)]


def _project_group(u, g, R, valid, prm, slot, recurrence_inputs):
    lb_ref, win_ref, cw_ref, zc_ref, safe_ref = prm
    q_ref, g_ref, k_ref, v_ref, gs_ref, sa_ref, mb_ref, zs_ref = slot
    c0 = g * GROUP_WIDTH
    cols = slice(c0, c0 + GROUP_WIDTH)

    sec = lambda first, last: _project_sections(u, win_ref, c0, first, last)

    if not recurrence_inputs:
        _project_gates(sec(N_RECURRENCE_SECTIONS, N_IN_SECTIONS), cols, R, valid, cw_ref, zc_ref, slot)
        return

    qv, fv, iv = sec(0, N_RECURRENCE_SECTIONS)
    q_ref[0:R, :] = qv * _sigmoid(qv)

    f = lb_ref[0:1, cols] + lb_ref[1:2, cols] * _sigmoid(fv)
    k_ref[0:R, :] = 1.0 - f
    lf = jnp.log2(f)
    tri = (lax.broadcasted_iota(jnp.int32, (CHUNK, CHUNK), 1)
           <= lax.broadcasted_iota(jnp.int32, (CHUNK, CHUNK), 0)).astype(BF16)
    tri2 = jnp.concatenate([tri, tri], axis=1)
    worst = jnp.zeros((1, GROUP_WIDTH), F32)
    for c in range(R // CHUNK):
        lfc = lf[c * CHUNK:(c + 1) * CHUNK, :]
        hi = lfc.astype(BF16)
        lo = (lfc - hi.astype(F32)).astype(BF16)
        gc = jnp.dot(tri2, jnp.concatenate([hi, lo], axis=0),
                     preferred_element_type=F32)
        g_ref[c * CHUNK:(c + 1) * CHUNK, :] = gc
        gmid = gc[CHUNK // 2 - 1:CHUNK // 2, :]
        glast = gc[CHUNK - 1:CHUNK, :]
        worst = jnp.maximum(worst, jnp.maximum(-gmid, gmid - glast))
    safe_ref[g] = (jnp.max(worst) <= MAX_SAFE_EXPONENT * LOG2_E).astype(jnp.int32)

    v_ref[0:R, :] = iv.astype(BF16)


def _project_gates(sections, cols, R, valid, cw_ref, zc_ref, slot):
    gs_ref, sa_ref, mb_ref, zs_ref = slot[4:8]
    gv, bg, cg, hc, ga, gb = sections
    gs_ref[0:R, :] = (gv * _sigmoid(gv)).astype(BF16)

    z = cg * hc
    zs_ref[0:SUBLANES, :] = zc_ref[:, cols]
    zs_ref[SUBLANES:SUBLANES + R, :] = z
    z1 = zs_ref[SUBLANES - 1:SUBLANES - 1 + R, :]
    z2 = zs_ref[SUBLANES - 2:SUBLANES - 2 + R, :]
    yb = bg * (cw_ref[0:1, cols] * z2 + cw_ref[1:2, cols] * z1 + cw_ref[2:3, cols] * z)
    zc_ref[:, cols] = zs_ref[valid:valid + SUBLANES, :]

    sa_ref[0:R, :] = _sigmoid(ga).astype(BF16)
    mb_ref[0:R, :] = (_sigmoid(gb) * yb).astype(BF16)


def _finish_chunk(row0, h, lanes, o, hnw, mg_ref, slot):
    gs_ref, sa_ref, mb_ref = slot[4:7]
    rows = pl.ds(row0, CHUNK)
    m = (_rms(o, hnw) * gs_ref[rows, lanes].astype(F32) * sa_ref[rows, lanes].astype(F32)
         + mb_ref[rows, lanes].astype(F32))
    mg_ref[rows, h * HEAD_DIM:(h + 1) * HEAD_DIM] = m.astype(BF16)


def _recur_group_fast(g, R, hnw, st_ref, st_old_ref, mg_ref, slot):
    q_ref, g_ref, k_ref, v_ref = slot[0:4]
    for hh in range(GROUP_HEADS):
        h = GROUP_HEADS * g + hh
        lanes = slice(hh * HEAD_DIM, (hh + 1) * HEAD_DIM)
        st = st_ref[h]
        st_old_ref[h] = st
        for c in range(R // CHUNK):
            o, st = _hgrn_chunk_fast(c * CHUNK, lanes, st, q_ref, g_ref, k_ref, v_ref)
            mg_ref[c * CHUNK:(c + 1) * CHUNK, h * HEAD_DIM:(h + 1) * HEAD_DIM] = _rms(o, hnw).astype(BF16)
        st_ref[h] = st


def _finish_group_fast(g, R, mg_ref, slot):
    gs_ref, sa_ref, mb_ref = slot[4:7]
    cols = slice(g * GROUP_WIDTH, (g + 1) * GROUP_WIDTH)
    mg_ref[0:R, cols] = mg_ref[0:R, cols] * gs_ref[0:R, :] * sa_ref[0:R, :] + mb_ref[0:R, :]


def _recur_group_exact(g, R, valid, hnw, rsel, st_in_ref, st_ref, mg_ref, slot):
    q_ref, g_ref, k_ref, v_ref = slot[0:4]
    for hh in range(GROUP_HEADS):
        h = GROUP_HEADS * g + hh
        lanes = slice(hh * HEAD_DIM, (hh + 1) * HEAD_DIM)
        st = st_in_ref[h]
        if valid < R:
            o, st = _hgrn_chunk_exact(0, lanes, valid, st, q_ref, g_ref, k_ref, v_ref, rsel)
            _finish_chunk(0, h, lanes, o, hnw, mg_ref, slot)
        else:
            def chunk_body(c, st):
                row0 = pl.multiple_of(c * CHUNK, CHUNK)
                o, st = _hgrn_chunk_exact(row0, lanes, CHUNK, st, q_ref, g_ref, k_ref, v_ref, rsel)
                _finish_chunk(row0, h, lanes, o, hnw, mg_ref, slot)
                return st

            st = lax.fori_loop(0, R // CHUNK, chunk_body, st)
        st_ref[h] = st


def _out_partial(g, R, mg_ref, wout_ref):
    rows = slice(g * GROUP_WIDTH, (g + 1) * GROUP_WIDTH)
    return jnp.dot(mg_ref[0:R, rows], wout_ref[rows, 0:D_MODEL], preferred_element_type=F32)


def _mixer_tile(load_x, store_out, R, valid, is_meta, refs):
    (lbp_ref, anw_ref, win_ref, hnw_ref, cw_ref, wout_ref, rsel_ref,
     st_ref, st_old_ref, zc_ref, lb_ref, mg_ref, safe_ref, slots) = refs
    prm = (lb_ref, win_ref, cw_ref, zc_ref, safe_ref)
    u = _rms(load_x(), anw_ref[...]).astype(BF16)
    rsel = rsel_ref[...]
    hnw = hnw_ref[...]
    project = lambda g, first: _project_group(u, g, R, valid, prm, slots[g], first)

    lbp = lbp_ref[...]
    mx = jnp.max(lbp, axis=0, keepdims=True)
    ex = jnp.exp(lbp - mx)
    lb = ex[0:1, :] / jnp.sum(ex, axis=0, keepdims=True)
    lb_ref[0:1, :] = lb
    lb_ref[1:2, :] = 1.0 - lb

    if is_meta:
        for g in range(N_GROUPS):
            project(g, True)
            project(g, False)
            _recur_group_exact(g, R, valid, hnw, rsel, st_ref, st_ref, mg_ref, slots[g])
        acc = load_x()
        for g in range(N_GROUPS):
            acc = acc + _out_partial(g, R, mg_ref, wout_ref)
        store_out(acc, False)
        return

    def store_early():
        acc = load_x()
        for g in range(N_GROUPS - 1):
            acc = acc + _out_partial(g, R, mg_ref, wout_ref)
        store_out(acc, False)

    for g in range(N_GROUPS):
        project(g, True)
        if g > 0:
            _finish_group_fast(g - 1, R, mg_ref, slots[g - 1])
        if g == N_GROUPS - 1 and g > 0:
            store_early()
        _recur_group_fast(g, R, hnw, st_ref, st_old_ref, mg_ref, slots[g])
        project(g, False)
    _finish_group_fast(N_GROUPS - 1, R, mg_ref, slots[N_GROUPS - 1])
    for g in range(N_GROUPS):
        def redo(g=g):
            _recur_group_exact(g, R, valid, hnw, rsel, st_old_ref, st_ref, mg_ref, slots[g])
            if g < N_GROUPS - 1:
                store_early()

        pl.when(safe_ref[g] == 0)(redo)
    last = _out_partial(N_GROUPS - 1, R, mg_ref, wout_ref)
    if N_GROUPS == 1:
        store_out(load_x() + last, False)
    else:
        store_out(last, True)


def _mixer_kernel(x_ref, meta_ref, lbp_ref, anw_ref, win_ref, hnw_ref, cw_ref, wout_ref, rsel_ref,
                  wup_f32_ref, wdn_f32_ref, h1_ref, h1m_ref, wup_ref, wdn_ref,
                  st_ref, st_old_ref, zc_ref, lb_ref, mg_ref, safe_ref,
                  st_meta_ref, zc_meta_ref, h1m_meta_ref, *slots):
    wup_ref[...] = wup_f32_ref[...].astype(BF16)
    wdn_ref[:, 0:D_MODEL] = wdn_f32_ref[...].astype(BF16)
    wdn_ref[:, D_MODEL:] = jnp.zeros((W_DOWN_ROWS_PER_STEP, WEIGHT_LANE_PAD), BF16)

    n = len(slots) // N_GROUPS
    refs = (lbp_ref, anw_ref, win_ref, hnw_ref, cw_ref, wout_ref, rsel_ref,
            st_ref, st_old_ref, zc_ref, lb_ref, mg_ref, safe_ref,
            [slots[i * n:(i + 1) * n] for i in range(N_GROUPS)])

    first_tile = pl.program_id(1) == 0

    @pl.when(jnp.logical_and(first_tile, pl.program_id(0) == 0))
    def _():
        st_ref[...] = jnp.zeros_like(st_ref)
        zc_ref[...] = jnp.zeros_like(zc_ref)
        load_meta = lambda: jnp.concatenate(
            [meta_ref[...], jnp.zeros((CHUNK - N_META, D_MODEL), F32)], axis=0)

        def store_meta(val, accumulate):
            h1m_meta_ref[...] = val[0:N_META, :]

        _mixer_tile(load_meta, store_meta, CHUNK, N_META, True, refs)
        st_meta_ref[...] = st_ref[...]
        zc_meta_ref[...] = zc_ref[...]

    @pl.when(first_tile)
    def _():
        st_ref[...] = st_meta_ref[...]
        zc_ref[...] = zc_meta_ref[...]
        h1m_ref[0] = h1m_meta_ref[...]

    def store_tile(val, accumulate):
        h1_ref[0] = h1_ref[0] + val if accumulate else val

    _mixer_tile(lambda: x_ref[0], store_tile, MIXER_TILE, MIXER_TILE, False, refs)


def _ffn_kernel(h1_ref, h1m_ref, fnw_ref, wup_ref, fcw_ref, fcb_ref, wdn_ref, finw_ref,
                out_ref, as_ref):
    T = FFN_TILE
    fnw = fnw_ref[...]

    @pl.when(pl.program_id(1) == 0)
    def _():
        um = _rms(h1m_ref[0], fnw).astype(BF16)
        am = jnp.dot(um, wup_ref[:, 0:D_FF], preferred_element_type=F32)
        as_ref[0:SUBLANES, :] = am[N_META - SUBLANES:N_META, :]

    x = h1_ref[0]
    u = _rms(x, fnw).astype(BF16)
    up = jnp.dot(u, wup_ref[...], preferred_element_type=F32)
    a = up[:, 0:D_FF]
    as_ref[SUBLANES:SUBLANES + T, :] = a
    a1 = as_ref[SUBLANES - 1:SUBLANES - 1 + T, :]
    a2 = as_ref[SUBLANES - 2:SUBLANES - 2 + T, :]
    ac = fcw_ref[0:1, :] * a2 + fcw_ref[1:2, :] * a1 + fcw_ref[2:3, :] * a + fcb_ref[...]
    as_ref[0:SUBLANES, :] = as_ref[T:T + SUBLANES, :]
    gated = (ac * _sigmoid(ac) * up[:, D_FF:]).astype(BF16)
    y = x + jnp.dot(gated, wdn_ref[:, 0:D_MODEL], preferred_element_type=F32)
    out_ref[0] = _rms(y, finw_ref[...])


def _resident(shape):
    return pl.BlockSpec(shape, lambda b, t: (0,) * len(shape), pipeline_mode=pl.Buffered(1))


def _weight(w):
    return jnp.pad(w.astype(BF16), ((0, 0), (0, WEIGHT_LANE_PAD)))


def _rsel_matrix():
    r = np.arange(N_HEADS * HEAD_DIM)[:, None] // HEAD_DIM
    l = np.arange(CHUNK)[None, :] % SUBLANES
    return jnp.asarray(r == l, dtype=BF16)


def kernel(x, meta_tokens, lb_param, attn_norm_w, w_in, hgrn_norm_w, conv_w, w_out, ffn_norm_w, w_up,
           ffn_conv_w, ffn_conv_b, w_down, final_norm_w):
    B, L, D = x.shape
    assert D == D_MODEL and L % MIXER_TILE == 0 and L % FFN_TILE == 0
    assert w_in.shape == (1, D, N_IN_SECTIONS * D) and w_up.shape == (1, D, 2 * D_FF)
    assert SUBLANES * HEAD_DIM == D_MODEL and N_HEADS == SUBLANES

    tile = lambda T: pl.BlockSpec((1, T, D), lambda b, t: (b, t, 0))
    meta_tile = pl.BlockSpec((1, N_META, D), lambda b, t: (b, 0, 0))
    params = pltpu.CompilerParams(dimension_semantics=("arbitrary", "arbitrary"),
                                  vmem_limit_bytes=VMEM_LIMIT_BYTES)

    TM = MIXER_TILE
    n_steps = B * (L // TM)
    step = lambda b, t: b * (L // TM) + t
    assert D % n_steps == 0 and D_FF % W_DOWN_ROWS_PER_STEP == 0 and D_FF // W_DOWN_ROWS_PER_STEP <= n_steps
    wup_rows = D // n_steps
    wup_block = pl.BlockSpec((wup_rows, 2 * D_FF), lambda b, t: (step(b, t), 0))
    wdn_index = lambda b, t: (jnp.minimum(step(b, t), D_FF // W_DOWN_ROWS_PER_STEP - 1), 0)
    slot = [pltpu.VMEM((TM, GROUP_WIDTH), F32)] * 3 + [pltpu.VMEM((TM, GROUP_WIDTH), BF16)] * 4 \
        + [pltpu.VMEM((TM + SUBLANES, GROUP_WIDTH), F32)]
    h1, h1m, w_up_bf16, w_down_bf16 = pl.pallas_call(
        _mixer_kernel,
        grid=(B, L // TM),
        in_specs=[tile(TM), _resident((N_META, D)), _resident((2, D)), _resident((1, D)),
                  _resident((D, N_IN_SECTIONS * D + WEIGHT_LANE_PAD)), _resident((1, HEAD_DIM)),
                  _resident((3, D)), _resident((D, D + WEIGHT_LANE_PAD)),
                  _resident((N_HEADS * HEAD_DIM, CHUNK)), wup_block,
                  pl.BlockSpec((W_DOWN_ROWS_PER_STEP, D), wdn_index)],
        out_specs=[tile(TM), meta_tile, wup_block,
                   pl.BlockSpec((W_DOWN_ROWS_PER_STEP, D + WEIGHT_LANE_PAD), wdn_index)],
        out_shape=[jax.ShapeDtypeStruct((B, L, D), F32), jax.ShapeDtypeStruct((B, N_META, D), F32),
                   jax.ShapeDtypeStruct((D, 2 * D_FF), BF16),
                   jax.ShapeDtypeStruct((D_FF, D + WEIGHT_LANE_PAD), BF16)],
        scratch_shapes=[pltpu.VMEM((N_HEADS, HEAD_DIM, HEAD_DIM), F32),
                        pltpu.VMEM((N_HEADS, HEAD_DIM, HEAD_DIM), F32),
                        pltpu.VMEM((SUBLANES, D), F32),
                        pltpu.VMEM((SUBLANES, D), F32),
                        pltpu.VMEM((TM, D), BF16),
                        pltpu.SMEM((N_GROUPS,), jnp.int32),
                        pltpu.VMEM((N_HEADS, HEAD_DIM, HEAD_DIM), F32),
                        pltpu.VMEM((SUBLANES, D), F32),
                        pltpu.VMEM((N_META, D), F32),
                        *(slot * N_GROUPS)],
        compiler_params=params,
        name="mixer",
    )(x, meta_tokens, lb_param, attn_norm_w, _weight(w_in[0]), hgrn_norm_w, conv_w[0],
      _weight(w_out[0]), _rsel_matrix(), w_up[0], w_down[0])

    TF = FFN_TILE
    out = pl.pallas_call(
        _ffn_kernel,
        grid=(B, L // TF),
        in_specs=[tile(TF), meta_tile, _resident((1, D)), _resident((D, 2 * D_FF)), _resident((3, D_FF)),
                  _resident((1, D_FF)), _resident((D_FF, D + WEIGHT_LANE_PAD)), _resident((1, D))],
        out_specs=tile(TF),
        out_shape=jax.ShapeDtypeStruct((B, L, D), F32),
        scratch_shapes=[pltpu.VMEM((TF + SUBLANES, D_FF), F32)],
        compiler_params=params,
        name="ffn",
    )(h1, h1m, ffn_norm_w, w_up_bf16, ffn_conv_w[0], ffn_conv_b, w_down_bf16, final_norm_w.reshape(1, D))
    return out
```

```python
import numpy as np
import jax
import jax.numpy as jnp
from jax import lax
from jax.experimental import pallas as pl
from jax.experimental.pallas import tpu as pltpu

D_MODEL = 1024
N_META = 16
N_HEADS = 8
HEAD_DIM = 128
D_FF = 2816
N_IN_SECTIONS = 9
N_RECURRENCE_SECTIONS = 3
EPS = 1e-6

SUBLANES = 8
CHUNK = 128
GROUP_HEADS = 8
GROUP_WIDTH = GROUP_HEADS * HEAD_DIM
N_GROUPS = N_HEADS // GROUP_HEADS
MAX_SAFE_EXPONENT = 60.0
LOG2_E = 1.4426950408889634
MIXER_TILE = 512
FFN_TILE = 512
WEIGHT_LANE_PAD = 128
W_DOWN_ROWS_PER_STEP = 128
V7X_VMEM_BYTES = 64 * 1024 * 1024
VMEM_LIMIT_BYTES = V7X_VMEM_BYTES - 4 * 1024 * 1024

F32 = jnp.float32
BF16 = jnp.bfloat16

_NT = (((1,), (1,)), ((), ()))
_TN = (((0,), (0,)), ((), ()))


def _rms(x, w):
    ms = jnp.mean(x * x, axis=-1, keepdims=True)
    return x * lax.rsqrt(ms + EPS) * w


def _sigmoid(x):
    return 1.0 / (1.0 + jnp.exp2(x * -LOG2_E))


def _bcast_rows(ref, lanes, row0, block, offset, nrows):
    pieces = [jnp.broadcast_to(_row(ref, lanes, row0, i * block + offset), (block, HEAD_DIM))
              for i in range(nrows // block)]
    return pieces[0] if len(pieces) == 1 else jnp.concatenate(pieces, axis=0)


def _row(ref, lanes, row0, r):
    group = ref[pl.ds(row0 + r // SUBLANES * SUBLANES, SUBLANES), lanes]
    return group[r % SUBLANES:r % SUBLANES + 1, :]


def _hgrn_chunk_exact(row0, lanes, valid, st, q_ref, g_ref, k_ref, v_ref, rsel):
    rows = pl.ds(row0, CHUNK)
    q = q_ref[rows, lanes]
    G = g_ref[rows, lanes]
    k = k_ref[rows, lanes]
    v = v_ref[rows, lanes]
    row = lax.broadcasted_iota(jnp.int32, (CHUNK, HEAD_DIM), 0)

    sub = row & (SUBLANES - 1)
    ps = []
    for j in range(SUBLANES):
        gj = _bcast_rows(g_ref, lanes, row0, SUBLANES, j, CHUNK)
        kj = _bcast_rows(k_ref, lanes, row0, SUBLANES, j, CHUNK)
        p = q * kj * jnp.exp2(G - gj)
        ps.append(jnp.where(sub >= j, p, 0.0).astype(BF16))
    pcat = jnp.concatenate(ps, axis=1)
    a = jnp.dot(pcat, rsel, preferred_element_type=F32)

    tl_xor = (lax.broadcasted_iota(jnp.int32, (CHUNK, CHUNK), 0)
              ^ lax.broadcasted_iota(jnp.int32, (CHUNK, CHUNK), 1))
    a = jnp.where(tl_xor < SUBLANES, a, 0.0)

    b = 2 * SUBLANES
    while b <= CHUNK:
        gm = _bcast_rows(g_ref, lanes, row0, b, b // 2 - 1, CHUNK)
        e = jnp.exp2(-jnp.abs(G - gm))
        upper = (row & (b // 2)) != 0
        qt = jnp.where(upper, q * e, 0.0).astype(BF16)
        kt = jnp.where(upper, 0.0, k * e).astype(BF16)
        ab = lax.dot_general(qt, kt, _NT, preferred_element_type=F32)
        a = a + (ab if b == CHUNK else jnp.where(tl_xor < b, ab, 0.0))
        b *= 2

    qi = (q * jnp.exp2(G)).astype(BF16)
    o = lax.dot_general(qi, st.astype(BF16), _NT, preferred_element_type=F32)
    o = o + jnp.dot(a.astype(BF16), v, preferred_element_type=F32)

    glast = _row(g_ref, lanes, row0, valid - 1)
    kd = k * jnp.exp2(glast - G)
    if valid < CHUNK:
        kd = jnp.where(row < valid, kd, 0.0)
    st = st * jnp.exp2(glast) + lax.dot_general(v, kd.astype(BF16), _TN, preferred_element_type=F32)
    return o, st


def _hgrn_chunk_fast(row0, lanes, st, q_ref, g_ref, k_ref, v_ref):
    rows = pl.ds(row0, CHUNK)
    q = q_ref[rows, lanes]
    G = g_ref[rows, lanes]
    k = k_ref[rows, lanes]
    v = v_ref[rows, lanes]
    gmid = g_ref[pl.ds(row0 + CHUNK // 2 - 1, 1), lanes]
    glast = g_ref[pl.ds(row0 + CHUNK - 1, 1), lanes]
    d = G - gmid
    qh = q * jnp.exp2(d)
    kh = k * jnp.exp2(-d)
    kh_bf = kh.astype(BF16)
    rhs = jnp.concatenate([kh_bf, (st * jnp.exp2(gmid)).astype(BF16)], axis=0)
    both = lax.dot_general(qh.astype(BF16), rhs, _NT, preferred_element_type=F32)
    causal = (lax.broadcasted_iota(jnp.int32, (CHUNK, CHUNK), 1)
              <= lax.broadcasted_iota(jnp.int32, (CHUNK, CHUNK), 0))
    a = jnp.where(causal, both[:, 0:CHUNK], 0.0)
    o = both[:, CHUNK:] + jnp.dot(a.astype(BF16), v, preferred_element_type=F32)
    kd = (kh * jnp.exp2(glast - gmid)).astype(BF16)
    st = st * jnp.exp2(glast) + lax.dot_general(v, kd, _TN, preferred_element_type=F32)
    return o, st


def _project_sections(u, win_ref, c0, first, last):
    if GROUP_WIDTH == D_MODEL:
        p = jnp.dot(u, win_ref[:, first * D_MODEL:last * D_MODEL], preferred_element_type=F32)
        return [p[:, i * D_MODEL:(i + 1) * D_MODEL] for i in range(last - first)]
    return [jnp.dot(u, win_ref[:, i * D_MODEL + c0:i * D_MODEL + c0 + GROUP_WIDTH], preferred_element_type=F32)
            for i in range(first, last)]


def _project_group(u, g, R, valid, prm, slot, recurrence_inputs):
    lb_ref, win_ref, cw_ref, zc_ref, safe_ref = prm
    q_ref, g_ref, k_ref, v_ref, gs_ref, sa_ref, mb_ref, zs_ref = slot
    c0 = g * GROUP_WIDTH
    cols = slice(c0, c0 + GROUP_WIDTH)

    sec = lambda first, last: _project_sections(u, win_ref, c0, first, last)

    if not recurrence_inputs:
        _project_gates(sec(N_RECURRENCE_SECTIONS, N_IN_SECTIONS), cols, R, valid, cw_ref, zc_ref, slot)
        return

    qv, fv, iv = sec(0, N_RECURRENCE_SECTIONS)
    q_ref[0:R, :] = qv * _sigmoid(qv)

    f = lb_ref[0:1, cols] + lb_ref[1:2, cols] * _sigmoid(fv)
    k_ref[0:R, :] = 1.0 - f
    lf = jnp.log2(f)
    tri = (lax.broadcasted_iota(jnp.int32, (CHUNK, CHUNK), 1)
           <= lax.broadcasted_iota(jnp.int32, (CHUNK, CHUNK), 0)).astype(BF16)
    tri2 = jnp.concatenate([tri, tri], axis=1)
    worst = jnp.zeros((1, GROUP_WIDTH), F32)
    for c in range(R // CHUNK):
        lfc = lf[c * CHUNK:(c + 1) * CHUNK, :]
        hi = lfc.astype(BF16)
        lo = (lfc - hi.astype(F32)).astype(BF16)
        gc = jnp.dot(tri2, jnp.concatenate([hi, lo], axis=0),
                     preferred_element_type=F32)
        g_ref[c * CHUNK:(c + 1) * CHUNK, :] = gc
        gmid = gc[CHUNK // 2 - 1:CHUNK // 2, :]
        glast = gc[CHUNK - 1:CHUNK, :]
        worst = jnp.maximum(worst, jnp.maximum(-gmid, gmid - glast))
    safe_ref[g] = (jnp.max(worst) <= MAX_SAFE_EXPONENT * LOG2_E).astype(jnp.int32)

    v_ref[0:R, :] = iv.astype(BF16)


def _project_gates(sections, cols, R, valid, cw_ref, zc_ref, slot):
    gs_ref, sa_ref, mb_ref, zs_ref = slot[4:8]
    gv, bg, cg, hc, ga, gb = sections
    gs_ref[0:R, :] = (gv * _sigmoid(gv)).astype(BF16)

    z = cg * hc
    zs_ref[0:SUBLANES, :] = zc_ref[:, cols]
    zs_ref[SUBLANES:SUBLANES + R, :] = z
    z1 = zs_ref[SUBLANES - 1:SUBLANES - 1 + R, :]
    z2 = zs_ref[SUBLANES - 2:SUBLANES - 2 + R, :]
    yb = bg * (cw_ref[0:1, cols] * z2 + cw_ref[1:2, cols] * z1 + cw_ref[2:3, cols] * z)
    zc_ref[:, cols] = zs_ref[valid:valid + SUBLANES, :]

    sa_ref[0:R, :] = _sigmoid(ga).astype(BF16)
    mb_ref[0:R, :] = (_sigmoid(gb) * yb).astype(BF16)


def _finish_chunk(row0, h, lanes, o, hnw, mg_ref, slot):
    gs_ref, sa_ref, mb_ref = slot[4:7]
    rows = pl.ds(row0, CHUNK)
    m = (_rms(o, hnw) * gs_ref[rows, lanes].astype(F32) * sa_ref[rows, lanes].astype(F32)
         + mb_ref[rows, lanes].astype(F32))
    mg_ref[rows, h * HEAD_DIM:(h + 1) * HEAD_DIM] = m.astype(BF16)


def _recur_group_fast(g, R, hnw, st_ref, st_old_ref, mg_ref, slot):
    q_ref, g_ref, k_ref, v_ref = slot[0:4]
    for hh in range(GROUP_HEADS):
        h = GROUP_HEADS * g + hh
        lanes = slice(hh * HEAD_DIM, (hh + 1) * HEAD_DIM)
        st = st_ref[h]
        st_old_ref[h] = st
        for c in range(R // CHUNK):
            o, st = _hgrn_chunk_fast(c * CHUNK, lanes, st, q_ref, g_ref, k_ref, v_ref)
            mg_ref[c * CHUNK:(c + 1) * CHUNK, h * HEAD_DIM:(h + 1) * HEAD_DIM] = _rms(o, hnw).astype(BF16)
        st_ref[h] = st


def _finish_group_fast(g, R, mg_ref, slot):
    gs_ref, sa_ref, mb_ref = slot[4:7]
    cols = slice(g * GROUP_WIDTH, (g + 1) * GROUP_WIDTH)
    mg_ref[0:R, cols] = mg_ref[0:R, cols] * gs_ref[0:R, :] * sa_ref[0:R, :] + mb_ref[0:R, :]


def _recur_group_exact(g, R, valid, hnw, rsel, st_in_ref, st_ref, mg_ref, slot):
    q_ref, g_ref, k_ref, v_ref = slot[0:4]
    for hh in range(GROUP_HEADS):
        h = GROUP_HEADS * g + hh
        lanes = slice(hh * HEAD_DIM, (hh + 1) * HEAD_DIM)
        st = st_in_ref[h]
        if valid < R:
            o, st = _hgrn_chunk_exact(0, lanes, valid, st, q_ref, g_ref, k_ref, v_ref, rsel)
            _finish_chunk(0, h, lanes, o, hnw, mg_ref, slot)
        else:
            def chunk_body(c, st):
                row0 = pl.multiple_of(c * CHUNK, CHUNK)
                o, st = _hgrn_chunk_exact(row0, lanes, CHUNK, st, q_ref, g_ref, k_ref, v_ref, rsel)
                _finish_chunk(row0, h, lanes, o, hnw, mg_ref, slot)
                return st

            st = lax.fori_loop(0, R // CHUNK, chunk_body, st)
        st_ref[h] = st


def _out_partial(g, R, mg_ref, wout_ref):
    rows = slice(g * GROUP_WIDTH, (g + 1) * GROUP_WIDTH)
    return jnp.dot(mg_ref[0:R, rows], wout_ref[rows, 0:D_MODEL], preferred_element_type=F32)


def _mixer_tile(load_x, store_out, R, valid, is_meta, refs):
    (lbp_ref, anw_ref, win_ref, hnw_ref, cw_ref, wout_ref, rsel_ref,
     st_ref, st_old_ref, zc_ref, lb_ref, mg_ref, safe_ref, slots) = refs
    prm = (lb_ref, win_ref, cw_ref, zc_ref, safe_ref)
    u = _rms(load_x(), anw_ref[...]).astype(BF16)
    rsel = rsel_ref[...]
    hnw = hnw_ref[...]
    project = lambda g, first: _project_group(u, g, R, valid, prm, slots[g], first)

    lbp = lbp_ref[...]
    mx = jnp.max(lbp, axis=0, keepdims=True)
    ex = jnp.exp(lbp - mx)
    lb = ex[0:1, :] / jnp.sum(ex, axis=0, keepdims=True)
    lb_ref[0:1, :] = lb
    lb_ref[1:2, :] = 1.0 - lb

    if is_meta:
        for g in range(N_GROUPS):
            project(g, True)
            project(g, False)
            _recur_group_exact(g, R, valid, hnw, rsel, st_ref, st_ref, mg_ref, slots[g])
        acc = load_x()
        for g in range(N_GROUPS):
            acc = acc + _out_partial(g, R, mg_ref, wout_ref)
        store_out(acc, False)
        return

    def store_early():
        acc = load_x()
        for g in range(N_GROUPS - 1):
            acc = acc + _out_partial(g, R, mg_ref, wout_ref)
        store_out(acc, False)

    for g in range(N_GROUPS):
        project(g, True)
        if g > 0:
            _finish_group_fast(g - 1, R, mg_ref, slots[g - 1])
        if g == N_GROUPS - 1:
            store_early()
        _recur_group_fast(g, R, hnw, st_ref, st_old_ref, mg_ref, slots[g])
        project(g, False)
    _finish_group_fast(N_GROUPS - 1, R, mg_ref, slots[N_GROUPS - 1])
    for g in range(N_GROUPS):
        def redo(g=g):
            _recur_group_exact(g, R, valid, hnw, rsel, st_old_ref, st_ref, mg_ref, slots[g])
            if g < N_GROUPS - 1:
                store_early()

        pl.when(safe_ref[g] == 0)(redo)
    store_out(_out_partial(N_GROUPS - 1, R, mg_ref, wout_ref), True)


def _mixer_kernel(x_ref, meta_ref, lbp_ref, anw_ref, win_ref, hnw_ref, cw_ref, wout_ref, rsel_ref,
                  wup_f32_ref, wdn_f32_ref, h1_ref, h1m_ref, wup_ref, wdn_ref,
                  st_ref, st_old_ref, zc_ref, lb_ref, mg_ref, safe_ref,
                  st_meta_ref, zc_meta_ref, h1m_meta_ref, *slots):
    wup_ref[...] = wup_f32_ref[...].astype(BF16)
    wdn_ref[:, 0:D_MODEL] = wdn_f32_ref[...].astype(BF16)
    wdn_ref[:, D_MODEL:] = jnp.zeros((W_DOWN_ROWS_PER_STEP, WEIGHT_LANE_PAD), BF16)

    n = len(slots) // N_GROUPS
    refs = (lbp_ref, anw_ref, win_ref, hnw_ref, cw_ref, wout_ref, rsel_ref,
            st_ref, st_old_ref, zc_ref, lb_ref, mg_ref, safe_ref,
            [slots[i * n:(i + 1) * n] for i in range(N_GROUPS)])

    first_tile = pl.program_id(1) == 0

    @pl.when(jnp.logical_and(first_tile, pl.program_id(0) == 0))
    def _():
        st_ref[...] = jnp.zeros_like(st_ref)
        zc_ref[...] = jnp.zeros_like(zc_ref)
        load_meta = lambda: jnp.concatenate(
            [meta_ref[...], jnp.zeros((CHUNK - N_META, D_MODEL), F32)], axis=0)

        def store_meta(val, accumulate):
            h1m_meta_ref[...] = val[0:N_META, :]

        _mixer_tile(load_meta, store_meta, CHUNK, N_META, True, refs)
        st_meta_ref[...] = st_ref[...]
        zc_meta_ref[...] = zc_ref[...]

    @pl.when(first_tile)
    def _():
        st_ref[...] = st_meta_ref[...]
        zc_ref[...] = zc_meta_ref[...]
        h1m_ref[0] = h1m_meta_ref[...]

    def store_tile(val, accumulate):
        h1_ref[0] = h1_ref[0] + val if accumulate else val

    _mixer_tile(lambda: x_ref[0], store_tile, MIXER_TILE, MIXER_TILE, False, refs)


def _ffn_kernel(h1_ref, h1m_ref, fnw_ref, wup_ref, fcw_ref, fcb_ref, wdn_ref, finw_ref,
                out_ref, as_ref):
    T = FFN_TILE
    fnw = fnw_ref[...]

    @pl.when(pl.program_id(1) == 0)
    def _():
        um = _rms(h1m_ref[0], fnw).astype(BF16)
        am = jnp.dot(um, wup_ref[:, 0:D_FF], preferred_element_type=F32)
        as_ref[0:SUBLANES, :] = am[N_META - SUBLANES:N_META, :]

    x = h1_ref[0]
    u = _rms(x, fnw).astype(BF16)
    up = jnp.dot(u, wup_ref[...], preferred_element_type=F32)
    a = up[:, 0:D_FF]
    as_ref[SUBLANES:SUBLANES + T, :] = a
    a1 = as_ref[SUBLANES - 1:SUBLANES - 1 + T, :]
    a2 = as_ref[SUBLANES - 2:SUBLANES - 2 + T, :]
    ac = fcw_ref[0:1, :] * a2 + fcw_ref[1:2, :] * a1 + fcw_ref[2:3, :] * a + fcb_ref[...]
    as_ref[0:SUBLANES, :] = as_ref[T:T + SUBLANES, :]
    gated = (ac * _sigmoid(ac) * up[:, D_FF:]).astype(BF16)
    y = x + jnp.dot(gated, wdn_ref[:, 0:D_MODEL], preferred_element_type=F32)
    out_ref[0] = _rms(y, finw_ref[...])


def _resident(shape):
    return pl.BlockSpec(shape, lambda b, t: (0,) * len(shape), pipeline_mode=pl.Buffered(1))


def _weight(w):
    return jnp.pad(w.astype(BF16), ((0, 0), (0, WEIGHT_LANE_PAD)))


def _rsel_matrix():
    r = np.arange(N_HEADS * HEAD_DIM)[:, None] // HEAD_DIM
    l = np.arange(CHUNK)[None, :] % SUBLANES
    return jnp.asarray(r == l, dtype=BF16)


def kernel(x, meta_tokens, lb_param, attn_norm_w, w_in, hgrn_norm_w, conv_w, w_out, ffn_norm_w, w_up,
           ffn_conv_w, ffn_conv_b, w_down, final_norm_w):
    B, L, D = x.shape
    assert D == D_MODEL and L % MIXER_TILE == 0 and L % FFN_TILE == 0
    assert w_in.shape == (1, D, N_IN_SECTIONS * D) and w_up.shape == (1, D, 2 * D_FF)
    assert SUBLANES * HEAD_DIM == D_MODEL and N_HEADS == SUBLANES

    tile = lambda T: pl.BlockSpec((1, T, D), lambda b, t: (b, t, 0))
    meta_tile = pl.BlockSpec((1, N_META, D), lambda b, t: (b, 0, 0))
    params = pltpu.CompilerParams(dimension_semantics=("arbitrary", "arbitrary"),
                                  vmem_limit_bytes=VMEM_LIMIT_BYTES)

    TM = MIXER_TILE
    n_steps = B * (L // TM)
    step = lambda b, t: b * (L // TM) + t
    assert D % n_steps == 0 and D_FF % W_DOWN_ROWS_PER_STEP == 0 and D_FF // W_DOWN_ROWS_PER_STEP <= n_steps
    wup_rows = D // n_steps
    wup_block = pl.BlockSpec((wup_rows, 2 * D_FF), lambda b, t: (step(b, t), 0))
    wdn_index = lambda b, t: (jnp.minimum(step(b, t), D_FF // W_DOWN_ROWS_PER_STEP - 1), 0)
    slot = [pltpu.VMEM((TM, GROUP_WIDTH), F32)] * 3 + [pltpu.VMEM((TM, GROUP_WIDTH), BF16)] * 4 \
        + [pltpu.VMEM((TM + SUBLANES, GROUP_WIDTH), F32)]
    h1, h1m, w_up_bf16, w_down_bf16 = pl.pallas_call(
        _mixer_kernel,
        grid=(B, L // TM),
        in_specs=[tile(TM), _resident((N_META, D)), _resident((2, D)), _resident((1, D)),
                  _resident((D, N_IN_SECTIONS * D + WEIGHT_LANE_PAD)), _resident((1, HEAD_DIM)),
                  _resident((3, D)), _resident((D, D + WEIGHT_LANE_PAD)),
                  _resident((N_HEADS * HEAD_DIM, CHUNK)), wup_block,
                  pl.BlockSpec((W_DOWN_ROWS_PER_STEP, D), wdn_index)],
        out_specs=[tile(TM), meta_tile, wup_block,
                   pl.BlockSpec((W_DOWN_ROWS_PER_STEP, D + WEIGHT_LANE_PAD), wdn_index)],
        out_shape=[jax.ShapeDtypeStruct((B, L, D), F32), jax.ShapeDtypeStruct((B, N_META, D), F32),
                   jax.ShapeDtypeStruct((D, 2 * D_FF), BF16),
                   jax.ShapeDtypeStruct((D_FF, D + WEIGHT_LANE_PAD), BF16)],
        scratch_shapes=[pltpu.VMEM((N_HEADS, HEAD_DIM, HEAD_DIM), F32),
                        pltpu.VMEM((N_HEADS, HEAD_DIM, HEAD_DIM), F32),
                        pltpu.VMEM((SUBLANES, D), F32),
                        pltpu.VMEM((SUBLANES, D), F32),
                        pltpu.VMEM((TM, D), BF16),
                        pltpu.SMEM((N_GROUPS,), jnp.int32),
                        pltpu.VMEM((N_HEADS, HEAD_DIM, HEAD_DIM), F32),
                        pltpu.VMEM((SUBLANES, D), F32),
                        pltpu.VMEM((N_META, D), F32),
                        *(slot * N_GROUPS)],
        compiler_params=params,
        name="mixer",
    )(x, meta_tokens, lb_param, attn_norm_w, _weight(w_in[0]), hgrn_norm_w, conv_w[0],
      _weight(w_out[0]), _rsel_matrix(), w_up[0], w_down[0])

    TF = FFN_TILE
    out = pl.pallas_call(
        _ffn_kernel,
        grid=(B, L // TF),
        in_specs=[tile(TF), meta_tile, _resident((1, D)), _resident((D, 2 * D_FF)), _resident((3, D_FF)),
                  _resident((1, D_FF)), _resident((D_FF, D + WEIGHT_LANE_PAD)), _resident((1, D))],
        out_specs=tile(TF),
        out_shape=jax.ShapeDtypeStruct((B, L, D), F32),
        scratch_shapes=[pltpu.VMEM((TF + SUBLANES, D_FF), F32)],
        compiler_params=params,
        name="ffn",
    )(h1, h1m, ffn_norm_w, w_up_bf16, ffn_conv_w[0], ffn_conv_b, w_down_bf16, final_norm_w.reshape(1, D))
    return out
```

```python
import numpy as np
import jax
import jax.numpy as jnp
from jax import lax
from jax.experimental import pallas as pl
from jax.experimental.pallas import tpu as pltpu

D_MODEL = 1024
N_META = 16
N_HEADS = 8
HEAD_DIM = 128
D_FF = 2816
N_IN_SECTIONS = 9
N_RECURRENCE_SECTIONS = 3
EPS = 1e-6

SUBLANES = 8
CHUNK = 128
GROUP_HEADS = 8
GROUP_WIDTH = GROUP_HEADS * HEAD_DIM
N_GROUPS = N_HEADS // GROUP_HEADS
MAX_SAFE_EXPONENT = 60.0
LOG2_E = 1.4426950408889634
MIXER_TILE = 512
FFN_TILE = 512
WEIGHT_LANE_PAD = 128
W_DOWN_ROWS_PER_STEP = 128
V7X_VMEM_BYTES = 64 * 1024 * 1024
VMEM_LIMIT_BYTES = V7X_VMEM_BYTES - 2 * 1024 * 1024

F32 = jnp.float32
BF16 = jnp.bfloat16

_NT = (((1,), (1,)), ((), ()))
_TN = (((0,), (0,)), ((), ()))


def _rms(x, w):
    ms = jnp.mean(x * x, axis=-1, keepdims=True)
    return x * lax.rsqrt(ms + EPS) * w


def _sigmoid(x):
    return 1.0 / (1.0 + jnp.exp2(x * -LOG2_E))


def _bcast_rows(ref, lanes, row0, block, offset, nrows):
    pieces = [jnp.broadcast_to(_row(ref, lanes, row0, i * block + offset), (block, HEAD_DIM))
              for i in range(nrows // block)]
    return pieces[0] if len(pieces) == 1 else jnp.concatenate(pieces, axis=0)


def _row(ref, lanes, row0, r):
    group = ref[pl.ds(row0 + r // SUBLANES * SUBLANES, SUBLANES), lanes]
    return group[r % SUBLANES:r % SUBLANES + 1, :]


def _hgrn_chunk_exact(row0, lanes, valid, st, q_ref, g_ref, k_ref, v_ref, rsel):
    rows = pl.ds(row0, CHUNK)
    q = q_ref[rows, lanes]
    G = g_ref[rows, lanes]
    k = k_ref[rows, lanes]
    v = v_ref[rows, lanes]
    row = lax.broadcasted_iota(jnp.int32, (CHUNK, HEAD_DIM), 0)

    sub = row & (SUBLANES - 1)
    ps = []
    for j in range(SUBLANES):
        gj = _bcast_rows(g_ref, lanes, row0, SUBLANES, j, CHUNK)
        kj = _bcast_rows(k_ref, lanes, row0, SUBLANES, j, CHUNK)
        p = q * kj * jnp.exp2(G - gj)
        ps.append(jnp.where(sub >= j, p, 0.0).astype(BF16))
    pcat = jnp.concatenate(ps, axis=1)
    a = jnp.dot(pcat, rsel, preferred_element_type=F32)

    tl_xor = (lax.broadcasted_iota(jnp.int32, (CHUNK, CHUNK), 0)
              ^ lax.broadcasted_iota(jnp.int32, (CHUNK, CHUNK), 1))
    a = jnp.where(tl_xor < SUBLANES, a, 0.0)

    b = 2 * SUBLANES
    while b <= CHUNK:
        gm = _bcast_rows(g_ref, lanes, row0, b, b // 2 - 1, CHUNK)
        e = jnp.exp2(-jnp.abs(G - gm))
        upper = (row & (b // 2)) != 0
        qt = jnp.where(upper, q * e, 0.0).astype(BF16)
        kt = jnp.where(upper, 0.0, k * e).astype(BF16)
        ab = lax.dot_general(qt, kt, _NT, preferred_element_type=F32)
        a = a + (ab if b == CHUNK else jnp.where(tl_xor < b, ab, 0.0))
        b *= 2

    qi = (q * jnp.exp2(G)).astype(BF16)
    o = lax.dot_general(qi, st.astype(BF16), _NT, preferred_element_type=F32)
    o = o + jnp.dot(a.astype(BF16), v, preferred_element_type=F32)

    glast = _row(g_ref, lanes, row0, valid - 1)
    kd = k * jnp.exp2(glast - G)
    if valid < CHUNK:
        kd = jnp.where(row < valid, kd, 0.0)
    st = st * jnp.exp2(glast) + lax.dot_general(v, kd.astype(BF16), _TN, preferred_element_type=F32)
    return o, st


def _hgrn_chunk_fast(row0, lanes, st, q_ref, g_ref, k_ref, v_ref):
    rows = pl.ds(row0, CHUNK)
    q = q_ref[rows, lanes]
    G = g_ref[rows, lanes]
    k = k_ref[rows, lanes]
    v = v_ref[rows, lanes]
    gmid = g_ref[pl.ds(row0 + CHUNK // 2 - 1, 1), lanes]
    glast = g_ref[pl.ds(row0 + CHUNK - 1, 1), lanes]
    d = G - gmid
    qh = q * jnp.exp2(d)
    kh = k * jnp.exp2(-d)
    a = lax.dot_general(qh.astype(BF16), kh.astype(BF16), _NT, preferred_element_type=F32)
    causal = (lax.broadcasted_iota(jnp.int32, (CHUNK, CHUNK), 1)
              <= lax.broadcasted_iota(jnp.int32, (CHUNK, CHUNK), 0))
    a = jnp.where(causal, a, 0.0)
    qi = (qh * jnp.exp2(gmid)).astype(BF16)
    o = lax.dot_general(qi, st.astype(BF16), _NT, preferred_element_type=F32)
    o = o + jnp.dot(a.astype(BF16), v, preferred_element_type=F32)
    kd = (kh * jnp.exp2(glast - gmid)).astype(BF16)
    st = st * jnp.exp2(glast) + lax.dot_general(v, kd, _TN, preferred_element_type=F32)
    return o, st


def _project_sections(u, win_ref, c0, first, last):
    if GROUP_WIDTH == D_MODEL:
        p = jnp.dot(u, win_ref[:, first * D_MODEL:last * D_MODEL], preferred_element_type=F32)
        return [p[:, i * D_MODEL:(i + 1) * D_MODEL] for i in range(last - first)]
    return [jnp.dot(u, win_ref[:, i * D_MODEL + c0:i * D_MODEL + c0 + GROUP_WIDTH], preferred_element_type=F32)
            for i in range(first, last)]


def _project_group(u, g, R, valid, prm, slot, recurrence_inputs):
    lb_ref, win_ref, cw_ref, zc_ref, safe_ref = prm
    q_ref, g_ref, k_ref, v_ref, gs_ref, sa_ref, mb_ref, zs_ref = slot
    c0 = g * GROUP_WIDTH
    cols = slice(c0, c0 + GROUP_WIDTH)

    sec = lambda first, last: _project_sections(u, win_ref, c0, first, last)

    if not recurrence_inputs:
        _project_gates(sec(N_RECURRENCE_SECTIONS, N_IN_SECTIONS), cols, R, valid, cw_ref, zc_ref, slot)
        return

    qv, fv, iv = sec(0, N_RECURRENCE_SECTIONS)
    q_ref[0:R, :] = qv * _sigmoid(qv)

    f = lb_ref[0:1, cols] + lb_ref[1:2, cols] * _sigmoid(fv)
    k_ref[0:R, :] = 1.0 - f
    lf = jnp.log2(f)
    tri = (lax.broadcasted_iota(jnp.int32, (CHUNK, CHUNK), 1)
           <= lax.broadcasted_iota(jnp.int32, (CHUNK, CHUNK), 0)).astype(BF16)
    tri2 = jnp.concatenate([tri, tri], axis=1)
    worst = jnp.zeros((1, GROUP_WIDTH), F32)
    for c in range(R // CHUNK):
        lfc = lf[c * CHUNK:(c + 1) * CHUNK, :]
        hi = lfc.astype(BF16)
        lo = (lfc - hi.astype(F32)).astype(BF16)
        gc = jnp.dot(tri2, jnp.concatenate([hi, lo], axis=0),
                     preferred_element_type=F32)
        g_ref[c * CHUNK:(c + 1) * CHUNK, :] = gc
        gmid = gc[CHUNK // 2 - 1:CHUNK // 2, :]
        glast = gc[CHUNK - 1:CHUNK, :]
        worst = jnp.maximum(worst, jnp.maximum(-gmid, gmid - glast))
    safe_ref[g] = (jnp.max(worst) <= MAX_SAFE_EXPONENT * LOG2_E).astype(jnp.int32)

    v_ref[0:R, :] = iv.astype(BF16)


def _project_gates(sections, cols, R, valid, cw_ref, zc_ref, slot):
    gs_ref, sa_ref, mb_ref, zs_ref = slot[4:8]
    gv, bg, cg, hc, ga, gb = sections
    gs_ref[0:R, :] = (gv * _sigmoid(gv)).astype(BF16)

    z = cg * hc
    zs_ref[0:SUBLANES, :] = zc_ref[:, cols]
    zs_ref[SUBLANES:SUBLANES + R, :] = z
    z1 = zs_ref[SUBLANES - 1:SUBLANES - 1 + R, :]
    z2 = zs_ref[SUBLANES - 2:SUBLANES - 2 + R, :]
    yb = bg * (cw_ref[0:1, cols] * z2 + cw_ref[1:2, cols] * z1 + cw_ref[2:3, cols] * z)
    zc_ref[:, cols] = zs_ref[valid:valid + SUBLANES, :]

    sa_ref[0:R, :] = _sigmoid(ga).astype(BF16)
    mb_ref[0:R, :] = (_sigmoid(gb) * yb).astype(BF16)


def _finish_chunk(row0, h, lanes, o, hnw, mg_ref, slot):
    gs_ref, sa_ref, mb_ref = slot[4:7]
    rows = pl.ds(row0, CHUNK)
    m = (_rms(o, hnw) * gs_ref[rows, lanes].astype(F32) * sa_ref[rows, lanes].astype(F32)
         + mb_ref[rows, lanes].astype(F32))
    mg_ref[rows, h * HEAD_DIM:(h + 1) * HEAD_DIM] = m.astype(BF16)


def _recur_group_fast(g, R, hnw, st_ref, st_old_ref, mg_ref, slot):
    q_ref, g_ref, k_ref, v_ref = slot[0:4]
    for hh in range(GROUP_HEADS):
        h = GROUP_HEADS * g + hh
        lanes = slice(hh * HEAD_DIM, (hh + 1) * HEAD_DIM)
        st = st_ref[h]
        st_old_ref[h] = st
        for c in range(R // CHUNK):
            o, st = _hgrn_chunk_fast(c * CHUNK, lanes, st, q_ref, g_ref, k_ref, v_ref)
            mg_ref[c * CHUNK:(c + 1) * CHUNK, h * HEAD_DIM:(h + 1) * HEAD_DIM] = _rms(o, hnw).astype(BF16)
        st_ref[h] = st


def _finish_group_fast(g, R, mg_ref, slot):
    gs_ref, sa_ref, mb_ref = slot[4:7]
    cols = slice(g * GROUP_WIDTH, (g + 1) * GROUP_WIDTH)
    mg_ref[0:R, cols] = mg_ref[0:R, cols] * gs_ref[0:R, :] * sa_ref[0:R, :] + mb_ref[0:R, :]


def _recur_group_exact(g, R, valid, hnw, rsel, st_in_ref, st_ref, mg_ref, slot):
    q_ref, g_ref, k_ref, v_ref = slot[0:4]
    for hh in range(GROUP_HEADS):
        h = GROUP_HEADS * g + hh
        lanes = slice(hh * HEAD_DIM, (hh + 1) * HEAD_DIM)
        st = st_in_ref[h]
        if valid < R:
            o, st = _hgrn_chunk_exact(0, lanes, valid, st, q_ref, g_ref, k_ref, v_ref, rsel)
            _finish_chunk(0, h, lanes, o, hnw, mg_ref, slot)
        else:
            def chunk_body(c, st):
                row0 = pl.multiple_of(c * CHUNK, CHUNK)
                o, st = _hgrn_chunk_exact(row0, lanes, CHUNK, st, q_ref, g_ref, k_ref, v_ref, rsel)
                _finish_chunk(row0, h, lanes, o, hnw, mg_ref, slot)
                return st

            st = lax.fori_loop(0, R // CHUNK, chunk_body, st)
        st_ref[h] = st


def _out_partial(g, R, mg_ref, wout_ref):
    rows = slice(g * GROUP_WIDTH, (g + 1) * GROUP_WIDTH)
    return jnp.dot(mg_ref[0:R, rows], wout_ref[rows, 0:D_MODEL], preferred_element_type=F32)


def _mixer_tile(load_x, store_out, R, valid, is_meta, refs):
    (lbp_ref, anw_ref, win_ref, hnw_ref, cw_ref, wout_ref, rsel_ref,
     st_ref, st_old_ref, zc_ref, lb_ref, mg_ref, safe_ref, slots) = refs
    prm = (lb_ref, win_ref, cw_ref, zc_ref, safe_ref)
    u = _rms(load_x(), anw_ref[...]).astype(BF16)
    rsel = rsel_ref[...]
    hnw = hnw_ref[...]
    project = lambda g, first: _project_group(u, g, R, valid, prm, slots[g], first)

    lbp = lbp_ref[...]
    mx = jnp.max(lbp, axis=0, keepdims=True)
    ex = jnp.exp(lbp - mx)
    lb = ex[0:1, :] / jnp.sum(ex, axis=0, keepdims=True)
    lb_ref[0:1, :] = lb
    lb_ref[1:2, :] = 1.0 - lb

    if is_meta:
        for g in range(N_GROUPS):
            project(g, True)
            project(g, False)
            _recur_group_exact(g, R, valid, hnw, rsel, st_ref, st_ref, mg_ref, slots[g])
        acc = load_x()
        for g in range(N_GROUPS):
            acc = acc + _out_partial(g, R, mg_ref, wout_ref)
        store_out(acc, False)
        return

    def store_early():
        acc = load_x()
        for g in range(N_GROUPS - 1):
            acc = acc + _out_partial(g, R, mg_ref, wout_ref)
        store_out(acc, False)

    for g in range(N_GROUPS):
        project(g, True)
        if g > 0:
            _finish_group_fast(g - 1, R, mg_ref, slots[g - 1])
        if g == N_GROUPS - 1:
            store_early()
        _recur_group_fast(g, R, hnw, st_ref, st_old_ref, mg_ref, slots[g])
        project(g, False)
    _finish_group_fast(N_GROUPS - 1, R, mg_ref, slots[N_GROUPS - 1])
    for g in range(N_GROUPS):
        def redo(g=g):
            _recur_group_exact(g, R, valid, hnw, rsel, st_old_ref, st_ref, mg_ref, slots[g])
            if g < N_GROUPS - 1:
                store_early()

        pl.when(safe_ref[g] == 0)(redo)
    store_out(_out_partial(N_GROUPS - 1, R, mg_ref, wout_ref), True)


def _mixer_kernel(x_ref, meta_ref, lbp_ref, anw_ref, win_ref, hnw_ref, cw_ref, wout_ref, rsel_ref,
                  wup_f32_ref, wdn_f32_ref, h1_ref, h1m_ref, wup_ref, wdn_ref,
                  st_ref, st_old_ref, zc_ref, lb_ref, mg_ref, safe_ref,
                  st_meta_ref, zc_meta_ref, h1m_meta_ref, *slots):
    wup_ref[...] = wup_f32_ref[...].astype(BF16)
    wdn_ref[:, 0:D_MODEL] = wdn_f32_ref[...].astype(BF16)
    wdn_ref[:, D_MODEL:] = jnp.zeros((W_DOWN_ROWS_PER_STEP, WEIGHT_LANE_PAD), BF16)

    n = len(slots) // N_GROUPS
    refs = (lbp_ref, anw_ref, win_ref, hnw_ref, cw_ref, wout_ref, rsel_ref,
            st_ref, st_old_ref, zc_ref, lb_ref, mg_ref, safe_ref,
            [slots[i * n:(i + 1) * n] for i in range(N_GROUPS)])

    first_tile = pl.program_id(1) == 0

    @pl.when(jnp.logical_and(first_tile, pl.program_id(0) == 0))
    def _():
        st_ref[...] = jnp.zeros_like(st_ref)
        zc_ref[...] = jnp.zeros_like(zc_ref)
        load_meta = lambda: jnp.concatenate(
            [meta_ref[...], jnp.zeros((CHUNK - N_META, D_MODEL), F32)], axis=0)

        def store_meta(val, accumulate):
            h1m_meta_ref[...] = val[0:N_META, :]

        _mixer_tile(load_meta, store_meta, CHUNK, N_META, True, refs)
        st_meta_ref[...] = st_ref[...]
        zc_meta_ref[...] = zc_ref[...]

    @pl.when(first_tile)
    def _():
        st_ref[...] = st_meta_ref[...]
        zc_ref[...] = zc_meta_ref[...]
        h1m_ref[0] = h1m_meta_ref[...]

    def store_tile(val, accumulate):
        h1_ref[0] = h1_ref[0] + val if accumulate else val

    _mixer_tile(lambda: x_ref[0], store_tile, MIXER_TILE, MIXER_TILE, False, refs)


def _ffn_kernel(h1_ref, h1m_ref, fnw_ref, wup_ref, fcw_ref, fcb_ref, wdn_ref, finw_ref,
                out_ref, as_ref):
    T = FFN_TILE
    fnw = fnw_ref[...]

    @pl.when(pl.program_id(1) == 0)
    def _():
        um = _rms(h1m_ref[0], fnw).astype(BF16)
        am = jnp.dot(um, wup_ref[:, 0:D_FF], preferred_element_type=F32)
        as_ref[0:SUBLANES, :] = am[N_META - SUBLANES:N_META, :]

    x = h1_ref[0]
    u = _rms(x, fnw).astype(BF16)
    up = jnp.dot(u, wup_ref[...], preferred_element_type=F32)
    a = up[:, 0:D_FF]
    as_ref[SUBLANES:SUBLANES + T, :] = a
    a1 = as_ref[SUBLANES - 1:SUBLANES - 1 + T, :]
    a2 = as_ref[SUBLANES - 2:SUBLANES - 2 + T, :]
    ac = fcw_ref[0:1, :] * a2 + fcw_ref[1:2, :] * a1 + fcw_ref[2:3, :] * a + fcb_ref[...]
    as_ref[0:SUBLANES, :] = as_ref[T:T + SUBLANES, :]
    gated = (ac * _sigmoid(ac) * up[:, D_FF:]).astype(BF16)
    y = x + jnp.dot(gated, wdn_ref[:, 0:D_MODEL], preferred_element_type=F32)
    out_ref[0] = _rms(y, finw_ref[...])


def _resident(shape):
    return pl.BlockSpec(shape, lambda b, t: (0,) * len(shape), pipeline_mode=pl.Buffered(1))


def _weight(w):
    return jnp.pad(w.astype(BF16), ((0, 0), (0, WEIGHT_LANE_PAD)))


def _rsel_matrix():
    r = np.arange(N_HEADS * HEAD_DIM)[:, None] // HEAD_DIM
    l = np.arange(CHUNK)[None, :] % SUBLANES
    return jnp.asarray(r == l, dtype=BF16)


def kernel(x, meta_tokens, lb_param, attn_norm_w, w_in, hgrn_norm_w, conv_w, w_out, ffn_norm_w, w_up,
           ffn_conv_w, ffn_conv_b, w_down, final_norm_w):
    B, L, D = x.shape
    assert D == D_MODEL and L % MIXER_TILE == 0 and L % FFN_TILE == 0
    assert w_in.shape == (1, D, N_IN_SECTIONS * D) and w_up.shape == (1, D, 2 * D_FF)
    assert SUBLANES * HEAD_DIM == D_MODEL and N_HEADS == SUBLANES

    tile = lambda T: pl.BlockSpec((1, T, D), lambda b, t: (b, t, 0))
    meta_tile = pl.BlockSpec((1, N_META, D), lambda b, t: (b, 0, 0))
    params = pltpu.CompilerParams(dimension_semantics=("arbitrary", "arbitrary"),
                                  vmem_limit_bytes=VMEM_LIMIT_BYTES)

    TM = MIXER_TILE
    n_steps = B * (L // TM)
    step = lambda b, t: b * (L // TM) + t
    assert D % n_steps == 0 and D_FF % W_DOWN_ROWS_PER_STEP == 0 and D_FF // W_DOWN_ROWS_PER_STEP <= n_steps
    wup_rows = D // n_steps
    wup_block = pl.BlockSpec((wup_rows, 2 * D_FF), lambda b, t: (step(b, t), 0))
    wdn_index = lambda b, t: (jnp.minimum(step(b, t), D_FF // W_DOWN_ROWS_PER_STEP - 1), 0)
    slot = [pltpu.VMEM((TM, GROUP_WIDTH), F32)] * 3 + [pltpu.VMEM((TM, GROUP_WIDTH), BF16)] * 4 \
        + [pltpu.VMEM((TM + SUBLANES, GROUP_WIDTH), F32)]
    h1, h1m, w_up_bf16, w_down_bf16 = pl.pallas_call(
        _mixer_kernel,
        grid=(B, L // TM),
        in_specs=[tile(TM), _resident((N_META, D)), _resident((2, D)), _resident((1, D)),
                  _resident((D, N_IN_SECTIONS * D + WEIGHT_LANE_PAD)), _resident((1, HEAD_DIM)),
                  _resident((3, D)), _resident((D, D + WEIGHT_LANE_PAD)),
                  _resident((N_HEADS * HEAD_DIM, CHUNK)), wup_block,
                  pl.BlockSpec((W_DOWN_ROWS_PER_STEP, D), wdn_index)],
        out_specs=[tile(TM), meta_tile, wup_block,
                   pl.BlockSpec((W_DOWN_ROWS_PER_STEP, D + WEIGHT_LANE_PAD), wdn_index)],
        out_shape=[jax.ShapeDtypeStruct((B, L, D), F32), jax.ShapeDtypeStruct((B, N_META, D), F32),
                   jax.ShapeDtypeStruct((D, 2 * D_FF), BF16),
                   jax.ShapeDtypeStruct((D_FF, D + WEIGHT_LANE_PAD), BF16)],
        scratch_shapes=[pltpu.VMEM((N_HEADS, HEAD_DIM, HEAD_DIM), F32),
                        pltpu.VMEM((N_HEADS, HEAD_DIM, HEAD_DIM), F32),
                        pltpu.VMEM((SUBLANES, D), F32),
                        pltpu.VMEM((SUBLANES, D), F32),
                        pltpu.VMEM((TM, D), BF16),
                        pltpu.SMEM((N_GROUPS,), jnp.int32),
                        pltpu.VMEM((N_HEADS, HEAD_DIM, HEAD_DIM), F32),
                        pltpu.VMEM((SUBLANES, D), F32),
                        pltpu.VMEM((N_META, D), F32),
                        *(slot * N_GROUPS)],
        compiler_params=params,
        name="mixer",
    )(x, meta_tokens, lb_param, attn_norm_w, _weight(w_in[0]), hgrn_norm_w, conv_w[0],
      _weight(w_out[0]), _rsel_matrix(), w_up[0], w_down[0])

    TF = FFN_TILE
    out = pl.pallas_call(
        _ffn_kernel,
        grid=(B, L // TF),
        in_specs=[tile(TF), meta_tile, _resident((1, D)), _resident((D, 2 * D_FF)), _resident((3, D_FF)),
                  _resident((1, D_FF)), _resident((D_FF, D + WEIGHT_LANE_PAD)), _resident((1, D))],
        out_specs=tile(TF),
        out_shape=jax.ShapeDtypeStruct((B, L, D), F32),
        scratch_shapes=[pltpu.VMEM((TF + SUBLANES, D_FF), F32)],
        compiler_params=params,
        name="ffn",
    )(h1, h1m, ffn_norm_w, w_up_bf16, ffn_conv_w[0], ffn_conv_b, w_down_bf16, final_norm_w.reshape(1, D))
    return out
```

```python
import numpy as np
import jax
import jax.numpy as jnp
from jax import lax
from jax.experimental import pallas as pl
from jax.experimental.pallas import tpu as pltpu

D_MODEL = 1024
N_META = 16
N_HEADS = 8
HEAD_DIM = 128
D_FF = 2816
N_IN_SECTIONS = 9
N_RECURRENCE_SECTIONS = 3
EPS = 1e-6

SUBLANES = 8
CHUNK = 128
GROUP_HEADS = 8
GROUP_WIDTH = GROUP_HEADS * HEAD_DIM
N_GROUPS = N_HEADS // GROUP_HEADS
MAX_SAFE_EXPONENT = 60.0
LOG2_E = 1.4426950408889634
MIXER_TILE = 512
FFN_TILE = 1024
WEIGHT_LANE_PAD = 128
W_DOWN_ROWS_PER_STEP = 128
V7X_VMEM_BYTES = 64 * 1024 * 1024
VMEM_LIMIT_BYTES = V7X_VMEM_BYTES - 2 * 1024 * 1024

F32 = jnp.float32
BF16 = jnp.bfloat16

_NT = (((1,), (1,)), ((), ()))
_TN = (((0,), (0,)), ((), ()))


def _rms(x, w):
    ms = jnp.mean(x * x, axis=-1, keepdims=True)
    return x * lax.rsqrt(ms + EPS) * w


def _sigmoid(x):
    return 1.0 / (1.0 + jnp.exp2(x * -LOG2_E))


def _bcast_rows(ref, lanes, row0, block, offset, nrows):
    pieces = [jnp.broadcast_to(_row(ref, lanes, row0, i * block + offset), (block, HEAD_DIM))
              for i in range(nrows // block)]
    return pieces[0] if len(pieces) == 1 else jnp.concatenate(pieces, axis=0)


def _row(ref, lanes, row0, r):
    group = ref[pl.ds(row0 + r // SUBLANES * SUBLANES, SUBLANES), lanes]
    return group[r % SUBLANES:r % SUBLANES + 1, :]


def _hgrn_chunk_exact(row0, lanes, valid, st, q_ref, g_ref, k_ref, v_ref, rsel):
    rows = pl.ds(row0, CHUNK)
    q = q_ref[rows, lanes]
    G = g_ref[rows, lanes]
    k = k_ref[rows, lanes]
    v = v_ref[rows, lanes]
    row = lax.broadcasted_iota(jnp.int32, (CHUNK, HEAD_DIM), 0)

    sub = row & (SUBLANES - 1)
    ps = []
    for j in range(SUBLANES):
        gj = _bcast_rows(g_ref, lanes, row0, SUBLANES, j, CHUNK)
        kj = _bcast_rows(k_ref, lanes, row0, SUBLANES, j, CHUNK)
        p = q * kj * jnp.exp2(G - gj)
        ps.append(jnp.where(sub >= j, p, 0.0).astype(BF16))
    pcat = jnp.concatenate(ps, axis=1)
    a = jnp.dot(pcat, rsel, preferred_element_type=F32)

    tl_xor = (lax.broadcasted_iota(jnp.int32, (CHUNK, CHUNK), 0)
              ^ lax.broadcasted_iota(jnp.int32, (CHUNK, CHUNK), 1))
    a = jnp.where(tl_xor < SUBLANES, a, 0.0)

    b = 2 * SUBLANES
    while b <= CHUNK:
        gm = _bcast_rows(g_ref, lanes, row0, b, b // 2 - 1, CHUNK)
        e = jnp.exp2(-jnp.abs(G - gm))
        upper = (row & (b // 2)) != 0
        qt = jnp.where(upper, q * e, 0.0).astype(BF16)
        kt = jnp.where(upper, 0.0, k * e).astype(BF16)
        ab = lax.dot_general(qt, kt, _NT, preferred_element_type=F32)
        a = a + (ab if b == CHUNK else jnp.where(tl_xor < b, ab, 0.0))
        b *= 2

    qi = (q * jnp.exp2(G)).astype(BF16)
    o = lax.dot_general(qi, st.astype(BF16), _NT, preferred_element_type=F32)
    o = o + jnp.dot(a.astype(BF16), v, preferred_element_type=F32)

    glast = _row(g_ref, lanes, row0, valid - 1)
    kd = k * jnp.exp2(glast - G)
    if valid < CHUNK:
        kd = jnp.where(row < valid, kd, 0.0)
    st = st * jnp.exp2(glast) + lax.dot_general(v, kd.astype(BF16), _TN, preferred_element_type=F32)
    return o, st


def _hgrn_chunk_fast(row0, lanes, st, q_ref, g_ref, k_ref, v_ref):
    rows = pl.ds(row0, CHUNK)
    q = q_ref[rows, lanes]
    G = g_ref[rows, lanes]
    k = k_ref[rows, lanes]
    v = v_ref[rows, lanes]
    gmid = g_ref[pl.ds(row0 + CHUNK // 2 - 1, 1), lanes]
    glast = g_ref[pl.ds(row0 + CHUNK - 1, 1), lanes]
    d = G - gmid
    qh = q * jnp.exp2(d)
    kh = k * jnp.exp2(-d)
    a = lax.dot_general(qh.astype(BF16), kh.astype(BF16), _NT, preferred_element_type=F32)
    causal = (lax.broadcasted_iota(jnp.int32, (CHUNK, CHUNK), 1)
              <= lax.broadcasted_iota(jnp.int32, (CHUNK, CHUNK), 0))
    a = jnp.where(causal, a, 0.0)
    qi = (qh * jnp.exp2(gmid)).astype(BF16)
    o = lax.dot_general(qi, st.astype(BF16), _NT, preferred_element_type=F32)
    o = o + jnp.dot(a.astype(BF16), v, preferred_element_type=F32)
    kd = (kh * jnp.exp2(glast - gmid)).astype(BF16)
    st = st * jnp.exp2(glast) + lax.dot_general(v, kd, _TN, preferred_element_type=F32)
    return o, st


def _project_sections(u, win_ref, c0, first, last):
    if GROUP_WIDTH == D_MODEL:
        p = jnp.dot(u, win_ref[:, first * D_MODEL:last * D_MODEL], preferred_element_type=F32)
        return [p[:, i * D_MODEL:(i + 1) * D_MODEL] for i in range(last - first)]
    return [jnp.dot(u, win_ref[:, i * D_MODEL + c0:i * D_MODEL + c0 + GROUP_WIDTH], preferred_element_type=F32)
            for i in range(first, last)]


def _project_group(u, g, R, valid, prm, slot, recurrence_inputs):
    lb_ref, win_ref, cw_ref, zc_ref, safe_ref = prm
    q_ref, g_ref, k_ref, v_ref, gs_ref, sa_ref, mb_ref, zs_ref = slot
    c0 = g * GROUP_WIDTH
    cols = slice(c0, c0 + GROUP_WIDTH)

    sec = lambda first, last: _project_sections(u, win_ref, c0, first, last)

    if not recurrence_inputs:
        _project_gates(sec(N_RECURRENCE_SECTIONS, N_IN_SECTIONS), cols, R, valid, cw_ref, zc_ref, slot)
        return

    qv, fv, iv = sec(0, N_RECURRENCE_SECTIONS)
    q_ref[0:R, :] = qv * _sigmoid(qv)

    f = lb_ref[0:1, cols] + lb_ref[1:2, cols] * _sigmoid(fv)
    k_ref[0:R, :] = 1.0 - f
    lf = jnp.log2(f)
    tri = (lax.broadcasted_iota(jnp.int32, (CHUNK, CHUNK), 1)
           <= lax.broadcasted_iota(jnp.int32, (CHUNK, CHUNK), 0)).astype(BF16)
    tri2 = jnp.concatenate([tri, tri], axis=1)
    worst = jnp.zeros((1, GROUP_WIDTH), F32)
    for c in range(R // CHUNK):
        lfc = lf[c * CHUNK:(c + 1) * CHUNK, :]
        hi = lfc.astype(BF16)
        lo = (lfc - hi.astype(F32)).astype(BF16)
        gc = jnp.dot(tri2, jnp.concatenate([hi, lo], axis=0),
                     preferred_element_type=F32)
        g_ref[c * CHUNK:(c + 1) * CHUNK, :] = gc
        gmid = gc[CHUNK // 2 - 1:CHUNK // 2, :]
        glast = gc[CHUNK - 1:CHUNK, :]
        worst = jnp.maximum(worst, jnp.maximum(-gmid, gmid - glast))
    safe_ref[g] = (jnp.max(worst) <= MAX_SAFE_EXPONENT * LOG2_E).astype(jnp.int32)

    v_ref[0:R, :] = iv.astype(BF16)


def _project_gates(sections, cols, R, valid, cw_ref, zc_ref, slot):
    gs_ref, sa_ref, mb_ref, zs_ref = slot[4:8]
    gv, bg, cg, hc, ga, gb = sections
    gs_ref[0:R, :] = (gv * _sigmoid(gv)).astype(BF16)

    z = cg * hc
    zs_ref[0:SUBLANES, :] = zc_ref[:, cols]
    zs_ref[SUBLANES:SUBLANES + R, :] = z
    z1 = zs_ref[SUBLANES - 1:SUBLANES - 1 + R, :]
    z2 = zs_ref[SUBLANES - 2:SUBLANES - 2 + R, :]
    yb = bg * (cw_ref[0:1, cols] * z2 + cw_ref[1:2, cols] * z1 + cw_ref[2:3, cols] * z)
    zc_ref[:, cols] = zs_ref[valid:valid + SUBLANES, :]

    sa_ref[0:R, :] = _sigmoid(ga).astype(BF16)
    mb_ref[0:R, :] = (_sigmoid(gb) * yb).astype(BF16)


def _finish_chunk(row0, h, lanes, o, hnw, mg_ref, slot):
    gs_ref, sa_ref, mb_ref = slot[4:7]
    rows = pl.ds(row0, CHUNK)
    m = (_rms(o, hnw) * gs_ref[rows, lanes].astype(F32) * sa_ref[rows, lanes].astype(F32)
         + mb_ref[rows, lanes].astype(F32))
    mg_ref[rows, h * HEAD_DIM:(h + 1) * HEAD_DIM] = m.astype(BF16)


def _recur_group_fast(g, R, hnw, st_ref, st_old_ref, mg_ref, slot):
    q_ref, g_ref, k_ref, v_ref = slot[0:4]
    for hh in range(GROUP_HEADS):
        h = GROUP_HEADS * g + hh
        lanes = slice(hh * HEAD_DIM, (hh + 1) * HEAD_DIM)
        st = st_ref[h]
        st_old_ref[h] = st
        for c in range(R // CHUNK):
            o, st = _hgrn_chunk_fast(c * CHUNK, lanes, st, q_ref, g_ref, k_ref, v_ref)
            mg_ref[c * CHUNK:(c + 1) * CHUNK, h * HEAD_DIM:(h + 1) * HEAD_DIM] = _rms(o, hnw).astype(BF16)
        st_ref[h] = st


def _finish_group_fast(g, R, mg_ref, slot):
    gs_ref, sa_ref, mb_ref = slot[4:7]
    cols = slice(g * GROUP_WIDTH, (g + 1) * GROUP_WIDTH)
    mg_ref[0:R, cols] = mg_ref[0:R, cols] * gs_ref[0:R, :] * sa_ref[0:R, :] + mb_ref[0:R, :]


def _recur_group_exact(g, R, valid, hnw, rsel, st_in_ref, st_ref, mg_ref, slot):
    q_ref, g_ref, k_ref, v_ref = slot[0:4]
    for hh in range(GROUP_HEADS):
        h = GROUP_HEADS * g + hh
        lanes = slice(hh * HEAD_DIM, (hh + 1) * HEAD_DIM)
        st = st_in_ref[h]
        if valid < R:
            o, st = _hgrn_chunk_exact(0, lanes, valid, st, q_ref, g_ref, k_ref, v_ref, rsel)
            _finish_chunk(0, h, lanes, o, hnw, mg_ref, slot)
        else:
            def chunk_body(c, st):
                row0 = pl.multiple_of(c * CHUNK, CHUNK)
                o, st = _hgrn_chunk_exact(row0, lanes, CHUNK, st, q_ref, g_ref, k_ref, v_ref, rsel)
                _finish_chunk(row0, h, lanes, o, hnw, mg_ref, slot)
                return st

            st = lax.fori_loop(0, R // CHUNK, chunk_body, st)
        st_ref[h] = st


def _out_partial(g, R, mg_ref, wout_ref):
    rows = slice(g * GROUP_WIDTH, (g + 1) * GROUP_WIDTH)
    return jnp.dot(mg_ref[0:R, rows], wout_ref[rows, 0:D_MODEL], preferred_element_type=F32)


def _mixer_tile(load_x, store_out, R, valid, is_meta, refs):
    (lbp_ref, anw_ref, win_ref, hnw_ref, cw_ref, wout_ref, rsel_ref,
     st_ref, st_old_ref, zc_ref, lb_ref, mg_ref, safe_ref, slots) = refs
    prm = (lb_ref, win_ref, cw_ref, zc_ref, safe_ref)
    u = _rms(load_x(), anw_ref[...]).astype(BF16)
    rsel = rsel_ref[...]
    hnw = hnw_ref[...]
    project = lambda g, first: _project_group(u, g, R, valid, prm, slots[g], first)

    lbp = lbp_ref[...]
    mx = jnp.max(lbp, axis=0, keepdims=True)
    ex = jnp.exp(lbp - mx)
    lb = ex[0:1, :] / jnp.sum(ex, axis=0, keepdims=True)
    lb_ref[0:1, :] = lb
    lb_ref[1:2, :] = 1.0 - lb

    if is_meta:
        for g in range(N_GROUPS):
            project(g, True)
            project(g, False)
            _recur_group_exact(g, R, valid, hnw, rsel, st_ref, st_ref, mg_ref, slots[g])
        acc = load_x()
        for g in range(N_GROUPS):
            acc = acc + _out_partial(g, R, mg_ref, wout_ref)
        store_out(acc, False)
        return

    def store_early():
        acc = load_x()
        for g in range(N_GROUPS - 1):
            acc = acc + _out_partial(g, R, mg_ref, wout_ref)
        store_out(acc, False)

    for g in range(N_GROUPS):
        project(g, True)
        if g > 0:
            _finish_group_fast(g - 1, R, mg_ref, slots[g - 1])
        if g == N_GROUPS - 1:
            store_early()
        _recur_group_fast(g, R, hnw, st_ref, st_old_ref, mg_ref, slots[g])
        project(g, False)
    _finish_group_fast(N_GROUPS - 1, R, mg_ref, slots[N_GROUPS - 1])
    for g in range(N_GROUPS):
        def redo(g=g):
            _recur_group_exact(g, R, valid, hnw, rsel, st_old_ref, st_ref, mg_ref, slots[g])
            if g < N_GROUPS - 1:
                store_early()

        pl.when(safe_ref[g] == 0)(redo)
    store_out(_out_partial(N_GROUPS - 1, R, mg_ref, wout_ref), True)


def _mixer_kernel(x_ref, meta_ref, lbp_ref, anw_ref, win_ref, hnw_ref, cw_ref, wout_ref, rsel_ref,
                  wup_f32_ref, wdn_f32_ref, h1_ref, h1m_ref, wup_ref, wdn_ref,
                  st_ref, st_old_ref, zc_ref, lb_ref, mg_ref, safe_ref,
                  st_meta_ref, zc_meta_ref, h1m_meta_ref, *slots):
    wup_ref[...] = wup_f32_ref[...].astype(BF16)
    wdn_ref[:, 0:D_MODEL] = wdn_f32_ref[...].astype(BF16)
    wdn_ref[:, D_MODEL:] = jnp.zeros((W_DOWN_ROWS_PER_STEP, WEIGHT_LANE_PAD), BF16)

    n = len(slots) // N_GROUPS
    refs = (lbp_ref, anw_ref, win_ref, hnw_ref, cw_ref, wout_ref, rsel_ref,
            st_ref, st_old_ref, zc_ref, lb_ref, mg_ref, safe_ref,
            [slots[i * n:(i + 1) * n] for i in range(N_GROUPS)])

    first_tile = pl.program_id(1) == 0

    @pl.when(jnp.logical_and(first_tile, pl.program_id(0) == 0))
    def _():
        st_ref[...] = jnp.zeros_like(st_ref)
        zc_ref[...] = jnp.zeros_like(zc_ref)
        load_meta = lambda: jnp.concatenate(
            [meta_ref[...], jnp.zeros((CHUNK - N_META, D_MODEL), F32)], axis=0)

        def store_meta(val, accumulate):
            h1m_meta_ref[...] = val[0:N_META, :]

        _mixer_tile(load_meta, store_meta, CHUNK, N_META, True, refs)
        st_meta_ref[...] = st_ref[...]
        zc_meta_ref[...] = zc_ref[...]

    @pl.when(first_tile)
    def _():
        st_ref[...] = st_meta_ref[...]
        zc_ref[...] = zc_meta_ref[...]
        h1m_ref[0] = h1m_meta_ref[...]

    def store_tile(val, accumulate):
        h1_ref[0] = h1_ref[0] + val if accumulate else val

    _mixer_tile(lambda: x_ref[0], store_tile, MIXER_TILE, MIXER_TILE, False, refs)


def _ffn_kernel(h1_ref, h1m_ref, fnw_ref, wup_ref, fcw_ref, fcb_ref, wdn_ref, finw_ref,
                out_ref, as_ref):
    T = FFN_TILE
    fnw = fnw_ref[...]

    @pl.when(pl.program_id(1) == 0)
    def _():
        um = _rms(h1m_ref[0], fnw).astype(BF16)
        am = jnp.dot(um, wup_ref[:, 0:D_FF], preferred_element_type=F32)
        as_ref[0:SUBLANES, :] = am[N_META - SUBLANES:N_META, :]

    x = h1_ref[0]
    u = _rms(x, fnw).astype(BF16)
    up = jnp.dot(u, wup_ref[...], preferred_element_type=F32)
    a = up[:, 0:D_FF]
    as_ref[SUBLANES:SUBLANES + T, :] = a
    a1 = as_ref[SUBLANES - 1:SUBLANES - 1 + T, :]
    a2 = as_ref[SUBLANES - 2:SUBLANES - 2 + T, :]
    ac = fcw_ref[0:1, :] * a2 + fcw_ref[1:2, :] * a1 + fcw_ref[2:3, :] * a + fcb_ref[...]
    as_ref[0:SUBLANES, :] = as_ref[T:T + SUBLANES, :]
    gated = (ac * _sigmoid(ac) * up[:, D_FF:]).astype(BF16)
    y = x + jnp.dot(gated, wdn_ref[:, 0:D_MODEL], preferred_element_type=F32)
    out_ref[0] = _rms(y, finw_ref[...])


def _resident(shape):
    return pl.BlockSpec(shape, lambda b, t: (0,) * len(shape), pipeline_mode=pl.Buffered(1))


def _weight(w):
    return jnp.pad(w.astype(BF16), ((0, 0), (0, WEIGHT_LANE_PAD)))


def _rsel_matrix():
    r = np.arange(N_HEADS * HEAD_DIM)[:, None] // HEAD_DIM
    l = np.arange(CHUNK)[None, :] % SUBLANES
    return jnp.asarray(r == l, dtype=BF16)


def kernel(x, meta_tokens, lb_param, attn_norm_w, w_in, hgrn_norm_w, conv_w, w_out, ffn_norm_w, w_up,
           ffn_conv_w, ffn_conv_b, w_down, final_norm_w):
    B, L, D = x.shape
    assert D == D_MODEL and L % MIXER_TILE == 0 and L % FFN_TILE == 0
    assert w_in.shape == (1, D, N_IN_SECTIONS * D) and w_up.shape == (1, D, 2 * D_FF)
    assert SUBLANES * HEAD_DIM == D_MODEL and N_HEADS == SUBLANES

    tile = lambda T: pl.BlockSpec((1, T, D), lambda b, t: (b, t, 0))
    meta_tile = pl.BlockSpec((1, N_META, D), lambda b, t: (b, 0, 0))
    params = pltpu.CompilerParams(dimension_semantics=("arbitrary", "arbitrary"),
                                  vmem_limit_bytes=VMEM_LIMIT_BYTES)

    TM = MIXER_TILE
    n_steps = B * (L // TM)
    step = lambda b, t: b * (L // TM) + t
    assert D % n_steps == 0 and D_FF % W_DOWN_ROWS_PER_STEP == 0 and D_FF // W_DOWN_ROWS_PER_STEP <= n_steps
    wup_rows = D // n_steps
    wup_block = pl.BlockSpec((wup_rows, 2 * D_FF), lambda b, t: (step(b, t), 0))
    wdn_index = lambda b, t: (jnp.minimum(step(b, t), D_FF // W_DOWN_ROWS_PER_STEP - 1), 0)
    slot = [pltpu.VMEM((TM, GROUP_WIDTH), F32)] * 3 + [pltpu.VMEM((TM, GROUP_WIDTH), BF16)] * 4 \
        + [pltpu.VMEM((TM + SUBLANES, GROUP_WIDTH), F32)]
    h1, h1m, w_up_bf16, w_down_bf16 = pl.pallas_call(
        _mixer_kernel,
        grid=(B, L // TM),
        in_specs=[tile(TM), _resident((N_META, D)), _resident((2, D)), _resident((1, D)),
                  _resident((D, N_IN_SECTIONS * D + WEIGHT_LANE_PAD)), _resident((1, HEAD_DIM)),
                  _resident((3, D)), _resident((D, D + WEIGHT_LANE_PAD)),
                  _resident((N_HEADS * HEAD_DIM, CHUNK)), wup_block,
                  pl.BlockSpec((W_DOWN_ROWS_PER_STEP, D), wdn_index)],
        out_specs=[tile(TM), meta_tile, wup_block,
                   pl.BlockSpec((W_DOWN_ROWS_PER_STEP, D + WEIGHT_LANE_PAD), wdn_index)],
        out_shape=[jax.ShapeDtypeStruct((B, L, D), F32), jax.ShapeDtypeStruct((B, N_META, D), F32),
                   jax.ShapeDtypeStruct((D, 2 * D_FF), BF16),
                   jax.ShapeDtypeStruct((D_FF, D + WEIGHT_LANE_PAD), BF16)],
        scratch_shapes=[pltpu.VMEM((N_HEADS, HEAD_DIM, HEAD_DIM), F32),
                        pltpu.VMEM((N_HEADS, HEAD_DIM, HEAD_DIM), F32),
                        pltpu.VMEM((SUBLANES, D), F32),
                        pltpu.VMEM((SUBLANES, D), F32),
                        pltpu.VMEM((TM, D), BF16),
                        pltpu.SMEM((N_GROUPS,), jnp.int32),
                        pltpu.VMEM((N_HEADS, HEAD_DIM, HEAD_DIM), F32),
                        pltpu.VMEM((SUBLANES, D), F32),
                        pltpu.VMEM((N_META, D), F32),
                        *(slot * N_GROUPS)],
        compiler_params=params,
        name="mixer",
    )(x, meta_tokens, lb_param, attn_norm_w, _weight(w_in[0]), hgrn_norm_w, conv_w[0],
      _weight(w_out[0]), _rsel_matrix(), w_up[0], w_down[0])

    TF = FFN_TILE
    out = pl.pallas_call(
        _ffn_kernel,
        grid=(B, L // TF),
        in_specs=[tile(TF), meta_tile, _resident((1, D)), _resident((D, 2 * D_FF)), _resident((3, D_FF)),
                  _resident((1, D_FF)), _resident((D_FF, D + WEIGHT_LANE_PAD)), _resident((1, D))],
        out_specs=tile(TF),
        out_shape=jax.ShapeDtypeStruct((B, L, D), F32),
        scratch_shapes=[pltpu.VMEM((TF + SUBLANES, D_FF), F32)],
        compiler_params=params,
        name="ffn",
    )(h1, h1m, ffn_norm_w, w_up_bf16, ffn_conv_w[0], ffn_conv_b, w_down_bf16, final_norm_w.reshape(1, D))
    return out
```

```python
import numpy as np
import jax
import jax.numpy as jnp
from jax import lax
from jax.experimental import pallas as pl
from jax.experimental.pallas import tpu as pltpu

D_MODEL = 1024
N_META = 16
N_HEADS = 8
HEAD_DIM = 128
D_FF = 2816
N_IN_SECTIONS = 9
N_RECURRENCE_SECTIONS = 3
EPS = 1e-6

SUBLANES = 8
CHUNK = 128
GROUP_HEADS = 8
GROUP_WIDTH = GROUP_HEADS * HEAD_DIM
N_GROUPS = N_HEADS // GROUP_HEADS
MAX_SAFE_EXPONENT = 60.0
LOG2_E = 1.4426950408889634
MIXER_TILE = 512
FFN_TILE = 256
WEIGHT_LANE_PAD = 128
W_DOWN_ROWS_PER_STEP = 128
V7X_VMEM_BYTES = 64 * 1024 * 1024
VMEM_LIMIT_BYTES = V7X_VMEM_BYTES - 4 * 1024 * 1024

F32 = jnp.float32
BF16 = jnp.bfloat16

_NT = (((1,), (1,)), ((), ()))
_TN = (((0,), (0,)), ((), ()))


def _rms(x, w):
    ms = jnp.mean(x * x, axis=-1, keepdims=True)
    return x * lax.rsqrt(ms + EPS) * w


def _sigmoid(x):
    return 1.0 / (1.0 + jnp.exp2(x * -LOG2_E))


def _bcast_rows(ref, lanes, row0, block, offset, nrows):
    pieces = [jnp.broadcast_to(_row(ref, lanes, row0, i * block + offset), (block, HEAD_DIM))
              for i in range(nrows // block)]
    return pieces[0] if len(pieces) == 1 else jnp.concatenate(pieces, axis=0)


def _row(ref, lanes, row0, r):
    group = ref[pl.ds(row0 + r // SUBLANES * SUBLANES, SUBLANES), lanes]
    return group[r % SUBLANES:r % SUBLANES + 1, :]


def _hgrn_chunk_exact(row0, lanes, valid, st, q_ref, g_ref, k_ref, v_ref, rsel):
    rows = pl.ds(row0, CHUNK)
    q = q_ref[rows, lanes]
    G = g_ref[rows, lanes]
    k = k_ref[rows, lanes]
    v = v_ref[rows, lanes]
    row = lax.broadcasted_iota(jnp.int32, (CHUNK, HEAD_DIM), 0)

    sub = row & (SUBLANES - 1)
    ps = []
    for j in range(SUBLANES):
        gj = _bcast_rows(g_ref, lanes, row0, SUBLANES, j, CHUNK)
        kj = _bcast_rows(k_ref, lanes, row0, SUBLANES, j, CHUNK)
        p = q * kj * jnp.exp2(G - gj)
        ps.append(jnp.where(sub >= j, p, 0.0).astype(BF16))
    pcat = jnp.concatenate(ps, axis=1)
    a = jnp.dot(pcat, rsel, preferred_element_type=F32)

    tl_xor = (lax.broadcasted_iota(jnp.int32, (CHUNK, CHUNK), 0)
              ^ lax.broadcasted_iota(jnp.int32, (CHUNK, CHUNK), 1))
    a = jnp.where(tl_xor < SUBLANES, a, 0.0)

    b = 2 * SUBLANES
    while b <= CHUNK:
        gm = _bcast_rows(g_ref, lanes, row0, b, b // 2 - 1, CHUNK)
        e = jnp.exp2(-jnp.abs(G - gm))
        upper = (row & (b // 2)) != 0
        qt = jnp.where(upper, q * e, 0.0).astype(BF16)
        kt = jnp.where(upper, 0.0, k * e).astype(BF16)
        ab = lax.dot_general(qt, kt, _NT, preferred_element_type=F32)
        a = a + (ab if b == CHUNK else jnp.where(tl_xor < b, ab, 0.0))
        b *= 2

    qi = (q * jnp.exp2(G)).astype(BF16)
    o = lax.dot_general(qi, st.astype(BF16), _NT, preferred_element_type=F32)
    o = o + jnp.dot(a.astype(BF16), v, preferred_element_type=F32)

    glast = _row(g_ref, lanes, row0, valid - 1)
    kd = k * jnp.exp2(glast - G)
    if valid < CHUNK:
        kd = jnp.where(row < valid, kd, 0.0)
    st = st * jnp.exp2(glast) + lax.dot_general(v, kd.astype(BF16), _TN, preferred_element_type=F32)
    return o, st


def _hgrn_chunk_fast(row0, lanes, st, q_ref, g_ref, k_ref, v_ref):
    rows = pl.ds(row0, CHUNK)
    q = q_ref[rows, lanes]
    G = g_ref[rows, lanes]
    k = k_ref[rows, lanes]
    v = v_ref[rows, lanes]
    gmid = g_ref[pl.ds(row0 + CHUNK // 2 - 1, 1), lanes]
    glast = g_ref[pl.ds(row0 + CHUNK - 1, 1), lanes]
    d = G - gmid
    qh = q * jnp.exp2(d)
    kh = k * jnp.exp2(-d)
    a = lax.dot_general(qh.astype(BF16), kh.astype(BF16), _NT, preferred_element_type=F32)
    causal = (lax.broadcasted_iota(jnp.int32, (CHUNK, CHUNK), 1)
              <= lax.broadcasted_iota(jnp.int32, (CHUNK, CHUNK), 0))
    a = jnp.where(causal, a, 0.0)
    qi = (qh * jnp.exp2(gmid)).astype(BF16)
    o = lax.dot_general(qi, st.astype(BF16), _NT, preferred_element_type=F32)
    o = o + jnp.dot(a.astype(BF16), v, preferred_element_type=F32)
    kd = (kh * jnp.exp2(glast - gmid)).astype(BF16)
    st = st * jnp.exp2(glast) + lax.dot_general(v, kd, _TN, preferred_element_type=F32)
    return o, st


def _project_sections(u, win_ref, c0, first, last):
    if GROUP_WIDTH == D_MODEL:
        p = jnp.dot(u, win_ref[:, first * D_MODEL:last * D_MODEL], preferred_element_type=F32)
        return [p[:, i * D_MODEL:(i + 1) * D_MODEL] for i in range(last - first)]
    return [jnp.dot(u, win_ref[:, i * D_MODEL + c0:i * D_MODEL + c0 + GROUP_WIDTH], preferred_element_type=F32)
            for i in range(first, last)]


def _project_group(u, g, R, valid, prm, slot, recurrence_inputs):
    lb_ref, win_ref, cw_ref, zc_ref, safe_ref = prm
    q_ref, g_ref, k_ref, v_ref, gs_ref, sa_ref, mb_ref, zs_ref = slot
    c0 = g * GROUP_WIDTH
    cols = slice(c0, c0 + GROUP_WIDTH)

    sec = lambda first, last: _project_sections(u, win_ref, c0, first, last)

    if not recurrence_inputs:
        _project_gates(sec(N_RECURRENCE_SECTIONS, N_IN_SECTIONS), cols, R, valid, cw_ref, zc_ref, slot)
        return

    qv, fv, iv = sec(0, N_RECURRENCE_SECTIONS)
    q_ref[0:R, :] = qv * _sigmoid(qv)

    f = lb_ref[0:1, cols] + lb_ref[1:2, cols] * _sigmoid(fv)
    k_ref[0:R, :] = 1.0 - f
    lf = jnp.log2(f)
    tri = (lax.broadcasted_iota(jnp.int32, (CHUNK, CHUNK), 1)
           <= lax.broadcasted_iota(jnp.int32, (CHUNK, CHUNK), 0)).astype(BF16)
    tri2 = jnp.concatenate([tri, tri], axis=1)
    worst = jnp.zeros((1, GROUP_WIDTH), F32)
    for c in range(R // CHUNK):
        lfc = lf[c * CHUNK:(c + 1) * CHUNK, :]
        hi = lfc.astype(BF16)
        lo = (lfc - hi.astype(F32)).astype(BF16)
        gc = jnp.dot(tri2, jnp.concatenate([hi, lo], axis=0),
                     preferred_element_type=F32)
        g_ref[c * CHUNK:(c + 1) * CHUNK, :] = gc
        gmid = gc[CHUNK // 2 - 1:CHUNK // 2, :]
        glast = gc[CHUNK - 1:CHUNK, :]
        worst = jnp.maximum(worst, jnp.maximum(-gmid, gmid - glast))
    safe_ref[g] = (jnp.max(worst) <= MAX_SAFE_EXPONENT * LOG2_E).astype(jnp.int32)

    v_ref[0:R, :] = iv.astype(BF16)


def _project_gates(sections, cols, R, valid, cw_ref, zc_ref, slot):
    gs_ref, sa_ref, mb_ref, zs_ref = slot[4:8]
    gv, bg, cg, hc, ga, gb = sections
    gs_ref[0:R, :] = (gv * _sigmoid(gv)).astype(BF16)

    z = cg * hc
    zs_ref[0:SUBLANES, :] = zc_ref[:, cols]
    zs_ref[SUBLANES:SUBLANES + R, :] = z
    z1 = zs_ref[SUBLANES - 1:SUBLANES - 1 + R, :]
    z2 = zs_ref[SUBLANES - 2:SUBLANES - 2 + R, :]
    yb = bg * (cw_ref[0:1, cols] * z2 + cw_ref[1:2, cols] * z1 + cw_ref[2:3, cols] * z)
    zc_ref[:, cols] = zs_ref[valid:valid + SUBLANES, :]

    sa_ref[0:R, :] = _sigmoid(ga).astype(BF16)
    mb_ref[0:R, :] = (_sigmoid(gb) * yb).astype(BF16)


def _finish_chunk(row0, h, lanes, o, hnw, mg_ref, slot):
    gs_ref, sa_ref, mb_ref = slot[4:7]
    rows = pl.ds(row0, CHUNK)
    m = (_rms(o, hnw) * gs_ref[rows, lanes].astype(F32) * sa_ref[rows, lanes].astype(F32)
         + mb_ref[rows, lanes].astype(F32))
    mg_ref[rows, h * HEAD_DIM:(h + 1) * HEAD_DIM] = m.astype(BF16)


def _recur_group_fast(g, R, hnw, st_ref, st_old_ref, mg_ref, slot):
    q_ref, g_ref, k_ref, v_ref = slot[0:4]
    for hh in range(GROUP_HEADS):
        h = GROUP_HEADS * g + hh
        lanes = slice(hh * HEAD_DIM, (hh + 1) * HEAD_DIM)
        st = st_ref[h]
        st_old_ref[h] = st
        for c in range(R // CHUNK):
            o, st = _hgrn_chunk_fast(c * CHUNK, lanes, st, q_ref, g_ref, k_ref, v_ref)
            mg_ref[c * CHUNK:(c + 1) * CHUNK, h * HEAD_DIM:(h + 1) * HEAD_DIM] = _rms(o, hnw).astype(BF16)
        st_ref[h] = st


def _finish_group_fast(g, R, mg_ref, slot):
    gs_ref, sa_ref, mb_ref = slot[4:7]
    cols = slice(g * GROUP_WIDTH, (g + 1) * GROUP_WIDTH)
    mg_ref[0:R, cols] = mg_ref[0:R, cols] * gs_ref[0:R, :] * sa_ref[0:R, :] + mb_ref[0:R, :]


def _recur_group_exact(g, R, valid, hnw, rsel, st_in_ref, st_ref, mg_ref, slot):
    q_ref, g_ref, k_ref, v_ref = slot[0:4]
    for hh in range(GROUP_HEADS):
        h = GROUP_HEADS * g + hh
        lanes = slice(hh * HEAD_DIM, (hh + 1) * HEAD_DIM)
        st = st_in_ref[h]
        if valid < R:
            o, st = _hgrn_chunk_exact(0, lanes, valid, st, q_ref, g_ref, k_ref, v_ref, rsel)
            _finish_chunk(0, h, lanes, o, hnw, mg_ref, slot)
        else:
            def chunk_body(c, st):
                row0 = pl.multiple_of(c * CHUNK, CHUNK)
                o, st = _hgrn_chunk_exact(row0, lanes, CHUNK, st, q_ref, g_ref, k_ref, v_ref, rsel)
                _finish_chunk(row0, h, lanes, o, hnw, mg_ref, slot)
                return st

            st = lax.fori_loop(0, R // CHUNK, chunk_body, st)
        st_ref[h] = st


def _out_partial(g, R, mg_ref, wout_ref):
    rows = slice(g * GROUP_WIDTH, (g + 1) * GROUP_WIDTH)
    return jnp.dot(mg_ref[0:R, rows], wout_ref[rows, 0:D_MODEL], preferred_element_type=F32)


def _mixer_tile(load_x, store_out, R, valid, is_meta, refs):
    (lbp_ref, anw_ref, win_ref, hnw_ref, cw_ref, wout_ref, rsel_ref,
     st_ref, st_old_ref, zc_ref, lb_ref, mg_ref, safe_ref, slots) = refs
    prm = (lb_ref, win_ref, cw_ref, zc_ref, safe_ref)
    u = _rms(load_x(), anw_ref[...]).astype(BF16)
    rsel = rsel_ref[...]
    hnw = hnw_ref[...]
    project = lambda g, first: _project_group(u, g, R, valid, prm, slots[g], first)

    lbp = lbp_ref[...]
    mx = jnp.max(lbp, axis=0, keepdims=True)
    ex = jnp.exp(lbp - mx)
    lb = ex[0:1, :] / jnp.sum(ex, axis=0, keepdims=True)
    lb_ref[0:1, :] = lb
    lb_ref[1:2, :] = 1.0 - lb

    if is_meta:
        for g in range(N_GROUPS):
            project(g, True)
            project(g, False)
            _recur_group_exact(g, R, valid, hnw, rsel, st_ref, st_ref, mg_ref, slots[g])
        acc = load_x()
        for g in range(N_GROUPS):
            acc = acc + _out_partial(g, R, mg_ref, wout_ref)
        store_out(acc, False)
        return

    def store_early():
        acc = load_x()
        for g in range(N_GROUPS - 1):
            acc = acc + _out_partial(g, R, mg_ref, wout_ref)
        store_out(acc, False)

    for g in range(N_GROUPS):
        project(g, True)
        if g > 0:
            _finish_group_fast(g - 1, R, mg_ref, slots[g - 1])
        if g == N_GROUPS - 1:
            store_early()
        _recur_group_fast(g, R, hnw, st_ref, st_old_ref, mg_ref, slots[g])
        project(g, False)
    _finish_group_fast(N_GROUPS - 1, R, mg_ref, slots[N_GROUPS - 1])
    for g in range(N_GROUPS):
        def redo(g=g):
            _recur_group_exact(g, R, valid, hnw, rsel, st_old_ref, st_ref, mg_ref, slots[g])
            if g < N_GROUPS - 1:
                store_early()

        pl.when(safe_ref[g] == 0)(redo)
    store_out(_out_partial(N_GROUPS - 1, R, mg_ref, wout_ref), True)


def _mixer_kernel(x_ref, meta_ref, lbp_ref, anw_ref, win_ref, hnw_ref, cw_ref, wout_ref, rsel_ref,
                  wup_f32_ref, wdn_f32_ref, h1_ref, h1m_ref, wup_ref, wdn_ref,
                  st_ref, st_old_ref, zc_ref, lb_ref, mg_ref, safe_ref,
                  st_meta_ref, zc_meta_ref, h1m_meta_ref, *slots):
    wup_ref[...] = wup_f32_ref[...].astype(BF16)
    wdn_ref[:, 0:D_MODEL] = wdn_f32_ref[...].astype(BF16)
    wdn_ref[:, D_MODEL:] = jnp.zeros((W_DOWN_ROWS_PER_STEP, WEIGHT_LANE_PAD), BF16)

    n = len(slots) // N_GROUPS
    refs = (lbp_ref, anw_ref, win_ref, hnw_ref, cw_ref, wout_ref, rsel_ref,
            st_ref, st_old_ref, zc_ref, lb_ref, mg_ref, safe_ref,
            [slots[i * n:(i + 1) * n] for i in range(N_GROUPS)])

    first_tile = pl.program_id(1) == 0

    @pl.when(jnp.logical_and(first_tile, pl.program_id(0) == 0))
    def _():
        st_ref[...] = jnp.zeros_like(st_ref)
        zc_ref[...] = jnp.zeros_like(zc_ref)
        load_meta = lambda: jnp.concatenate(
            [meta_ref[...], jnp.zeros((CHUNK - N_META, D_MODEL), F32)], axis=0)

        def store_meta(val, accumulate):
            h1m_meta_ref[...] = val[0:N_META, :]

        _mixer_tile(load_meta, store_meta, CHUNK, N_META, True, refs)
        st_meta_ref[...] = st_ref[...]
        zc_meta_ref[...] = zc_ref[...]

    @pl.when(first_tile)
    def _():
        st_ref[...] = st_meta_ref[...]
        zc_ref[...] = zc_meta_ref[...]
        h1m_ref[0] = h1m_meta_ref[...]

    def store_tile(val, accumulate):
        h1_ref[0] = h1_ref[0] + val if accumulate else val

    _mixer_tile(lambda: x_ref[0], store_tile, MIXER_TILE, MIXER_TILE, False, refs)


def _ffn_kernel(h1_ref, h1m_ref, fnw_ref, wup_ref, fcw_ref, fcb_ref, wdn_ref, finw_ref,
                out_ref, as_ref):
    T = FFN_TILE
    fnw = fnw_ref[...]

    @pl.when(pl.program_id(1) == 0)
    def _():
        um = _rms(h1m_ref[0], fnw).astype(BF16)
        am = jnp.dot(um, wup_ref[:, 0:D_FF], preferred_element_type=F32)
        as_ref[0:SUBLANES, :] = am[N_META - SUBLANES:N_META, :]

    x = h1_ref[0]
    u = _rms(x, fnw).astype(BF16)
    up = jnp.dot(u, wup_ref[...], preferred_element_type=F32)
    a = up[:, 0:D_FF]
    as_ref[SUBLANES:SUBLANES + T, :] = a
    a1 = as_ref[SUBLANES - 1:SUBLANES - 1 + T, :]
    a2 = as_ref[SUBLANES - 2:SUBLANES - 2 + T, :]
    ac = fcw_ref[0:1, :] * a2 + fcw_ref[1:2, :] * a1 + fcw_ref[2:3, :] * a + fcb_ref[...]
    as_ref[0:SUBLANES, :] = as_ref[T:T + SUBLANES, :]
    gated = (ac * _sigmoid(ac) * up[:, D_FF:]).astype(BF16)
    y = x + jnp.dot(gated, wdn_ref[:, 0:D_MODEL], preferred_element_type=F32)
    out_ref[0] = _rms(y, finw_ref[...])


def _resident(shape):
    return pl.BlockSpec(shape, lambda b, t: (0,) * len(shape), pipeline_mode=pl.Buffered(1))


def _weight(w):
    return jnp.pad(w.astype(BF16), ((0, 0), (0, WEIGHT_LANE_PAD)))


def _rsel_matrix():
    r = np.arange(N_HEADS * HEAD_DIM)[:, None] // HEAD_DIM
    l = np.arange(CHUNK)[None, :] % SUBLANES
    return jnp.asarray(r == l, dtype=BF16)


def kernel(x, meta_tokens, lb_param, attn_norm_w, w_in, hgrn_norm_w, conv_w, w_out, ffn_norm_w, w_up,
           ffn_conv_w, ffn_conv_b, w_down, final_norm_w):
    B, L, D = x.shape
    assert D == D_MODEL and L % MIXER_TILE == 0 and L % FFN_TILE == 0
    assert w_in.shape == (1, D, N_IN_SECTIONS * D) and w_up.shape == (1, D, 2 * D_FF)
    assert SUBLANES * HEAD_DIM == D_MODEL and N_HEADS == SUBLANES

    tile = lambda T: pl.BlockSpec((1, T, D), lambda b, t: (b, t, 0))
    meta_tile = pl.BlockSpec((1, N_META, D), lambda b, t: (b, 0, 0))
    params = pltpu.CompilerParams(dimension_semantics=("arbitrary", "arbitrary"),
                                  vmem_limit_bytes=VMEM_LIMIT_BYTES)

    TM = MIXER_TILE
    n_steps = B * (L // TM)
    step = lambda b, t: b * (L // TM) + t
    assert D % n_steps == 0 and D_FF % W_DOWN_ROWS_PER_STEP == 0 and D_FF // W_DOWN_ROWS_PER_STEP <= n_steps
    wup_rows = D // n_steps
    wup_block = pl.BlockSpec((wup_rows, 2 * D_FF), lambda b, t: (step(b, t), 0))
    wdn_index = lambda b, t: (jnp.minimum(step(b, t), D_FF // W_DOWN_ROWS_PER_STEP - 1), 0)
    slot = [pltpu.VMEM((TM, GROUP_WIDTH), F32)] * 3 + [pltpu.VMEM((TM, GROUP_WIDTH), BF16)] * 4 \
        + [pltpu.VMEM((TM + SUBLANES, GROUP_WIDTH), F32)]
    h1, h1m, w_up_bf16, w_down_bf16 = pl.pallas_call(
        _mixer_kernel,
        grid=(B, L // TM),
        in_specs=[tile(TM), _resident((N_META, D)), _resident((2, D)), _resident((1, D)),
                  _resident((D, N_IN_SECTIONS * D + WEIGHT_LANE_PAD)), _resident((1, HEAD_DIM)),
                  _resident((3, D)), _resident((D, D + WEIGHT_LANE_PAD)),
                  _resident((N_HEADS * HEAD_DIM, CHUNK)), wup_block,
                  pl.BlockSpec((W_DOWN_ROWS_PER_STEP, D), wdn_index)],
        out_specs=[tile(TM), meta_tile, wup_block,
                   pl.BlockSpec((W_DOWN_ROWS_PER_STEP, D + WEIGHT_LANE_PAD), wdn_index)],
        out_shape=[jax.ShapeDtypeStruct((B, L, D), F32), jax.ShapeDtypeStruct((B, N_META, D), F32),
                   jax.ShapeDtypeStruct((D, 2 * D_FF), BF16),
                   jax.ShapeDtypeStruct((D_FF, D + WEIGHT_LANE_PAD), BF16)],
        scratch_shapes=[pltpu.VMEM((N_HEADS, HEAD_DIM, HEAD_DIM), F32),
                        pltpu.VMEM((N_HEADS, HEAD_DIM, HEAD_DIM), F32),
                        pltpu.VMEM((SUBLANES, D), F32),
                        pltpu.VMEM((SUBLANES, D), F32),
                        pltpu.VMEM((TM, D), BF16),
                        pltpu.SMEM((N_GROUPS,), jnp.int32),
                        pltpu.VMEM((N_HEADS, HEAD_DIM, HEAD_DIM), F32),
                        pltpu.VMEM((SUBLANES, D), F32),
                        pltpu.VMEM((N_META, D), F32),
                        *(slot * N_GROUPS)],
        compiler_params=params,
        name="mixer",
    )(x, meta_tokens, lb_param, attn_norm_w, _weight(w_in[0]), hgrn_norm_w, conv_w[0],
      _weight(w_out[0]), _rsel_matrix(), w_up[0], w_down[0])

    TF = FFN_TILE
    out = pl.pallas_call(
        _ffn_kernel,
        grid=(B, L // TF),
        in_specs=[tile(TF), meta_tile, _resident((1, D)), _resident((D, 2 * D_FF)), _resident((3, D_FF)),
                  _resident((1, D_FF)), _resident((D_FF, D + WEIGHT_LANE_PAD)), _resident((1, D))],
        out_specs=tile(TF),
        out_shape=jax.ShapeDtypeStruct((B, L, D), F32),
        scratch_shapes=[pltpu.VMEM((TF + SUBLANES, D_FF), F32)],
        compiler_params=params,
        name="ffn",
    )(h1, h1m, ffn_norm_w, w_up_bf16, ffn_conv_w[0], ffn_conv_b, w_down_bf16, final_norm_w.reshape(1, D))
    return out
```

```python
import numpy as np
import jax
import jax.numpy as jnp
from jax import lax
from jax.experimental import pallas as pl
from jax.experimental.pallas import tpu as pltpu

D_MODEL = 1024
N_META = 16
N_HEADS = 8
HEAD_DIM = 128
D_FF = 2816
N_IN_SECTIONS = 9
N_RECURRENCE_SECTIONS = 3
EPS = 1e-6

SUBLANES = 8
CHUNK = 128
GROUP_HEADS = 8
GROUP_WIDTH = GROUP_HEADS * HEAD_DIM
N_GROUPS = N_HEADS // GROUP_HEADS
MAX_SAFE_EXPONENT = 60.0
LOG2_E = 1.4426950408889634
MIXER_TILE = 512
FFN_TILE = 512
WEIGHT_LANE_PAD = 128
W_DOWN_ROWS_PER_STEP = 128
V7X_VMEM_BYTES = 64 * 1024 * 1024
VMEM_LIMIT_BYTES = V7X_VMEM_BYTES - 4 * 1024 * 1024

F32 = jnp.float32
BF16 = jnp.bfloat16

_NT = (((1,), (1,)), ((), ()))
_TN = (((0,), (0,)), ((), ()))


def _rms(x, w):
    ms = jnp.mean(x * x, axis=-1, keepdims=True)
    return x * lax.rsqrt(ms + EPS) * w


def _sigmoid(x):
    return 1.0 / (1.0 + jnp.exp2(x * -LOG2_E))


def _bcast_rows(ref, lanes, row0, block, offset, nrows):
    pieces = [jnp.broadcast_to(_row(ref, lanes, row0, i * block + offset), (block, HEAD_DIM))
              for i in range(nrows // block)]
    return pieces[0] if len(pieces) == 1 else jnp.concatenate(pieces, axis=0)


def _row(ref, lanes, row0, r):
    group = ref[pl.ds(row0 + r // SUBLANES * SUBLANES, SUBLANES), lanes]
    return group[r % SUBLANES:r % SUBLANES + 1, :]


def _hgrn_chunk_exact(row0, lanes, valid, st, q_ref, g_ref, k_ref, v_ref, rsel):
    rows = pl.ds(row0, CHUNK)
    q = q_ref[rows, lanes]
    G = g_ref[rows, lanes]
    k = k_ref[rows, lanes]
    v = v_ref[rows, lanes]
    row = lax.broadcasted_iota(jnp.int32, (CHUNK, HEAD_DIM), 0)

    sub = row & (SUBLANES - 1)
    ps = []
    for j in range(SUBLANES):
        gj = _bcast_rows(g_ref, lanes, row0, SUBLANES, j, CHUNK)
        kj = _bcast_rows(k_ref, lanes, row0, SUBLANES, j, CHUNK)
        p = q * kj * jnp.exp2(G - gj)
        ps.append(jnp.where(sub >= j, p, 0.0).astype(BF16))
    pcat = jnp.concatenate(ps, axis=1)
    a = jnp.dot(pcat, rsel[...], preferred_element_type=F32)

    tl_xor = (lax.broadcasted_iota(jnp.int32, (CHUNK, CHUNK), 0)
              ^ lax.broadcasted_iota(jnp.int32, (CHUNK, CHUNK), 1))
    a = jnp.where(tl_xor < SUBLANES, a, 0.0)

    b = 2 * SUBLANES
    while b <= CHUNK:
        gm = _bcast_rows(g_ref, lanes, row0, b, b // 2 - 1, CHUNK)
        e = jnp.exp2(-jnp.abs(G - gm))
        upper = (row & (b // 2)) != 0
        qt = jnp.where(upper, q * e, 0.0).astype(BF16)
        kt = jnp.where(upper, 0.0, k * e).astype(BF16)
        ab = lax.dot_general(qt, kt, _NT, preferred_element_type=F32)
        a = a + (ab if b == CHUNK else jnp.where(tl_xor < b, ab, 0.0))
        b *= 2

    qi = (q * jnp.exp2(G)).astype(BF16)
    o = lax.dot_general(qi, st.astype(BF16), _NT, preferred_element_type=F32)
    o = o + jnp.dot(a.astype(BF16), v, preferred_element_type=F32)

    glast = _row(g_ref, lanes, row0, valid - 1)
    kd = k * jnp.exp2(glast - G)
    if valid < CHUNK:
        kd = jnp.where(row < valid, kd, 0.0)
    st = st * jnp.exp2(glast) + lax.dot_general(v, kd.astype(BF16), _TN, preferred_element_type=F32)
    return o, st


def _hgrn_chunk_fast(row0, lanes, st, q_ref, g_ref, k_ref, v_ref):
    rows = pl.ds(row0, CHUNK)
    q = q_ref[rows, lanes]
    G = g_ref[rows, lanes]
    k = k_ref[rows, lanes]
    v = v_ref[rows, lanes]
    gmid = g_ref[pl.ds(row0 + CHUNK // 2 - 1, 1), lanes]
    glast = g_ref[pl.ds(row0 + CHUNK - 1, 1), lanes]
    d = G - gmid
    qh = q * jnp.exp2(d)
    kh = k * jnp.exp2(-d)
    a = lax.dot_general(qh.astype(BF16), kh.astype(BF16), _NT, preferred_element_type=F32)
    causal = (lax.broadcasted_iota(jnp.int32, (CHUNK, CHUNK), 1)
              <= lax.broadcasted_iota(jnp.int32, (CHUNK, CHUNK), 0))
    a = jnp.where(causal, a, 0.0)
    qi = (qh * jnp.exp2(gmid)).astype(BF16)
    o = lax.dot_general(qi, st.astype(BF16), _NT, preferred_element_type=F32)
    o = o + jnp.dot(a.astype(BF16), v, preferred_element_type=F32)
    kd = (kh * jnp.exp2(glast - gmid)).astype(BF16)
    st = st * jnp.exp2(glast) + lax.dot_general(v, kd, _TN, preferred_element_type=F32)
    return o, st


def _project_sections(u, win_ref, c0, first, last):
    if GROUP_WIDTH == D_MODEL:
        p = jnp.dot(u, win_ref[:, first * D_MODEL:last * D_MODEL], preferred_element_type=F32)
        return [p[:, i * D_MODEL:(i + 1) * D_MODEL] for i in range(last - first)]
    return [jnp.dot(u, win_ref[:, i * D_MODEL + c0:i * D_MODEL + c0 + GROUP_WIDTH], preferred_element_type=F32)
            for i in range(first, last)]


def _project_group(u, g, R, valid, prm, slot, recurrence_inputs):
    lb_ref, win_ref, cw_ref, zc_ref, safe_ref = prm
    q_ref, g_ref, k_ref, v_ref, gs_ref, sa_ref, mb_ref, zs_ref = slot
    c0 = g * GROUP_WIDTH
    cols = slice(c0, c0 + GROUP_WIDTH)

    sec = lambda first, last: _project_sections(u, win_ref, c0, first, last)

    if not recurrence_inputs:
        _project_gates(sec(N_RECURRENCE_SECTIONS, N_IN_SECTIONS), cols, R, valid, cw_ref, zc_ref, slot)
        return

    qv, fv, iv = sec(0, N_RECURRENCE_SECTIONS)
    q_ref[0:R, :] = qv * _sigmoid(qv)

    f = lb_ref[0:1, cols] + lb_ref[1:2, cols] * _sigmoid(fv)
    k_ref[0:R, :] = 1.0 - f
    lf = jnp.log2(f)
    tri = (lax.broadcasted_iota(jnp.int32, (CHUNK, CHUNK), 1)
           <= lax.broadcasted_iota(jnp.int32, (CHUNK, CHUNK), 0)).astype(BF16)
    tri2 = jnp.concatenate([tri, tri], axis=1)
    worst = jnp.zeros((1, GROUP_WIDTH), F32)
    for c in range(R // CHUNK):
        lfc = lf[c * CHUNK:(c + 1) * CHUNK, :]
        hi = lfc.astype(BF16)
        lo = (lfc - hi.astype(F32)).astype(BF16)
        gc = jnp.dot(tri2, jnp.concatenate([hi, lo], axis=0),
                     preferred_element_type=F32)
        g_ref[c * CHUNK:(c + 1) * CHUNK, :] = gc
        gmid = gc[CHUNK // 2 - 1:CHUNK // 2, :]
        glast = gc[CHUNK - 1:CHUNK, :]
        worst = jnp.maximum(worst, jnp.maximum(-gmid, gmid - glast))
    safe_ref[g] = (jnp.max(worst) <= MAX_SAFE_EXPONENT * LOG2_E).astype(jnp.int32)

    v_ref[0:R, :] = iv.astype(BF16)


def _project_gates(sections, cols, R, valid, cw_ref, zc_ref, slot):
    gs_ref, sa_ref, mb_ref, zs_ref = slot[4:8]
    gv, bg, cg, hc, ga, gb = sections
    gs_ref[0:R, :] = (gv * _sigmoid(gv)).astype(BF16)

    z = cg * hc
    zs_ref[0:SUBLANES, :] = zc_ref[:, cols]
    zs_ref[SUBLANES:SUBLANES + R, :] = z
    z1 = zs_ref[SUBLANES - 1:SUBLANES - 1 + R, :]
    z2 = zs_ref[SUBLANES - 2:SUBLANES - 2 + R, :]
    yb = bg * (cw_ref[0:1, cols] * z2 + cw_ref[1:2, cols] * z1 + cw_ref[2:3, cols] * z)
    zc_ref[:, cols] = zs_ref[valid:valid + SUBLANES, :]

    sa_ref[0:R, :] = _sigmoid(ga).astype(BF16)
    mb_ref[0:R, :] = (_sigmoid(gb) * yb).astype(BF16)


def _finish_chunk(row0, h, lanes, o, hnw, mg_ref, slot):
    gs_ref, sa_ref, mb_ref = slot[4:7]
    rows = pl.ds(row0, CHUNK)
    m = (_rms(o, hnw) * gs_ref[rows, lanes].astype(F32) * sa_ref[rows, lanes].astype(F32)
         + mb_ref[rows, lanes].astype(F32))
    mg_ref[rows, h * HEAD_DIM:(h + 1) * HEAD_DIM] = m.astype(BF16)


def _recur_group_fast(g, R, hnw, st_ref, st_old_ref, mg_ref, slot):
    q_ref, g_ref, k_ref, v_ref = slot[0:4]
    for hh in range(GROUP_HEADS):
        h = GROUP_HEADS * g + hh
        lanes = slice(hh * HEAD_DIM, (hh + 1) * HEAD_DIM)
        st = st_ref[h]
        st_old_ref[h] = st
        for c in range(R // CHUNK):
            o, st = _hgrn_chunk_fast(c * CHUNK, lanes, st, q_ref, g_ref, k_ref, v_ref)
            mg_ref[c * CHUNK:(c + 1) * CHUNK, h * HEAD_DIM:(h + 1) * HEAD_DIM] = _rms(o, hnw).astype(BF16)
        st_ref[h] = st


def _finish_group_fast(g, R, mg_ref, slot):
    gs_ref, sa_ref, mb_ref = slot[4:7]
    cols = slice(g * GROUP_WIDTH, (g + 1) * GROUP_WIDTH)
    mg_ref[0:R, cols] = mg_ref[0:R, cols] * gs_ref[0:R, :] * sa_ref[0:R, :] + mb_ref[0:R, :]


def _recur_group_exact(g, R, valid, hnw, rsel, st_in_ref, st_ref, mg_ref, slot):
    q_ref, g_ref, k_ref, v_ref = slot[0:4]
    for hh in range(GROUP_HEADS):
        h = GROUP_HEADS * g + hh
        lanes = slice(hh * HEAD_DIM, (hh + 1) * HEAD_DIM)
        st = st_in_ref[h]
        if valid < R:
            o, st = _hgrn_chunk_exact(0, lanes, valid, st, q_ref, g_ref, k_ref, v_ref, rsel)
            _finish_chunk(0, h, lanes, o, hnw, mg_ref, slot)
        else:
            def chunk_body(c, st):
                row0 = pl.multiple_of(c * CHUNK, CHUNK)
                o, st = _hgrn_chunk_exact(row0, lanes, CHUNK, st, q_ref, g_ref, k_ref, v_ref, rsel)
                _finish_chunk(row0, h, lanes, o, hnw, mg_ref, slot)
                return st

            st = lax.fori_loop(0, R // CHUNK, chunk_body, st)
        st_ref[h] = st


def _out_partial(g, R, mg_ref, wout_ref):
    rows = slice(g * GROUP_WIDTH, (g + 1) * GROUP_WIDTH)
    return jnp.dot(mg_ref[0:R, rows], wout_ref[rows, 0:D_MODEL], preferred_element_type=F32)


def _mixer_tile(load_x, store_out, R, valid, is_meta, refs):
    (lbp_ref, anw_ref, win_ref, hnw_ref, cw_ref, wout_ref, rsel_ref,
     st_ref, st_old_ref, zc_ref, lb_ref, mg_ref, safe_ref, slots) = refs
    prm = (lb_ref, win_ref, cw_ref, zc_ref, safe_ref)
    u = _rms(load_x(), anw_ref[...]).astype(BF16)
    rsel = rsel_ref
    hnw = hnw_ref[...]
    project = lambda g, first: _project_group(u, g, R, valid, prm, slots[g], first)

    lbp = lbp_ref[...]
    mx = jnp.max(lbp, axis=0, keepdims=True)
    ex = jnp.exp(lbp - mx)
    lb = ex[0:1, :] / jnp.sum(ex, axis=0, keepdims=True)
    lb_ref[0:1, :] = lb
    lb_ref[1:2, :] = 1.0 - lb

    if is_meta:
        for g in range(N_GROUPS):
            project(g, True)
            project(g, False)
            _recur_group_exact(g, R, valid, hnw, rsel, st_ref, st_ref, mg_ref, slots[g])
        acc = load_x()
        for g in range(N_GROUPS):
            acc = acc + _out_partial(g, R, mg_ref, wout_ref)
        store_out(acc, False)
        return

    def store_early():
        acc = load_x()
        for g in range(N_GROUPS - 1):
            acc = acc + _out_partial(g, R, mg_ref, wout_ref)
        store_out(acc, False)

    for g in range(N_GROUPS):
        project(g, True)
        if g > 0:
            _finish_group_fast(g - 1, R, mg_ref, slots[g - 1])
        if g == N_GROUPS - 1:
            store_early()
        _recur_group_fast(g, R, hnw, st_ref, st_old_ref, mg_ref, slots[g])
        project(g, False)
    _finish_group_fast(N_GROUPS - 1, R, mg_ref, slots[N_GROUPS - 1])
    for g in range(N_GROUPS):
        def redo(g=g):
            _recur_group_exact(g, R, valid, hnw, rsel, st_old_ref, st_ref, mg_ref, slots[g])
            if g < N_GROUPS - 1:
                store_early()

        pl.when(safe_ref[g] == 0)(redo)
    store_out(_out_partial(N_GROUPS - 1, R, mg_ref, wout_ref), True)


def _mixer_kernel(x_ref, meta_ref, lbp_ref, anw_ref, win_ref, hnw_ref, cw_ref, wout_ref, rsel_ref,
                  wup_f32_ref, wdn_f32_ref, h1_ref, h1m_ref, wup_ref, wdn_ref,
                  st_ref, st_old_ref, zc_ref, lb_ref, mg_ref, safe_ref,
                  st_meta_ref, zc_meta_ref, h1m_meta_ref, *slots):
    wup_ref[...] = wup_f32_ref[...].astype(BF16)
    wdn_ref[:, 0:D_MODEL] = wdn_f32_ref[...].astype(BF16)
    wdn_ref[:, D_MODEL:] = jnp.zeros((W_DOWN_ROWS_PER_STEP, WEIGHT_LANE_PAD), BF16)

    n = len(slots) // N_GROUPS
    refs = (lbp_ref, anw_ref, win_ref, hnw_ref, cw_ref, wout_ref, rsel_ref,
            st_ref, st_old_ref, zc_ref, lb_ref, mg_ref, safe_ref,
            [slots[i * n:(i + 1) * n] for i in range(N_GROUPS)])

    first_tile = pl.program_id(1) == 0

    @pl.when(jnp.logical_and(first_tile, pl.program_id(0) == 0))
    def _():
        st_ref[...] = jnp.zeros_like(st_ref)
        zc_ref[...] = jnp.zeros_like(zc_ref)
        load_meta = lambda: jnp.concatenate(
            [meta_ref[...], jnp.zeros((CHUNK - N_META, D_MODEL), F32)], axis=0)

        def store_meta(val, accumulate):
            h1m_meta_ref[...] = val[0:N_META, :]

        _mixer_tile(load_meta, store_meta, CHUNK, N_META, True, refs)
        st_meta_ref[...] = st_ref[...]
        zc_meta_ref[...] = zc_ref[...]

    @pl.when(first_tile)
    def _():
        st_ref[...] = st_meta_ref[...]
        zc_ref[...] = zc_meta_ref[...]
        h1m_ref[0] = h1m_meta_ref[...]

    def store_tile(val, accumulate):
        h1_ref[0] = h1_ref[0] + val if accumulate else val

    _mixer_tile(lambda: x_ref[0], store_tile, MIXER_TILE, MIXER_TILE, False, refs)


def _ffn_kernel(h1_ref, h1m_ref, fnw_ref, wup_ref, fcw_ref, fcb_ref, wdn_ref, finw_ref,
                out_ref, as_ref):
    T = FFN_TILE
    fnw = fnw_ref[...]

    @pl.when(pl.program_id(1) == 0)
    def _():
        um = _rms(h1m_ref[0], fnw).astype(BF16)
        am = jnp.dot(um, wup_ref[:, 0:D_FF], preferred_element_type=F32)
        as_ref[0:SUBLANES, :] = am[N_META - SUBLANES:N_META, :]

    x = h1_ref[0]
    u = _rms(x, fnw).astype(BF16)
    up = jnp.dot(u, wup_ref[...], preferred_element_type=F32)
    a = up[:, 0:D_FF]
    as_ref[SUBLANES:SUBLANES + T, :] = a
    a1 = as_ref[SUBLANES - 1:SUBLANES - 1 + T, :]
    a2 = as_ref[SUBLANES - 2:SUBLANES - 2 + T, :]
    ac = fcw_ref[0:1, :] * a2 + fcw_ref[1:2, :] * a1 + fcw_ref[2:3, :] * a + fcb_ref[...]
    as_ref[0:SUBLANES, :] = as_ref[T:T + SUBLANES, :]
    gated = (ac * _sigmoid(ac) * up[:, D_FF:]).astype(BF16)
    y = x + jnp.dot(gated, wdn_ref[:, 0:D_MODEL], preferred_element_type=F32)
    out_ref[0] = _rms(y, finw_ref[...])


def _resident(shape):
    return pl.BlockSpec(shape, lambda b, t: (0,) * len(shape), pipeline_mode=pl.Buffered(1))


def _weight(w):
    return jnp.pad(w.astype(BF16), ((0, 0), (0, WEIGHT_LANE_PAD)))


def _rsel_matrix():
    r = np.arange(N_HEADS * HEAD_DIM)[:, None] // HEAD_DIM
    l = np.arange(CHUNK)[None, :] % SUBLANES
    return jnp.asarray(r == l, dtype=BF16)


def kernel(x, meta_tokens, lb_param, attn_norm_w, w_in, hgrn_norm_w, conv_w, w_out, ffn_norm_w, w_up,
           ffn_conv_w, ffn_conv_b, w_down, final_norm_w):
    B, L, D = x.shape
    assert D == D_MODEL and L % MIXER_TILE == 0 and L % FFN_TILE == 0
    assert w_in.shape == (1, D, N_IN_SECTIONS * D) and w_up.shape == (1, D, 2 * D_FF)
    assert SUBLANES * HEAD_DIM == D_MODEL and N_HEADS == SUBLANES

    tile = lambda T: pl.BlockSpec((1, T, D), lambda b, t: (b, t, 0))
    meta_tile = pl.BlockSpec((1, N_META, D), lambda b, t: (b, 0, 0))
    params = pltpu.CompilerParams(dimension_semantics=("arbitrary", "arbitrary"),
                                  vmem_limit_bytes=VMEM_LIMIT_BYTES)

    TM = MIXER_TILE
    n_steps = B * (L // TM)
    step = lambda b, t: b * (L // TM) + t
    assert D % n_steps == 0 and D_FF % W_DOWN_ROWS_PER_STEP == 0 and D_FF // W_DOWN_ROWS_PER_STEP <= n_steps
    wup_rows = D // n_steps
    wup_block = pl.BlockSpec((wup_rows, 2 * D_FF), lambda b, t: (step(b, t), 0))
    wdn_index = lambda b, t: (jnp.minimum(step(b, t), D_FF // W_DOWN_ROWS_PER_STEP - 1), 0)
    slot = [pltpu.VMEM((TM, GROUP_WIDTH), F32)] * 3 + [pltpu.VMEM((TM, GROUP_WIDTH), BF16)] * 4 \
        + [pltpu.VMEM((TM + SUBLANES, GROUP_WIDTH), F32)]
    h1, h1m, w_up_bf16, w_down_bf16 = pl.pallas_call(
        _mixer_kernel,
        grid=(B, L // TM),
        in_specs=[tile(TM), _resident((N_META, D)), _resident((2, D)), _resident((1, D)),
                  _resident((D, N_IN_SECTIONS * D + WEIGHT_LANE_PAD)), _resident((1, HEAD_DIM)),
                  _resident((3, D)), _resident((D, D + WEIGHT_LANE_PAD)),
                  _resident((N_HEADS * HEAD_DIM, CHUNK)), wup_block,
                  pl.BlockSpec((W_DOWN_ROWS_PER_STEP, D), wdn_index)],
        out_specs=[tile(TM), meta_tile, wup_block,
                   pl.BlockSpec((W_DOWN_ROWS_PER_STEP, D + WEIGHT_LANE_PAD), wdn_index)],
        out_shape=[jax.ShapeDtypeStruct((B, L, D), F32), jax.ShapeDtypeStruct((B, N_META, D), F32),
                   jax.ShapeDtypeStruct((D, 2 * D_FF), BF16),
                   jax.ShapeDtypeStruct((D_FF, D + WEIGHT_LANE_PAD), BF16)],
        scratch_shapes=[pltpu.VMEM((N_HEADS, HEAD_DIM, HEAD_DIM), F32),
                        pltpu.VMEM((N_HEADS, HEAD_DIM, HEAD_DIM), F32),
                        pltpu.VMEM((SUBLANES, D), F32),
                        pltpu.VMEM((SUBLANES, D), F32),
                        pltpu.VMEM((TM, D), BF16),
                        pltpu.SMEM((N_GROUPS,), jnp.int32),
                        pltpu.VMEM((N_HEADS, HEAD_DIM, HEAD_DIM), F32),
                        pltpu.VMEM((SUBLANES, D), F32),
                        pltpu.VMEM((N_META, D), F32),
                        *(slot * N_GROUPS)],
        compiler_params=params,
        name="mixer",
    )(x, meta_tokens, lb_param, attn_norm_w, _weight(w_in[0]), hgrn_norm_w, conv_w[0],
      _weight(w_out[0]), _rsel_matrix(), w_up[0], w_down[0])

    TF = FFN_TILE
    out = pl.pallas_call(
        _ffn_kernel,
        grid=(B, L // TF),
        in_specs=[tile(TF), meta_tile, _resident((1, D)), _resident((D, 2 * D_FF)), _resident((3, D_FF)),
                  _resident((1, D_FF)), _resident((D_FF, D + WEIGHT_LANE_PAD)), _resident((1, D))],
        out_specs=tile(TF),
        out_shape=jax.ShapeDtypeStruct((B, L, D), F32),
        scratch_shapes=[pltpu.VMEM((TF + SUBLANES, D_FF), F32)],
        compiler_params=params,
        name="ffn",
    )(h1, h1m, ffn_norm_w, w_up_bf16, ffn_conv_w[0], ffn_conv_b, w_down_bf16, final_norm_w.reshape(1, D))
    return out
```

```python
import numpy as np
import jax
import jax.numpy as jnp
from jax import lax
from jax.experimental import pallas as pl
from jax.experimental.pallas import tpu as pltpu

D_MODEL = 1024
N_META = 16
N_HEADS = 8
HEAD_DIM = 128
D_FF = 2816
N_IN_SECTIONS = 9
N_RECURRENCE_SECTIONS = 3
EPS = 1e-6

SUBLANES = 8
CHUNK = 128
GROUP_HEADS = 8
GROUP_WIDTH = GROUP_HEADS * HEAD_DIM
N_GROUPS = N_HEADS // GROUP_HEADS
MAX_SAFE_EXPONENT = 60.0
LOG2_E = 1.4426950408889634
MIXER_TILE = 512
FFN_TILE = 512
WEIGHT_LANE_PAD = 128
W_DOWN_ROWS_PER_STEP = 128
V7X_VMEM_BYTES = 64 * 1024 * 1024
VMEM_LIMIT_BYTES = V7X_VMEM_BYTES - 4 * 1024 * 1024

F32 = jnp.float32
BF16 = jnp.bfloat16

_NT = (((1,), (1,)), ((), ()))
_TN = (((0,), (0,)), ((), ()))


def _rms(x, w):
    ms = jnp.mean(x * x, axis=-1, keepdims=True)
    return x * lax.rsqrt(ms + EPS) * w


def _sigmoid(x):
    return 1.0 / (1.0 + jnp.exp2(x * -LOG2_E))


def _bcast_rows(ref, lanes, row0, block, offset, nrows):
    pieces = [jnp.broadcast_to(_row(ref, lanes, row0, i * block + offset), (block, HEAD_DIM))
              for i in range(nrows // block)]
    return pieces[0] if len(pieces) == 1 else jnp.concatenate(pieces, axis=0)


def _row(ref, lanes, row0, r):
    group = ref[pl.ds(row0 + r // SUBLANES * SUBLANES, SUBLANES), lanes]
    return group[r % SUBLANES:r % SUBLANES + 1, :]


def _hgrn_chunk_exact(row0, lanes, valid, st, q_ref, g_ref, k_ref, v_ref, rsel):
    rows = pl.ds(row0, CHUNK)
    q = q_ref[rows, lanes]
    G = g_ref[rows, lanes]
    k = k_ref[rows, lanes]
    v = v_ref[rows, lanes]
    row = lax.broadcasted_iota(jnp.int32, (CHUNK, HEAD_DIM), 0)

    sub = row & (SUBLANES - 1)
    ps = []
    for j in range(SUBLANES):
        gj = _bcast_rows(g_ref, lanes, row0, SUBLANES, j, CHUNK)
        kj = _bcast_rows(k_ref, lanes, row0, SUBLANES, j, CHUNK)
        p = q * kj * jnp.exp2(G - gj)
        ps.append(jnp.where(sub >= j, p, 0.0).astype(BF16))
    pcat = jnp.concatenate(ps, axis=1)
    a = jnp.dot(pcat, rsel[...], preferred_element_type=F32)

    tl_xor = (lax.broadcasted_iota(jnp.int32, (CHUNK, CHUNK), 0)
              ^ lax.broadcasted_iota(jnp.int32, (CHUNK, CHUNK), 1))
    a = jnp.where(tl_xor < SUBLANES, a, 0.0)

    b = 2 * SUBLANES
    while b <= CHUNK:
        gm = _bcast_rows(g_ref, lanes, row0, b, b // 2 - 1, CHUNK)
        e = jnp.exp2(-jnp.abs(G - gm))
        upper = (row & (b // 2)) != 0
        qt = jnp.where(upper, q * e, 0.0).astype(BF16)
        kt = jnp.where(upper, 0.0, k * e).astype(BF16)
        ab = lax.dot_general(qt, kt, _NT, preferred_element_type=F32)
        a = a + (ab if b == CHUNK else jnp.where(tl_xor < b, ab, 0.0))
        b *= 2

    qi = (q * jnp.exp2(G)).astype(BF16)
    o = lax.dot_general(qi, st.astype(BF16), _NT, preferred_element_type=F32)
    o = o + jnp.dot(a.astype(BF16), v, preferred_element_type=F32)

    glast = _row(g_ref, lanes, row0, valid - 1)
    kd = k * jnp.exp2(glast - G)
    if valid < CHUNK:
        kd = jnp.where(row < valid, kd, 0.0)
    st = st * jnp.exp2(glast) + lax.dot_general(v, kd.astype(BF16), _TN, preferred_element_type=F32)
    return o, st


def _hgrn_chunk_fast(row0, lanes, st, q_ref, g_ref, k_ref, v_ref):
    rows = pl.ds(row0, CHUNK)
    q = q_ref[rows, lanes]
    G = g_ref[rows, lanes]
    k = k_ref[rows, lanes]
    v = v_ref[rows, lanes]
    gmid = g_ref[pl.ds(row0 + CHUNK // 2 - 1, 1), lanes]
    glast = g_ref[pl.ds(row0 + CHUNK - 1, 1), lanes]
    d = G - gmid
    qh = q * jnp.exp2(d)
    kh = k * jnp.exp2(-d)
    a = lax.dot_general(qh.astype(BF16), kh.astype(BF16), _NT, preferred_element_type=F32)
    causal = (lax.broadcasted_iota(jnp.int32, (CHUNK, CHUNK), 1)
              <= lax.broadcasted_iota(jnp.int32, (CHUNK, CHUNK), 0))
    a = jnp.where(causal, a, 0.0)
    qi = (qh * jnp.exp2(gmid)).astype(BF16)
    o = lax.dot_general(qi, st.astype(BF16), _NT, preferred_element_type=F32)
    o = o + jnp.dot(a.astype(BF16), v, preferred_element_type=F32)
    kd = (kh * jnp.exp2(glast - gmid)).astype(BF16)
    st = st * jnp.exp2(glast) + lax.dot_general(v, kd, _TN, preferred_element_type=F32)
    return o, st


def _project_sections(u, win_ref, c0, first, last):
    if GROUP_WIDTH == D_MODEL:
        p = jnp.dot(u, win_ref[:, first * D_MODEL:last * D_MODEL], preferred_element_type=F32)
        return [p[:, i * D_MODEL:(i + 1) * D_MODEL] for i in range(last - first)]
    return [jnp.dot(u, win_ref[:, i * D_MODEL + c0:i * D_MODEL + c0 + GROUP_WIDTH], preferred_element_type=F32)
            for i in range(first, last)]


def _project_group(u, g, R, valid, prm, slot, recurrence_inputs):
    lb_ref, win_ref, cw_ref, zc_ref, safe_ref = prm
    q_ref, g_ref, k_ref, v_ref, gs_ref, sa_ref, mb_ref, zs_ref = slot
    c0 = g * GROUP_WIDTH
    cols = slice(c0, c0 + GROUP_WIDTH)

    sec = lambda first, last: _project_sections(u, win_ref, c0, first, last)

    if not recurrence_inputs:
        _project_gates(sec(N_RECURRENCE_SECTIONS, N_IN_SECTIONS), cols, R, valid, cw_ref, zc_ref, slot)
        return

    qv, fv, iv = sec(0, N_RECURRENCE_SECTIONS)
    q_ref[0:R, :] = qv * _sigmoid(qv)

    f = lb_ref[0:1, cols] + lb_ref[1:2, cols] * _sigmoid(fv)
    k_ref[0:R, :] = 1.0 - f
    lf = jnp.log2(f)
    tri = (lax.broadcasted_iota(jnp.int32, (CHUNK, CHUNK), 1)
           <= lax.broadcasted_iota(jnp.int32, (CHUNK, CHUNK), 0)).astype(BF16)
    tri2 = jnp.concatenate([tri, tri], axis=1)
    worst = jnp.zeros((1, GROUP_WIDTH), F32)
    for c in range(R // CHUNK):
        lfc = lf[c * CHUNK:(c + 1) * CHUNK, :]
        hi = lfc.astype(BF16)
        lo = (lfc - hi.astype(F32)).astype(BF16)
        gc = jnp.dot(tri2, jnp.concatenate([hi, lo], axis=0),
                     preferred_element_type=F32)
        g_ref[c * CHUNK:(c + 1) * CHUNK, :] = gc
        gmid = gc[CHUNK // 2 - 1:CHUNK // 2, :]
        glast = gc[CHUNK - 1:CHUNK, :]
        worst = jnp.maximum(worst, jnp.maximum(-gmid, gmid - glast))
    safe_ref[g] = (jnp.max(worst) <= MAX_SAFE_EXPONENT * LOG2_E).astype(jnp.int32)

    v_ref[0:R, :] = iv.astype(BF16)


def _project_gates(sections, cols, R, valid, cw_ref, zc_ref, slot):
    gs_ref, sa_ref, mb_ref, zs_ref = slot[4:8]
    gv, bg, cg, hc, ga, gb = sections
    gs_ref[0:R, :] = (gv * _sigmoid(gv)).astype(BF16)

    z = cg * hc
    zs_ref[0:SUBLANES, :] = zc_ref[:, cols]
    zs_ref[SUBLANES:SUBLANES + R, :] = z
    z1 = zs_ref[SUBLANES - 1:SUBLANES - 1 + R, :]
    z2 = zs_ref[SUBLANES - 2:SUBLANES - 2 + R, :]
    yb = bg * (cw_ref[0:1, cols] * z2 + cw_ref[1:2, cols] * z1 + cw_ref[2:3, cols] * z)
    zc_ref[:, cols] = zs_ref[valid:valid + SUBLANES, :]

    sa_ref[0:R, :] = _sigmoid(ga).astype(BF16)
    mb_ref[0:R, :] = (_sigmoid(gb) * yb).astype(BF16)


def _finish_chunk(row0, h, lanes, o, hnw, mg_ref, slot):
    gs_ref, sa_ref, mb_ref = slot[4:7]
    rows = pl.ds(row0, CHUNK)
    m = (_rms(o, hnw) * gs_ref[rows, lanes].astype(F32) * sa_ref[rows, lanes].astype(F32)
         + mb_ref[rows, lanes].astype(F32))
    mg_ref[rows, h * HEAD_DIM:(h + 1) * HEAD_DIM] = m.astype(BF16)


def _recur_group_fast(g, R, hnw, st_ref, st_old_ref, mg_ref, slot):
    q_ref, g_ref, k_ref, v_ref = slot[0:4]
    for hh in range(GROUP_HEADS):
        h = GROUP_HEADS * g + hh
        lanes = slice(hh * HEAD_DIM, (hh + 1) * HEAD_DIM)
        st = st_ref[h]
        st_old_ref[h] = st
        for c in range(R // CHUNK):
            o, st = _hgrn_chunk_fast(c * CHUNK, lanes, st, q_ref, g_ref, k_ref, v_ref)
            mg_ref[c * CHUNK:(c + 1) * CHUNK, h * HEAD_DIM:(h + 1) * HEAD_DIM] = _rms(o, hnw).astype(BF16)
        st_ref[h] = st


def _finish_group_fast(g, R, mg_ref, slot):
    gs_ref, sa_ref, mb_ref = slot[4:7]
    cols = slice(g * GROUP_WIDTH, (g + 1) * GROUP_WIDTH)
    mg_ref[0:R, cols] = mg_ref[0:R, cols] * gs_ref[0:R, :] * sa_ref[0:R, :] + mb_ref[0:R, :]


def _recur_group_exact(g, R, valid, hnw, rsel, st_in_ref, st_ref, mg_ref, slot):
    q_ref, g_ref, k_ref, v_ref = slot[0:4]
    for hh in range(GROUP_HEADS):
        h = GROUP_HEADS * g + hh
        lanes = slice(hh * HEAD_DIM, (hh + 1) * HEAD_DIM)
        st = st_in_ref[h]
        if valid < R:
            o, st = _hgrn_chunk_exact(0, lanes, valid, st, q_ref, g_ref, k_ref, v_ref, rsel)
            _finish_chunk(0, h, lanes, o, hnw, mg_ref, slot)
        else:
            def chunk_body(c, st):
                row0 = pl.multiple_of(c * CHUNK, CHUNK)
                o, st = _hgrn_chunk_exact(row0, lanes, CHUNK, st, q_ref, g_ref, k_ref, v_ref, rsel)
                _finish_chunk(row0, h, lanes, o, hnw, mg_ref, slot)
                return st

            st = lax.fori_loop(0, R // CHUNK, chunk_body, st)
        st_ref[h] = st


def _out_partial(g, R, mg_ref, wout_ref):
    rows = slice(g * GROUP_WIDTH, (g + 1) * GROUP_WIDTH)
    return jnp.dot(mg_ref[0:R, rows], wout_ref[rows, 0:D_MODEL], preferred_element_type=F32)


def _mixer_tile(load_x, store_out, R, valid, is_meta, refs):
    (lbp_ref, anw_ref, win_ref, hnw_ref, cw_ref, wout_ref, rsel_ref,
     st_ref, st_old_ref, zc_ref, lb_ref, mg_ref, safe_ref, slots) = refs
    prm = (lb_ref, win_ref, cw_ref, zc_ref, safe_ref)
    u = _rms(load_x(), anw_ref[...]).astype(BF16)
    rsel = rsel_ref
    hnw = hnw_ref[...]
    project = lambda g, first: _project_group(u, g, R, valid, prm, slots[g], first)

    lbp = lbp_ref[...]
    mx = jnp.max(lbp, axis=0, keepdims=True)
    ex = jnp.exp(lbp - mx)
    lb = ex[0:1, :] / jnp.sum(ex, axis=0, keepdims=True)
    lb_ref[0:1, :] = lb
    lb_ref[1:2, :] = 1.0 - lb

    if is_meta:
        for g in range(N_GROUPS):
            project(g, True)
            project(g, False)
            _recur_group_exact(g, R, valid, hnw, rsel, st_ref, st_ref, mg_ref, slots[g])
        acc = load_x()
        for g in range(N_GROUPS):
            acc = acc + _out_partial(g, R, mg_ref, wout_ref)
        store_out(acc, False)
        return

    def store_early():
        acc = load_x()
        for g in range(N_GROUPS - 1):
            acc = acc + _out_partial(g, R, mg_ref, wout_ref)
        store_out(acc, False)

    for g in range(N_GROUPS):
        project(g, True)
        if g > 0:
            _finish_group_fast(g - 1, R, mg_ref, slots[g - 1])
        if g == N_GROUPS - 1:
            store_early()
        _recur_group_fast(g, R, hnw, st_ref, st_old_ref, mg_ref, slots[g])
        project(g, False)
    _finish_group_fast(N_GROUPS - 1, R, mg_ref, slots[N_GROUPS - 1])
    for g in range(N_GROUPS):
        def redo(g=g):
            _recur_group_exact(g, R, valid, hnw, rsel, st_old_ref, st_ref, mg_ref, slots[g])
            if g < N_GROUPS - 1:
                store_early()

        pl.when(safe_ref[g] == 0)(redo)
    store_out(_out_partial(N_GROUPS - 1, R, mg_ref, wout_ref), True)


def _mixer_kernel(x_ref, meta_ref, lbp_ref, anw_ref, win_ref, hnw_ref, cw_ref, wout_ref, rsel_ref,
                  wup_f32_ref, wdn_f32_ref, h1_ref, h1m_ref, wup_ref, wdn_ref,
                  st_ref, st_old_ref, zc_ref, lb_ref, mg_ref, safe_ref,
                  st_meta_ref, zc_meta_ref, h1m_meta_ref, wout_bf16_ref, *slots):
    @pl.when(jnp.logical_and(pl.program_id(0) == 0, pl.program_id(1) == 0))
    def _():
        wout_bf16_ref[:, 0:D_MODEL] = wout_ref[...].astype(BF16)

    wout_ref = wout_bf16_ref
    wup_ref[...] = wup_f32_ref[...].astype(BF16)
    wdn_ref[:, 0:D_MODEL] = wdn_f32_ref[...].astype(BF16)
    wdn_ref[:, D_MODEL:] = jnp.zeros((W_DOWN_ROWS_PER_STEP, WEIGHT_LANE_PAD), BF16)

    n = len(slots) // N_GROUPS
    refs = (lbp_ref, anw_ref, win_ref, hnw_ref, cw_ref, wout_ref, rsel_ref,
            st_ref, st_old_ref, zc_ref, lb_ref, mg_ref, safe_ref,
            [slots[i * n:(i + 1) * n] for i in range(N_GROUPS)])

    first_tile = pl.program_id(1) == 0

    @pl.when(jnp.logical_and(first_tile, pl.program_id(0) == 0))
    def _():
        st_ref[...] = jnp.zeros_like(st_ref)
        zc_ref[...] = jnp.zeros_like(zc_ref)
        load_meta = lambda: jnp.concatenate(
            [meta_ref[...], jnp.zeros((CHUNK - N_META, D_MODEL), F32)], axis=0)

        def store_meta(val, accumulate):
            h1m_meta_ref[...] = val[0:N_META, :]

        _mixer_tile(load_meta, store_meta, CHUNK, N_META, True, refs)
        st_meta_ref[...] = st_ref[...]
        zc_meta_ref[...] = zc_ref[...]

    @pl.when(first_tile)
    def _():
        st_ref[...] = st_meta_ref[...]
        zc_ref[...] = zc_meta_ref[...]
        h1m_ref[0] = h1m_meta_ref[...]

    def store_tile(val, accumulate):
        h1_ref[0] = h1_ref[0] + val if accumulate else val

    _mixer_tile(lambda: x_ref[0], store_tile, MIXER_TILE, MIXER_TILE, False, refs)


def _ffn_kernel(h1_ref, h1m_ref, fnw_ref, wup_ref, fcw_ref, fcb_ref, wdn_ref, finw_ref,
                out_ref, as_ref):
    T = FFN_TILE
    fnw = fnw_ref[...]

    @pl.when(pl.program_id(1) == 0)
    def _():
        um = _rms(h1m_ref[0], fnw).astype(BF16)
        am = jnp.dot(um, wup_ref[:, 0:D_FF], preferred_element_type=F32)
        as_ref[0:SUBLANES, :] = am[N_META - SUBLANES:N_META, :]

    x = h1_ref[0]
    u = _rms(x, fnw).astype(BF16)
    up = jnp.dot(u, wup_ref[...], preferred_element_type=F32)
    a = up[:, 0:D_FF]
    as_ref[SUBLANES:SUBLANES + T, :] = a
    a1 = as_ref[SUBLANES - 1:SUBLANES - 1 + T, :]
    a2 = as_ref[SUBLANES - 2:SUBLANES - 2 + T, :]
    ac = fcw_ref[0:1, :] * a2 + fcw_ref[1:2, :] * a1 + fcw_ref[2:3, :] * a + fcb_ref[...]
    as_ref[0:SUBLANES, :] = as_ref[T:T + SUBLANES, :]
    gated = (ac * _sigmoid(ac) * up[:, D_FF:]).astype(BF16)
    y = x + jnp.dot(gated, wdn_ref[:, 0:D_MODEL], preferred_element_type=F32)
    out_ref[0] = _rms(y, finw_ref[...])


def _resident(shape):
    return pl.BlockSpec(shape, lambda b, t: (0,) * len(shape), pipeline_mode=pl.Buffered(1))


def _weight(w):
    return jnp.pad(w.astype(BF16), ((0, 0), (0, WEIGHT_LANE_PAD)))


def _rsel_matrix():
    r = np.arange(N_HEADS * HEAD_DIM)[:, None] // HEAD_DIM
    l = np.arange(CHUNK)[None, :] % SUBLANES
    return jnp.asarray(r == l, dtype=BF16)


def kernel(x, meta_tokens, lb_param, attn_norm_w, w_in, hgrn_norm_w, conv_w, w_out, ffn_norm_w, w_up,
           ffn_conv_w, ffn_conv_b, w_down, final_norm_w):
    B, L, D = x.shape
    assert D == D_MODEL and L % MIXER_TILE == 0 and L % FFN_TILE == 0
    assert w_in.shape == (1, D, N_IN_SECTIONS * D) and w_up.shape == (1, D, 2 * D_FF)
    assert SUBLANES * HEAD_DIM == D_MODEL and N_HEADS == SUBLANES

    tile = lambda T: pl.BlockSpec((1, T, D), lambda b, t: (b, t, 0))
    meta_tile = pl.BlockSpec((1, N_META, D), lambda b, t: (b, 0, 0))
    params = pltpu.CompilerParams(dimension_semantics=("arbitrary", "arbitrary"),
                                  vmem_limit_bytes=VMEM_LIMIT_BYTES)

    TM = MIXER_TILE
    n_steps = B * (L // TM)
    step = lambda b, t: b * (L // TM) + t
    assert D % n_steps == 0 and D_FF % W_DOWN_ROWS_PER_STEP == 0 and D_FF // W_DOWN_ROWS_PER_STEP <= n_steps
    wup_rows = D // n_steps
    wup_block = pl.BlockSpec((wup_rows, 2 * D_FF), lambda b, t: (step(b, t), 0))
    wdn_index = lambda b, t: (jnp.minimum(step(b, t), D_FF // W_DOWN_ROWS_PER_STEP - 1), 0)
    slot = [pltpu.VMEM((TM, GROUP_WIDTH), F32)] * 3 + [pltpu.VMEM((TM, GROUP_WIDTH), BF16)] * 4 \
        + [pltpu.VMEM((TM + SUBLANES, GROUP_WIDTH), F32)]
    h1, h1m, w_up_bf16, w_down_bf16 = pl.pallas_call(
        _mixer_kernel,
        grid=(B, L // TM),
        in_specs=[tile(TM), _resident((N_META, D)), _resident((2, D)), _resident((1, D)),
                  _resident((D, N_IN_SECTIONS * D + WEIGHT_LANE_PAD)), _resident((1, HEAD_DIM)),
                  _resident((3, D)), _resident((D, D)),
                  _resident((N_HEADS * HEAD_DIM, CHUNK)), wup_block,
                  pl.BlockSpec((W_DOWN_ROWS_PER_STEP, D), wdn_index)],
        out_specs=[tile(TM), meta_tile, wup_block,
                   pl.BlockSpec((W_DOWN_ROWS_PER_STEP, D + WEIGHT_LANE_PAD), wdn_index)],
        out_shape=[jax.ShapeDtypeStruct((B, L, D), F32), jax.ShapeDtypeStruct((B, N_META, D), F32),
                   jax.ShapeDtypeStruct((D, 2 * D_FF), BF16),
                   jax.ShapeDtypeStruct((D_FF, D + WEIGHT_LANE_PAD), BF16)],
        scratch_shapes=[pltpu.VMEM((N_HEADS, HEAD_DIM, HEAD_DIM), F32),
                        pltpu.VMEM((N_HEADS, HEAD_DIM, HEAD_DIM), F32),
                        pltpu.VMEM((SUBLANES, D), F32),
                        pltpu.VMEM((SUBLANES, D), F32),
                        pltpu.VMEM((TM, D), BF16),
                        pltpu.SMEM((N_GROUPS,), jnp.int32),
                        pltpu.VMEM((N_HEADS, HEAD_DIM, HEAD_DIM), F32),
                        pltpu.VMEM((SUBLANES, D), F32),
                        pltpu.VMEM((N_META, D), F32),
                        pltpu.VMEM((D, D + WEIGHT_LANE_PAD), BF16),
                        *(slot * N_GROUPS)],
        compiler_params=params,
        name="mixer",
    )(x, meta_tokens, lb_param, attn_norm_w, _weight(w_in[0]), hgrn_norm_w, conv_w[0],
      w_out[0], _rsel_matrix(), w_up[0], w_down[0])

    TF = FFN_TILE
    out = pl.pallas_call(
        _ffn_kernel,
        grid=(B, L // TF),
        in_specs=[tile(TF), meta_tile, _resident((1, D)), _resident((D, 2 * D_FF)), _resident((3, D_FF)),
                  _resident((1, D_FF)), _resident((D_FF, D + WEIGHT_LANE_PAD)), _resident((1, D))],
        out_specs=tile(TF),
        out_shape=jax.ShapeDtypeStruct((B, L, D), F32),
        scratch_shapes=[pltpu.VMEM((TF + SUBLANES, D_FF), F32)],
        compiler_params=params,
        name="ffn",
    )(h1, h1m, ffn_norm_w, w_up_bf16, ffn_conv_w[0], ffn_conv_b, w_down_bf16, final_norm_w.reshape(1, D))
    return out
```
